```python
import math, functools
import jax, jax.numpy as jnp
from jax import lax
import numpy as np

D_MODEL = 1024
BATCH = 2
SEQ = 8192
DEPTH = 1
DEC_BATCH = 128
DEC_SEQ = 1
PAST_LEN = 8192
PAGE_SIZE = 128

WIN_GROUPS = ((128, 1), (512, 4), (2048, 16))
N_GROUPS = 3
A_HEADS = 4
A_HEAD_DIM = D_MODEL // 16
D_A = A_HEADS * A_HEAD_DIM
QBLK = 128
HG_KEY_DIM = 128
HG_VAL_DIM = 128
D_B = D_MODEL // 2
HG_HEADS = D_B // HG_KEY_DIM
HG_CHUNK = 64
MEM_LEN = 256
MEM_HEADS = 4
MEM_HEAD_DIM = D_MODEL // 16
D_M = MEM_HEADS * MEM_HEAD_DIM
N_BRANCH = 3
EPS = 1e-6
IN_SPLITS = (N_GROUPS * D_A, N_GROUPS * D_A, N_GROUPS * D_A, D_A,
             D_B, D_B, D_B, D_B,
             D_M, D_M,
             N_BRANCH * D_MODEL)
D_IN = sum(IN_SPLITS)
F32 = jnp.float32

kernel_name = 'hybrid_dilated_hgrn2_mem_decoder_step'


def _split_points():
    pts, acc = [], 0
    for s in IN_SPLITS[:-1]:
        acc += s
        pts.append(acc)
    return pts


def rms_norm(x, gain):
    x32 = x.astype(F32)
    y = x32 * lax.rsqrt(jnp.mean(x32 * x32, axis=-1, keepdims=True) + EPS)
    return (y * gain.astype(F32)).astype(x.dtype)


def dilated_window_prompt(q, k, v, window, dil):
    B, S, H, Dh = q.shape
    steps = window // dil
    n = S // dil
    nb = -(-n // QBLK)
    pad = nb * QBLK - n

    def blocks(t):
        t = t.astype(F32).reshape(B, n, dil, H, Dh).transpose(0, 2, 1, 3, 4)
        t = jnp.pad(t, ((0, 0), (0, 0), (0, pad), (0, 0), (0, 0)))
        return t.reshape(B, dil, nb, QBLK, H, Dh)

    def with_prev(t):
        prev = jnp.pad(t[:, :, :-1], ((0, 0), (0, 0), (1, 0), (0, 0), (0, 0), (0, 0)))
        return jnp.concatenate([prev, t], axis=3)

    qb = blocks(q)
    kk = with_prev(blocks(k))
    vv = with_prev(blocks(v))
    s = jnp.einsum('brnqhd,brnkhd->brnhqk', qb, kk) * (Dh ** -0.5)
    qi = jnp.arange(QBLK)[:, None]
    kj = jnp.arange(2 * QBLK)[None, :]
    dist = QBLK + qi - kj
    key_idx = (jnp.arange(nb)[:, None, None] - 1) * QBLK + kj[None]
    mask = (dist >= 0) & (dist <= steps) & (key_idx >= 0)
    s = jnp.where(mask[:, None], s, -jnp.inf)
    mx = jnp.max(s, axis=-1, keepdims=True)
    p = jnp.exp(s - mx)
    l = jnp.sum(p, axis=-1, keepdims=True)
    o = jnp.einsum('brnhqk,brnkhd->brnqhd', p, vv)
    o = o / jnp.moveaxis(l[..., 0], 3, 4)[..., None]
    lse = jnp.moveaxis((mx + jnp.log(l))[..., 0], 3, 4)

    def unblock(t):
        t = t.reshape((B, dil, nb * QBLK) + t.shape[4:])[:, :, :n]
        t = jnp.moveaxis(t, 1, 2)
        return t.reshape((B, S) + t.shape[3:])

    return unblock(o), unblock(lse)


def dilated_window_step(q, k, v, kv_buf, window, dil):
    T = q.shape[1]
    L = kv_buf.shape[1]
    Dh = q.shape[-1]
    steps = window // dil
    kc = jnp.concatenate([kv_buf[:, :, 0].astype(F32), k.astype(F32)], axis=1)
    vc = jnp.concatenate([kv_buf[:, :, 1].astype(F32), v.astype(F32)], axis=1)
    idx = L + jnp.arange(T)[:, None] - dil * jnp.arange(steps + 1)[None, :]
    valid = idx >= 0
    idx = jnp.maximum(idx, 0)
    kg = kc[:, idx]
    vg = vc[:, idx]
    s = jnp.einsum('bthd,btmhd->bthm', q.astype(F32), kg) * (Dh ** -0.5)
    s = jnp.where(valid[None, :, None, :], s, -jnp.inf)
    mx = jnp.max(s, axis=-1, keepdims=True)
    p = jnp.exp(s - mx)
    l = jnp.sum(p, axis=-1, keepdims=True)
    o = jnp.einsum('bthm,btmhd->bthd', p, vg) / l
    lse = (mx + jnp.log(l))[..., 0]
    return o, lse


def combine_groups(outs, lses):
    w = jax.nn.softmax(jnp.stack(lses, axis=0), axis=0)
    return jnp.sum(w[..., None] * jnp.stack(outs, axis=0), axis=0)


def window_mixer_prompt(qa, ka, va):
    S = qa.shape[1]
    outs, lses, rows = [], [], []
    for g, (w, d) in enumerate(WIN_GROUPS):
        o, lse = dilated_window_prompt(qa[:, :, g], ka[:, :, g], va[:, :, g], w, d)
        outs.append(o)
        lses.append(lse)
        L = min(w, S)
        rows.append(jnp.stack([ka[:, S - L:, g], va[:, S - L:, g]], axis=2))
    return combine_groups(outs, lses), rows


def window_mixer_step(qa, ka, va, bufs):
    outs, lses, rows = [], [], []
    for g, (w, d) in enumerate(WIN_GROUPS):
        o, lse = dilated_window_step(qa[:, :, g], ka[:, :, g], va[:, :, g], bufs[g], w, d)
        outs.append(o)
        lses.append(lse)
        rows.append(jnp.stack([ka[:, :, g], va[:, :, g]], axis=2))
    return combine_groups(outs, lses), rows


def hgrn2_chunked(q, k, v, g, s0):
    B, L, H, Dk = q.shape
    Dv = v.shape[-1]
    C = min(HG_CHUNK, L)
    nc = -(-L // C)
    pad = nc * C - L

    def chunks(t):
        t = jnp.pad(t.astype(F32), ((0, 0), (0, pad), (0, 0), (0, 0)))
        return t.reshape(B, nc, C, H, t.shape[-1]).transpose(1, 0, 3, 2, 4)

    causal = jnp.tril(jnp.ones((C, C), dtype=bool))[..., None]

    def step(s, inp):
        qc, kc, vc, gc = inp
        G = jnp.cumsum(gc, axis=2)
        o = jnp.einsum('bhtd,bhde->bhte', qc * jnp.exp(G), s)
        diff = G[:, :, :, None, :] - G[:, :, None, :, :]
        decay = jnp.where(causal, jnp.exp(jnp.where(causal, diff, 0.0)), 0.0)
        att = jnp.einsum('bhtsd,bhsd->bhts', qc[:, :, :, None, :] * decay, kc)
        o = o + jnp.einsum('bhts,bhse->bhte', att, vc)
        g_last = G[:, :, -1:, :]
        s = jnp.exp(g_last[:, :, 0, :, None]) * s + jnp.einsum('bhsd,bhse->bhde', kc * jnp.exp(g_last - G), vc)
        return s, o

    s_fin, o = lax.scan(step, s0.astype(F32), (chunks(q), chunks(k), chunks(v), chunks(g)))
    o = o.transpose(1, 0, 3, 2, 4).reshape(B, nc * C, H, Dv)[:, :L]
    return o, s_fin


def memory_kv(mem, gain, w_kv):
    B = mem.shape[0]
    hm = rms_norm(mem, gain)
    return jnp.einsum('bmd,de->bme', hm, w_kv).reshape(B, MEM_LEN, 2, MEM_HEADS, MEM_HEAD_DIM)


def memory_attention(q, mem_kv):
    s = jnp.einsum('bthd,bmhd->bhtm', q.astype(F32), mem_kv[:, :, 0].astype(F32)) * (MEM_HEAD_DIM ** -0.5)
    p = jax.nn.softmax(s, axis=-1)
    return jnp.einsum('bhtm,bmhd->bthd', p, mem_kv[:, :, 1].astype(F32))


def layer_forward(x, win_fn, hg_state0, mem_kv, lb, norm_in_l, w_in_l, norm_hgrn_l,
                  w_branch_a_l, w_branch_b_l, w_branch_m_l, w_out_l):
    B, T, _ = x.shape
    h = rms_norm(x, norm_in_l)
    z = jnp.einsum('btd,de->bte', h, w_in_l)
    qa, ka, va, ga, qb, fb, ib, gb, qm, gm, zg = jnp.split(z, _split_points(), axis=-1)
    grp = lambda t: t.reshape(B, T, N_GROUPS, A_HEADS, A_HEAD_DIM)
    oa, win_rows = win_fn(grp(qa), grp(ka), grp(va))
    ua = oa.reshape(B, T, D_A) * jax.nn.silu(ga.astype(F32))
    f = lb + (1.0 - lb) * jax.nn.sigmoid(fb.astype(F32))
    f = f.reshape(B, T, HG_HEADS, HG_KEY_DIM)
    hq = jax.nn.silu(qb.astype(F32)).reshape(B, T, HG_HEADS, HG_KEY_DIM)
    ob, hg_state = hgrn2_chunked(hq, 1.0 - f, ib.reshape(B, T, HG_HEADS, HG_VAL_DIM), jnp.log(f), hg_state0)
    ub = rms_norm(ob, norm_hgrn_l.reshape(HG_HEADS, HG_VAL_DIM)).reshape(B, T, D_B) * jax.nn.silu(gb.astype(F32))
    om = memory_attention(qm.reshape(B, T, MEM_HEADS, MEM_HEAD_DIM), mem_kv)
    um = om.reshape(B, T, D_M) * jax.nn.silu(gm.astype(F32))
    gates = jax.nn.sigmoid(zg.astype(F32)).reshape(B, T, N_BRANCH, D_MODEL)
    merged = (gates[:, :, 0] * jnp.einsum('btc,cd->btd', ua, w_branch_a_l)
              + gates[:, :, 1] * jnp.einsum('btc,cd->btd', ub, w_branch_b_l)
              + gates[:, :, 2] * jnp.einsum('btc,cd->btd', um, w_branch_m_l))
    x = x + jnp.einsum('btd,de->bte', merged, w_out_l).astype(x.dtype)
    return x, win_rows, hg_state


def setup_inputs(seed: int = 0) -> dict:
    key = jax.random.key(seed)
    ks = jax.random.split(key, 20)

    def nrm(k, shape, scale=1.0):
        return jax.random.normal(k, shape, F32) * scale

    def gain(k, shape):
        return 1.0 + 0.02 * jax.random.normal(k, shape, F32)

    win_len = [min(w, PAST_LEN) for w, _ in WIN_GROUPS]
    kv_tail = (2, A_HEADS, A_HEAD_DIM)
    return {
        'x_prompt': nrm(ks[0], (BATCH, SEQ, D_MODEL)),
        'x_sample': nrm(ks[1], (DEC_BATCH, DEC_SEQ, D_MODEL)),
        'mem_prompt': nrm(ks[2], (BATCH, MEM_LEN, D_MODEL)),
        'cache_w1_kv': nrm(ks[3], (DEPTH, DEC_BATCH, win_len[0]) + kv_tail),
        'cache_w2_kv': nrm(ks[4], (DEPTH, DEC_BATCH, win_len[1]) + kv_tail),
        'cache_w3_kv': nrm(ks[5], (DEPTH, DEC_BATCH, win_len[2]) + kv_tail),
        'cache_mem_kv': nrm(ks[6], (DEPTH, DEC_BATCH, MEM_LEN, 2, MEM_HEADS, MEM_HEAD_DIM)),
        'state_hgrn': nrm(ks[7], (DEPTH, DEC_BATCH, HG_HEADS, HG_KEY_DIM, HG_VAL_DIM), 0.5),
        'norm_in': gain(ks[8], (DEPTH, D_MODEL)),
        'w_in': nrm(ks[9], (DEPTH, D_MODEL, D_IN), D_MODEL ** -0.5),
        'lb_logits': nrm(ks[10], (DEPTH + 1, D_B), 0.5),
        'norm_hgrn': gain(ks[11], (DEPTH, D_B)),
        'norm_mem': gain(ks[12], (DEPTH, D_MODEL)),
        'w_mem_kv': nrm(ks[13], (DEPTH, D_MODEL, 2 * D_M), D_MODEL ** -0.5),
        'w_branch_a': nrm(ks[14], (DEPTH, D_A, D_MODEL), D_A ** -0.5),
        'w_branch_b': nrm(ks[15], (DEPTH, D_B, D_MODEL), D_B ** -0.5),
        'w_branch_m': nrm(ks[16], (DEPTH, D_M, D_MODEL), D_M ** -0.5),
        'w_out': nrm(ks[17], (DEPTH, D_MODEL, D_MODEL), D_MODEL ** -0.5),
        'norm_final': gain(ks[18], (D_MODEL,)),
    }


def reference(x_prompt, x_sample, mem_prompt, cache_w1_kv, cache_w2_kv, cache_w3_kv, cache_mem_kv,
              state_hgrn, norm_in, w_in, lb_logits, norm_hgrn, norm_mem, w_mem_kv, w_branch_a,
              w_branch_b, w_branch_m, w_out, norm_final):
    lb_all = jnp.cumsum(jax.nn.softmax(lb_logits.astype(F32), axis=0), axis=0)
    xp, xs = x_prompt, x_sample
    pw1, pw2, pw3, pmem, phg = [], [], [], [], []
    sw1, sw2, sw3, shg = [], [], [], []
    for l in range(DEPTH):
        mem_kv_p = memory_kv(mem_prompt, norm_mem[l], w_mem_kv[l])
        hg0 = jnp.zeros((xp.shape[0], HG_HEADS, HG_KEY_DIM, HG_VAL_DIM), F32)
        xp, rows_p, hg_p = layer_forward(xp, window_mixer_prompt, hg0, mem_kv_p, lb_all[l],
                                         norm_in[l], w_in[l], norm_hgrn[l], w_branch_a[l],
                                         w_branch_b[l], w_branch_m[l], w_out[l])
        step_fn = functools.partial(window_mixer_step, bufs=(cache_w1_kv[l], cache_w2_kv[l], cache_w3_kv[l]))
        xs, rows_s, hg_s = layer_forward(xs, step_fn, state_hgrn[l], cache_mem_kv[l], lb_all[l],
                                         norm_in[l], w_in[l], norm_hgrn[l], w_branch_a[l],
                                         w_branch_b[l], w_branch_m[l], w_out[l])
        pw1.append(rows_p[0]); pw2.append(rows_p[1]); pw3.append(rows_p[2])
        pmem.append(mem_kv_p); phg.append(hg_p)
        sw1.append(rows_s[0]); sw2.append(rows_s[1]); sw3.append(rows_s[2])
        shg.append(hg_s)
    y_prompt = rms_norm(xp, norm_final)
    y_sample = rms_norm(xs, norm_final)
    return (y_prompt, y_sample, jnp.stack(pw1), jnp.stack(pw2), jnp.stack(pw3), jnp.stack(pmem),
            jnp.stack(phg), jnp.stack(sw1), jnp.stack(sw2), jnp.stack(sw3), jnp.stack(shg))
```

```python
import functools

import jax
import jax.numpy as jnp
from jax import lax
from jax.experimental import pallas as pl
from jax.experimental.pallas import tpu as pltpu

F32 = jnp.float32
BF16 = jnp.bfloat16

D_MODEL = 1024
WIN_GROUPS = ((128, 1), (512, 4), (2048, 16))
N_GROUPS = 3
A_HEADS = 4
A_HEAD_DIM = 64
D_A = 256
QBLK = 128
HG_HEADS = 4
HG_DIM = 128
D_B = 512
HG_CHUNK = 64
HG_SUB = 16
MEM_LEN = 256
D_M = 256
EPS = 1e-6
D_IN = 8192
OFF_QA, OFF_KA, OFF_VA, OFF_GA = 0, 768, 1536, 2304
OFF_QB, OFF_FB, OFF_IB, OFF_GB = 2560, 3072, 3584, 4096
OFF_QM, OFF_GM, OFF_ZG = 4608, 4864, 5120
ATT_SCALE = A_HEAD_DIM ** -0.5

VMEM_LIMIT = 56 * 1024 * 1024

NT_DIMS = (((1,), (1,)), ((), ()))
TN_DIMS = (((0,), (0,)), ((), ()))


def _sigmoid(x):
    return 1.0 / (1.0 + jnp.exp(-x))


def _silu(x):
    return x * _sigmoid(x)


def _rms(x, gain):
    return x * lax.rsqrt(jnp.mean(x * x, axis=-1, keepdims=True) + EPS) * gain


def _head_masks(width, head_dim, n_heads):
    lane = lax.broadcasted_iota(jnp.int32, (1, width), 1)
    return [(lane >= h * head_dim) & (lane < (h + 1) * head_dim) for h in range(n_heads)]


def _stack_heads(q, masks):
    zero = jnp.zeros_like(q)
    return jnp.concatenate([jnp.where(m, q, zero) for m in masks], axis=0)


def _unstack_heads(o_all, masks, t):
    out = jnp.where(masks[0], o_all[0:t], 0.0)
    for h in range(1, len(masks)):
        out = out + jnp.where(masks[h], o_all[h * t:(h + 1) * t], 0.0)
    return out


def _expand_cols(cols, masks):
    out = jnp.where(masks[0], cols[0], 0.0)
    for h in range(1, len(masks)):
        out = out + jnp.where(masks[h], cols[h], 0.0)
    return out


def _mem_kv_kernel(mem_ref, gain_ref, w_ref, out_ref):
    hm = _rms(mem_ref[0], gain_ref[...]).astype(BF16)
    out_ref[0] = jnp.dot(hm, w_ref[...], preferred_element_type=F32)


def _mem_kv_call(mem, gain, w_bf):
    b = mem.shape[0]
    return pl.pallas_call(
        _mem_kv_kernel,
        out_shape=jax.ShapeDtypeStruct((b, MEM_LEN, 2 * D_M), F32),
        grid=(b,),
        in_specs=[pl.BlockSpec((1, MEM_LEN, D_MODEL), lambda i: (i, 0, 0)),
                  pl.BlockSpec((1, D_MODEL), lambda i: (0, 0)),
                  pl.BlockSpec((D_MODEL, 2 * D_M), lambda i: (0, 0))],
        out_specs=pl.BlockSpec((1, MEM_LEN, 2 * D_M), lambda i: (i, 0, 0)),
        name="mem_kv",
    )(mem, gain, w_bf)


IN_TILE = 512
IN_CHUNK = 256
KV_TAIL = 2048


def _inproj_kernel(x_ref, gain_ref, w_ref, z_ref, fb_ref, kv_ref):
    h = _rms(x_ref[0], gain_ref[...]).astype(BF16)
    for c in range(D_IN // IN_CHUNK):
        lo = c * IN_CHUNK
        zc = jnp.dot(h, w_ref[:, lo:lo + IN_CHUNK], preferred_element_type=F32)
        z_ref[0, :, lo:lo + IN_CHUNK] = zc.astype(BF16)
        if OFF_FB <= lo < OFF_FB + D_B:
            fb_ref[0, :, lo - OFF_FB:lo - OFF_FB + IN_CHUNK] = zc
        if OFF_KA <= lo < OFF_GA:
            kv_ref[0, :, lo - OFF_KA:lo - OFF_KA + IN_CHUNK] = zc


def _inproj_call(x, gain, w_bf):
    b, s, _ = x.shape
    nt = s // IN_TILE
    first_tail = (s - KV_TAIL) // IN_TILE
    return pl.pallas_call(
        _inproj_kernel,
        out_shape=(jax.ShapeDtypeStruct((b, s, D_IN), BF16),
                   jax.ShapeDtypeStruct((b, s, D_B), F32),
                   jax.ShapeDtypeStruct((b, KV_TAIL, 2 * N_GROUPS * D_A), F32)),
        grid=(b, nt),
        in_specs=[pl.BlockSpec((1, IN_TILE, D_MODEL), lambda i, j: (i, j, 0)),
                  pl.BlockSpec((1, D_MODEL), lambda i, j: (0, 0)),
                  pl.BlockSpec((D_MODEL, D_IN), lambda i, j: (0, 0), pipeline_mode=pl.Buffered(1))],
        out_specs=(pl.BlockSpec((1, IN_TILE, D_IN), lambda i, j: (i, j, 0)),
                   pl.BlockSpec((1, IN_TILE, D_B), lambda i, j: (i, j, 0)),
                   pl.BlockSpec((1, IN_TILE, 2 * N_GROUPS * D_A),
                                lambda i, j: (i, jnp.maximum(j - first_tail, 0), 0))),
        compiler_params=pltpu.CompilerParams(
            dimension_semantics=("arbitrary", "arbitrary"), vmem_limit_bytes=VMEM_LIMIT),
        name="inproj",
    )(x, gain, w_bf)


def _dilated_kernel(q_ref, kp_ref, kc_ref, vp_ref, vc_ref, out_ref):
    ub = pl.program_id(2)
    masks = _head_masks(D_A, A_HEAD_DIM, A_HEADS)
    q = q_ref[0] * jnp.asarray(ATT_SCALE, BF16)
    qs = _stack_heads(q, masks)
    kk = jnp.concatenate([kp_ref[0], kc_ref[0]], axis=0)
    vv = jnp.concatenate([vp_ref[0], vc_ref[0]], axis=0)
    s = lax.dot_general(qs, kk, NT_DIMS, preferred_element_type=F32)
    qi = lax.broadcasted_iota(jnp.int32, s.shape, 0) % QBLK
    kj = lax.broadcasted_iota(jnp.int32, s.shape, 1)
    first_key = jnp.where(ub == 0, QBLK, 0)
    valid = (kj >= qi) & (kj <= qi + QBLK) & (kj >= first_key)
    s = jnp.where(valid, s, -jnp.inf)
    mx = jnp.max(s, axis=-1, keepdims=True)
    p = jnp.exp(s - mx)
    l = jnp.sum(p, axis=-1, keepdims=True)
    o_all = jnp.dot(p.astype(BF16), vv, preferred_element_type=F32) / l
    lse = mx + jnp.log(l)
    out_ref[0, :, 0:D_A] = _unstack_heads(o_all, masks, QBLK)
    out_ref[0, :, D_A:2 * D_A] = _expand_cols(
        [lse[h * QBLK:(h + 1) * QBLK] for h in range(A_HEADS)], masks)


def _dilated_call(z_bf, g, dil):
    b, s, _ = z_bf.shape
    n = s // dil
    nb = n // QBLK
    zv = z_bf.reshape(b, n, dil * D_IN)
    cpr = D_IN // D_A
    qcol, kcol, vcol = OFF_QA // D_A + g, OFF_KA // D_A + g, OFF_VA // D_A + g

    def cur(col):
        return pl.BlockSpec((1, QBLK, D_A), lambda i, r, u: (i, u, r * cpr + col))

    def prev(col):
        return pl.BlockSpec((1, QBLK, D_A), lambda i, r, u: (i, jnp.maximum(u - 1, 0), r * cpr + col))

    out = pl.pallas_call(
        _dilated_kernel,
        out_shape=jax.ShapeDtypeStruct((b, n, dil * 2 * D_A), F32),
        grid=(b, dil, nb),
        in_specs=[cur(qcol), prev(kcol), cur(kcol), prev(vcol), cur(vcol)],
        out_specs=pl.BlockSpec((1, QBLK, 2 * D_A), lambda i, r, u: (i, u, r)),
        compiler_params=pltpu.CompilerParams(
            dimension_semantics=("arbitrary", "arbitrary", "arbitrary")),
        name=f"dilated_attn_g{g}",
    )(zv, zv, zv, zv, zv)
    return out.reshape(b, s, 2 * D_A)


def _lower_bound(lbl):
    l0, l1 = lbl[0:1], lbl[1:2]
    m = jnp.maximum(l0, l1)
    e0, e1 = jnp.exp(l0 - m), jnp.exp(l1 - m)
    return e0 / (e0 + e1)


def _head_rms(ob, gain):
    parts = []
    for h in range(HG_HEADS):
        sl = slice(h * HG_DIM, (h + 1) * HG_DIM)
        parts.append(_rms(ob[:, sl], gain[:, sl]))
    return jnp.concatenate(parts, axis=1)


def _merge_out(x, ua, ub, um, zg0, zg1, zg2, wa_ref, wb_ref, wm_ref, wo_ref, nf):
    merged = (_sigmoid(zg0) * jnp.dot(ua.astype(BF16), wa_ref[...], preferred_element_type=F32)
              + _sigmoid(zg1) * jnp.dot(ub.astype(BF16), wb_ref[...], preferred_element_type=F32)
              + _sigmoid(zg2) * jnp.dot(um.astype(BF16), wm_ref[...], preferred_element_type=F32))
    y = x + jnp.dot(merged.astype(BF16), wo_ref[...], preferred_element_type=F32)
    return _rms(y, nf)


TAIL_TILE = 256


def _cumsum_rows(g, tri):
    g1 = g.astype(BF16)
    r1 = g - g1.astype(F32)
    g2 = r1.astype(BF16)
    g3 = (r1 - g2.astype(F32)).astype(BF16)
    return (jnp.dot(tri, g1, preferred_element_type=F32)
            + jnp.dot(tri, g2, preferred_element_type=F32)
            + jnp.dot(tri, g3, preferred_element_type=F32))


def _prompt_tail_kernel(x_ref, ga_ref, qb_ref, ib_ref, gb_ref, qm_ref, gm_ref, zg0_ref, zg1_ref, zg2_ref,
                        fb_ref, o1_ref, o2_ref, o3_ref, mkv_ref, lbl_ref, nh_ref, nf_ref,
                        wa_ref, wb_ref, wm_ref, wo_ref,
                        y_ref, hst_ref, st_ref, ob_ref):
    j = pl.program_id(1)
    t = TAIL_TILE

    @pl.when(j == 0)
    def _():
        st_ref[...] = jnp.zeros_like(st_ref)

    os_ = [r[0, :, 0:D_A] for r in (o1_ref, o2_ref, o3_ref)]
    ls_ = [r[0, :, D_A:2 * D_A] for r in (o1_ref, o2_ref, o3_ref)]
    mx = jnp.maximum(jnp.maximum(ls_[0], ls_[1]), ls_[2])
    es_ = [jnp.exp(l - mx) for l in ls_]
    oa = (es_[0] * os_[0] + es_[1] * os_[1] + es_[2] * os_[2]) / (es_[0] + es_[1] + es_[2])
    ua = oa * _silu(ga_ref[0].astype(F32))

    masks = _head_masks(D_M, A_HEAD_DIM, A_HEADS)
    qm = qm_ref[0] * jnp.asarray(ATT_SCALE, BF16)
    qs = _stack_heads(qm, masks)
    mk = mkv_ref[0, :, 0:D_M].astype(BF16)
    mv = mkv_ref[0, :, D_M:2 * D_M].astype(BF16)
    s = lax.dot_general(qs, mk, NT_DIMS, preferred_element_type=F32)
    p = jnp.exp(s - jnp.max(s, axis=-1, keepdims=True))
    l = jnp.sum(p, axis=-1, keepdims=True)
    om = _unstack_heads(jnp.dot(p.astype(BF16), mv, preferred_element_type=F32) / l, masks, t)
    um = om * _silu(gm_ref[0].astype(F32))

    lb = _lower_bound(lbl_ref[...])
    c = HG_CHUNK
    tri = (lax.broadcasted_iota(jnp.int32, (c, c), 0) >= lax.broadcasted_iota(jnp.int32, (c, c), 1)).astype(BF16)

    def chunk(ci, carry):
        r0 = pl.multiple_of(ci * c, c)
        f = lb + (1.0 - lb) * _sigmoid(fb_ref[0, pl.ds(r0, c), :])
        kk = 1.0 - f
        gcs = _cumsum_rows(jnp.log(f), tri)
        q = _silu(qb_ref[0, pl.ds(r0, c), :].astype(F32))
        v = ib_ref[0, pl.ds(r0, c), :]
        glast = gcs[c - 1:c]
        qe = (q * jnp.exp(gcs)).astype(BF16)
        kd = (kk * jnp.exp(glast - gcs)).astype(BF16)
        for h in range(HG_HEADS):
            sl = slice(h * HG_DIM, (h + 1) * HG_DIM)
            st = st_ref[h]
            o_inter = lax.dot_general(qe[:, sl], st.astype(BF16), NT_DIMS, preferred_element_type=F32)
            intra = []
            for i in range(c // HG_SUB):
                lo, hi = i * HG_SUB, (i + 1) * HG_SUB
                ref = gcs[lo - 1:lo, sl] if i > 0 else jnp.zeros((1, HG_DIM), F32)
                qt = (q[lo:hi, sl] * jnp.exp(gcs[lo:hi, sl] - ref)).astype(BF16)
                kt = (kk[0:hi, sl] * jnp.exp(ref - gcs[0:hi, sl])).astype(BF16)
                att = lax.dot_general(qt, kt, NT_DIMS, preferred_element_type=F32)
                row = lax.broadcasted_iota(jnp.int32, att.shape, 0) + lo
                col = lax.broadcasted_iota(jnp.int32, att.shape, 1)
                att = jnp.where(col <= row, att, 0.0)
                intra.append(jnp.dot(att.astype(BF16), v[0:hi, sl], preferred_element_type=F32))
            ob_ref[pl.ds(r0, c), sl] = o_inter + jnp.concatenate(intra, axis=0)
            st_ref[h] = st * jnp.exp(glast[:, sl]) + lax.dot_general(
                v[:, sl], kd[:, sl], TN_DIMS, preferred_element_type=F32)
        return carry

    lax.fori_loop(0, t // c, chunk, 0)
    ub = _head_rms(ob_ref[...], nh_ref[...]) * _silu(gb_ref[0].astype(F32))

    y_ref[0] = _merge_out(x_ref[0], ua, ub, um, zg0_ref[0].astype(F32), zg1_ref[0].astype(F32),
                          zg2_ref[0].astype(F32), wa_ref, wb_ref, wm_ref, wo_ref, nf_ref[...])

    @pl.when(j == pl.num_programs(1) - 1)
    def _():
        for h in range(HG_HEADS):
            hst_ref[0, h] = st_ref[h].T


def _prompt_tail_call(x, z_bf, fb, o1, o2, o3, mkv, lbl, nh, nf, wa, wb, wm, wo):
    b, s, _ = x.shape
    t = TAIL_TILE

    def zblk(width, off):
        return pl.BlockSpec((1, t, width), lambda i, j: (i, j, off // width))

    def tok(width):
        return pl.BlockSpec((1, t, width), lambda i, j: (i, j, 0))

    def const(shape):
        return pl.BlockSpec(shape, lambda i, j: (0,) * len(shape))

    return pl.pallas_call(
        _prompt_tail_kernel,
        out_shape=(jax.ShapeDtypeStruct((b, s, D_MODEL), F32),
                   jax.ShapeDtypeStruct((b, HG_HEADS, HG_DIM, HG_DIM), F32)),
        grid=(b, s // t),
        in_specs=[tok(D_MODEL),
                  zblk(D_A, OFF_GA), zblk(D_B, OFF_QB), zblk(D_B, OFF_IB), zblk(D_B, OFF_GB),
                  zblk(D_M, OFF_QM), zblk(D_M, OFF_GM),
                  zblk(D_MODEL, OFF_ZG), zblk(D_MODEL, OFF_ZG + D_MODEL), zblk(D_MODEL, OFF_ZG + 2 * D_MODEL),
                  tok(D_B), tok(2 * D_A), tok(2 * D_A), tok(2 * D_A),
                  pl.BlockSpec((1, MEM_LEN, 2 * D_M), lambda i, j: (i, 0, 0)),
                  const((2, D_B)), const((1, D_B)), const((1, D_MODEL)),
                  const((D_A, D_MODEL)), const((D_B, D_MODEL)), const((D_M, D_MODEL)),
                  const((D_MODEL, D_MODEL))],
        out_specs=(tok(D_MODEL),
                   pl.BlockSpec((1, HG_HEADS, HG_DIM, HG_DIM), lambda i, j: (i, 0, 0, 0))),
        scratch_shapes=[pltpu.VMEM((HG_HEADS, HG_DIM, HG_DIM), F32),
                        pltpu.VMEM((t, D_B), F32)],
        compiler_params=pltpu.CompilerParams(
            dimension_semantics=("arbitrary", "arbitrary"), vmem_limit_bytes=VMEM_LIMIT),
        name="prompt_tail",
    )(x, z_bf, z_bf, z_bf, z_bf, z_bf, z_bf, z_bf, z_bf, z_bf, fb, o1, o2, o3, mkv, lbl, nh, nf,
      wa, wb, wm, wo)


S_COLS = 1024


def _sample_inproj_kernel(x_ref, gain_ref, w_ref, z_ref, zt_ref):
    h = _rms(x_ref[...], gain_ref[...]).astype(BF16)
    z = jnp.dot(h, w_ref[...], preferred_element_type=F32)
    z_ref[...] = z
    zt_ref[...] = z.T


def _sample_inproj_call(x, gain, w_bf):
    db = x.shape[0]
    return pl.pallas_call(
        _sample_inproj_kernel,
        out_shape=(jax.ShapeDtypeStruct((db, D_IN), F32),
                   jax.ShapeDtypeStruct((D_IN, db), F32)),
        grid=(D_IN // S_COLS,),
        in_specs=[pl.BlockSpec((db, D_MODEL), lambda c: (0, 0)),
                  pl.BlockSpec((1, D_MODEL), lambda c: (0, 0)),
                  pl.BlockSpec((D_MODEL, S_COLS), lambda c: (0, c))],
        out_specs=(pl.BlockSpec((db, S_COLS), lambda c: (0, c)),
                   pl.BlockSpec((S_COLS, db), lambda c: (c, 0))),
        name="sample_inproj",
    )(x, gain, w_bf)


S_TILE = 8


def _joint_attention(groups, e_mat, et_mat, masks):
    bt = groups[0][0].shape[0]
    sc_all, sn_all = [], []
    for q, rows, knew, _ in groups:
        prod = jnp.concatenate([rows[b][:, 0:D_A] * q[b:b + 1] for b in range(bt)], axis=0)
        sc_all.append(jnp.dot(prod.astype(BF16), e_mat, preferred_element_type=F32))
        sn_all.append(None if knew is None
                      else jnp.dot((knew * q).astype(BF16), e_mat, preferred_element_type=F32))
    mx_rows = []
    for b in range(bt):
        mx = None
        for (q, rows, _, _), sc, sn in zip(groups, sc_all, sn_all):
            m = rows[b].shape[0]
            cand = jnp.max(sc[b * m:(b + 1) * m], axis=0, keepdims=True)
            if sn is not None:
                cand = jnp.maximum(cand, sn[b:b + 1])
            mx = cand if mx is None else jnp.maximum(mx, cand)
        mx_rows.append(mx)
    den = [jnp.zeros((1, 128), F32) for _ in range(bt)]
    num = [jnp.zeros((1, D_A), F32) for _ in range(bt)]
    for (q, rows, _, vnew), sc, sn in zip(groups, sc_all, sn_all):
        m = rows[0].shape[0]
        p = jnp.concatenate([jnp.exp(sc[b * m:(b + 1) * m] - mx_rows[b]) for b in range(bt)], axis=0)
        pe = jnp.dot(p.astype(BF16), et_mat, preferred_element_type=F32)
        for b in range(bt):
            den[b] = den[b] + jnp.sum(p[b * m:(b + 1) * m], axis=0, keepdims=True)
            num[b] = num[b] + jnp.sum(pe[b * m:(b + 1) * m] * rows[b][:, D_A:2 * D_A], axis=0, keepdims=True)
        if sn is not None:
            pn = jnp.exp(sn - jnp.concatenate(mx_rows, axis=0))
            pne = jnp.dot(pn.astype(BF16), et_mat, preferred_element_type=F32) * vnew
            for b in range(bt):
                den[b] = den[b] + pn[b:b + 1]
                num[b] = num[b] + pne[b:b + 1]
    den = jnp.concatenate(den, axis=0)
    num = jnp.concatenate(num, axis=0)
    return num / _expand_cols([den[:, h:h + 1] for h in range(A_HEADS)], masks)


def _sample_mix_kernel(z_ref, zt_ref, c1_ref, c2_ref, c3_ref, cm_ref, st_ref, lblt_ref, nh_ref,
                       u_ref, nst_ref, ob_ref):
    i = pl.program_id(0)
    bt = S_TILE
    masks = _head_masks(D_A, A_HEAD_DIM, A_HEADS)
    er = lax.broadcasted_iota(jnp.int32, (D_A, 128), 0) // A_HEAD_DIM
    ec = lax.broadcasted_iota(jnp.int32, (D_A, 128), 1)
    e_mat = (er == ec).astype(BF16)
    tr = lax.broadcasted_iota(jnp.int32, (128, D_A), 0)
    tc = lax.broadcasted_iota(jnp.int32, (128, D_A), 1) // A_HEAD_DIM
    et_mat = (tr == tc).astype(BF16)

    z = z_ref[...]

    groups = []
    for g, c_ref in enumerate((c1_ref, c2_ref, c3_ref)):
        groups.append((z[:, OFF_QA + g * D_A:OFF_QA + (g + 1) * D_A] * ATT_SCALE,
                       [c_ref[b] for b in range(bt)],
                       z[:, OFF_KA + g * D_A:OFF_KA + (g + 1) * D_A],
                       z[:, OFF_VA + g * D_A:OFF_VA + (g + 1) * D_A]))
    ua = _joint_attention(groups, e_mat, et_mat, masks) * _silu(z[:, OFF_GA:OFF_GA + D_A])

    mem = [(z[:, OFF_QM:OFF_QM + D_M] * ATT_SCALE, [cm_ref[b] for b in range(bt)], None, None)]
    um = _joint_attention(mem, e_mat, et_mat, masks) * _silu(z[:, OFF_GM:OFF_GM + D_M])

    zt = pltpu.roll(zt_ref[...], (128 - bt * i) % 128, 1)
    lbt = lblt_ref[...]
    l0, l1 = lbt[:, 0:1], lbt[:, 1:2]
    mlb = jnp.maximum(l0, l1)
    e0, e1 = jnp.exp(l0 - mlb), jnp.exp(l1 - mlb)
    lb_col = e0 / (e0 + e1)
    q_t = _silu(zt[0:D_B])
    f_t = lb_col + (1.0 - lb_col) * _sigmoid(zt[D_B:2 * D_B])
    for b in range(bt):
        for h in range(HG_HEADS):
            sl = slice(h * HG_DIM, (h + 1) * HG_DIM)
            fcol = f_t[sl, b:b + 1]
            qcol = q_t[sl, b:b + 1]
            vrow = z[b:b + 1, OFF_IB + h * HG_DIM:OFF_IB + (h + 1) * HG_DIM]
            sn = fcol * st_ref[b, sl, :] + (1.0 - fcol) * vrow
            nst_ref[b, sl, :] = sn
            ob_ref[b:b + 1, sl] = jnp.sum(sn * qcol, axis=0, keepdims=True)
    ub = _head_rms(ob_ref[...], nh_ref[...]) * _silu(z[:, OFF_GB:OFF_GB + D_B])

    u_ref[:, 0:D_A] = ua
    u_ref[:, D_A:D_A + D_B] = ub
    u_ref[:, D_A + D_B:D_A + D_B + D_M] = um


def _sample_mix_call(z, zt_b, c1, c2, c3, cm, st, lblt, nh):
    db = z.shape[0]
    bt = S_TILE
    c1v = c1.reshape(db, 128, 512)
    c2v = c2.reshape(db, 128, 4 * 512)
    c3v = c3.reshape(db, 128, 16 * 512)
    cmv = cm.reshape(db, MEM_LEN, 2 * D_M)
    stv = st.reshape(db, HG_HEADS * HG_DIM, HG_DIM)
    u, nst = pl.pallas_call(
        _sample_mix_kernel,
        out_shape=(jax.ShapeDtypeStruct((db, D_A + D_B + D_M), F32),
                   jax.ShapeDtypeStruct((db, HG_HEADS * HG_DIM, HG_DIM), F32)),
        grid=(db // bt,),
        in_specs=[pl.BlockSpec((bt, D_IN), lambda i: (i, 0)),
                  pl.BlockSpec((2 * D_B, db), lambda i: (0, 0)),
                  pl.BlockSpec((bt, 128, 512), lambda i: (i, 0, 0)),
                  pl.BlockSpec((bt, 128, 512), lambda i: (i, 0, 0)),
                  pl.BlockSpec((bt, 128, 512), lambda i: (i, 0, 0)),
                  pl.BlockSpec((bt, MEM_LEN, 2 * D_M), lambda i: (i, 0, 0)),
                  pl.BlockSpec((bt, HG_HEADS * HG_DIM, HG_DIM), lambda i: (i, 0, 0)),
                  pl.BlockSpec((D_B, 2), lambda i: (0, 0)),
                  pl.BlockSpec((1, D_B), lambda i: (0, 0))],
        out_specs=(pl.BlockSpec((bt, D_A + D_B + D_M), lambda i: (i, 0)),
                   pl.BlockSpec((bt, HG_HEADS * HG_DIM, HG_DIM), lambda i: (i, 0, 0))),
        scratch_shapes=[pltpu.VMEM((bt, D_B), F32)],
        compiler_params=pltpu.CompilerParams(
            dimension_semantics=("arbitrary",), vmem_limit_bytes=VMEM_LIMIT),
        name="sample_mix",
    )(z, zt_b, c1v, c2v, c3v, cmv, stv, lblt, nh)
    return u, nst


def _sample_out_kernel(x_ref, u_ref, zg_ref, nf_ref, wa_ref, wb_ref, wm_ref, wo_ref, y_ref):
    u = u_ref[...]
    zg = zg_ref[...]
    y_ref[...] = _merge_out(x_ref[...], u[:, 0:D_A], u[:, D_A:D_A + D_B], u[:, D_A + D_B:],
                            zg[:, 0:D_MODEL], zg[:, D_MODEL:2 * D_MODEL], zg[:, 2 * D_MODEL:],
                            wa_ref, wb_ref, wm_ref, wo_ref, nf_ref[...])


def _sample_out_call(x, u, z, nf, wa, wb, wm, wo):
    db = x.shape[0]
    return pl.pallas_call(
        _sample_out_kernel,
        out_shape=jax.ShapeDtypeStruct((db, D_MODEL), F32),
        grid=(1,),
        in_specs=[pl.BlockSpec((db, D_MODEL), lambda i: (0, 0)),
                  pl.BlockSpec((db, D_A + D_B + D_M), lambda i: (0, 0)),
                  pl.BlockSpec((db, 3 * D_MODEL), lambda i: (0, 0)),
                  pl.BlockSpec((1, D_MODEL), lambda i: (0, 0)),
                  pl.BlockSpec((D_A, D_MODEL), lambda i: (0, 0)),
                  pl.BlockSpec((D_B, D_MODEL), lambda i: (0, 0)),
                  pl.BlockSpec((D_M, D_MODEL), lambda i: (0, 0)),
                  pl.BlockSpec((D_MODEL, D_MODEL), lambda i: (0, 0))],
        out_specs=pl.BlockSpec((db, D_MODEL), lambda i: (0, 0)),
        compiler_params=pltpu.CompilerParams(vmem_limit_bytes=VMEM_LIMIT),
        name="sample_out",
    )(x, u, z, nf, wa, wb, wm, wo)


def _cache_rows(kv, length):
    outs = []
    for g in range(N_GROUPS):
        k = kv[:, -length[g]:, g * D_A:(g + 1) * D_A]
        v = kv[:, -length[g]:, (N_GROUPS + g) * D_A:(N_GROUPS + g + 1) * D_A]
        outs.append(jnp.stack([k, v], axis=2).reshape(kv.shape[0], length[g], 2, A_HEADS, A_HEAD_DIM)[None])
    return outs


def kernel(x_prompt, x_sample, mem_prompt, cache_w1_kv, cache_w2_kv, cache_w3_kv, cache_mem_kv, state_hgrn,
           norm_in, w_in, lb_logits, norm_hgrn, norm_mem, w_mem_kv, w_branch_a, w_branch_b, w_branch_m,
           w_out, norm_final):
    b, s, _ = x_prompt.shape
    db = x_sample.shape[0]
    w_in_bf = w_in[0].astype(BF16)
    wa, wb, wm, wo = (w[0].astype(BF16) for w in (w_branch_a, w_branch_b, w_branch_m, w_out))
    gain_in = norm_in[0][None]
    nh = norm_hgrn[0][None]
    nf = norm_final[None]
    lbl = lb_logits.astype(F32)

    mkv = _mem_kv_call(mem_prompt, norm_mem[0][None], w_mem_kv[0].astype(BF16))
    z_bf, fb, kv_tail = _inproj_call(x_prompt, gain_in, w_in_bf)
    og = [_dilated_call(z_bf, g, dil) for g, (_, dil) in enumerate(WIN_GROUPS)]
    y_prompt, hg_prompt = _prompt_tail_call(x_prompt, z_bf, fb, og[0], og[1], og[2], mkv, lbl, nh, nf,
                                            wa, wb, wm, wo)
    pw = _cache_rows(kv_tail, [min(w, s) for w, _ in WIN_GROUPS])
    new_mem = mkv.reshape(1, b, MEM_LEN, 2, A_HEADS, A_HEAD_DIM)

    xs = x_sample.reshape(db, D_MODEL)
    zs, zs_t = _sample_inproj_call(xs, gain_in, w_in_bf)
    u, nst = _sample_mix_call(zs, zs_t[OFF_QB:OFF_QB + 2 * D_B], cache_w1_kv[0], cache_w2_kv[0],
                              cache_w3_kv[0], cache_mem_kv[0], state_hgrn[0], lbl.T, nh)
    y_sample = _sample_out_call(xs, u, zs[:, OFF_ZG:], nf, wa, wb, wm, wo).reshape(db, 1, D_MODEL)
    sw = _cache_rows(zs[:, None, OFF_KA:OFF_GA], [1, 1, 1])
    new_hg_sample = nst.reshape(1, db, HG_HEADS, HG_DIM, HG_DIM)

    return (y_prompt, y_sample, pw[0], pw[1], pw[2], new_mem, hg_prompt[None],
            sw[0], sw[1], sw[2], new_hg_sample)
```

```python
import jax
import jax.numpy as jnp
from jax import lax
from jax.experimental import pallas as pl
from jax.experimental.pallas import tpu as pltpu

F32 = jnp.float32
BF16 = jnp.bfloat16

D_MODEL = 1024
WIN_GROUPS = ((128, 1), (512, 4), (2048, 16))
N_GROUPS = 3
A_HEADS = 4
A_HEAD_DIM = 64
D_A = 256
QBLK = 128
HG_HEADS = 4
HG_DIM = 128
D_B = 512
HG_CHUNK = 64
HG_SUB = 16
MEM_LEN = 256
D_M = 256
EPS = 1e-6
D_IN = 8192
LANES = 128
OFF_QA, OFF_KA, OFF_VA, OFF_GA = 0, 768, 1536, 2304
OFF_QB, OFF_FB, OFF_IB, OFF_GB = 2560, 3072, 3584, 4096
OFF_QM, OFF_GM, OFF_ZG = 4608, 4864, 5120
ATT_SCALE = A_HEAD_DIM ** -0.5

VMEM_LIMIT = 56 * 1024 * 1024

NT_DIMS = (((1,), (1,)), ((), ()))
TN_DIMS = (((0,), (0,)), ((), ()))


def _sigmoid(x):
    return 1.0 / (1.0 + jnp.exp(-x))


def _silu(x):
    return x * _sigmoid(x)


def _rms(x, gain):
    return x * lax.rsqrt(jnp.mean(x * x, axis=-1, keepdims=True) + EPS) * gain


def _head_masks(width, head_dim, n_heads):
    lane = lax.broadcasted_iota(jnp.int32, (1, width), 1)
    return [(lane >= h * head_dim) & (lane < (h + 1) * head_dim) for h in range(n_heads)]


def _stack_heads(q, masks):
    zero = jnp.zeros_like(q)
    return jnp.concatenate([jnp.where(m, q, zero) for m in masks], axis=0)


def _unstack_heads(o_all, masks, t):
    out = jnp.where(masks[0], o_all[0:t], 0.0)
    for h in range(1, len(masks)):
        out = out + jnp.where(masks[h], o_all[h * t:(h + 1) * t], 0.0)
    return out


def _expand_cols(cols, masks):
    out = jnp.where(masks[0], cols[0], 0.0)
    for h in range(1, len(masks)):
        out = out + jnp.where(masks[h], cols[h], 0.0)
    return out


def _mem_kv_kernel(mem_ref, gain_ref, w_ref, out_ref):
    hm = _rms(mem_ref[0], gain_ref[...]).astype(BF16)
    out_ref[0] = jnp.dot(hm, w_ref[...], preferred_element_type=F32)


def _mem_kv_call(mem, gain, w_bf):
    b = mem.shape[0]
    return pl.pallas_call(
        _mem_kv_kernel,
        out_shape=jax.ShapeDtypeStruct((b, MEM_LEN, 2 * D_M), F32),
        grid=(b,),
        in_specs=[pl.BlockSpec((1, MEM_LEN, D_MODEL), lambda i: (i, 0, 0)),
                  pl.BlockSpec((1, D_MODEL), lambda i: (0, 0)),
                  pl.BlockSpec((D_MODEL, 2 * D_M), lambda i: (0, 0))],
        out_specs=pl.BlockSpec((1, MEM_LEN, 2 * D_M), lambda i: (i, 0, 0)),
        name="mem_kv",
    )(mem, gain, w_bf)


IN_TILE = 512
IN_CHUNK = 256
KV_TAIL = 2048
D_QKV = 3 * D_A
D_GQ = 3 * D_A
D_HB = 3 * D_B
D_ZG = 3 * D_MODEL


def _inproj_kernel(x_ref, gain_ref, w_ref, qkv0_ref, qkv1_ref, qkv2_ref, gq_ref, hb_ref, zg_ref, fb_ref,
                   kv_ref, scr_ref):
    qkv_refs = (qkv0_ref, qkv1_ref, qkv2_ref)
    h = _rms(x_ref[0], gain_ref[...]).astype(BF16)
    t = IN_TILE
    for c in range(D_IN // IN_CHUNK):
        lo = c * IN_CHUNK
        zc = jnp.dot(h, w_ref[:, lo:lo + IN_CHUNK], preferred_element_type=F32)
        if lo < OFF_GA:
            sec, g = divmod(c, N_GROUPS)
            dil = WIN_GROUPS[g][1]
            dst = slice(sec * D_A, (sec + 1) * D_A)
            if dil == 1:
                qkv_refs[g][0, 0, :, dst] = zc.astype(BF16)
            else:
                for k in range(IN_CHUNK // LANES):
                    scr_ref[k] = zc[:, k * LANES:(k + 1) * LANES]
                for r in range(dil):
                    qkv_refs[g][0, r, :, dst] = jnp.concatenate(
                        [scr_ref[k, pl.ds(r, t // dil, stride=dil), :] for k in range(IN_CHUNK // LANES)],
                        axis=1).astype(BF16)
            if sec > 0:
                kv_ref[0, :, lo - OFF_KA:lo - OFF_KA + IN_CHUNK] = zc
        elif lo < OFF_QB:
            gq_ref[0, :, 0:D_A] = zc.astype(BF16)
        elif lo < OFF_FB:
            hb_ref[0, :, lo - OFF_QB:lo - OFF_QB + IN_CHUNK] = zc.astype(BF16)
        elif lo < OFF_IB:
            fb_ref[0, :, lo - OFF_FB:lo - OFF_FB + IN_CHUNK] = zc
        elif lo < OFF_QM:
            hb_ref[0, :, lo - OFF_IB + D_B:lo - OFF_IB + D_B + IN_CHUNK] = zc.astype(BF16)
        elif lo < OFF_ZG:
            gq_ref[0, :, lo - OFF_QM + D_A:lo - OFF_QM + D_A + IN_CHUNK] = zc.astype(BF16)
        else:
            zg_ref[0, :, lo - OFF_ZG:lo - OFF_ZG + IN_CHUNK] = zc.astype(BF16)


def _inproj_call(x, gain, w_bf):
    b, s, _ = x.shape
    t = IN_TILE
    nt = s // t
    first_tail = (s - KV_TAIL) // t

    def tok(width):
        return pl.BlockSpec((1, t, width), lambda i, j: (i, j, 0))

    qkv_shapes, qkv_specs = [], []
    for _, dil in WIN_GROUPS:
        qkv_shapes.append(jax.ShapeDtypeStruct((b, dil, s // dil, D_QKV), BF16))
        qkv_specs.append(pl.BlockSpec((1, dil, t // dil, D_QKV), lambda i, j: (i, 0, j, 0)))
    return pl.pallas_call(
        _inproj_kernel,
        out_shape=(*qkv_shapes,
                   jax.ShapeDtypeStruct((b, s, D_GQ), BF16),
                   jax.ShapeDtypeStruct((b, s, D_HB), BF16),
                   jax.ShapeDtypeStruct((b, s, D_ZG), BF16),
                   jax.ShapeDtypeStruct((b, s, D_B), F32),
                   jax.ShapeDtypeStruct((b, KV_TAIL, 2 * N_GROUPS * D_A), F32)),
        grid=(b, nt),
        in_specs=[tok(D_MODEL),
                  pl.BlockSpec((1, D_MODEL), lambda i, j: (0, 0)),
                  pl.BlockSpec((D_MODEL, D_IN), lambda i, j: (0, 0), pipeline_mode=pl.Buffered(1))],
        out_specs=(*qkv_specs, tok(D_GQ), tok(D_HB), tok(D_ZG), tok(D_B),
                   pl.BlockSpec((1, t, 2 * N_GROUPS * D_A),
                                lambda i, j: (i, jnp.maximum(j - first_tail, 0), 0))),
        scratch_shapes=[pltpu.VMEM((IN_CHUNK // LANES, t, LANES), F32)],
        compiler_params=pltpu.CompilerParams(
            dimension_semantics=("arbitrary", "arbitrary"), vmem_limit_bytes=VMEM_LIMIT),
        name="inproj",
    )(x, gain, w_bf)


def _dilated_block(q, kk, vv, first_block, masks):
    qs = _stack_heads(q * jnp.asarray(ATT_SCALE, BF16), masks)
    s = lax.dot_general(qs, kk, NT_DIMS, preferred_element_type=F32)
    qi = lax.broadcasted_iota(jnp.int32, s.shape, 0) % QBLK
    kj = lax.broadcasted_iota(jnp.int32, s.shape, 1)
    first_key = jnp.where(first_block, QBLK, 0)
    valid = (kj >= qi) & (kj <= qi + QBLK) & (kj >= first_key)
    s = jnp.where(valid, s, -jnp.inf)
    mx = jnp.max(s, axis=-1, keepdims=True)
    p = jnp.exp(s - mx)
    l = jnp.sum(p, axis=-1, keepdims=True)
    o_all = jnp.dot(p.astype(BF16), vv, preferred_element_type=F32) / l
    lse = mx + jnp.log(l)
    return (_unstack_heads(o_all, masks, QBLK),
            _expand_cols([lse[h * QBLK:(h + 1) * QBLK] for h in range(A_HEADS)], masks))


def _make_dilated_kernel(dil):
    def body(prev_ref, cur_ref, out_ref):
        first_block = pl.program_id(1) == 0
        masks = _head_masks(D_A, A_HEAD_DIM, A_HEADS)

        def one(r):
            q = cur_ref[0, r, :, 0:D_A]
            kk = jnp.concatenate([prev_ref[0, r, :, D_A:2 * D_A], cur_ref[0, r, :, D_A:2 * D_A]], axis=0)
            vv = jnp.concatenate([prev_ref[0, r, :, 2 * D_A:3 * D_A], cur_ref[0, r, :, 2 * D_A:3 * D_A]], axis=0)
            return _dilated_block(q, kk, vv, first_block, masks)

        def residue(r, carry):
            o, lse = one(r)
            both = (o, lse)
            for k in range(OG_SLABS):
                src = both[k // 2][:, (k % 2) * LANES:(k % 2 + 1) * LANES]
                if dil == 1:
                    out_ref[0, k] = src
                else:
                    out_ref[0, k, pl.ds(r, QBLK, stride=dil), :] = src
            return carry

        if dil == 1:
            residue(0, 0)
        else:
            lax.fori_loop(0, dil, residue, 0)
    return body


OG_SLABS = 2 * D_A // LANES


def _dilated_call(qkv, dil, g):
    b, _, n, _ = qkv.shape
    nb = n // QBLK
    return pl.pallas_call(
        _make_dilated_kernel(dil),
        out_shape=jax.ShapeDtypeStruct((b, OG_SLABS, n * dil, LANES), F32),
        grid=(b, nb),
        in_specs=[pl.BlockSpec((1, dil, QBLK, D_QKV), lambda i, u: (i, 0, jnp.maximum(u - 1, 0), 0)),
                  pl.BlockSpec((1, dil, QBLK, D_QKV), lambda i, u: (i, 0, u, 0))],
        out_specs=pl.BlockSpec((1, OG_SLABS, QBLK * dil, LANES), lambda i, u: (i, 0, u, 0)),
        compiler_params=pltpu.CompilerParams(
            dimension_semantics=("arbitrary", "arbitrary"), vmem_limit_bytes=VMEM_LIMIT),
        name=f"dilated_attn_g{g}",
    )(qkv, qkv)


def _lower_bound(l0, l1):
    m = jnp.maximum(l0, l1)
    e0, e1 = jnp.exp(l0 - m), jnp.exp(l1 - m)
    return e0 / (e0 + e1)


def _head_rms(ob, gain):
    parts = []
    for h in range(HG_HEADS):
        sl = slice(h * HG_DIM, (h + 1) * HG_DIM)
        parts.append(_rms(ob[:, sl], gain[:, sl]))
    return jnp.concatenate(parts, axis=1)


def _merge_out(x, ua, ub, um, zg0, zg1, zg2, wa_ref, wb_ref, wm_ref, wo_ref, nf):
    merged = (_sigmoid(zg0) * jnp.dot(ua.astype(BF16), wa_ref[...], preferred_element_type=F32)
              + _sigmoid(zg1) * jnp.dot(ub.astype(BF16), wb_ref[...], preferred_element_type=F32)
              + _sigmoid(zg2) * jnp.dot(um.astype(BF16), wm_ref[...], preferred_element_type=F32))
    y = x + jnp.dot(merged.astype(BF16), wo_ref[...], preferred_element_type=F32)
    return _rms(y, nf)


TAIL_TILE = 256


def _cumsum_rows(g, tri):
    g1 = g.astype(BF16)
    r1 = g - g1.astype(F32)
    g2 = r1.astype(BF16)
    g3 = (r1 - g2.astype(F32)).astype(BF16)
    return (jnp.dot(tri, g1, preferred_element_type=F32)
            + jnp.dot(tri, g2, preferred_element_type=F32)
            + jnp.dot(tri, g3, preferred_element_type=F32))


def _prompt_tail_kernel(x_ref, gq_ref, hb_ref, zg_ref, fb_ref, o1_ref, o2_ref, o3_ref, mkv_ref, lbl_ref,
                        nh_ref, nf_ref, wa_ref, wb_ref, wm_ref, wo_ref,
                        y_ref, hst_ref, st_ref, ob_ref):
    j = pl.program_id(1)
    t = TAIL_TILE

    @pl.when(j == 0)
    def _():
        st_ref[...] = jnp.zeros_like(st_ref)

    os_ = [jnp.concatenate([r[0, 0], r[0, 1]], axis=1) for r in (o1_ref, o2_ref, o3_ref)]
    ls_ = [jnp.concatenate([r[0, 2], r[0, 3]], axis=1) for r in (o1_ref, o2_ref, o3_ref)]
    mx = jnp.maximum(jnp.maximum(ls_[0], ls_[1]), ls_[2])
    es_ = [jnp.exp(l - mx) for l in ls_]
    oa = (es_[0] * os_[0] + es_[1] * os_[1] + es_[2] * os_[2]) / (es_[0] + es_[1] + es_[2])
    ua = oa * _silu(gq_ref[0, :, 0:D_A].astype(F32))

    masks = _head_masks(D_M, A_HEAD_DIM, A_HEADS)
    qm = gq_ref[0, :, D_A:2 * D_A] * jnp.asarray(ATT_SCALE, BF16)
    qs = _stack_heads(qm, masks)
    mk = mkv_ref[0, :, 0:D_M].astype(BF16)
    mv = mkv_ref[0, :, D_M:2 * D_M].astype(BF16)
    s = lax.dot_general(qs, mk, NT_DIMS, preferred_element_type=F32)
    p = jnp.exp(s - jnp.max(s, axis=-1, keepdims=True))
    l = jnp.sum(p, axis=-1, keepdims=True)
    om = _unstack_heads(jnp.dot(p.astype(BF16), mv, preferred_element_type=F32) / l, masks, t)
    um = om * _silu(gq_ref[0, :, 2 * D_A:3 * D_A].astype(F32))

    lb = _lower_bound(lbl_ref[0:1], lbl_ref[1:2])
    c = HG_CHUNK
    tri = (lax.broadcasted_iota(jnp.int32, (c, c), 0) >= lax.broadcasted_iota(jnp.int32, (c, c), 1)).astype(BF16)

    def chunk(ci, carry):
        r0 = pl.multiple_of(ci * c, c)
        f = lb + (1.0 - lb) * _sigmoid(fb_ref[0, pl.ds(r0, c), :])
        kk = 1.0 - f
        gcs = _cumsum_rows(jnp.log(f), tri)
        q = _silu(hb_ref[0, pl.ds(r0, c), 0:D_B].astype(F32))
        v = hb_ref[0, pl.ds(r0, c), D_B:2 * D_B]
        glast = gcs[c - 1:c]
        qe = (q * jnp.exp(gcs)).astype(BF16)
        kd = (kk * jnp.exp(glast - gcs)).astype(BF16)
        for h in range(HG_HEADS):
            sl = slice(h * HG_DIM, (h + 1) * HG_DIM)
            st = st_ref[h]
            o_inter = lax.dot_general(qe[:, sl], st.astype(BF16), NT_DIMS, preferred_element_type=F32)
            intra = []
            for i in range(c // HG_SUB):
                lo, hi = i * HG_SUB, (i + 1) * HG_SUB
                ref = gcs[lo - 1:lo, sl] if i > 0 else jnp.zeros((1, HG_DIM), F32)
                qt = (q[lo:hi, sl] * jnp.exp(gcs[lo:hi, sl] - ref)).astype(BF16)
                kt = (kk[0:hi, sl] * jnp.exp(ref - gcs[0:hi, sl])).astype(BF16)
                att = lax.dot_general(qt, kt, NT_DIMS, preferred_element_type=F32)
                row = lax.broadcasted_iota(jnp.int32, att.shape, 0) + lo
                col = lax.broadcasted_iota(jnp.int32, att.shape, 1)
                att = jnp.where(col <= row, att, 0.0)
                intra.append(jnp.dot(att.astype(BF16), v[0:hi, sl], preferred_element_type=F32))
            ob_ref[pl.ds(r0, c), sl] = o_inter + jnp.concatenate(intra, axis=0)
            st_ref[h] = st * jnp.exp(glast[:, sl]) + lax.dot_general(
                v[:, sl], kd[:, sl], TN_DIMS, preferred_element_type=F32)
        return carry

    lax.fori_loop(0, t // c, chunk, 0)
    ub = _head_rms(ob_ref[...], nh_ref[...]) * _silu(hb_ref[0, :, 2 * D_B:3 * D_B].astype(F32))

    y_ref[0] = _merge_out(x_ref[0], ua, ub, um, zg_ref[0, :, 0:D_MODEL].astype(F32),
                          zg_ref[0, :, D_MODEL:2 * D_MODEL].astype(F32),
                          zg_ref[0, :, 2 * D_MODEL:3 * D_MODEL].astype(F32),
                          wa_ref, wb_ref, wm_ref, wo_ref, nf_ref[...])

    @pl.when(j == pl.num_programs(1) - 1)
    def _():
        for h in range(HG_HEADS):
            hst_ref[0, h] = st_ref[h].T


def _prompt_tail_call(x, gq, hb, zg, fb, o1, o2, o3, mkv, lbl, nh, nf, wa, wb, wm, wo):
    b, s, _ = x.shape
    t = TAIL_TILE

    def tok(width):
        return pl.BlockSpec((1, t, width), lambda i, j: (i, j, 0))

    def const(shape):
        return pl.BlockSpec(shape, lambda i, j: (0,) * len(shape))

    return pl.pallas_call(
        _prompt_tail_kernel,
        out_shape=(jax.ShapeDtypeStruct((b, s, D_MODEL), F32),
                   jax.ShapeDtypeStruct((b, HG_HEADS, HG_DIM, HG_DIM), F32)),
        grid=(b, s // t),
        in_specs=[tok(D_MODEL), tok(D_GQ), tok(D_HB), tok(D_ZG), tok(D_B),
                  *[pl.BlockSpec((1, OG_SLABS, t, LANES), lambda i, j: (i, 0, j, 0))] * N_GROUPS,
                  pl.BlockSpec((1, MEM_LEN, 2 * D_M), lambda i, j: (i, 0, 0)),
                  const((2, D_B)), const((1, D_B)), const((1, D_MODEL)),
                  const((D_A, D_MODEL)), const((D_B, D_MODEL)), const((D_M, D_MODEL)),
                  const((D_MODEL, D_MODEL))],
        out_specs=(tok(D_MODEL),
                   pl.BlockSpec((1, HG_HEADS, HG_DIM, HG_DIM), lambda i, j: (i, 0, 0, 0))),
        scratch_shapes=[pltpu.VMEM((HG_HEADS, HG_DIM, HG_DIM), F32),
                        pltpu.VMEM((t, D_B), F32)],
        compiler_params=pltpu.CompilerParams(
            dimension_semantics=("arbitrary", "arbitrary"), vmem_limit_bytes=VMEM_LIMIT),
        name="prompt_tail",
    )(x, gq, hb, zg, fb, o1, o2, o3, mkv, lbl, nh, nf, wa, wb, wm, wo)


S_COLS = 1024


def _sample_inproj_kernel(x_ref, gain_ref, w_ref, z_ref):
    h = _rms(x_ref[...], gain_ref[...]).astype(BF16)
    z_ref[...] = jnp.dot(h, w_ref[...], preferred_element_type=F32)


def _sample_inproj_call(x, gain, w_bf):
    db = x.shape[0]
    return pl.pallas_call(
        _sample_inproj_kernel,
        out_shape=jax.ShapeDtypeStruct((db, D_IN), F32),
        grid=(D_IN // S_COLS,),
        in_specs=[pl.BlockSpec((db, D_MODEL), lambda c: (0, 0)),
                  pl.BlockSpec((1, D_MODEL), lambda c: (0, 0)),
                  pl.BlockSpec((D_MODEL, S_COLS), lambda c: (0, c))],
        out_specs=pl.BlockSpec((db, S_COLS), lambda c: (0, c)),
        name="sample_inproj",
    )(x, gain, w_bf)


S_TILE = 2
Z_ROWS = D_IN // LANES
U_ROWS = 8


def _sample_mix_kernel(zr_ref, c1_ref, c2_ref, c3_ref, cm_ref, st_ref, lblt_ref, nh_ref,
                       u_ref, nst_ref):
    bt = S_TILE
    zt = jnp.concatenate([zr_ref[b] for b in range(bt)], axis=0).T

    def col(b, off, n=HG_DIM):
        j, l = divmod(off, LANES)
        return zt[l:l + n, Z_ROWS * b + j:Z_ROWS * b + j + 1]

    lbt = lblt_ref[...]
    lb_col = _lower_bound(lbt[:, 0:1], lbt[:, 1:2])
    nh = nh_ref[...]

    for b in range(bt):
        branch_cols = []
        for srcs in (((c1_ref, 1, 0), (c2_ref, 4, 1), (c3_ref, 16, 2)), ((cm_ref, 1, None),)):
            head_cols = []
            for h in range(A_HEADS):
                scores, extra = [], []
                for c_ref, dil, g in srcs:
                    q_off = (OFF_QM if g is None else OFF_QA + g * D_A) + h * A_HEAD_DIM
                    qc = col(b, q_off, A_HEAD_DIM) * ATT_SCALE
                    kt = c_ref[b, h * A_HEAD_DIM:(h + 1) * A_HEAD_DIM, :]
                    sc = jnp.sum(kt * qc, axis=0, keepdims=True)
                    if dil > 1:
                        lane = lax.broadcasted_iota(jnp.int32, sc.shape, 1)
                        sc = jnp.where(lane % dil == 0, sc, -jnp.inf)
                    scores.append(sc)
                    if g is not None:
                        kn = col(b, OFF_KA + g * D_A + h * A_HEAD_DIM, A_HEAD_DIM)
                        extra.append(jnp.sum(kn * qc, axis=0, keepdims=True))
                mx = None
                for sc in scores + extra:
                    m1 = jnp.max(sc, axis=1, keepdims=True)
                    mx = m1 if mx is None else jnp.maximum(mx, m1)
                den = jnp.zeros((1, 1), F32)
                num = jnp.zeros((A_HEAD_DIM, 1), F32)
                for (c_ref, dil, g), sc in zip(srcs, scores):
                    p = jnp.exp(sc - mx)
                    vt = c_ref[b, D_A + h * A_HEAD_DIM:D_A + (h + 1) * A_HEAD_DIM, :]
                    den = den + jnp.sum(p, axis=1, keepdims=True)
                    num = num + jnp.sum(vt * p, axis=1, keepdims=True)
                for (c_ref, dil, g), sn in zip(srcs, extra):
                    pn = jnp.exp(sn - mx)
                    den = den + pn
                    num = num + pn * col(b, OFF_VA + g * D_A + h * A_HEAD_DIM, A_HEAD_DIM)
                head_cols.append(num / den)
            gate_off = OFF_GA if srcs[0][2] is not None else OFF_GM
            for half in range(D_A // LANES):
                o = jnp.concatenate(head_cols[2 * half:2 * half + 2], axis=0)
                branch_cols.append(o * _silu(col(b, gate_off + half * LANES)))

        lane = lax.broadcasted_iota(jnp.int32, (LANES, LANES), 1)
        ut = jnp.zeros((LANES, LANES), F32)
        for k, cvec in enumerate(branch_cols):
            ut = ut + jnp.where(lane == k, cvec, 0.0)
        u_ref[b, 0:4, :] = ut.T[0:4, :]

        for h in range(HG_HEADS):
            sl = slice(h * HG_DIM, (h + 1) * HG_DIM)
            fcol = lb_col[sl] + (1.0 - lb_col[sl]) * _sigmoid(col(b, OFF_FB + h * HG_DIM))
            qcol = _silu(col(b, OFF_QB + h * HG_DIM))
            vrow = zr_ref[b, (OFF_IB + h * HG_DIM) // LANES:(OFF_IB + h * HG_DIM) // LANES + 1, :]
            grow = zr_ref[b, (OFF_GB + h * HG_DIM) // LANES:(OFF_GB + h * HG_DIM) // LANES + 1, :]
            sn = fcol * st_ref[b, sl, :] + (1.0 - fcol) * vrow
            nst_ref[b, sl, :] = sn
            orow = jnp.sum(sn * qcol, axis=0, keepdims=True)
            u_ref[b, 4 + h:5 + h, :] = _rms(orow, nh[:, sl]) * _silu(grow)


def _stored_view(cache):
    db, length = cache.shape[0], cache.shape[1]
    return jnp.transpose(cache, (0, 2, 3, 4, 1)).reshape(db, 2 * D_A, length)


def _sample_mix_call(z, c1, c2, c3, cm, st, lblt, nh):
    db = z.shape[0]
    bt = S_TILE
    zr = z.reshape(db, Z_ROWS, LANES)
    caches = [_stored_view(c) for c in (c1, c2, c3, cm)]
    stv = st.reshape(db, HG_HEADS * HG_DIM, HG_DIM)

    def rows(a):
        return pl.BlockSpec((bt,) + a.shape[1:], lambda i: (i, 0, 0))

    u, nst = pl.pallas_call(
        _sample_mix_kernel,
        out_shape=(jax.ShapeDtypeStruct((db, U_ROWS, LANES), F32),
                   jax.ShapeDtypeStruct(stv.shape, F32)),
        grid=(db // bt,),
        in_specs=[rows(zr), *[rows(c) for c in caches], rows(stv),
                  pl.BlockSpec((D_B, 2), lambda i: (0, 0)),
                  pl.BlockSpec((1, D_B), lambda i: (0, 0))],
        out_specs=(pl.BlockSpec((bt, U_ROWS, LANES), lambda i: (i, 0, 0)), rows(stv)),
        compiler_params=pltpu.CompilerParams(
            dimension_semantics=("arbitrary",), vmem_limit_bytes=VMEM_LIMIT),
        name="sample_mix",
    )(zr, *caches, stv, lblt, nh)
    return u.reshape(db, U_ROWS * LANES), nst


def _sample_out_kernel(x_ref, u_ref, zg_ref, nf_ref, wa_ref, wb_ref, wm_ref, wo_ref, y_ref):
    u = u_ref[...]
    zg = zg_ref[...]
    y_ref[...] = _merge_out(x_ref[...], u[:, 0:D_A], u[:, D_A + D_M:], u[:, D_A:D_A + D_M],
                            zg[:, 0:D_MODEL], zg[:, D_MODEL:2 * D_MODEL], zg[:, 2 * D_MODEL:],
                            wa_ref, wb_ref, wm_ref, wo_ref, nf_ref[...])


def _sample_out_call(x, u, zg, nf, wa, wb, wm, wo):
    db = x.shape[0]

    def full(a):
        return pl.BlockSpec(a.shape, lambda i: (0,) * a.ndim)

    args = (x, u, zg, nf, wa, wb, wm, wo)
    return pl.pallas_call(
        _sample_out_kernel,
        out_shape=jax.ShapeDtypeStruct((db, D_MODEL), F32),
        grid=(1,),
        in_specs=[full(a) for a in args],
        out_specs=pl.BlockSpec((db, D_MODEL), lambda i: (0, 0)),
        compiler_params=pltpu.CompilerParams(vmem_limit_bytes=VMEM_LIMIT),
        name="sample_out",
    )(*args)


def _cache_rows(kv, length):
    outs = []
    for g in range(N_GROUPS):
        k = kv[:, -length[g]:, g * D_A:(g + 1) * D_A]
        v = kv[:, -length[g]:, (N_GROUPS + g) * D_A:(N_GROUPS + g + 1) * D_A]
        outs.append(jnp.stack([k, v], axis=2).reshape(kv.shape[0], length[g], 2, A_HEADS, A_HEAD_DIM)[None])
    return outs


def kernel(x_prompt, x_sample, mem_prompt, cache_w1_kv, cache_w2_kv, cache_w3_kv, cache_mem_kv, state_hgrn,
           norm_in, w_in, lb_logits, norm_hgrn, norm_mem, w_mem_kv, w_branch_a, w_branch_b, w_branch_m,
           w_out, norm_final):
    b, s, _ = x_prompt.shape
    db = x_sample.shape[0]
    w_in_bf = w_in[0].astype(BF16)
    wa, wb, wm, wo = (w[0].astype(BF16) for w in (w_branch_a, w_branch_b, w_branch_m, w_out))
    gain_in = norm_in[0][None]
    nh = norm_hgrn[0][None]
    nf = norm_final[None]
    lbl = lb_logits.astype(F32)

    mkv = _mem_kv_call(mem_prompt, norm_mem[0][None], w_mem_kv[0].astype(BF16))
    qkv0, qkv1, qkv2, gq, hb, zg, fb, kv_tail = _inproj_call(x_prompt, gain_in, w_in_bf)
    og = [_dilated_call(qkv, dil, g) for g, (qkv, (_, dil)) in enumerate(zip((qkv0, qkv1, qkv2), WIN_GROUPS))]
    y_prompt, hg_prompt = _prompt_tail_call(x_prompt, gq, hb, zg, fb, og[0], og[1], og[2], mkv, lbl, nh, nf,
                                            wa, wb, wm, wo)
    pw = _cache_rows(kv_tail, [min(w, s) for w, _ in WIN_GROUPS])
    new_mem = mkv.reshape(1, b, MEM_LEN, 2, A_HEADS, A_HEAD_DIM)

    xs = x_sample.reshape(db, D_MODEL)
    zs = _sample_inproj_call(xs, gain_in, w_in_bf)
    u, nst = _sample_mix_call(zs, cache_w1_kv[0], cache_w2_kv[0], cache_w3_kv[0], cache_mem_kv[0],
                              state_hgrn[0], lbl.T, nh)
    y_sample = _sample_out_call(xs, u, zs[:, OFF_ZG:], nf, wa, wb, wm, wo).reshape(db, 1, D_MODEL)
    sw = _cache_rows(zs[:, None, OFF_KA:OFF_GA], [1, 1, 1])
    new_hg_sample = nst.reshape(1, db, HG_HEADS, HG_DIM, HG_DIM)

    return (y_prompt, y_sample, pw[0], pw[1], pw[2], new_mem, hg_prompt[None],
            sw[0], sw[1], sw[2], new_hg_sample)
```

```python
import jax
import jax.numpy as jnp
from jax import lax
from jax.experimental import pallas as pl
from jax.experimental.pallas import tpu as pltpu

F32 = jnp.float32
BF16 = jnp.bfloat16

D_MODEL = 1024
WIN_GROUPS = ((128, 1), (512, 4), (2048, 16))
N_GROUPS = 3
A_HEADS = 4
A_HEAD_DIM = 64
D_A = 256
QBLK = 128
HG_HEADS = 4
HG_DIM = 128
D_B = 512
HG_CHUNK = 64
HG_SUB = 16
MEM_LEN = 256
D_M = 256
EPS = 1e-6
D_IN = 8192
LANES = 128
OFF_QA, OFF_KA, OFF_VA, OFF_GA = 0, 768, 1536, 2304
OFF_QB, OFF_FB, OFF_IB, OFF_GB = 2560, 3072, 3584, 4096
OFF_QM, OFF_GM, OFF_ZG = 4608, 4864, 5120
ATT_SCALE = A_HEAD_DIM ** -0.5

VMEM_LIMIT = 56 * 1024 * 1024

NT_DIMS = (((1,), (1,)), ((), ()))
TN_DIMS = (((0,), (0,)), ((), ()))


def _sigmoid(x):
    return 1.0 / (1.0 + jnp.exp(-x))


def _silu(x):
    return x * _sigmoid(x)


def _rms(x, gain):
    return x * lax.rsqrt(jnp.mean(x * x, axis=-1, keepdims=True) + EPS) * gain


def _head_masks(width, head_dim, n_heads):
    lane = lax.broadcasted_iota(jnp.int32, (1, width), 1)
    return [(lane >= h * head_dim) & (lane < (h + 1) * head_dim) for h in range(n_heads)]


def _stack_heads(q, masks):
    zero = jnp.zeros_like(q)
    return jnp.concatenate([jnp.where(m, q, zero) for m in masks], axis=0)


def _unstack_heads(o_all, masks, t):
    out = jnp.where(masks[0], o_all[0:t], 0.0)
    for h in range(1, len(masks)):
        out = out + jnp.where(masks[h], o_all[h * t:(h + 1) * t], 0.0)
    return out


def _expand_cols(cols, masks):
    out = jnp.where(masks[0], cols[0], 0.0)
    for h in range(1, len(masks)):
        out = out + jnp.where(masks[h], cols[h], 0.0)
    return out


def _mem_kv_kernel(mem_ref, gain_ref, w_ref, out_ref):
    hm = _rms(mem_ref[0], gain_ref[...]).astype(BF16)
    out_ref[0] = jnp.dot(hm, w_ref[...], preferred_element_type=F32)


def _mem_kv_call(mem, gain, w_bf):
    b = mem.shape[0]
    return pl.pallas_call(
        _mem_kv_kernel,
        out_shape=jax.ShapeDtypeStruct((b, MEM_LEN, 2 * D_M), F32),
        grid=(b,),
        in_specs=[pl.BlockSpec((1, MEM_LEN, D_MODEL), lambda i: (i, 0, 0)),
                  pl.BlockSpec((1, D_MODEL), lambda i: (0, 0)),
                  pl.BlockSpec((D_MODEL, 2 * D_M), lambda i: (0, 0))],
        out_specs=pl.BlockSpec((1, MEM_LEN, 2 * D_M), lambda i: (i, 0, 0)),
        name="mem_kv",
    )(mem, gain, w_bf)


IN_TILE = 512
IN_CHUNK = 256
KV_TAIL = 2048
D_QKV = 3 * D_A
D_GQ = 3 * D_A
D_HB = 3 * D_B
D_ZG = 3 * D_MODEL


def _inproj_kernel(x_ref, gain_ref, w_ref, qkv0_ref, qkv1_ref, qkv2_ref, gq_ref, hb_ref, zg_ref, fb_ref,
                   kv_ref, scr_ref):
    qkv_refs = (qkv0_ref, qkv1_ref, qkv2_ref)
    h = _rms(x_ref[0], gain_ref[...]).astype(BF16)
    t = IN_TILE
    for c in range(D_IN // IN_CHUNK):
        lo = c * IN_CHUNK
        zc = jnp.dot(h, w_ref[:, lo:lo + IN_CHUNK], preferred_element_type=F32)
        if lo < OFF_GA:
            sec, g = divmod(c, N_GROUPS)
            dil = WIN_GROUPS[g][1]
            dst = slice(sec * D_A, (sec + 1) * D_A)
            if dil == 1:
                qkv_refs[g][0, 0, :, dst] = zc.astype(BF16)
            else:
                for k in range(IN_CHUNK // LANES):
                    scr_ref[k] = zc[:, k * LANES:(k + 1) * LANES]
                for r in range(dil):
                    qkv_refs[g][0, r, :, dst] = jnp.concatenate(
                        [scr_ref[k, pl.ds(r, t // dil, stride=dil), :] for k in range(IN_CHUNK // LANES)],
                        axis=1).astype(BF16)
            if sec > 0:
                kv_ref[0, :, lo - OFF_KA:lo - OFF_KA + IN_CHUNK] = zc
        elif lo < OFF_QB:
            gq_ref[0, :, 0:D_A] = zc.astype(BF16)
        elif lo < OFF_FB:
            hb_ref[0, :, lo - OFF_QB:lo - OFF_QB + IN_CHUNK] = zc.astype(BF16)
        elif lo < OFF_IB:
            fb_ref[0, :, lo - OFF_FB:lo - OFF_FB + IN_CHUNK] = zc
        elif lo < OFF_QM:
            hb_ref[0, :, lo - OFF_IB + D_B:lo - OFF_IB + D_B + IN_CHUNK] = zc.astype(BF16)
        elif lo < OFF_ZG:
            gq_ref[0, :, lo - OFF_QM + D_A:lo - OFF_QM + D_A + IN_CHUNK] = zc.astype(BF16)
        else:
            zg_ref[0, :, lo - OFF_ZG:lo - OFF_ZG + IN_CHUNK] = zc.astype(BF16)


def _inproj_call(x, gain, w_bf):
    b, s, _ = x.shape
    t = IN_TILE
    nt = s // t
    first_tail = (s - KV_TAIL) // t

    def tok(width):
        return pl.BlockSpec((1, t, width), lambda i, j: (i, j, 0))

    qkv_shapes, qkv_specs = [], []
    for _, dil in WIN_GROUPS:
        qkv_shapes.append(jax.ShapeDtypeStruct((b, dil, s // dil, D_QKV), BF16))
        qkv_specs.append(pl.BlockSpec((1, dil, t // dil, D_QKV), lambda i, j: (i, 0, j, 0)))
    return pl.pallas_call(
        _inproj_kernel,
        out_shape=(*qkv_shapes,
                   jax.ShapeDtypeStruct((b, s, D_GQ), BF16),
                   jax.ShapeDtypeStruct((b, s, D_HB), BF16),
                   jax.ShapeDtypeStruct((b, s, D_ZG), BF16),
                   jax.ShapeDtypeStruct((b, s, D_B), F32),
                   jax.ShapeDtypeStruct((b, KV_TAIL, 2 * N_GROUPS * D_A), F32)),
        grid=(b, nt),
        in_specs=[tok(D_MODEL),
                  pl.BlockSpec((1, D_MODEL), lambda i, j: (0, 0)),
                  pl.BlockSpec((D_MODEL, D_IN), lambda i, j: (0, 0), pipeline_mode=pl.Buffered(1))],
        out_specs=(*qkv_specs, tok(D_GQ), tok(D_HB), tok(D_ZG), tok(D_B),
                   pl.BlockSpec((1, t, 2 * N_GROUPS * D_A),
                                lambda i, j: (i, jnp.maximum(j - first_tail, 0), 0))),
        scratch_shapes=[pltpu.VMEM((IN_CHUNK // LANES, t, LANES), F32)],
        compiler_params=pltpu.CompilerParams(
            dimension_semantics=("arbitrary", "arbitrary"), vmem_limit_bytes=VMEM_LIMIT),
        name="inproj",
    )(x, gain, w_bf)


def _dilated_block(q, kk, vv, first_block, masks):
    qs = _stack_heads(q * jnp.asarray(ATT_SCALE, BF16), masks)
    s = lax.dot_general(qs, kk, NT_DIMS, preferred_element_type=F32)
    qi = lax.broadcasted_iota(jnp.int32, s.shape, 0) % QBLK
    kj = lax.broadcasted_iota(jnp.int32, s.shape, 1)
    first_key = jnp.where(first_block, QBLK, 0)
    valid = (kj >= qi) & (kj <= qi + QBLK) & (kj >= first_key)
    s = jnp.where(valid, s, -jnp.inf)
    mx = jnp.max(s, axis=-1, keepdims=True)
    p = jnp.exp(s - mx)
    l = jnp.sum(p, axis=-1, keepdims=True)
    o_all = jnp.dot(p.astype(BF16), vv, preferred_element_type=F32) / l
    lse = mx + jnp.log(l)
    return (_unstack_heads(o_all, masks, QBLK),
            _expand_cols([lse[h * QBLK:(h + 1) * QBLK] for h in range(A_HEADS)], masks))


def _make_dilated_kernel(dil):
    def body(prev_ref, cur_ref, out_ref):
        first_block = pl.program_id(1) == 0
        masks = _head_masks(D_A, A_HEAD_DIM, A_HEADS)

        def one(r):
            q = cur_ref[0, r, :, 0:D_A]
            kk = jnp.concatenate([prev_ref[0, r, :, D_A:2 * D_A], cur_ref[0, r, :, D_A:2 * D_A]], axis=0)
            vv = jnp.concatenate([prev_ref[0, r, :, 2 * D_A:3 * D_A], cur_ref[0, r, :, 2 * D_A:3 * D_A]], axis=0)
            return _dilated_block(q, kk, vv, first_block, masks)

        def residue(r, carry):
            o, lse = one(r)
            both = (o, lse)
            for k in range(OG_SLABS):
                src = both[k // 2][:, (k % 2) * LANES:(k % 2 + 1) * LANES]
                if dil == 1:
                    out_ref[0, k] = src
                else:
                    out_ref[0, k, pl.ds(r, QBLK, stride=dil), :] = src
            return carry

        if dil == 1:
            residue(0, 0)
        else:
            lax.fori_loop(0, dil, residue, 0)
    return body


OG_SLABS = 2 * D_A // LANES


def _dilated_call(qkv, dil, g):
    b, _, n, _ = qkv.shape
    nb = n // QBLK
    return pl.pallas_call(
        _make_dilated_kernel(dil),
        out_shape=jax.ShapeDtypeStruct((b, OG_SLABS, n * dil, LANES), F32),
        grid=(b, nb),
        in_specs=[pl.BlockSpec((1, dil, QBLK, D_QKV), lambda i, u: (i, 0, jnp.maximum(u - 1, 0), 0)),
                  pl.BlockSpec((1, dil, QBLK, D_QKV), lambda i, u: (i, 0, u, 0))],
        out_specs=pl.BlockSpec((1, OG_SLABS, QBLK * dil, LANES), lambda i, u: (i, 0, u, 0)),
        compiler_params=pltpu.CompilerParams(
            dimension_semantics=("arbitrary", "arbitrary"), vmem_limit_bytes=VMEM_LIMIT),
        name=f"dilated_attn_g{g}",
    )(qkv, qkv)


def _lower_bound(l0, l1):
    m = jnp.maximum(l0, l1)
    e0, e1 = jnp.exp(l0 - m), jnp.exp(l1 - m)
    return e0 / (e0 + e1)


def _head_rms(ob, gain):
    parts = []
    for h in range(HG_HEADS):
        sl = slice(h * HG_DIM, (h + 1) * HG_DIM)
        parts.append(_rms(ob[:, sl], gain[:, sl]))
    return jnp.concatenate(parts, axis=1)


def _merge_out(x, ua, ub, um, zg0, zg1, zg2, wa_ref, wb_ref, wm_ref, wo_ref, nf):
    merged = (_sigmoid(zg0) * jnp.dot(ua.astype(BF16), wa_ref[...], preferred_element_type=F32)
              + _sigmoid(zg1) * jnp.dot(ub.astype(BF16), wb_ref[...], preferred_element_type=F32)
              + _sigmoid(zg2) * jnp.dot(um.astype(BF16), wm_ref[...], preferred_element_type=F32))
    y = x + jnp.dot(merged.astype(BF16), wo_ref[...], preferred_element_type=F32)
    return _rms(y, nf)


TAIL_TILE = 256


def _cumsum_rows(g, tri):
    g1 = g.astype(BF16)
    r1 = g - g1.astype(F32)
    g2 = r1.astype(BF16)
    g3 = (r1 - g2.astype(F32)).astype(BF16)
    return (jnp.dot(tri, g1, preferred_element_type=F32)
            + jnp.dot(tri, g2, preferred_element_type=F32)
            + jnp.dot(tri, g3, preferred_element_type=F32))


def _hgrn_tile(fb, qb, v, lb, st_ref):
    t = fb.shape[0]
    c, sub = HG_CHUNK, HG_SUB
    n_sub = c // sub
    f = lb + (1.0 - lb) * _sigmoid(fb)
    kk = 1.0 - f
    row = lax.broadcasted_iota(jnp.int32, (t, t), 0)
    col = lax.broadcasted_iota(jnp.int32, (t, t), 1)
    same_chunk = (row // c) == (col // c)
    gcs = _cumsum_rows(jnp.log(f), (same_chunk & (col <= row)).astype(BF16))
    q = _silu(qb.astype(F32))

    loc = lax.broadcasted_iota(jnp.int32, (c, 1), 0)
    qt_c, kt_ci, qe_c, kd_c, dec_c = [], [[] for _ in range(n_sub)], [], [], []
    for ci in range(t // c):
        rows = slice(ci * c, (ci + 1) * c)
        g_c, q_c, k_c = gcs[rows], q[rows], kk[rows]
        refs = [jnp.zeros((1, D_B), F32)] + [g_c[i * sub - 1:i * sub] for i in range(1, n_sub)]
        own = refs[n_sub - 1]
        for i in range(n_sub - 2, -1, -1):
            own = jnp.where(loc < (i + 1) * sub, refs[i], own)
        qt_c.append((q_c * jnp.exp(g_c - own)).astype(BF16))
        for i in range(n_sub):
            seen = loc < (i + 1) * sub
            kt_ci[i].append(jnp.where(seen, k_c * jnp.exp(jnp.where(seen, refs[i] - g_c, 0.0)), 0.0).astype(BF16))
        g_last = g_c[c - 1:c]
        qe_c.append((q_c * jnp.exp(g_c)).astype(BF16))
        kd_c.append((k_c * jnp.exp(g_last - g_c)).astype(BF16))
        dec_c.append(jnp.exp(g_last))
    qt = jnp.concatenate(qt_c, axis=0)
    kts = [jnp.concatenate(parts, axis=0) for parts in kt_ci]
    sub_of_row = (lax.broadcasted_iota(jnp.int32, (t, 1), 0) % c) // sub
    causal = same_chunk & (col <= row)

    outs = []
    for h in range(HG_HEADS):
        sl = slice(h * HG_DIM, (h + 1) * HG_DIM)
        zero = jnp.zeros((t, HG_DIM), BF16)
        q_big = jnp.concatenate([jnp.where(sub_of_row == i, qt[:, sl], zero) for i in range(n_sub)], axis=1)
        k_big = jnp.concatenate([kts[i][:, sl] for i in range(n_sub)], axis=1)
        att = lax.dot_general(q_big, k_big, NT_DIMS, preferred_element_type=F32)
        att = jnp.where(causal, att, 0.0).astype(BF16)
        o_intra = jnp.dot(att, v[:, sl], preferred_element_type=F32)
        st = st_ref[h]
        inter = []
        for ci in range(t // c):
            rows = slice(ci * c, (ci + 1) * c)
            inter.append(lax.dot_general(qe_c[ci][:, sl], st.astype(BF16), NT_DIMS, preferred_element_type=F32))
            st = st * dec_c[ci][:, sl] + lax.dot_general(
                v[rows, sl], kd_c[ci][:, sl], TN_DIMS, preferred_element_type=F32)
        st_ref[h] = st
        outs.append(o_intra + jnp.concatenate(inter, axis=0))
    return jnp.concatenate(outs, axis=1)


def _prompt_tail_kernel(x_ref, gq_ref, hb_ref, zg_ref, fb_ref, o1_ref, o2_ref, o3_ref, mkv_ref, lbl_ref,
                        nh_ref, nf_ref, wa_ref, wb_ref, wm_ref, wo_ref,
                        y_ref, hst_ref, st_ref):
    j = pl.program_id(1)
    t = TAIL_TILE

    @pl.when(j == 0)
    def _():
        st_ref[...] = jnp.zeros_like(st_ref)

    os_ = [jnp.concatenate([r[0, 0], r[0, 1]], axis=1) for r in (o1_ref, o2_ref, o3_ref)]
    ls_ = [jnp.concatenate([r[0, 2], r[0, 3]], axis=1) for r in (o1_ref, o2_ref, o3_ref)]
    mx = jnp.maximum(jnp.maximum(ls_[0], ls_[1]), ls_[2])
    es_ = [jnp.exp(l - mx) for l in ls_]
    oa = (es_[0] * os_[0] + es_[1] * os_[1] + es_[2] * os_[2]) / (es_[0] + es_[1] + es_[2])
    ua = oa * _silu(gq_ref[0, :, 0:D_A].astype(F32))

    masks = _head_masks(D_M, A_HEAD_DIM, A_HEADS)
    qm = gq_ref[0, :, D_A:2 * D_A] * jnp.asarray(ATT_SCALE, BF16)
    qs = _stack_heads(qm, masks)
    mk = mkv_ref[0, :, 0:D_M].astype(BF16)
    mv = mkv_ref[0, :, D_M:2 * D_M].astype(BF16)
    s = lax.dot_general(qs, mk, NT_DIMS, preferred_element_type=F32)
    p = jnp.exp(s - jnp.max(s, axis=-1, keepdims=True))
    l = jnp.sum(p, axis=-1, keepdims=True)
    om = _unstack_heads(jnp.dot(p.astype(BF16), mv, preferred_element_type=F32) / l, masks, t)
    um = om * _silu(gq_ref[0, :, 2 * D_A:3 * D_A].astype(F32))

    lb = _lower_bound(lbl_ref[0:1], lbl_ref[1:2])
    ob = _hgrn_tile(fb_ref[0], hb_ref[0, :, 0:D_B], hb_ref[0, :, D_B:2 * D_B], lb, st_ref)
    ub = _head_rms(ob, nh_ref[...]) * _silu(hb_ref[0, :, 2 * D_B:3 * D_B].astype(F32))

    y_ref[0] = _merge_out(x_ref[0], ua, ub, um, zg_ref[0, :, 0:D_MODEL].astype(F32),
                          zg_ref[0, :, D_MODEL:2 * D_MODEL].astype(F32),
                          zg_ref[0, :, 2 * D_MODEL:3 * D_MODEL].astype(F32),
                          wa_ref, wb_ref, wm_ref, wo_ref, nf_ref[...])

    @pl.when(j == pl.num_programs(1) - 1)
    def _():
        for h in range(HG_HEADS):
            hst_ref[0, h] = st_ref[h].T


def _prompt_tail_call(x, gq, hb, zg, fb, o1, o2, o3, mkv, lbl, nh, nf, wa, wb, wm, wo):
    b, s, _ = x.shape
    t = TAIL_TILE

    def tok(width):
        return pl.BlockSpec((1, t, width), lambda i, j: (i, j, 0))

    def const(shape):
        return pl.BlockSpec(shape, lambda i, j: (0,) * len(shape))

    return pl.pallas_call(
        _prompt_tail_kernel,
        out_shape=(jax.ShapeDtypeStruct((b, s, D_MODEL), F32),
                   jax.ShapeDtypeStruct((b, HG_HEADS, HG_DIM, HG_DIM), F32)),
        grid=(b, s // t),
        in_specs=[tok(D_MODEL), tok(D_GQ), tok(D_HB), tok(D_ZG), tok(D_B),
                  *[pl.BlockSpec((1, OG_SLABS, t, LANES), lambda i, j: (i, 0, j, 0))] * N_GROUPS,
                  pl.BlockSpec((1, MEM_LEN, 2 * D_M), lambda i, j: (i, 0, 0)),
                  const((2, D_B)), const((1, D_B)), const((1, D_MODEL)),
                  const((D_A, D_MODEL)), const((D_B, D_MODEL)), const((D_M, D_MODEL)),
                  const((D_MODEL, D_MODEL))],
        out_specs=(tok(D_MODEL),
                   pl.BlockSpec((1, HG_HEADS, HG_DIM, HG_DIM), lambda i, j: (i, 0, 0, 0))),
        scratch_shapes=[pltpu.VMEM((HG_HEADS, HG_DIM, HG_DIM), F32)],
        compiler_params=pltpu.CompilerParams(
            dimension_semantics=("arbitrary", "arbitrary"), vmem_limit_bytes=VMEM_LIMIT),
        name="prompt_tail",
    )(x, gq, hb, zg, fb, o1, o2, o3, mkv, lbl, nh, nf, wa, wb, wm, wo)


S_COLS = 1024


def _sample_inproj_kernel(x_ref, gain_ref, w_ref, z_ref):
    h = _rms(x_ref[...], gain_ref[...]).astype(BF16)
    z_ref[...] = jnp.dot(h, w_ref[...], preferred_element_type=F32)


def _sample_inproj_call(x, gain, w_bf):
    db = x.shape[0]
    return pl.pallas_call(
        _sample_inproj_kernel,
        out_shape=jax.ShapeDtypeStruct((db, D_IN), F32),
        grid=(D_IN // S_COLS,),
        in_specs=[pl.BlockSpec((db, D_MODEL), lambda c: (0, 0)),
                  pl.BlockSpec((1, D_MODEL), lambda c: (0, 0)),
                  pl.BlockSpec((D_MODEL, S_COLS), lambda c: (0, c))],
        out_specs=pl.BlockSpec((db, S_COLS), lambda c: (0, c)),
        name="sample_inproj",
    )(x, gain, w_bf)


S_TILE = 2
Z_ROWS = D_IN // LANES
U_ROWS = 8


def _sample_mix_kernel(zr_ref, c1_ref, c2_ref, c3_ref, cm_ref, st_ref, lblt_ref, nh_ref,
                       u_ref, nst_ref):
    bt = S_TILE
    zt = jnp.concatenate([zr_ref[b] for b in range(bt)], axis=0).T

    def col(b, off, n=HG_DIM):
        j, l = divmod(off, LANES)
        return zt[l:l + n, Z_ROWS * b + j:Z_ROWS * b + j + 1]

    lbt = lblt_ref[...]
    lb_col = _lower_bound(lbt[:, 0:1], lbt[:, 1:2])
    nh = nh_ref[...]

    def col2(b, off):
        return jnp.concatenate([col(b, off), col(b, off + LANES)], axis=0)

    def per_head(x):
        return jnp.concatenate([jnp.sum(x[h * A_HEAD_DIM:(h + 1) * A_HEAD_DIM], axis=0, keepdims=True)
                                for h in range(A_HEADS)], axis=0)

    def spread(x):
        return jnp.concatenate([jnp.broadcast_to(x[h:h + 1], (A_HEAD_DIM, 1)) for h in range(A_HEADS)], axis=0)

    def attend(b, srcs):
        parts, new_scores, qs = [], [], []
        for c_ref, dil, q_off, k_off, _ in srcs:
            q = col2(b, q_off) * ATT_SCALE
            qs.append(q)
            s = per_head(c_ref[b, 0:D_A, :] * q)
            if dil > 1:
                lane = lax.broadcasted_iota(jnp.int32, s.shape, 1)
                s = jnp.where(lane % dil == 0, s, -jnp.inf)
            parts.append(s)
            if k_off is not None:
                new_scores.append(per_head(col2(b, k_off) * q))
        if new_scores:
            lane = lax.broadcasted_iota(jnp.int32, (A_HEADS, LANES), 1)
            slab = jnp.full((A_HEADS, LANES), -jnp.inf, F32)
            for k, sn in enumerate(new_scores):
                slab = jnp.where(lane == k, sn, slab)
            parts.append(slab)
        s_all = jnp.concatenate(parts, axis=1)
        p_all = jnp.exp(s_all - jnp.max(s_all, axis=1, keepdims=True))
        den = jnp.sum(p_all, axis=1, keepdims=True)
        acc = [jnp.zeros((A_HEAD_DIM, LANES), F32) for _ in range(A_HEADS)]
        lo = 0
        for c_ref, _, _, _, _ in srcs:
            length = c_ref.shape[2]
            for k in range(length // LANES):
                ls = slice(k * LANES, (k + 1) * LANES)
                for h in range(A_HEADS):
                    vt = c_ref[b, D_A + h * A_HEAD_DIM:D_A + (h + 1) * A_HEAD_DIM, ls]
                    acc[h] = acc[h] + vt * p_all[h:h + 1, lo + k * LANES:lo + (k + 1) * LANES]
            lo += length
        num = jnp.sum(jnp.concatenate(acc, axis=0), axis=1, keepdims=True)
        k = 0
        for _, _, _, k_off, v_off in srcs:
            if k_off is not None:
                num = num + spread(p_all[:, lo + k:lo + k + 1]) * col2(b, v_off)
                k += 1
        return num / spread(den)

    window = [(c_ref, dil, OFF_QA + g * D_A, OFF_KA + g * D_A, OFF_VA + g * D_A)
              for g, (c_ref, (_, dil)) in enumerate(zip((c1_ref, c2_ref, c3_ref), WIN_GROUPS))]
    memory = [(cm_ref, 1, OFF_QM, None, None)]

    for b in range(bt):
        ua = attend(b, window) * _silu(col2(b, OFF_GA))
        um = attend(b, memory) * _silu(col2(b, OFF_GM))
        branch_cols = [ua[0:LANES], ua[LANES:2 * LANES], um[0:LANES], um[LANES:2 * LANES]]

        lane = lax.broadcasted_iota(jnp.int32, (LANES, LANES), 1)
        ut = jnp.zeros((LANES, LANES), F32)
        for k, cvec in enumerate(branch_cols):
            ut = ut + jnp.where(lane == k, cvec, 0.0)
        u_ref[b, 0:4, :] = ut.T[0:4, :]

        orows = []
        for h in range(HG_HEADS):
            sl = slice(h * HG_DIM, (h + 1) * HG_DIM)
            fcol = lb_col[sl] + (1.0 - lb_col[sl]) * _sigmoid(col(b, OFF_FB + h * HG_DIM))
            qcol = _silu(col(b, OFF_QB + h * HG_DIM))
            vrow = zr_ref[b, OFF_IB // LANES + h:OFF_IB // LANES + h + 1, :]
            sn = fcol * st_ref[b, sl, :] + (1.0 - fcol) * vrow
            nst_ref[b, sl, :] = sn
            orows.append(jnp.sum(sn * qcol, axis=0, keepdims=True))
        gate = zr_ref[b, OFF_GB // LANES:OFF_GB // LANES + HG_HEADS, :]
        u_ref[b, 4:4 + HG_HEADS, :] = _rms(jnp.concatenate(orows, axis=0), nh) * _silu(gate)


def _stored_view(cache):
    db, length = cache.shape[0], cache.shape[1]
    return jnp.transpose(cache, (0, 2, 3, 4, 1)).reshape(db, 2 * D_A, length)


def _sample_mix_call(z, c1, c2, c3, cm, st, lblt, nh):
    db = z.shape[0]
    bt = S_TILE
    zr = z.reshape(db, Z_ROWS, LANES)
    caches = [_stored_view(c) for c in (c1, c2, c3, cm)]
    stv = st.reshape(db, HG_HEADS * HG_DIM, HG_DIM)

    def rows(a):
        return pl.BlockSpec((bt,) + a.shape[1:], lambda i: (i, 0, 0))

    u, nst = pl.pallas_call(
        _sample_mix_kernel,
        out_shape=(jax.ShapeDtypeStruct((db, U_ROWS, LANES), F32),
                   jax.ShapeDtypeStruct(stv.shape, F32)),
        grid=(db // bt,),
        in_specs=[rows(zr), *[rows(c) for c in caches], rows(stv),
                  pl.BlockSpec((D_B, 2), lambda i: (0, 0)),
                  pl.BlockSpec((HG_HEADS, HG_DIM), lambda i: (0, 0))],
        out_specs=(pl.BlockSpec((bt, U_ROWS, LANES), lambda i: (i, 0, 0)), rows(stv)),
        compiler_params=pltpu.CompilerParams(
            dimension_semantics=("arbitrary",), vmem_limit_bytes=VMEM_LIMIT),
        name="sample_mix",
    )(zr, *caches, stv, lblt, nh)
    return u.reshape(db, U_ROWS * LANES), nst


def _sample_out_kernel(x_ref, u_ref, zg_ref, nf_ref, wa_ref, wb_ref, wm_ref, wo_ref, y_ref):
    u = u_ref[...]
    zg = zg_ref[...]
    y_ref[...] = _merge_out(x_ref[...], u[:, 0:D_A], u[:, D_A + D_M:], u[:, D_A:D_A + D_M],
                            zg[:, 0:D_MODEL], zg[:, D_MODEL:2 * D_MODEL], zg[:, 2 * D_MODEL:],
                            wa_ref, wb_ref, wm_ref, wo_ref, nf_ref[...])


def _sample_out_call(x, u, zg, nf, wa, wb, wm, wo):
    db = x.shape[0]

    def full(a):
        return pl.BlockSpec(a.shape, lambda i: (0,) * a.ndim)

    args = (x, u, zg, nf, wa, wb, wm, wo)
    return pl.pallas_call(
        _sample_out_kernel,
        out_shape=jax.ShapeDtypeStruct((db, D_MODEL), F32),
        grid=(1,),
        in_specs=[full(a) for a in args],
        out_specs=pl.BlockSpec((db, D_MODEL), lambda i: (0, 0)),
        compiler_params=pltpu.CompilerParams(vmem_limit_bytes=VMEM_LIMIT),
        name="sample_out",
    )(*args)


def _cache_rows(kv, length):
    outs = []
    for g in range(N_GROUPS):
        k = kv[:, -length[g]:, g * D_A:(g + 1) * D_A]
        v = kv[:, -length[g]:, (N_GROUPS + g) * D_A:(N_GROUPS + g + 1) * D_A]
        outs.append(jnp.stack([k, v], axis=2).reshape(kv.shape[0], length[g], 2, A_HEADS, A_HEAD_DIM)[None])
    return outs


def kernel(x_prompt, x_sample, mem_prompt, cache_w1_kv, cache_w2_kv, cache_w3_kv, cache_mem_kv, state_hgrn,
           norm_in, w_in, lb_logits, norm_hgrn, norm_mem, w_mem_kv, w_branch_a, w_branch_b, w_branch_m,
           w_out, norm_final):
    b, s, _ = x_prompt.shape
    db = x_sample.shape[0]
    w_in_bf = w_in[0].astype(BF16)
    wa, wb, wm, wo = (w[0].astype(BF16) for w in (w_branch_a, w_branch_b, w_branch_m, w_out))
    gain_in = norm_in[0][None]
    nh = norm_hgrn[0][None]
    nf = norm_final[None]
    lbl = lb_logits.astype(F32)

    mkv = _mem_kv_call(mem_prompt, norm_mem[0][None], w_mem_kv[0].astype(BF16))
    qkv0, qkv1, qkv2, gq, hb, zg, fb, kv_tail = _inproj_call(x_prompt, gain_in, w_in_bf)
    og = [_dilated_call(qkv, dil, g) for g, (qkv, (_, dil)) in enumerate(zip((qkv0, qkv1, qkv2), WIN_GROUPS))]
    y_prompt, hg_prompt = _prompt_tail_call(x_prompt, gq, hb, zg, fb, og[0], og[1], og[2], mkv, lbl, nh, nf,
                                            wa, wb, wm, wo)
    pw = _cache_rows(kv_tail, [min(w, s) for w, _ in WIN_GROUPS])
    new_mem = mkv.reshape(1, b, MEM_LEN, 2, A_HEADS, A_HEAD_DIM)

    xs = x_sample.reshape(db, D_MODEL)
    zs = _sample_inproj_call(xs, gain_in, w_in_bf)
    u, nst = _sample_mix_call(zs, cache_w1_kv[0], cache_w2_kv[0], cache_w3_kv[0], cache_mem_kv[0],
                              state_hgrn[0], lbl.T, nh.reshape(HG_HEADS, HG_DIM))
    y_sample = _sample_out_call(xs, u, zs[:, OFF_ZG:], nf, wa, wb, wm, wo).reshape(db, 1, D_MODEL)
    sw = _cache_rows(zs[:, None, OFF_KA:OFF_GA], [1, 1, 1])
    new_hg_sample = nst.reshape(1, db, HG_HEADS, HG_DIM, HG_DIM)

    return (y_prompt, y_sample, pw[0], pw[1], pw[2], new_mem, hg_prompt[None],
            sw[0], sw[1], sw[2], new_hg_sample)
```

```python
import jax
import jax.numpy as jnp
from jax import lax
from jax.experimental import pallas as pl
from jax.experimental.pallas import tpu as pltpu

F32 = jnp.float32
BF16 = jnp.bfloat16

D_MODEL = 1024
WIN_GROUPS = ((128, 1), (512, 4), (2048, 16))
N_GROUPS = 3
A_HEADS = 4
A_HEAD_DIM = 64
D_A = 256
QBLK = 128
HG_HEADS = 4
HG_DIM = 128
D_B = 512
HG_CHUNK = 64
HG_SUB = 16
MEM_LEN = 256
D_M = 256
EPS = 1e-6
D_IN = 8192
LANES = 128
OFF_QA, OFF_KA, OFF_VA, OFF_GA = 0, 768, 1536, 2304
OFF_QB, OFF_FB, OFF_IB, OFF_GB = 2560, 3072, 3584, 4096
OFF_QM, OFF_GM, OFF_ZG = 4608, 4864, 5120
ATT_SCALE = A_HEAD_DIM ** -0.5

VMEM_LIMIT = 56 * 1024 * 1024

NT_DIMS = (((1,), (1,)), ((), ()))
TN_DIMS = (((0,), (0,)), ((), ()))


def _sigmoid(x):
    return 0.5 * jnp.tanh(0.5 * x) + 0.5


def _silu(x):
    h = 0.5 * x
    return h * jnp.tanh(h) + h


def _rms(x, gain):
    return x * lax.rsqrt(jnp.mean(x * x, axis=-1, keepdims=True) + EPS) * gain


def _head_masks(width, head_dim, n_heads):
    lane = lax.broadcasted_iota(jnp.int32, (1, width), 1)
    return [(lane >= h * head_dim) & (lane < (h + 1) * head_dim) for h in range(n_heads)]


def _stack_heads(q, masks):
    zero = jnp.zeros_like(q)
    return jnp.concatenate([jnp.where(m, q, zero) for m in masks], axis=0)


def _unstack_heads(o_all, masks, t):
    n = len(masks)
    out = o_all[(n - 1) * t:n * t]
    for h in range(n - 2, -1, -1):
        out = jnp.where(masks[h], o_all[h * t:(h + 1) * t], out)
    return out


def _expand_cols(cols, masks):
    out = jnp.broadcast_to(cols[-1], (cols[-1].shape[0], masks[0].shape[1]))
    for h in range(len(masks) - 2, -1, -1):
        out = jnp.where(masks[h], cols[h], out)
    return out


def _mem_kv_kernel(mem_ref, gain_ref, w_ref, out_ref):
    hm = _rms(mem_ref[0], gain_ref[...]).astype(BF16)
    out_ref[0] = jnp.dot(hm, w_ref[...], preferred_element_type=F32)


def _mem_kv_call(mem, gain, w_bf):
    b = mem.shape[0]
    return pl.pallas_call(
        _mem_kv_kernel,
        out_shape=jax.ShapeDtypeStruct((b, MEM_LEN, 2 * D_M), F32),
        grid=(b,),
        in_specs=[pl.BlockSpec((1, MEM_LEN, D_MODEL), lambda i: (i, 0, 0)),
                  pl.BlockSpec((1, D_MODEL), lambda i: (0, 0)),
                  pl.BlockSpec((D_MODEL, 2 * D_M), lambda i: (0, 0))],
        out_specs=pl.BlockSpec((1, MEM_LEN, 2 * D_M), lambda i: (i, 0, 0)),
        name="mem_kv",
    )(mem, gain, w_bf)


IN_TILE = 512
IN_CHUNK = 256
KV_TAIL = 2048
D_QKV = 3 * D_A
D_GQ = 3 * D_A
D_HB = 3 * D_B
D_ZG = 3 * D_MODEL


def _inproj_kernel(x_ref, gain_ref, w_ref, qkv0_ref, qkv1_ref, qkv2_ref, gq_ref, hb_ref, zg_ref, fb_ref,
                   kv_ref, scr_ref):
    qkv_refs = (qkv0_ref, qkv1_ref, qkv2_ref)
    h = _rms(x_ref[0], gain_ref[...]).astype(BF16)
    t = IN_TILE
    for c in range(D_IN // IN_CHUNK):
        lo = c * IN_CHUNK
        zc = jnp.dot(h, w_ref[:, lo:lo + IN_CHUNK], preferred_element_type=F32)
        if lo < OFF_GA:
            sec, g = divmod(c, N_GROUPS)
            dil = WIN_GROUPS[g][1]
            dst = slice(sec * D_A, (sec + 1) * D_A)
            if dil == 1:
                qkv_refs[g][0, 0, :, dst] = zc.astype(BF16)
            else:
                for k in range(IN_CHUNK // LANES):
                    scr_ref[k] = zc[:, k * LANES:(k + 1) * LANES]
                for r in range(dil):
                    qkv_refs[g][0, r, :, dst] = jnp.concatenate(
                        [scr_ref[k, pl.ds(r, t // dil, stride=dil), :] for k in range(IN_CHUNK // LANES)],
                        axis=1).astype(BF16)
            if sec > 0:
                kv_ref[0, :, lo - OFF_KA:lo - OFF_KA + IN_CHUNK] = zc
        elif lo < OFF_QB:
            gq_ref[0, :, 0:D_A] = zc.astype(BF16)
        elif lo < OFF_FB:
            hb_ref[0, :, lo - OFF_QB:lo - OFF_QB + IN_CHUNK] = zc.astype(BF16)
        elif lo < OFF_IB:
            fb_ref[0, :, lo - OFF_FB:lo - OFF_FB + IN_CHUNK] = zc
        elif lo < OFF_QM:
            hb_ref[0, :, lo - OFF_IB + D_B:lo - OFF_IB + D_B + IN_CHUNK] = zc.astype(BF16)
        elif lo < OFF_ZG:
            gq_ref[0, :, lo - OFF_QM + D_A:lo - OFF_QM + D_A + IN_CHUNK] = zc.astype(BF16)
        else:
            zg_ref[0, :, lo - OFF_ZG:lo - OFF_ZG + IN_CHUNK] = zc.astype(BF16)


def _inproj_call(x, gain, w_bf):
    b, s, _ = x.shape
    t = IN_TILE
    nt = s // t
    first_tail = (s - KV_TAIL) // t

    def tok(width):
        return pl.BlockSpec((1, t, width), lambda i, j: (i, j, 0))

    qkv_shapes, qkv_specs = [], []
    for _, dil in WIN_GROUPS:
        qkv_shapes.append(jax.ShapeDtypeStruct((b, dil, s // dil, D_QKV), BF16))
        qkv_specs.append(pl.BlockSpec((1, dil, t // dil, D_QKV), lambda i, j: (i, 0, j, 0)))
    return pl.pallas_call(
        _inproj_kernel,
        out_shape=(*qkv_shapes,
                   jax.ShapeDtypeStruct((b, s, D_GQ), BF16),
                   jax.ShapeDtypeStruct((b, s, D_HB), BF16),
                   jax.ShapeDtypeStruct((b, s, D_ZG), BF16),
                   jax.ShapeDtypeStruct((b, s, D_B), F32),
                   jax.ShapeDtypeStruct((b, KV_TAIL, 2 * N_GROUPS * D_A), F32)),
        grid=(b, nt),
        in_specs=[tok(D_MODEL),
                  pl.BlockSpec((1, D_MODEL), lambda i, j: (0, 0)),
                  pl.BlockSpec((D_MODEL, D_IN), lambda i, j: (0, 0), pipeline_mode=pl.Buffered(1))],
        out_specs=(*qkv_specs, tok(D_GQ), tok(D_HB), tok(D_ZG), tok(D_B),
                   pl.BlockSpec((1, t, 2 * N_GROUPS * D_A),
                                lambda i, j: (i, jnp.maximum(j - first_tail, 0), 0))),
        scratch_shapes=[pltpu.VMEM((IN_CHUNK // LANES, t, LANES), F32)],
        compiler_params=pltpu.CompilerParams(
            dimension_semantics=("arbitrary", "arbitrary"), vmem_limit_bytes=VMEM_LIMIT),
        name="inproj",
    )(x, gain, w_bf)


def _dilated_block(q, kk, vv, neg, masks):
    qs = _stack_heads(q * jnp.asarray(ATT_SCALE, BF16), masks)
    s = lax.dot_general(qs, kk, NT_DIMS, preferred_element_type=F32) + neg
    mx = jnp.max(s, axis=-1, keepdims=True)
    p = jnp.exp(s - mx)
    l = jnp.sum(p, axis=-1, keepdims=True)
    o_all = jnp.dot(p.astype(BF16), vv, preferred_element_type=F32)
    l_e = _expand_cols([l[h * QBLK:(h + 1) * QBLK] for h in range(A_HEADS)], masks)
    mx_e = _expand_cols([mx[h * QBLK:(h + 1) * QBLK] for h in range(A_HEADS)], masks)
    return _unstack_heads(o_all, masks, QBLK) / l_e, mx_e + jnp.log(l_e)


DIL_STEP = 2048
OG_SLABS = 2 * D_A // LANES


def _make_dilated_kernel(dil):
    nblk = DIL_STEP // (dil * QBLK)

    def body(prev_ref, cur_ref, out_ref):
        first_step = pl.program_id(1) == 0
        masks = _head_masks(D_A, A_HEAD_DIM, A_HEADS)
        qi = lax.broadcasted_iota(jnp.int32, (A_HEADS * QBLK, 2 * QBLK), 0) % QBLK
        kj = lax.broadcasted_iota(jnp.int32, (A_HEADS * QBLK, 2 * QBLK), 1)
        band = (kj >= qi) & (kj <= qi + QBLK)
        neg_band = jnp.where(band, 0.0, -jnp.inf)
        neg_first = jnp.where(first_step, jnp.where(band & (kj >= QBLK), 0.0, -jnp.inf), neg_band)

        for r in range(dil):
            for jb in range(nblk):
                rows = slice(jb * QBLK, (jb + 1) * QBLK)
                q = cur_ref[0, r, rows, 0:D_A]
                if jb == 0:
                    kv = [jnp.concatenate([prev_ref[0, r, :, c * D_A:(c + 1) * D_A],
                                           cur_ref[0, r, rows, c * D_A:(c + 1) * D_A]], axis=0) for c in (1, 2)]
                else:
                    both = slice((jb - 1) * QBLK, (jb + 1) * QBLK)
                    kv = [cur_ref[0, r, both, c * D_A:(c + 1) * D_A] for c in (1, 2)]
                o, lse = _dilated_block(q, kv[0], kv[1], neg_first if jb == 0 else neg_band, masks)
                res = (o, lse)
                for k in range(OG_SLABS):
                    src = res[k // 2][:, (k % 2) * LANES:(k % 2 + 1) * LANES]
                    if dil == 1:
                        out_ref[0, k, rows, :] = src
                    else:
                        out_ref[0, k, pl.ds(jb * QBLK * dil + r, QBLK, stride=dil), :] = src
    return body


def _dilated_call(qkv, dil, g):
    b, _, n, _ = qkv.shape
    rows = DIL_STEP // dil
    nb = n // rows
    return pl.pallas_call(
        _make_dilated_kernel(dil),
        out_shape=jax.ShapeDtypeStruct((b, OG_SLABS, n * dil, LANES), F32),
        grid=(b, nb),
        in_specs=[pl.BlockSpec((1, dil, QBLK, D_QKV),
                               lambda i, u: (i, 0, jnp.maximum(u * (rows // QBLK) - 1, 0), 0)),
                  pl.BlockSpec((1, dil, rows, D_QKV), lambda i, u: (i, 0, u, 0))],
        out_specs=pl.BlockSpec((1, OG_SLABS, DIL_STEP, LANES), lambda i, u: (i, 0, u, 0)),
        compiler_params=pltpu.CompilerParams(
            dimension_semantics=("arbitrary", "arbitrary"), vmem_limit_bytes=VMEM_LIMIT),
        name=f"dilated_attn_g{g}",
    )(qkv, qkv)


def _lower_bound(l0, l1):
    m = jnp.maximum(l0, l1)
    e0, e1 = jnp.exp(l0 - m), jnp.exp(l1 - m)
    return e0 / (e0 + e1)


def _head_rms(ob, gain):
    parts = []
    for h in range(HG_HEADS):
        sl = slice(h * HG_DIM, (h + 1) * HG_DIM)
        parts.append(_rms(ob[:, sl], gain[:, sl]))
    return jnp.concatenate(parts, axis=1)


def _merge_out(x, ua, ub, um, zg0, zg1, zg2, wa_ref, wb_ref, wm_ref, wo_ref, nf):
    merged = (_sigmoid(zg0) * jnp.dot(ua.astype(BF16), wa_ref[...], preferred_element_type=F32)
              + _sigmoid(zg1) * jnp.dot(ub.astype(BF16), wb_ref[...], preferred_element_type=F32)
              + _sigmoid(zg2) * jnp.dot(um.astype(BF16), wm_ref[...], preferred_element_type=F32))
    y = x + jnp.dot(merged.astype(BF16), wo_ref[...], preferred_element_type=F32)
    return _rms(y, nf)


TAIL_TILE = 256


def _cumsum_rows(g, tri):
    g1 = g.astype(BF16)
    g2 = (g - g1.astype(F32)).astype(BF16)
    return jnp.dot(tri, g1, preferred_element_type=F32) + jnp.dot(tri, g2, preferred_element_type=F32)


def _hgrn_tile(fb, qb, v, lb, st_ref):
    t = fb.shape[0]
    c, sub = HG_CHUNK, HG_SUB
    n_sub = c // sub
    f = lb + (1.0 - lb) * _sigmoid(fb)
    kk = 1.0 - f
    row = lax.broadcasted_iota(jnp.int32, (t, t), 0)
    col = lax.broadcasted_iota(jnp.int32, (t, t), 1)
    same_chunk = (row // c) == (col // c)
    gcs = _cumsum_rows(jnp.log(f), (same_chunk & (col <= row)).astype(BF16))
    q = _silu(qb.astype(F32))

    loc = lax.broadcasted_iota(jnp.int32, (c, 1), 0)
    qt_c, kt_ci, qe_c, kd_c, dec_c = [], [[] for _ in range(n_sub)], [], [], []
    for ci in range(t // c):
        rows = slice(ci * c, (ci + 1) * c)
        g_c, q_c, k_c = gcs[rows], q[rows], kk[rows]
        refs = [jnp.zeros((1, D_B), F32)] + [g_c[i * sub - 1:i * sub] for i in range(1, n_sub)]
        own = refs[n_sub - 1]
        for i in range(n_sub - 2, -1, -1):
            own = jnp.where(loc < (i + 1) * sub, refs[i], own)
        qt_c.append((q_c * jnp.exp(g_c - own)).astype(BF16))
        for i in range(n_sub):
            seen = loc < (i + 1) * sub
            kt_ci[i].append(jnp.where(seen, k_c * jnp.exp(jnp.where(seen, refs[i] - g_c, 0.0)), 0.0).astype(BF16))
        g_last = g_c[c - 1:c]
        qe_c.append((q_c * jnp.exp(g_c)).astype(BF16))
        kd_c.append((k_c * jnp.exp(g_last - g_c)).astype(BF16))
        dec_c.append(jnp.exp(g_last))
    qt = jnp.concatenate(qt_c, axis=0)
    kts = [jnp.concatenate(parts, axis=0) for parts in kt_ci]
    sub_of_row = (lax.broadcasted_iota(jnp.int32, (t, 1), 0) % c) // sub
    causal = same_chunk & (col <= row)

    outs = []
    for h in range(HG_HEADS):
        sl = slice(h * HG_DIM, (h + 1) * HG_DIM)
        zero = jnp.zeros((t, HG_DIM), BF16)
        q_big = jnp.concatenate([jnp.where(sub_of_row == i, qt[:, sl], zero) for i in range(n_sub)], axis=1)
        k_big = jnp.concatenate([kts[i][:, sl] for i in range(n_sub)], axis=1)
        att = lax.dot_general(q_big, k_big, NT_DIMS, preferred_element_type=F32)
        att = jnp.where(causal, att, 0.0).astype(BF16)
        o_intra = jnp.dot(att, v[:, sl], preferred_element_type=F32)
        st = st_ref[h]
        inter = []
        for ci in range(t // c):
            rows = slice(ci * c, (ci + 1) * c)
            inter.append(lax.dot_general(qe_c[ci][:, sl], st.astype(BF16), NT_DIMS, preferred_element_type=F32))
            st = st * dec_c[ci][:, sl] + lax.dot_general(
                v[rows, sl], kd_c[ci][:, sl], TN_DIMS, preferred_element_type=F32)
        st_ref[h] = st
        outs.append(o_intra + jnp.concatenate(inter, axis=0))
    return jnp.concatenate(outs, axis=1)


def _prompt_tail_kernel(x_ref, gq_ref, hb_ref, zg_ref, fb_ref, o1_ref, o2_ref, o3_ref, mkv_ref, lbl_ref,
                        nh_ref, nf_ref, wa_ref, wb_ref, wm_ref, wo_ref,
                        y_ref, hst_ref, st_ref):
    j = pl.program_id(1)
    t = TAIL_TILE

    @pl.when(j == 0)
    def _():
        st_ref[...] = jnp.zeros_like(st_ref)

    os_ = [jnp.concatenate([r[0, 0], r[0, 1]], axis=1) for r in (o1_ref, o2_ref, o3_ref)]
    ls_ = [jnp.concatenate([r[0, 2], r[0, 3]], axis=1) for r in (o1_ref, o2_ref, o3_ref)]
    mx = jnp.maximum(jnp.maximum(ls_[0], ls_[1]), ls_[2])
    es_ = [jnp.exp(l - mx) for l in ls_]
    oa = (es_[0] * os_[0] + es_[1] * os_[1] + es_[2] * os_[2]) / (es_[0] + es_[1] + es_[2])
    ua = oa * _silu(gq_ref[0, :, 0:D_A].astype(F32))

    masks = _head_masks(D_M, A_HEAD_DIM, A_HEADS)
    qm = gq_ref[0, :, D_A:2 * D_A] * jnp.asarray(ATT_SCALE, BF16)
    qs = _stack_heads(qm, masks)
    mk = mkv_ref[0, :, 0:D_M].astype(BF16)
    mv = mkv_ref[0, :, D_M:2 * D_M].astype(BF16)
    s = lax.dot_general(qs, mk, NT_DIMS, preferred_element_type=F32)
    p = jnp.exp(s - jnp.max(s, axis=-1, keepdims=True))
    l = jnp.sum(p, axis=-1, keepdims=True)
    om = _unstack_heads(jnp.dot(p.astype(BF16), mv, preferred_element_type=F32) / l, masks, t)
    um = om * _silu(gq_ref[0, :, 2 * D_A:3 * D_A].astype(F32))

    lb = _lower_bound(lbl_ref[0:1], lbl_ref[1:2])
    ob = _hgrn_tile(fb_ref[0], hb_ref[0, :, 0:D_B], hb_ref[0, :, D_B:2 * D_B], lb, st_ref)
    ub = _head_rms(ob, nh_ref[...]) * _silu(hb_ref[0, :, 2 * D_B:3 * D_B].astype(F32))

    y_ref[0] = _merge_out(x_ref[0], ua, ub, um, zg_ref[0, :, 0:D_MODEL].astype(F32),
                          zg_ref[0, :, D_MODEL:2 * D_MODEL].astype(F32),
                          zg_ref[0, :, 2 * D_MODEL:3 * D_MODEL].astype(F32),
                          wa_ref, wb_ref, wm_ref, wo_ref, nf_ref[...])

    @pl.when(j == pl.num_programs(1) - 1)
    def _():
        for h in range(HG_HEADS):
            hst_ref[0, h] = st_ref[h].T


def _prompt_tail_call(x, gq, hb, zg, fb, o1, o2, o3, mkv, lbl, nh, nf, wa, wb, wm, wo):
    b, s, _ = x.shape
    t = TAIL_TILE

    def tok(width):
        return pl.BlockSpec((1, t, width), lambda i, j: (i, j, 0))

    def const(shape):
        return pl.BlockSpec(shape, lambda i, j: (0,) * len(shape))

    return pl.pallas_call(
        _prompt_tail_kernel,
        out_shape=(jax.ShapeDtypeStruct((b, s, D_MODEL), F32),
                   jax.ShapeDtypeStruct((b, HG_HEADS, HG_DIM, HG_DIM), F32)),
        grid=(b, s // t),
        in_specs=[tok(D_MODEL), tok(D_GQ), tok(D_HB), tok(D_ZG), tok(D_B),
                  *[pl.BlockSpec((1, OG_SLABS, t, LANES), lambda i, j: (i, 0, j, 0))] * N_GROUPS,
                  pl.BlockSpec((1, MEM_LEN, 2 * D_M), lambda i, j: (i, 0, 0)),
                  const((2, D_B)), const((1, D_B)), const((1, D_MODEL)),
                  const((D_A, D_MODEL)), const((D_B, D_MODEL)), const((D_M, D_MODEL)),
                  const((D_MODEL, D_MODEL))],
        out_specs=(tok(D_MODEL),
                   pl.BlockSpec((1, HG_HEADS, HG_DIM, HG_DIM), lambda i, j: (i, 0, 0, 0))),
        scratch_shapes=[pltpu.VMEM((HG_HEADS, HG_DIM, HG_DIM), F32)],
        compiler_params=pltpu.CompilerParams(
            dimension_semantics=("arbitrary", "arbitrary"), vmem_limit_bytes=VMEM_LIMIT),
        name="prompt_tail",
    )(x, gq, hb, zg, fb, o1, o2, o3, mkv, lbl, nh, nf, wa, wb, wm, wo)


S_COLS = 1024


def _sample_inproj_kernel(x_ref, gain_ref, w_ref, z_ref):
    h = _rms(x_ref[...], gain_ref[...]).astype(BF16)
    z_ref[...] = jnp.dot(h, w_ref[...], preferred_element_type=F32)


def _sample_inproj_call(x, gain, w_bf):
    db = x.shape[0]
    return pl.pallas_call(
        _sample_inproj_kernel,
        out_shape=jax.ShapeDtypeStruct((db, D_IN), F32),
        grid=(D_IN // S_COLS,),
        in_specs=[pl.BlockSpec((db, D_MODEL), lambda c: (0, 0)),
                  pl.BlockSpec((1, D_MODEL), lambda c: (0, 0)),
                  pl.BlockSpec((D_MODEL, S_COLS), lambda c: (0, c))],
        out_specs=pl.BlockSpec((db, S_COLS), lambda c: (0, c)),
        name="sample_inproj",
    )(x, gain, w_bf)


S_TILE = 2
Z_ROWS = D_IN // LANES
U_ROWS = 8


def _sample_mix_kernel(zr_ref, c1_ref, c2_ref, c3_ref, cm_ref, st_ref, lblt_ref, nh_ref,
                       u_ref, nst_ref):
    bt = S_TILE
    zt = jnp.concatenate([zr_ref[b] for b in range(bt)], axis=0).T

    def col(b, off, n=HG_DIM):
        j, l = divmod(off, LANES)
        return zt[l:l + n, Z_ROWS * b + j:Z_ROWS * b + j + 1]

    lbt = lblt_ref[...]
    lb_col = _lower_bound(lbt[:, 0:1], lbt[:, 1:2])
    nh = nh_ref[...]

    def col2(b, off):
        return jnp.concatenate([col(b, off), col(b, off + LANES)], axis=0)

    def per_head(x):
        return jnp.concatenate([jnp.sum(x[h * A_HEAD_DIM:(h + 1) * A_HEAD_DIM], axis=0, keepdims=True)
                                for h in range(A_HEADS)], axis=0)

    def spread(x):
        return jnp.concatenate([jnp.broadcast_to(x[h:h + 1], (A_HEAD_DIM, 1)) for h in range(A_HEADS)], axis=0)

    def attend(b, srcs):
        parts, new_scores, qs = [], [], []
        for c_ref, dil, q_off, k_off, _ in srcs:
            q = col2(b, q_off) * ATT_SCALE
            qs.append(q)
            s = per_head(c_ref[b, 0:D_A, :] * q)
            if dil > 1:
                lane = lax.broadcasted_iota(jnp.int32, s.shape, 1)
                s = jnp.where(lane % dil == 0, s, -jnp.inf)
            parts.append(s)
            if k_off is not None:
                new_scores.append(per_head(col2(b, k_off) * q))
        if new_scores:
            lane = lax.broadcasted_iota(jnp.int32, (A_HEADS, LANES), 1)
            slab = jnp.full((A_HEADS, LANES), -jnp.inf, F32)
            for k, sn in enumerate(new_scores):
                slab = jnp.where(lane == k, sn, slab)
            parts.append(slab)
        s_all = jnp.concatenate(parts, axis=1)
        p_all = jnp.exp(s_all - jnp.max(s_all, axis=1, keepdims=True))
        den = jnp.sum(p_all, axis=1, keepdims=True)
        acc = [jnp.zeros((A_HEAD_DIM, LANES), F32) for _ in range(A_HEADS)]
        lo = 0
        for c_ref, _, _, _, _ in srcs:
            length = c_ref.shape[2]
            for k in range(length // LANES):
                ls = slice(k * LANES, (k + 1) * LANES)
                for h in range(A_HEADS):
                    vt = c_ref[b, D_A + h * A_HEAD_DIM:D_A + (h + 1) * A_HEAD_DIM, ls]
                    acc[h] = acc[h] + vt * p_all[h:h + 1, lo + k * LANES:lo + (k + 1) * LANES]
            lo += length
        num = jnp.sum(jnp.concatenate(acc, axis=0), axis=1, keepdims=True)
        k = 0
        for _, _, _, k_off, v_off in srcs:
            if k_off is not None:
                num = num + spread(p_all[:, lo + k:lo + k + 1]) * col2(b, v_off)
                k += 1
        return num / spread(den)

    window = [(c_ref, dil, OFF_QA + g * D_A, OFF_KA + g * D_A, OFF_VA + g * D_A)
              for g, (c_ref, (_, dil)) in enumerate(zip((c1_ref, c2_ref, c3_ref), WIN_GROUPS))]
    memory = [(cm_ref, 1, OFF_QM, None, None)]

    for b in range(bt):
        ua = attend(b, window) * _silu(col2(b, OFF_GA))
        um = attend(b, memory) * _silu(col2(b, OFF_GM))
        branch_cols = [ua[0:LANES], ua[LANES:2 * LANES], um[0:LANES], um[LANES:2 * LANES]]

        lane = lax.broadcasted_iota(jnp.int32, (LANES, LANES), 1)
        ut = jnp.zeros((LANES, LANES), F32)
        for k, cvec in enumerate(branch_cols):
            ut = ut + jnp.where(lane == k, cvec, 0.0)
        u_ref[b, 0:4, :] = ut.T[0:4, :]

        orows = []
        for h in range(HG_HEADS):
            sl = slice(h * HG_DIM, (h + 1) * HG_DIM)
            fcol = lb_col[sl] + (1.0 - lb_col[sl]) * _sigmoid(col(b, OFF_FB + h * HG_DIM))
            qcol = _silu(col(b, OFF_QB + h * HG_DIM))
            vrow = zr_ref[b, OFF_IB // LANES + h:OFF_IB // LANES + h + 1, :]
            sn = fcol * st_ref[b, sl, :] + (1.0 - fcol) * vrow
            nst_ref[b, sl, :] = sn
            orows.append(jnp.sum(sn * qcol, axis=0, keepdims=True))
        gate = zr_ref[b, OFF_GB // LANES:OFF_GB // LANES + HG_HEADS, :]
        u_ref[b, 4:4 + HG_HEADS, :] = _rms(jnp.concatenate(orows, axis=0), nh) * _silu(gate)


def _stored_view(cache):
    db, length = cache.shape[0], cache.shape[1]
    return jnp.transpose(cache, (0, 2, 3, 4, 1)).reshape(db, 2 * D_A, length)


def _sample_mix_call(z, c1, c2, c3, cm, st, lblt, nh):
    db = z.shape[0]
    bt = S_TILE
    zr = z.reshape(db, Z_ROWS, LANES)
    caches = [_stored_view(c) for c in (c1, c2, c3, cm)]
    stv = st.reshape(db, HG_HEADS * HG_DIM, HG_DIM)

    def rows(a):
        return pl.BlockSpec((bt,) + a.shape[1:], lambda i: (i, 0, 0))

    u, nst = pl.pallas_call(
        _sample_mix_kernel,
        out_shape=(jax.ShapeDtypeStruct((db, U_ROWS, LANES), F32),
                   jax.ShapeDtypeStruct(stv.shape, F32)),
        grid=(db // bt,),
        in_specs=[rows(zr), *[rows(c) for c in caches], rows(stv),
                  pl.BlockSpec((D_B, 2), lambda i: (0, 0)),
                  pl.BlockSpec((HG_HEADS, HG_DIM), lambda i: (0, 0))],
        out_specs=(pl.BlockSpec((bt, U_ROWS, LANES), lambda i: (i, 0, 0)), rows(stv)),
        compiler_params=pltpu.CompilerParams(
            dimension_semantics=("arbitrary",), vmem_limit_bytes=VMEM_LIMIT),
        name="sample_mix",
    )(zr, *caches, stv, lblt, nh)
    return u.reshape(db, U_ROWS * LANES), nst


def _sample_out_kernel(x_ref, u_ref, zg_ref, nf_ref, wa_ref, wb_ref, wm_ref, wo_ref, y_ref):
    u = u_ref[...]
    zg = zg_ref[...]
    y_ref[...] = _merge_out(x_ref[...], u[:, 0:D_A], u[:, D_A + D_M:], u[:, D_A:D_A + D_M],
                            zg[:, 0:D_MODEL], zg[:, D_MODEL:2 * D_MODEL], zg[:, 2 * D_MODEL:],
                            wa_ref, wb_ref, wm_ref, wo_ref, nf_ref[...])


def _sample_out_call(x, u, zg, nf, wa, wb, wm, wo):
    db = x.shape[0]

    def full(a):
        return pl.BlockSpec(a.shape, lambda i: (0,) * a.ndim)

    args = (x, u, zg, nf, wa, wb, wm, wo)
    return pl.pallas_call(
        _sample_out_kernel,
        out_shape=jax.ShapeDtypeStruct((db, D_MODEL), F32),
        grid=(1,),
        in_specs=[full(a) for a in args],
        out_specs=pl.BlockSpec((db, D_MODEL), lambda i: (0, 0)),
        compiler_params=pltpu.CompilerParams(vmem_limit_bytes=VMEM_LIMIT),
        name="sample_out",
    )(*args)


def _cache_rows(kv, length):
    outs = []
    for g in range(N_GROUPS):
        k = kv[:, -length[g]:, g * D_A:(g + 1) * D_A]
        v = kv[:, -length[g]:, (N_GROUPS + g) * D_A:(N_GROUPS + g + 1) * D_A]
        outs.append(jnp.stack([k, v], axis=2).reshape(kv.shape[0], length[g], 2, A_HEADS, A_HEAD_DIM)[None])
    return outs


def kernel(x_prompt, x_sample, mem_prompt, cache_w1_kv, cache_w2_kv, cache_w3_kv, cache_mem_kv, state_hgrn,
           norm_in, w_in, lb_logits, norm_hgrn, norm_mem, w_mem_kv, w_branch_a, w_branch_b, w_branch_m,
           w_out, norm_final):
    b, s, _ = x_prompt.shape
    db = x_sample.shape[0]
    w_in_bf = w_in[0].astype(BF16)
    wa, wb, wm, wo = (w[0].astype(BF16) for w in (w_branch_a, w_branch_b, w_branch_m, w_out))
    gain_in = norm_in[0][None]
    nh = norm_hgrn[0][None]
    nf = norm_final[None]
    lbl = lb_logits.astype(F32)

    mkv = _mem_kv_call(mem_prompt, norm_mem[0][None], w_mem_kv[0].astype(BF16))
    qkv0, qkv1, qkv2, gq, hb, zg, fb, kv_tail = _inproj_call(x_prompt, gain_in, w_in_bf)
    og = [_dilated_call(qkv, dil, g) for g, (qkv, (_, dil)) in enumerate(zip((qkv0, qkv1, qkv2), WIN_GROUPS))]
    y_prompt, hg_prompt = _prompt_tail_call(x_prompt, gq, hb, zg, fb, og[0], og[1], og[2], mkv, lbl, nh, nf,
                                            wa, wb, wm, wo)
    pw = _cache_rows(kv_tail, [min(w, s) for w, _ in WIN_GROUPS])
    new_mem = mkv.reshape(1, b, MEM_LEN, 2, A_HEADS, A_HEAD_DIM)

    xs = x_sample.reshape(db, D_MODEL)
    zs = _sample_inproj_call(xs, gain_in, w_in_bf)
    u, nst = _sample_mix_call(zs, cache_w1_kv[0], cache_w2_kv[0], cache_w3_kv[0], cache_mem_kv[0],
                              state_hgrn[0], lbl.T, nh.reshape(HG_HEADS, HG_DIM))
    y_sample = _sample_out_call(xs, u, zs[:, OFF_ZG:], nf, wa, wb, wm, wo).reshape(db, 1, D_MODEL)
    sw = _cache_rows(zs[:, None, OFF_KA:OFF_GA], [1, 1, 1])
    new_hg_sample = nst.reshape(1, db, HG_HEADS, HG_DIM, HG_DIM)

    return (y_prompt, y_sample, pw[0], pw[1], pw[2], new_mem, hg_prompt[None],
            sw[0], sw[1], sw[2], new_hg_sample)
```

```python
import jax
import jax.numpy as jnp
from jax import lax
from jax.experimental import pallas as pl
from jax.experimental.pallas import tpu as pltpu

F32 = jnp.float32
BF16 = jnp.bfloat16

D_MODEL = 1024
WIN_GROUPS = ((128, 1), (512, 4), (2048, 16))
N_GROUPS = 3
A_HEADS = 4
A_HEAD_DIM = 64
D_A = 256
QBLK = 128
HG_HEADS = 4
HG_DIM = 128
D_B = 512
HG_CHUNK = 64
HG_SUB = 16
MEM_LEN = 256
D_M = 256
EPS = 1e-6
D_IN = 8192
LANES = 128
OFF_QA, OFF_KA, OFF_VA, OFF_GA = 0, 768, 1536, 2304
OFF_QB, OFF_FB, OFF_IB, OFF_GB = 2560, 3072, 3584, 4096
OFF_QM, OFF_GM, OFF_ZG = 4608, 4864, 5120
ATT_SCALE = A_HEAD_DIM ** -0.5

VMEM_LIMIT = 56 * 1024 * 1024

NT_DIMS = (((1,), (1,)), ((), ()))
TN_DIMS = (((0,), (0,)), ((), ()))


def _sigmoid(x):
    return 0.5 * jnp.tanh(0.5 * x) + 0.5


def _silu(x):
    h = 0.5 * x
    return h * jnp.tanh(h) + h


def _rms(x, gain):
    return x * lax.rsqrt(jnp.mean(x * x, axis=-1, keepdims=True) + EPS) * gain


def _head_masks(width, head_dim, n_heads):
    lane = lax.broadcasted_iota(jnp.int32, (1, width), 1)
    return [(lane >= h * head_dim) & (lane < (h + 1) * head_dim) for h in range(n_heads)]


def _stack_heads(q, masks):
    zero = jnp.zeros_like(q)
    return jnp.concatenate([jnp.where(m, q, zero) for m in masks], axis=0)


def _unstack_heads(o_all, masks, t):
    n = len(masks)
    out = o_all[(n - 1) * t:n * t]
    for h in range(n - 2, -1, -1):
        out = jnp.where(masks[h], o_all[h * t:(h + 1) * t], out)
    return out


def _expand_cols(cols, masks):
    out = jnp.broadcast_to(cols[-1], (cols[-1].shape[0], masks[0].shape[1]))
    for h in range(len(masks) - 2, -1, -1):
        out = jnp.where(masks[h], cols[h], out)
    return out


def _mem_kv_kernel(mem_ref, gain_ref, w_ref, out_ref):
    hm = _rms(mem_ref[0], gain_ref[...]).astype(BF16)
    out_ref[0] = jnp.dot(hm, w_ref[...], preferred_element_type=F32)


def _mem_kv_call(mem, gain, w_bf):
    b = mem.shape[0]
    return pl.pallas_call(
        _mem_kv_kernel,
        out_shape=jax.ShapeDtypeStruct((b, MEM_LEN, 2 * D_M), F32),
        grid=(b,),
        in_specs=[pl.BlockSpec((1, MEM_LEN, D_MODEL), lambda i: (i, 0, 0)),
                  pl.BlockSpec((1, D_MODEL), lambda i: (0, 0)),
                  pl.BlockSpec((D_MODEL, 2 * D_M), lambda i: (0, 0))],
        out_specs=pl.BlockSpec((1, MEM_LEN, 2 * D_M), lambda i: (i, 0, 0)),
        name="mem_kv",
    )(mem, gain, w_bf)


IN_TILE = 512
IN_CHUNK = 256
KV_TAIL = 2048
D_QKV = 3 * D_A
D_GQ = 3 * D_A
D_HB = 3 * D_B
D_ZG = 3 * D_MODEL


def _inproj_kernel(x_ref, gain_ref, w_ref, qkv0_ref, qkv1_ref, qkv2_ref, gq_ref, hb_ref, zg_ref, fb_ref,
                   kv_ref, scr_ref):
    qkv_refs = (qkv0_ref, qkv1_ref, qkv2_ref)
    h = _rms(x_ref[0], gain_ref[...]).astype(BF16)
    t = IN_TILE
    for c in range(D_IN // IN_CHUNK):
        lo = c * IN_CHUNK
        zc = jnp.dot(h, w_ref[:, lo:lo + IN_CHUNK], preferred_element_type=F32)
        if lo < OFF_GA:
            sec, g = divmod(c, N_GROUPS)
            dil = WIN_GROUPS[g][1]
            dst = slice(sec * D_A, (sec + 1) * D_A)
            if dil == 1:
                qkv_refs[g][0, 0, :, dst] = zc.astype(BF16)
            else:
                for k in range(IN_CHUNK // LANES):
                    scr_ref[k] = zc[:, k * LANES:(k + 1) * LANES]
                for r in range(dil):
                    qkv_refs[g][0, r, :, dst] = jnp.concatenate(
                        [scr_ref[k, pl.ds(r, t // dil, stride=dil), :] for k in range(IN_CHUNK // LANES)],
                        axis=1).astype(BF16)
            if sec > 0:
                kv_ref[0, :, lo - OFF_KA:lo - OFF_KA + IN_CHUNK] = zc
        elif lo < OFF_QB:
            gq_ref[0, :, 0:D_A] = zc.astype(BF16)
        elif lo < OFF_FB:
            hb_ref[0, :, lo - OFF_QB:lo - OFF_QB + IN_CHUNK] = zc.astype(BF16)
        elif lo < OFF_IB:
            fb_ref[0, :, lo - OFF_FB:lo - OFF_FB + IN_CHUNK] = zc
        elif lo < OFF_QM:
            hb_ref[0, :, lo - OFF_IB + D_B:lo - OFF_IB + D_B + IN_CHUNK] = zc.astype(BF16)
        elif lo < OFF_ZG:
            gq_ref[0, :, lo - OFF_QM + D_A:lo - OFF_QM + D_A + IN_CHUNK] = zc.astype(BF16)
        else:
            zg_ref[0, :, lo - OFF_ZG:lo - OFF_ZG + IN_CHUNK] = zc.astype(BF16)


def _inproj_call(x, gain, w_bf):
    b, s, _ = x.shape
    t = IN_TILE
    nt = s // t
    first_tail = (s - KV_TAIL) // t

    def tok(width):
        return pl.BlockSpec((1, t, width), lambda i, j: (i, j, 0))

    qkv_shapes, qkv_specs = [], []
    for _, dil in WIN_GROUPS:
        qkv_shapes.append(jax.ShapeDtypeStruct((b, dil, s // dil, D_QKV), BF16))
        qkv_specs.append(pl.BlockSpec((1, dil, t // dil, D_QKV), lambda i, j: (i, 0, j, 0)))
    return pl.pallas_call(
        _inproj_kernel,
        out_shape=(*qkv_shapes,
                   jax.ShapeDtypeStruct((b, s, D_GQ), BF16),
                   jax.ShapeDtypeStruct((b, s, D_HB), BF16),
                   jax.ShapeDtypeStruct((b, s, D_ZG), BF16),
                   jax.ShapeDtypeStruct((b, s, D_B), F32),
                   jax.ShapeDtypeStruct((b, KV_TAIL, 2 * N_GROUPS * D_A), F32)),
        grid=(b, nt),
        in_specs=[tok(D_MODEL),
                  pl.BlockSpec((1, D_MODEL), lambda i, j: (0, 0)),
                  pl.BlockSpec((D_MODEL, D_IN), lambda i, j: (0, 0), pipeline_mode=pl.Buffered(1))],
        out_specs=(*qkv_specs, tok(D_GQ), tok(D_HB), tok(D_ZG), tok(D_B),
                   pl.BlockSpec((1, t, 2 * N_GROUPS * D_A),
                                lambda i, j: (i, jnp.maximum(j - first_tail, 0), 0))),
        scratch_shapes=[pltpu.VMEM((IN_CHUNK // LANES, t, LANES), F32)],
        compiler_params=pltpu.CompilerParams(
            dimension_semantics=("arbitrary", "arbitrary"), vmem_limit_bytes=VMEM_LIMIT),
        name="inproj",
    )(x, gain, w_bf)


def _dilated_block(q, kk, vv, neg, masks):
    qs = _stack_heads(q * jnp.asarray(ATT_SCALE, BF16), masks)
    s = lax.dot_general(qs, kk, NT_DIMS, preferred_element_type=F32) + neg
    mx = jnp.max(s, axis=-1, keepdims=True)
    p = jnp.exp(s - mx)
    l = jnp.sum(p, axis=-1, keepdims=True)
    o_all = jnp.dot(p.astype(BF16), vv, preferred_element_type=F32)
    l_e = _expand_cols([l[h * QBLK:(h + 1) * QBLK] for h in range(A_HEADS)], masks)
    mx_e = _expand_cols([mx[h * QBLK:(h + 1) * QBLK] for h in range(A_HEADS)], masks)
    return _unstack_heads(o_all, masks, QBLK) / l_e, mx_e + jnp.log(l_e)


DIL_STEP = 2048
OG_SLABS = 2 * D_A // LANES


def _make_dilated_kernel(dil):
    nblk = DIL_STEP // (dil * QBLK)

    def body(prev_ref, cur_ref, out_ref):
        first_step = pl.program_id(1) == 0
        masks = _head_masks(D_A, A_HEAD_DIM, A_HEADS)
        qi = lax.broadcasted_iota(jnp.int32, (A_HEADS * QBLK, 2 * QBLK), 0) % QBLK
        kj = lax.broadcasted_iota(jnp.int32, (A_HEADS * QBLK, 2 * QBLK), 1)
        band = (kj >= qi) & (kj <= qi + QBLK)
        neg_band = jnp.where(band, 0.0, -jnp.inf)
        neg_first = jnp.where(first_step, jnp.where(band & (kj >= QBLK), 0.0, -jnp.inf), neg_band)

        for r in range(dil):
            for jb in range(nblk):
                rows = slice(jb * QBLK, (jb + 1) * QBLK)
                q = cur_ref[0, r, rows, 0:D_A]
                if jb == 0:
                    kv = [jnp.concatenate([prev_ref[0, r, :, c * D_A:(c + 1) * D_A],
                                           cur_ref[0, r, rows, c * D_A:(c + 1) * D_A]], axis=0) for c in (1, 2)]
                else:
                    both = slice((jb - 1) * QBLK, (jb + 1) * QBLK)
                    kv = [cur_ref[0, r, both, c * D_A:(c + 1) * D_A] for c in (1, 2)]
                o, lse = _dilated_block(q, kv[0], kv[1], neg_first if jb == 0 else neg_band, masks)
                res = (o, lse)
                for k in range(OG_SLABS):
                    src = res[k // 2][:, (k % 2) * LANES:(k % 2 + 1) * LANES]
                    if dil == 1:
                        out_ref[0, k, rows, :] = src
                    else:
                        out_ref[0, k, pl.ds(jb * QBLK * dil + r, QBLK, stride=dil), :] = src
    return body


def _dilated_call(qkv, dil, g):
    b, _, n, _ = qkv.shape
    rows = DIL_STEP // dil
    nb = n // rows
    return pl.pallas_call(
        _make_dilated_kernel(dil),
        out_shape=jax.ShapeDtypeStruct((b, OG_SLABS, n * dil, LANES), F32),
        grid=(b, nb),
        in_specs=[pl.BlockSpec((1, dil, QBLK, D_QKV),
                               lambda i, u: (i, 0, jnp.maximum(u * (rows // QBLK) - 1, 0), 0)),
                  pl.BlockSpec((1, dil, rows, D_QKV), lambda i, u: (i, 0, u, 0))],
        out_specs=pl.BlockSpec((1, OG_SLABS, DIL_STEP, LANES), lambda i, u: (i, 0, u, 0)),
        compiler_params=pltpu.CompilerParams(
            dimension_semantics=("arbitrary", "arbitrary"), vmem_limit_bytes=VMEM_LIMIT),
        name=f"dilated_attn_g{g}",
    )(qkv, qkv)


def _lower_bound(l0, l1):
    m = jnp.maximum(l0, l1)
    e0, e1 = jnp.exp(l0 - m), jnp.exp(l1 - m)
    return e0 / (e0 + e1)


def _head_rms(ob, gain):
    parts = []
    for h in range(HG_HEADS):
        sl = slice(h * HG_DIM, (h + 1) * HG_DIM)
        parts.append(_rms(ob[:, sl], gain[:, sl]))
    return jnp.concatenate(parts, axis=1)


def _merge_out(x, ua, ub, um, zg0, zg1, zg2, wa_ref, wb_ref, wm_ref, wo_ref, nf):
    merged = (_sigmoid(zg0) * jnp.dot(ua.astype(BF16), wa_ref[...], preferred_element_type=F32)
              + _sigmoid(zg1) * jnp.dot(ub.astype(BF16), wb_ref[...], preferred_element_type=F32)
              + _sigmoid(zg2) * jnp.dot(um.astype(BF16), wm_ref[...], preferred_element_type=F32))
    y = x + jnp.dot(merged.astype(BF16), wo_ref[...], preferred_element_type=F32)
    return _rms(y, nf)


TAIL_TILE = 256


def _cumsum_rows(g, tri):
    g1 = g.astype(BF16)
    g2 = (g - g1.astype(F32)).astype(BF16)
    return jnp.dot(tri, g1, preferred_element_type=F32) + jnp.dot(tri, g2, preferred_element_type=F32)


def _hgrn_tile(fb, qb, v, lb, st_ref):
    t = fb.shape[0]
    c, sub = HG_CHUNK, HG_SUB
    n_sub = c // sub
    f = lb + (1.0 - lb) * _sigmoid(fb)
    kk = 1.0 - f
    row = lax.broadcasted_iota(jnp.int32, (t, t), 0)
    col = lax.broadcasted_iota(jnp.int32, (t, t), 1)
    same_chunk = (row // c) == (col // c)
    gcs = _cumsum_rows(jnp.log(f), (same_chunk & (col <= row)).astype(BF16))
    q = _silu(qb.astype(F32))

    loc = lax.broadcasted_iota(jnp.int32, (c, 1), 0)
    qt_c, kt_ci, qe_c, kd_c, dec_c = [], [[] for _ in range(n_sub)], [], [], []
    for ci in range(t // c):
        rows = slice(ci * c, (ci + 1) * c)
        g_c, q_c, k_c = gcs[rows], q[rows], kk[rows]
        refs = [jnp.zeros((1, D_B), F32)] + [g_c[i * sub - 1:i * sub] for i in range(1, n_sub)]
        own = refs[n_sub - 1]
        for i in range(n_sub - 2, -1, -1):
            own = jnp.where(loc < (i + 1) * sub, refs[i], own)
        qt_c.append((q_c * jnp.exp(g_c - own)).astype(BF16))
        for i in range(n_sub):
            seen = loc < (i + 1) * sub
            kt_ci[i].append(jnp.where(seen, k_c * jnp.exp(jnp.where(seen, refs[i] - g_c, 0.0)), 0.0).astype(BF16))
        g_last = g_c[c - 1:c]
        qe_c.append((q_c * jnp.exp(g_c)).astype(BF16))
        kd_c.append((k_c * jnp.exp(g_last - g_c)).astype(BF16))
        dec_c.append(jnp.exp(g_last))
    qt = jnp.concatenate(qt_c, axis=0)
    kts = [jnp.concatenate(parts, axis=0) for parts in kt_ci]
    sub_of_row = (lax.broadcasted_iota(jnp.int32, (t, 1), 0) % c) // sub
    causal = same_chunk & (col <= row)

    outs = []
    for h in range(HG_HEADS):
        sl = slice(h * HG_DIM, (h + 1) * HG_DIM)
        zero = jnp.zeros((t, HG_DIM), BF16)
        q_big = jnp.concatenate([jnp.where(sub_of_row == i, qt[:, sl], zero) for i in range(n_sub)], axis=1)
        k_big = jnp.concatenate([kts[i][:, sl] for i in range(n_sub)], axis=1)
        att = lax.dot_general(q_big, k_big, NT_DIMS, preferred_element_type=F32)
        att = jnp.where(causal, att, 0.0).astype(BF16)
        o_intra = jnp.dot(att, v[:, sl], preferred_element_type=F32)
        st = st_ref[h]
        inter = []
        for ci in range(t // c):
            rows = slice(ci * c, (ci + 1) * c)
            inter.append(lax.dot_general(qe_c[ci][:, sl], st.astype(BF16), NT_DIMS, preferred_element_type=F32))
            st = st * dec_c[ci][:, sl] + lax.dot_general(
                v[rows, sl], kd_c[ci][:, sl], TN_DIMS, preferred_element_type=F32)
        st_ref[h] = st
        outs.append(o_intra + jnp.concatenate(inter, axis=0))
    return jnp.concatenate(outs, axis=1)


def _prompt_tail_kernel(x_ref, gq_ref, hb_ref, zg_ref, fb_ref, o1_ref, o2_ref, o3_ref, mkv_ref, lbl_ref,
                        nh_ref, nf_ref, wa_ref, wb_ref, wm_ref, wo_ref,
                        zr_ref, c1_ref, c2_ref, c3_ref, cm_ref, sst_ref, lblt_ref, nh4_ref,
                        y_ref, hst_ref, u_ref, nst_ref, st_ref):
    j = pl.program_id(1)
    t = TAIL_TILE

    @pl.when(j == 0)
    def _():
        st_ref[...] = jnp.zeros_like(st_ref)

    _sample_rows(zr_ref, c1_ref, c2_ref, c3_ref, cm_ref, sst_ref, lblt_ref, nh4_ref, u_ref, nst_ref)

    os_ = [jnp.concatenate([r[0, 0], r[0, 1]], axis=1) for r in (o1_ref, o2_ref, o3_ref)]
    ls_ = [jnp.concatenate([r[0, 2], r[0, 3]], axis=1) for r in (o1_ref, o2_ref, o3_ref)]
    mx = jnp.maximum(jnp.maximum(ls_[0], ls_[1]), ls_[2])
    es_ = [jnp.exp(l - mx) for l in ls_]
    oa = (es_[0] * os_[0] + es_[1] * os_[1] + es_[2] * os_[2]) / (es_[0] + es_[1] + es_[2])
    ua = oa * _silu(gq_ref[0, :, 0:D_A].astype(F32))

    masks = _head_masks(D_M, A_HEAD_DIM, A_HEADS)
    qm = gq_ref[0, :, D_A:2 * D_A] * jnp.asarray(ATT_SCALE, BF16)
    qs = _stack_heads(qm, masks)
    mk = mkv_ref[0, :, 0:D_M].astype(BF16)
    mv = mkv_ref[0, :, D_M:2 * D_M].astype(BF16)
    s = lax.dot_general(qs, mk, NT_DIMS, preferred_element_type=F32)
    p = jnp.exp(s - jnp.max(s, axis=-1, keepdims=True))
    l = jnp.sum(p, axis=-1, keepdims=True)
    om = _unstack_heads(jnp.dot(p.astype(BF16), mv, preferred_element_type=F32) / l, masks, t)
    um = om * _silu(gq_ref[0, :, 2 * D_A:3 * D_A].astype(F32))

    lb = _lower_bound(lbl_ref[0:1], lbl_ref[1:2])
    ob = _hgrn_tile(fb_ref[0], hb_ref[0, :, 0:D_B], hb_ref[0, :, D_B:2 * D_B], lb, st_ref)
    ub = _head_rms(ob, nh_ref[...]) * _silu(hb_ref[0, :, 2 * D_B:3 * D_B].astype(F32))

    y_ref[0] = _merge_out(x_ref[0], ua, ub, um, zg_ref[0, :, 0:D_MODEL].astype(F32),
                          zg_ref[0, :, D_MODEL:2 * D_MODEL].astype(F32),
                          zg_ref[0, :, 2 * D_MODEL:3 * D_MODEL].astype(F32),
                          wa_ref, wb_ref, wm_ref, wo_ref, nf_ref[...])

    @pl.when(j == pl.num_programs(1) - 1)
    def _():
        for h in range(HG_HEADS):
            hst_ref[0, h] = st_ref[h].T


def _prompt_tail_call(x, gq, hb, zg, fb, o1, o2, o3, mkv, lbl, nh, nf, wa, wb, wm, wo,
                      zs, c1, c2, c3, cm, sst):
    b, s, _ = x.shape
    t = TAIL_TILE
    nt = s // t
    db = zs.shape[0]
    bt = S_TILE
    assert b * nt * bt == db, "sample rows must split evenly over the prompt grid"
    zr = zs.reshape(db, Z_ROWS, LANES)
    caches = [_stored_view(c) for c in (c1, c2, c3, cm)]
    stv = sst.reshape(db, HG_HEADS * HG_DIM, HG_DIM)

    def tok(width):
        return pl.BlockSpec((1, t, width), lambda i, j: (i, j, 0))

    def const(shape):
        return pl.BlockSpec(shape, lambda i, j: (0,) * len(shape), pipeline_mode=pl.Buffered(1))

    def rows(a):
        return pl.BlockSpec((bt,) + a.shape[1:], lambda i, j: (i * nt + j, 0, 0))

    y, hst, u, nst = pl.pallas_call(
        _prompt_tail_kernel,
        out_shape=(jax.ShapeDtypeStruct((b, s, D_MODEL), F32),
                   jax.ShapeDtypeStruct((b, HG_HEADS, HG_DIM, HG_DIM), F32),
                   jax.ShapeDtypeStruct((db, U_ROWS, LANES), F32),
                   jax.ShapeDtypeStruct(stv.shape, F32)),
        grid=(b, nt),
        in_specs=[tok(D_MODEL), tok(D_GQ), tok(D_HB), tok(D_ZG), tok(D_B),
                  *[pl.BlockSpec((1, OG_SLABS, t, LANES), lambda i, j: (i, 0, j, 0))] * N_GROUPS,
                  pl.BlockSpec((1, MEM_LEN, 2 * D_M), lambda i, j: (i, 0, 0)),
                  const((2, D_B)), const((1, D_B)), const((1, D_MODEL)),
                  const((D_A, D_MODEL)), const((D_B, D_MODEL)), const((D_M, D_MODEL)),
                  const((D_MODEL, D_MODEL)),
                  rows(zr), *[rows(c) for c in caches], rows(stv),
                  const((D_B, 2)), const((HG_HEADS, HG_DIM))],
        out_specs=(tok(D_MODEL),
                   pl.BlockSpec((1, HG_HEADS, HG_DIM, HG_DIM), lambda i, j: (i, 0, 0, 0)),
                   pl.BlockSpec((bt, U_ROWS, LANES), lambda i, j: (i * nt + j, 0, 0)),
                   rows(stv)),
        scratch_shapes=[pltpu.VMEM((HG_HEADS, HG_DIM, HG_DIM), F32)],
        compiler_params=pltpu.CompilerParams(
            dimension_semantics=("arbitrary", "arbitrary"), vmem_limit_bytes=VMEM_LIMIT),
        name="prompt_tail",
    )(x, gq, hb, zg, fb, o1, o2, o3, mkv, lbl, nh, nf, wa, wb, wm, wo,
      zr, *caches, stv, lbl.T, nh.reshape(HG_HEADS, HG_DIM))
    return y, hst, u.reshape(db, U_ROWS * LANES), nst


S_COLS = 1024


def _sample_inproj_kernel(x_ref, gain_ref, w_ref, z_ref):
    h = _rms(x_ref[...], gain_ref[...]).astype(BF16)
    z_ref[...] = jnp.dot(h, w_ref[...], preferred_element_type=F32)


def _sample_inproj_call(x, gain, w_bf):
    db = x.shape[0]
    return pl.pallas_call(
        _sample_inproj_kernel,
        out_shape=jax.ShapeDtypeStruct((db, D_IN), F32),
        grid=(D_IN // S_COLS,),
        in_specs=[pl.BlockSpec((db, D_MODEL), lambda c: (0, 0)),
                  pl.BlockSpec((1, D_MODEL), lambda c: (0, 0)),
                  pl.BlockSpec((D_MODEL, S_COLS), lambda c: (0, c))],
        out_specs=pl.BlockSpec((db, S_COLS), lambda c: (0, c)),
        name="sample_inproj",
    )(x, gain, w_bf)


S_TILE = 2
Z_ROWS = D_IN // LANES
U_ROWS = 8


def _sample_rows(zr_ref, c1_ref, c2_ref, c3_ref, cm_ref, st_ref, lblt_ref, nh_ref, u_ref, nst_ref):
    bt = S_TILE
    zt = jnp.concatenate([zr_ref[b] for b in range(bt)], axis=0).T

    def col(b, off, n=HG_DIM):
        j, l = divmod(off, LANES)
        return zt[l:l + n, Z_ROWS * b + j:Z_ROWS * b + j + 1]

    lbt = lblt_ref[...]
    lb_col = _lower_bound(lbt[:, 0:1], lbt[:, 1:2])
    nh = nh_ref[...]

    def col2(b, off):
        return jnp.concatenate([col(b, off), col(b, off + LANES)], axis=0)

    def per_head(x):
        return jnp.concatenate([jnp.sum(x[h * A_HEAD_DIM:(h + 1) * A_HEAD_DIM], axis=0, keepdims=True)
                                for h in range(A_HEADS)], axis=0)

    def spread(x):
        return jnp.concatenate([jnp.broadcast_to(x[h:h + 1], (A_HEAD_DIM, 1)) for h in range(A_HEADS)], axis=0)

    def attend(b, srcs):
        parts, new_scores, qs = [], [], []
        for c_ref, dil, q_off, k_off, _ in srcs:
            q = col2(b, q_off) * ATT_SCALE
            qs.append(q)
            s = per_head(c_ref[b, 0:D_A, :] * q)
            if dil > 1:
                lane = lax.broadcasted_iota(jnp.int32, s.shape, 1)
                s = jnp.where(lane % dil == 0, s, -jnp.inf)
            parts.append(s)
            if k_off is not None:
                new_scores.append(per_head(col2(b, k_off) * q))
        if new_scores:
            lane = lax.broadcasted_iota(jnp.int32, (A_HEADS, LANES), 1)
            slab = jnp.full((A_HEADS, LANES), -jnp.inf, F32)
            for k, sn in enumerate(new_scores):
                slab = jnp.where(lane == k, sn, slab)
            parts.append(slab)
        s_all = jnp.concatenate(parts, axis=1)
        p_all = jnp.exp(s_all - jnp.max(s_all, axis=1, keepdims=True))
        den = jnp.sum(p_all, axis=1, keepdims=True)
        acc = [jnp.zeros((A_HEAD_DIM, LANES), F32) for _ in range(A_HEADS)]
        lo = 0
        for c_ref, _, _, _, _ in srcs:
            length = c_ref.shape[2]
            for k in range(length // LANES):
                ls = slice(k * LANES, (k + 1) * LANES)
                for h in range(A_HEADS):
                    vt = c_ref[b, D_A + h * A_HEAD_DIM:D_A + (h + 1) * A_HEAD_DIM, ls]
                    acc[h] = acc[h] + vt * p_all[h:h + 1, lo + k * LANES:lo + (k + 1) * LANES]
            lo += length
        num = jnp.sum(jnp.concatenate(acc, axis=0), axis=1, keepdims=True)
        k = 0
        for _, _, _, k_off, v_off in srcs:
            if k_off is not None:
                num = num + spread(p_all[:, lo + k:lo + k + 1]) * col2(b, v_off)
                k += 1
        return num / spread(den)

    window = [(c_ref, dil, OFF_QA + g * D_A, OFF_KA + g * D_A, OFF_VA + g * D_A)
              for g, (c_ref, (_, dil)) in enumerate(zip((c1_ref, c2_ref, c3_ref), WIN_GROUPS))]
    memory = [(cm_ref, 1, OFF_QM, None, None)]

    for b in range(bt):
        ua = attend(b, window) * _silu(col2(b, OFF_GA))
        um = attend(b, memory) * _silu(col2(b, OFF_GM))
        branch_cols = [ua[0:LANES], ua[LANES:2 * LANES], um[0:LANES], um[LANES:2 * LANES]]

        lane = lax.broadcasted_iota(jnp.int32, (LANES, LANES), 1)
        ut = jnp.zeros((LANES, LANES), F32)
        for k, cvec in enumerate(branch_cols):
            ut = ut + jnp.where(lane == k, cvec, 0.0)
        u_ref[b, 0:4, :] = ut.T[0:4, :]

        orows = []
        for h in range(HG_HEADS):
            sl = slice(h * HG_DIM, (h + 1) * HG_DIM)
            fcol = lb_col[sl] + (1.0 - lb_col[sl]) * _sigmoid(col(b, OFF_FB + h * HG_DIM))
            qcol = _silu(col(b, OFF_QB + h * HG_DIM))
            vrow = zr_ref[b, OFF_IB // LANES + h:OFF_IB // LANES + h + 1, :]
            sn = fcol * st_ref[b, sl, :] + (1.0 - fcol) * vrow
            nst_ref[b, sl, :] = sn
            orows.append(jnp.sum(sn * qcol, axis=0, keepdims=True))
        gate = zr_ref[b, OFF_GB // LANES:OFF_GB // LANES + HG_HEADS, :]
        u_ref[b, 4:4 + HG_HEADS, :] = _rms(jnp.concatenate(orows, axis=0), nh) * _silu(gate)


def _stored_view(cache):
    db, length = cache.shape[0], cache.shape[1]
    return jnp.transpose(cache, (0, 2, 3, 4, 1)).reshape(db, 2 * D_A, length)


def _sample_out_kernel(x_ref, u_ref, zg_ref, nf_ref, wa_ref, wb_ref, wm_ref, wo_ref, y_ref):
    u = u_ref[...]
    zg = zg_ref[...]
    y_ref[...] = _merge_out(x_ref[...], u[:, 0:D_A], u[:, D_A + D_M:], u[:, D_A:D_A + D_M],
                            zg[:, 0:D_MODEL], zg[:, D_MODEL:2 * D_MODEL], zg[:, 2 * D_MODEL:],
                            wa_ref, wb_ref, wm_ref, wo_ref, nf_ref[...])


def _sample_out_call(x, u, zg, nf, wa, wb, wm, wo):
    db = x.shape[0]

    def full(a):
        return pl.BlockSpec(a.shape, lambda i: (0,) * a.ndim)

    args = (x, u, zg, nf, wa, wb, wm, wo)
    return pl.pallas_call(
        _sample_out_kernel,
        out_shape=jax.ShapeDtypeStruct((db, D_MODEL), F32),
        grid=(1,),
        in_specs=[full(a) for a in args],
        out_specs=pl.BlockSpec((db, D_MODEL), lambda i: (0, 0)),
        compiler_params=pltpu.CompilerParams(vmem_limit_bytes=VMEM_LIMIT),
        name="sample_out",
    )(*args)


def _cache_rows(kv, length):
    outs = []
    for g in range(N_GROUPS):
        k = kv[:, -length[g]:, g * D_A:(g + 1) * D_A]
        v = kv[:, -length[g]:, (N_GROUPS + g) * D_A:(N_GROUPS + g + 1) * D_A]
        outs.append(jnp.stack([k, v], axis=2).reshape(kv.shape[0], length[g], 2, A_HEADS, A_HEAD_DIM)[None])
    return outs


def kernel(x_prompt, x_sample, mem_prompt, cache_w1_kv, cache_w2_kv, cache_w3_kv, cache_mem_kv, state_hgrn,
           norm_in, w_in, lb_logits, norm_hgrn, norm_mem, w_mem_kv, w_branch_a, w_branch_b, w_branch_m,
           w_out, norm_final):
    b, s, _ = x_prompt.shape
    db = x_sample.shape[0]
    w_in_bf = w_in[0].astype(BF16)
    wa, wb, wm, wo = (w[0].astype(BF16) for w in (w_branch_a, w_branch_b, w_branch_m, w_out))
    gain_in = norm_in[0][None]
    nh = norm_hgrn[0][None]
    nf = norm_final[None]
    lbl = lb_logits.astype(F32)

    xs = x_sample.reshape(db, D_MODEL)
    zs = _sample_inproj_call(xs, gain_in, w_in_bf)
    mkv = _mem_kv_call(mem_prompt, norm_mem[0][None], w_mem_kv[0].astype(BF16))
    qkv0, qkv1, qkv2, gq, hb, zg, fb, kv_tail = _inproj_call(x_prompt, gain_in, w_in_bf)
    og = [_dilated_call(qkv, dil, g) for g, (qkv, (_, dil)) in enumerate(zip((qkv0, qkv1, qkv2), WIN_GROUPS))]

    y_prompt, hg_prompt, u, nst = _prompt_tail_call(
        x_prompt, gq, hb, zg, fb, og[0], og[1], og[2], mkv, lbl, nh, nf, wa, wb, wm, wo,
        zs, cache_w1_kv[0], cache_w2_kv[0], cache_w3_kv[0], cache_mem_kv[0], state_hgrn[0])
    pw = _cache_rows(kv_tail, [min(w, s) for w, _ in WIN_GROUPS])
    new_mem = mkv.reshape(1, b, MEM_LEN, 2, A_HEADS, A_HEAD_DIM)

    y_sample = _sample_out_call(xs, u, zs[:, OFF_ZG:], nf, wa, wb, wm, wo).reshape(db, 1, D_MODEL)
    sw = _cache_rows(zs[:, None, OFF_KA:OFF_GA], [1, 1, 1])
    new_hg_sample = nst.reshape(1, db, HG_HEADS, HG_DIM, HG_DIM)

    return (y_prompt, y_sample, pw[0], pw[1], pw[2], new_mem, hg_prompt[None],
            sw[0], sw[1], sw[2], new_hg_sample)
```

```python
import jax
import jax.numpy as jnp
from jax import lax
from jax.experimental import pallas as pl
from jax.experimental.pallas import tpu as pltpu

F32 = jnp.float32
BF16 = jnp.bfloat16

D_MODEL = 1024
WIN_GROUPS = ((128, 1), (512, 4), (2048, 16))
N_GROUPS = 3
A_HEADS = 4
A_HEAD_DIM = 64
D_A = 256
QBLK = 128
HG_HEADS = 4
HG_DIM = 128
D_B = 512
HG_CHUNK = 64
HG_SUB = 16
MEM_LEN = 256
D_M = 256
EPS = 1e-6
D_IN = 8192
LANES = 128
OFF_QA, OFF_KA, OFF_VA, OFF_GA = 0, 768, 1536, 2304
OFF_QB, OFF_FB, OFF_IB, OFF_GB = 2560, 3072, 3584, 4096
OFF_QM, OFF_GM, OFF_ZG = 4608, 4864, 5120
ATT_SCALE = A_HEAD_DIM ** -0.5

VMEM_LIMIT = 56 * 1024 * 1024

NT_DIMS = (((1,), (1,)), ((), ()))
TN_DIMS = (((0,), (0,)), ((), ()))


def _sigmoid(x):
    return 0.5 * jnp.tanh(0.5 * x) + 0.5


def _silu(x):
    h = 0.5 * x
    return h * jnp.tanh(h) + h


def _rms(x, gain):
    return x * lax.rsqrt(jnp.mean(x * x, axis=-1, keepdims=True) + EPS) * gain


def _head_masks(width, head_dim, n_heads):
    lane = lax.broadcasted_iota(jnp.int32, (1, width), 1)
    return [(lane >= h * head_dim) & (lane < (h + 1) * head_dim) for h in range(n_heads)]


def _stack_heads(q, masks):
    zero = jnp.zeros_like(q)
    return jnp.concatenate([jnp.where(m, q, zero) for m in masks], axis=0)


def _unstack_heads(o_all, masks, t):
    n = len(masks)
    out = o_all[(n - 1) * t:n * t]
    for h in range(n - 2, -1, -1):
        out = jnp.where(masks[h], o_all[h * t:(h + 1) * t], out)
    return out


def _expand_cols(cols, masks):
    out = jnp.broadcast_to(cols[-1], (cols[-1].shape[0], masks[0].shape[1]))
    for h in range(len(masks) - 2, -1, -1):
        out = jnp.where(masks[h], cols[h], out)
    return out


def _mem_kv_kernel(mem_ref, gain_ref, w_ref, out_ref):
    hm = _rms(mem_ref[0], gain_ref[...]).astype(BF16)
    out_ref[0] = jnp.dot(hm, w_ref[...], preferred_element_type=F32)


def _mem_kv_call(mem, gain, w_bf):
    b = mem.shape[0]
    return pl.pallas_call(
        _mem_kv_kernel,
        out_shape=jax.ShapeDtypeStruct((b, MEM_LEN, 2 * D_M), F32),
        grid=(b,),
        in_specs=[pl.BlockSpec((1, MEM_LEN, D_MODEL), lambda i: (i, 0, 0)),
                  pl.BlockSpec((1, D_MODEL), lambda i: (0, 0)),
                  pl.BlockSpec((D_MODEL, 2 * D_M), lambda i: (0, 0))],
        out_specs=pl.BlockSpec((1, MEM_LEN, 2 * D_M), lambda i: (i, 0, 0)),
        name="mem_kv",
    )(mem, gain, w_bf)


IN_TILE = 256
IN_CHUNK = 256
KV_TAIL = 2048
D_QKV = 3 * D_A
D_GQ = 3 * D_A
D_HB = 3 * D_B
D_ZG = 3 * D_MODEL


def _inproj_kernel(x_ref, gain_ref, w_ref, zr_ref, c1_ref, c2_ref, c3_ref, cm_ref, sst_ref, lblt_ref, nh4_ref,
                   qkv0_ref, qkv1_ref, qkv2_ref, gq_ref, hb_ref, zg_ref, fb_ref, kv_ref, u_ref, nst_ref,
                   scr_ref):
    _sample_rows(zr_ref, c1_ref, c2_ref, c3_ref, cm_ref, sst_ref, lblt_ref, nh4_ref, u_ref, nst_ref)
    qkv_refs = (qkv0_ref, qkv1_ref, qkv2_ref)
    h = _rms(x_ref[0], gain_ref[...]).astype(BF16)
    t = IN_TILE
    for c in range(D_IN // IN_CHUNK):
        lo = c * IN_CHUNK
        zc = jnp.dot(h, w_ref[:, lo:lo + IN_CHUNK], preferred_element_type=F32)
        if lo < OFF_GA:
            sec, g = divmod(c, N_GROUPS)
            dil = WIN_GROUPS[g][1]
            dst = slice(sec * D_A, (sec + 1) * D_A)
            if dil == 1:
                qkv_refs[g][0, 0, :, dst] = zc.astype(BF16)
            else:
                for k in range(IN_CHUNK // LANES):
                    scr_ref[k] = zc[:, k * LANES:(k + 1) * LANES]
                for r in range(dil):
                    qkv_refs[g][0, r, :, dst] = jnp.concatenate(
                        [scr_ref[k, pl.ds(r, t // dil, stride=dil), :] for k in range(IN_CHUNK // LANES)],
                        axis=1).astype(BF16)
            if sec > 0:
                kv_ref[0, :, lo - OFF_KA:lo - OFF_KA + IN_CHUNK] = zc
        elif lo < OFF_QB:
            gq_ref[0, :, 0:D_A] = zc.astype(BF16)
        elif lo < OFF_FB:
            hb_ref[0, :, lo - OFF_QB:lo - OFF_QB + IN_CHUNK] = zc.astype(BF16)
        elif lo < OFF_IB:
            fb_ref[0, :, lo - OFF_FB:lo - OFF_FB + IN_CHUNK] = zc
        elif lo < OFF_QM:
            hb_ref[0, :, lo - OFF_IB + D_B:lo - OFF_IB + D_B + IN_CHUNK] = zc.astype(BF16)
        elif lo < OFF_ZG:
            gq_ref[0, :, lo - OFF_QM + D_A:lo - OFF_QM + D_A + IN_CHUNK] = zc.astype(BF16)
        else:
            zg_ref[0, :, lo - OFF_ZG:lo - OFF_ZG + IN_CHUNK] = zc.astype(BF16)


def _sample_specs(ops, nt, first_row):
    first_blk = first_row // S_TILE

    def rows(a):
        return pl.BlockSpec((S_TILE,) + a.shape[1:], lambda i, j: (first_blk + i * nt + j, 0, 0))

    def const(a):
        return pl.BlockSpec(a.shape, lambda i, j: (0, 0), pipeline_mode=pl.Buffered(1))

    return [rows(a) for a in ops[:6]] + [const(a) for a in ops[6:]]


def _inproj_call(x, gain, w_bf, sample_ops):
    b, s, _ = x.shape
    t = IN_TILE
    nt = s // t
    first_tail = (s - KV_TAIL) // t
    n_rows = b * nt * S_TILE

    def tok(width):
        return pl.BlockSpec((1, t, width), lambda i, j: (i, j, 0))

    def srow(shape):
        return pl.BlockSpec((S_TILE,) + shape, lambda i, j: (i * nt + j, 0, 0))

    qkv_shapes, qkv_specs = [], []
    for _, dil in WIN_GROUPS:
        qkv_shapes.append(jax.ShapeDtypeStruct((b, dil, s // dil, D_QKV), BF16))
        qkv_specs.append(pl.BlockSpec((1, dil, t // dil, D_QKV), lambda i, j: (i, 0, j, 0)))
    return pl.pallas_call(
        _inproj_kernel,
        out_shape=(*qkv_shapes,
                   jax.ShapeDtypeStruct((b, s, D_GQ), BF16),
                   jax.ShapeDtypeStruct((b, s, D_HB), BF16),
                   jax.ShapeDtypeStruct((b, s, D_ZG), BF16),
                   jax.ShapeDtypeStruct((b, s, D_B), F32),
                   jax.ShapeDtypeStruct((b, KV_TAIL, 2 * N_GROUPS * D_A), F32),
                   jax.ShapeDtypeStruct((n_rows, U_ROWS, LANES), F32),
                   jax.ShapeDtypeStruct((n_rows, HG_HEADS * HG_DIM, HG_DIM), F32)),
        grid=(b, nt),
        in_specs=[tok(D_MODEL),
                  pl.BlockSpec((1, D_MODEL), lambda i, j: (0, 0)),
                  pl.BlockSpec((D_MODEL, D_IN), lambda i, j: (0, 0), pipeline_mode=pl.Buffered(1)),
                  *_sample_specs(sample_ops, nt, 0)],
        out_specs=(*qkv_specs, tok(D_GQ), tok(D_HB), tok(D_ZG), tok(D_B),
                   pl.BlockSpec((1, t, 2 * N_GROUPS * D_A),
                                lambda i, j: (i, jnp.maximum(j - first_tail, 0), 0)),
                   srow((U_ROWS, LANES)), srow((HG_HEADS * HG_DIM, HG_DIM))),
        scratch_shapes=[pltpu.VMEM((IN_CHUNK // LANES, t, LANES), F32)],
        compiler_params=pltpu.CompilerParams(
            dimension_semantics=("arbitrary", "arbitrary"), vmem_limit_bytes=VMEM_LIMIT),
        name="inproj",
    )(x, gain, w_bf, *sample_ops)


def _dilated_block(q, kk, vv, neg, masks):
    qs = _stack_heads(q * jnp.asarray(ATT_SCALE, BF16), masks)
    s = lax.dot_general(qs, kk, NT_DIMS, preferred_element_type=F32) + neg
    mx = jnp.max(s, axis=-1, keepdims=True)
    p = jnp.exp(s - mx)
    l = jnp.sum(p, axis=-1, keepdims=True)
    o_all = jnp.dot(p.astype(BF16), vv, preferred_element_type=F32)
    l_e = _expand_cols([l[h * QBLK:(h + 1) * QBLK] for h in range(A_HEADS)], masks)
    mx_e = _expand_cols([mx[h * QBLK:(h + 1) * QBLK] for h in range(A_HEADS)], masks)
    return _unstack_heads(o_all, masks, QBLK) / l_e, mx_e + jnp.log(l_e)


DIL_STEP = 2048
OG_SLABS = 2 * D_A // LANES


def _make_dilated_kernel(dil):
    nblk = DIL_STEP // (dil * QBLK)

    def body(prev_ref, cur_ref, out_ref):
        first_step = pl.program_id(1) == 0
        masks = _head_masks(D_A, A_HEAD_DIM, A_HEADS)
        qi = lax.broadcasted_iota(jnp.int32, (A_HEADS * QBLK, 2 * QBLK), 0) % QBLK
        kj = lax.broadcasted_iota(jnp.int32, (A_HEADS * QBLK, 2 * QBLK), 1)
        band = (kj >= qi) & (kj <= qi + QBLK)
        neg_band = jnp.where(band, 0.0, -jnp.inf)
        neg_first = jnp.where(first_step, jnp.where(band & (kj >= QBLK), 0.0, -jnp.inf), neg_band)

        for r in range(dil):
            for jb in range(nblk):
                rows = slice(jb * QBLK, (jb + 1) * QBLK)
                q = cur_ref[0, r, rows, 0:D_A]
                if jb == 0:
                    kv = [jnp.concatenate([prev_ref[0, r, :, c * D_A:(c + 1) * D_A],
                                           cur_ref[0, r, rows, c * D_A:(c + 1) * D_A]], axis=0) for c in (1, 2)]
                else:
                    both = slice((jb - 1) * QBLK, (jb + 1) * QBLK)
                    kv = [cur_ref[0, r, both, c * D_A:(c + 1) * D_A] for c in (1, 2)]
                o, lse = _dilated_block(q, kv[0], kv[1], neg_first if jb == 0 else neg_band, masks)
                res = (o, lse)
                for k in range(OG_SLABS):
                    src = res[k // 2][:, (k % 2) * LANES:(k % 2 + 1) * LANES]
                    if dil == 1:
                        out_ref[0, k, rows, :] = src
                    else:
                        out_ref[0, k, pl.ds(jb * QBLK * dil + r, QBLK, stride=dil), :] = src
    return body


def _dilated_call(qkv, dil, g):
    b, _, n, _ = qkv.shape
    rows = DIL_STEP // dil
    nb = n // rows
    return pl.pallas_call(
        _make_dilated_kernel(dil),
        out_shape=jax.ShapeDtypeStruct((b, OG_SLABS, n * dil, LANES), F32),
        grid=(b, nb),
        in_specs=[pl.BlockSpec((1, dil, QBLK, D_QKV),
                               lambda i, u: (i, 0, jnp.maximum(u * (rows // QBLK) - 1, 0), 0)),
                  pl.BlockSpec((1, dil, rows, D_QKV), lambda i, u: (i, 0, u, 0))],
        out_specs=pl.BlockSpec((1, OG_SLABS, DIL_STEP, LANES), lambda i, u: (i, 0, u, 0)),
        compiler_params=pltpu.CompilerParams(
            dimension_semantics=("arbitrary", "arbitrary"), vmem_limit_bytes=VMEM_LIMIT),
        name=f"dilated_attn_g{g}",
    )(qkv, qkv)


def _lower_bound(l0, l1):
    m = jnp.maximum(l0, l1)
    e0, e1 = jnp.exp(l0 - m), jnp.exp(l1 - m)
    return e0 / (e0 + e1)


def _head_rms(ob, gain):
    parts = []
    for h in range(HG_HEADS):
        sl = slice(h * HG_DIM, (h + 1) * HG_DIM)
        parts.append(_rms(ob[:, sl], gain[:, sl]))
    return jnp.concatenate(parts, axis=1)


def _merge_out(x, ua, ub, um, zg0, zg1, zg2, wa_ref, wb_ref, wm_ref, wo_ref, nf):
    merged = (_sigmoid(zg0) * jnp.dot(ua.astype(BF16), wa_ref[...], preferred_element_type=F32)
              + _sigmoid(zg1) * jnp.dot(ub.astype(BF16), wb_ref[...], preferred_element_type=F32)
              + _sigmoid(zg2) * jnp.dot(um.astype(BF16), wm_ref[...], preferred_element_type=F32))
    y = x + jnp.dot(merged.astype(BF16), wo_ref[...], preferred_element_type=F32)
    return _rms(y, nf)


TAIL_TILE = 256


def _cumsum_rows(g, tri):
    g1 = g.astype(BF16)
    g2 = (g - g1.astype(F32)).astype(BF16)
    return jnp.dot(tri, g1, preferred_element_type=F32) + jnp.dot(tri, g2, preferred_element_type=F32)


def _hgrn_tile(fb, qb, v, lb, st_ref):
    t = fb.shape[0]
    c, sub = HG_CHUNK, HG_SUB
    n_sub = c // sub
    f = lb + (1.0 - lb) * _sigmoid(fb)
    kk = 1.0 - f
    row = lax.broadcasted_iota(jnp.int32, (t, t), 0)
    col = lax.broadcasted_iota(jnp.int32, (t, t), 1)
    same_chunk = (row // c) == (col // c)
    gcs = _cumsum_rows(jnp.log(f), (same_chunk & (col <= row)).astype(BF16))
    q = _silu(qb.astype(F32))

    loc = lax.broadcasted_iota(jnp.int32, (c, 1), 0)
    qt_c, kt_ci, qe_c, kd_c, dec_c = [], [[] for _ in range(n_sub)], [], [], []
    for ci in range(t // c):
        rows = slice(ci * c, (ci + 1) * c)
        g_c, q_c, k_c = gcs[rows], q[rows], kk[rows]
        refs = [jnp.zeros((1, D_B), F32)] + [g_c[i * sub - 1:i * sub] for i in range(1, n_sub)]
        own = refs[n_sub - 1]
        for i in range(n_sub - 2, -1, -1):
            own = jnp.where(loc < (i + 1) * sub, refs[i], own)
        qt_c.append((q_c * jnp.exp(g_c - own)).astype(BF16))
        for i in range(n_sub):
            seen = loc < (i + 1) * sub
            kt_ci[i].append(jnp.where(seen, k_c * jnp.exp(jnp.where(seen, refs[i] - g_c, 0.0)), 0.0).astype(BF16))
        g_last = g_c[c - 1:c]
        qe_c.append((q_c * jnp.exp(g_c)).astype(BF16))
        kd_c.append((k_c * jnp.exp(g_last - g_c)).astype(BF16))
        dec_c.append(jnp.exp(g_last))
    qt = jnp.concatenate(qt_c, axis=0)
    kts = [jnp.concatenate(parts, axis=0) for parts in kt_ci]
    sub_of_row = (lax.broadcasted_iota(jnp.int32, (t, 1), 0) % c) // sub
    causal = same_chunk & (col <= row)

    outs = []
    for h in range(HG_HEADS):
        sl = slice(h * HG_DIM, (h + 1) * HG_DIM)
        zero = jnp.zeros((t, HG_DIM), BF16)
        q_big = jnp.concatenate([jnp.where(sub_of_row == i, qt[:, sl], zero) for i in range(n_sub)], axis=1)
        k_big = jnp.concatenate([kts[i][:, sl] for i in range(n_sub)], axis=1)
        att = lax.dot_general(q_big, k_big, NT_DIMS, preferred_element_type=F32)
        att = jnp.where(causal, att, 0.0).astype(BF16)
        o_intra = jnp.dot(att, v[:, sl], preferred_element_type=F32)
        st = st_ref[h]
        inter = []
        for ci in range(t // c):
            rows = slice(ci * c, (ci + 1) * c)
            inter.append(lax.dot_general(qe_c[ci][:, sl], st.astype(BF16), NT_DIMS, preferred_element_type=F32))
            st = st * dec_c[ci][:, sl] + lax.dot_general(
                v[rows, sl], kd_c[ci][:, sl], TN_DIMS, preferred_element_type=F32)
        st_ref[h] = st
        outs.append(o_intra + jnp.concatenate(inter, axis=0))
    return jnp.concatenate(outs, axis=1)


def _prompt_tail_kernel(x_ref, gq_ref, hb_ref, zg_ref, fb_ref, o1_ref, o2_ref, o3_ref, mkv_ref, lbl_ref,
                        nh_ref, nf_ref, wa_ref, wb_ref, wm_ref, wo_ref,
                        zr_ref, c1_ref, c2_ref, c3_ref, cm_ref, sst_ref, lblt_ref, nh4_ref, ua_ref, nsta_ref,
                        y_ref, hst_ref, u_ref, nst_ref, st_ref):
    j = pl.program_id(1)
    t = TAIL_TILE

    @pl.when(j == 0)
    def _():
        st_ref[...] = jnp.zeros_like(st_ref)

    u_ref[0] = ua_ref[...]
    nst_ref[0] = nsta_ref[...]
    _sample_rows(zr_ref, c1_ref, c2_ref, c3_ref, cm_ref, sst_ref, lblt_ref, nh4_ref, u_ref.at[1], nst_ref.at[1])

    os_ = [jnp.concatenate([r[0, 0], r[0, 1]], axis=1) for r in (o1_ref, o2_ref, o3_ref)]
    ls_ = [jnp.concatenate([r[0, 2], r[0, 3]], axis=1) for r in (o1_ref, o2_ref, o3_ref)]
    mx = jnp.maximum(jnp.maximum(ls_[0], ls_[1]), ls_[2])
    es_ = [jnp.exp(l - mx) for l in ls_]
    oa = (es_[0] * os_[0] + es_[1] * os_[1] + es_[2] * os_[2]) / (es_[0] + es_[1] + es_[2])
    ua = oa * _silu(gq_ref[0, :, 0:D_A].astype(F32))

    masks = _head_masks(D_M, A_HEAD_DIM, A_HEADS)
    qm = gq_ref[0, :, D_A:2 * D_A] * jnp.asarray(ATT_SCALE, BF16)
    qs = _stack_heads(qm, masks)
    mk = mkv_ref[0, :, 0:D_M].astype(BF16)
    mv = mkv_ref[0, :, D_M:2 * D_M].astype(BF16)
    s = lax.dot_general(qs, mk, NT_DIMS, preferred_element_type=F32)
    p = jnp.exp(s - jnp.max(s, axis=-1, keepdims=True))
    l = jnp.sum(p, axis=-1, keepdims=True)
    om = _unstack_heads(jnp.dot(p.astype(BF16), mv, preferred_element_type=F32) / l, masks, t)
    um = om * _silu(gq_ref[0, :, 2 * D_A:3 * D_A].astype(F32))

    lb = _lower_bound(lbl_ref[0:1], lbl_ref[1:2])
    ob = _hgrn_tile(fb_ref[0], hb_ref[0, :, 0:D_B], hb_ref[0, :, D_B:2 * D_B], lb, st_ref)
    ub = _head_rms(ob, nh_ref[...]) * _silu(hb_ref[0, :, 2 * D_B:3 * D_B].astype(F32))

    y_ref[0] = _merge_out(x_ref[0], ua, ub, um, zg_ref[0, :, 0:D_MODEL].astype(F32),
                          zg_ref[0, :, D_MODEL:2 * D_MODEL].astype(F32),
                          zg_ref[0, :, 2 * D_MODEL:3 * D_MODEL].astype(F32),
                          wa_ref, wb_ref, wm_ref, wo_ref, nf_ref[...])

    @pl.when(j == pl.num_programs(1) - 1)
    def _():
        for h in range(HG_HEADS):
            hst_ref[0, h] = st_ref[h].T


def _prompt_tail_call(x, gq, hb, zg, fb, o1, o2, o3, mkv, lbl, nh, nf, wa, wb, wm, wo,
                      sample_ops, u_first, nst_first):
    b, s, _ = x.shape
    t = TAIL_TILE
    nt = s // t
    db = sample_ops[0].shape[0]
    half = u_first.shape[0]
    assert half + b * nt * S_TILE == db and 2 * half == db, "sample rows must split evenly over both grids"

    def tok(width):
        return pl.BlockSpec((1, t, width), lambda i, j: (i, j, 0))

    def const(shape):
        return pl.BlockSpec(shape, lambda i, j: (0,) * len(shape), pipeline_mode=pl.Buffered(1))

    def first(a):
        return pl.BlockSpec((S_TILE,) + a.shape[1:], lambda i, j: (i * nt + j, 0, 0))

    def halves(shape):
        return pl.BlockSpec((2, S_TILE) + shape, lambda i, j: (0, i * nt + j, 0, 0))

    y, hst, u, nst = pl.pallas_call(
        _prompt_tail_kernel,
        out_shape=(jax.ShapeDtypeStruct((b, s, D_MODEL), F32),
                   jax.ShapeDtypeStruct((b, HG_HEADS, HG_DIM, HG_DIM), F32),
                   jax.ShapeDtypeStruct((2, half, U_ROWS, LANES), F32),
                   jax.ShapeDtypeStruct((2, half, HG_HEADS * HG_DIM, HG_DIM), F32)),
        grid=(b, nt),
        in_specs=[tok(D_MODEL), tok(D_GQ), tok(D_HB), tok(D_ZG), tok(D_B),
                  *[pl.BlockSpec((1, OG_SLABS, t, LANES), lambda i, j: (i, 0, j, 0))] * N_GROUPS,
                  pl.BlockSpec((1, MEM_LEN, 2 * D_M), lambda i, j: (i, 0, 0)),
                  const((2, D_B)), const((1, D_B)), const((1, D_MODEL)),
                  const((D_A, D_MODEL)), const((D_B, D_MODEL)), const((D_M, D_MODEL)),
                  const((D_MODEL, D_MODEL)),
                  *_sample_specs(sample_ops, nt, half), first(u_first), first(nst_first)],
        out_specs=(tok(D_MODEL),
                   pl.BlockSpec((1, HG_HEADS, HG_DIM, HG_DIM), lambda i, j: (i, 0, 0, 0)),
                   halves((U_ROWS, LANES)), halves((HG_HEADS * HG_DIM, HG_DIM))),
        scratch_shapes=[pltpu.VMEM((HG_HEADS, HG_DIM, HG_DIM), F32)],
        compiler_params=pltpu.CompilerParams(
            dimension_semantics=("arbitrary", "arbitrary"), vmem_limit_bytes=VMEM_LIMIT),
        name="prompt_tail",
    )(x, gq, hb, zg, fb, o1, o2, o3, mkv, lbl, nh, nf, wa, wb, wm, wo, *sample_ops, u_first, nst_first)
    return y, hst, u.reshape(db, U_ROWS * LANES), nst.reshape(db, HG_HEADS * HG_DIM, HG_DIM)


S_COLS = 1024


def _sample_inproj_kernel(x_ref, gain_ref, w_ref, z_ref):
    h = _rms(x_ref[...], gain_ref[...]).astype(BF16)
    z_ref[...] = jnp.dot(h, w_ref[...], preferred_element_type=F32)


def _sample_inproj_call(x, gain, w_bf):
    db = x.shape[0]
    return pl.pallas_call(
        _sample_inproj_kernel,
        out_shape=jax.ShapeDtypeStruct((db, D_IN), F32),
        grid=(D_IN // S_COLS,),
        in_specs=[pl.BlockSpec((db, D_MODEL), lambda c: (0, 0)),
                  pl.BlockSpec((1, D_MODEL), lambda c: (0, 0)),
                  pl.BlockSpec((D_MODEL, S_COLS), lambda c: (0, c))],
        out_specs=pl.BlockSpec((db, S_COLS), lambda c: (0, c)),
        name="sample_inproj",
    )(x, gain, w_bf)


S_TILE = 1
Z_ROWS = D_IN // LANES
U_ROWS = 8


def _sample_rows(zr_ref, c1_ref, c2_ref, c3_ref, cm_ref, st_ref, lblt_ref, nh_ref, u_ref, nst_ref):
    bt = S_TILE
    pad = [jnp.zeros((LANES - bt * Z_ROWS, LANES), F32)] if bt * Z_ROWS < LANES else []
    zt = jnp.concatenate([zr_ref[b] for b in range(bt)] + pad, axis=0).T

    def col(b, off, n=HG_DIM):
        j, l = divmod(off, LANES)
        return zt[l:l + n, Z_ROWS * b + j:Z_ROWS * b + j + 1]

    lbt = lblt_ref[...]
    lb_col = _lower_bound(lbt[:, 0:1], lbt[:, 1:2])
    nh = nh_ref[...]

    def col2(b, off):
        return jnp.concatenate([col(b, off), col(b, off + LANES)], axis=0)

    def per_head(x):
        return jnp.concatenate([jnp.sum(x[h * A_HEAD_DIM:(h + 1) * A_HEAD_DIM], axis=0, keepdims=True)
                                for h in range(A_HEADS)], axis=0)

    def spread(x):
        return jnp.concatenate([jnp.broadcast_to(x[h:h + 1], (A_HEAD_DIM, 1)) for h in range(A_HEADS)], axis=0)

    def attend(b, srcs):
        parts, new_scores, qs = [], [], []
        for c_ref, dil, q_off, k_off, _ in srcs:
            q = col2(b, q_off) * ATT_SCALE
            qs.append(q)
            s = per_head(c_ref[b, 0:D_A, :] * q)
            if dil > 1:
                lane = lax.broadcasted_iota(jnp.int32, s.shape, 1)
                s = jnp.where(lane % dil == 0, s, -jnp.inf)
            parts.append(s)
            if k_off is not None:
                new_scores.append(per_head(col2(b, k_off) * q))
        if new_scores:
            lane = lax.broadcasted_iota(jnp.int32, (A_HEADS, LANES), 1)
            slab = jnp.full((A_HEADS, LANES), -jnp.inf, F32)
            for k, sn in enumerate(new_scores):
                slab = jnp.where(lane == k, sn, slab)
            parts.append(slab)
        s_all = jnp.concatenate(parts, axis=1)
        p_all = jnp.exp(s_all - jnp.max(s_all, axis=1, keepdims=True))
        den = jnp.sum(p_all, axis=1, keepdims=True)
        acc = [jnp.zeros((A_HEAD_DIM, LANES), F32) for _ in range(A_HEADS)]
        lo = 0
        for c_ref, _, _, _, _ in srcs:
            length = c_ref.shape[2]
            for k in range(length // LANES):
                ls = slice(k * LANES, (k + 1) * LANES)
                for h in range(A_HEADS):
                    vt = c_ref[b, D_A + h * A_HEAD_DIM:D_A + (h + 1) * A_HEAD_DIM, ls]
                    acc[h] = acc[h] + vt * p_all[h:h + 1, lo + k * LANES:lo + (k + 1) * LANES]
            lo += length
        num = jnp.sum(jnp.concatenate(acc, axis=0), axis=1, keepdims=True)
        k = 0
        for _, _, _, k_off, v_off in srcs:
            if k_off is not None:
                num = num + spread(p_all[:, lo + k:lo + k + 1]) * col2(b, v_off)
                k += 1
        return num / spread(den)

    window = [(c_ref, dil, OFF_QA + g * D_A, OFF_KA + g * D_A, OFF_VA + g * D_A)
              for g, (c_ref, (_, dil)) in enumerate(zip((c1_ref, c2_ref, c3_ref), WIN_GROUPS))]
    memory = [(cm_ref, 1, OFF_QM, None, None)]

    for b in range(bt):
        ua = attend(b, window) * _silu(col2(b, OFF_GA))
        um = attend(b, memory) * _silu(col2(b, OFF_GM))
        branch_cols = [ua[0:LANES], ua[LANES:2 * LANES], um[0:LANES], um[LANES:2 * LANES]]

        lane = lax.broadcasted_iota(jnp.int32, (LANES, LANES), 1)
        ut = jnp.zeros((LANES, LANES), F32)
        for k, cvec in enumerate(branch_cols):
            ut = ut + jnp.where(lane == k, cvec, 0.0)
        u_ref[b, 0:4, :] = ut.T[0:4, :]

        orows = []
        for h in range(HG_HEADS):
            sl = slice(h * HG_DIM, (h + 1) * HG_DIM)
            fcol = lb_col[sl] + (1.0 - lb_col[sl]) * _sigmoid(col(b, OFF_FB + h * HG_DIM))
            qcol = _silu(col(b, OFF_QB + h * HG_DIM))
            vrow = zr_ref[b, OFF_IB // LANES + h:OFF_IB // LANES + h + 1, :]
            sn = fcol * st_ref[b, sl, :] + (1.0 - fcol) * vrow
            nst_ref[b, sl, :] = sn
            orows.append(jnp.sum(sn * qcol, axis=0, keepdims=True))
        gate = zr_ref[b, OFF_GB // LANES:OFF_GB // LANES + HG_HEADS, :]
        u_ref[b, 4:4 + HG_HEADS, :] = _rms(jnp.concatenate(orows, axis=0), nh) * _silu(gate)


def _stored_view(cache):
    db, length = cache.shape[0], cache.shape[1]
    return jnp.transpose(cache, (0, 2, 3, 4, 1)).reshape(db, 2 * D_A, length)


def _sample_out_kernel(x_ref, u_ref, zg_ref, nf_ref, wa_ref, wb_ref, wm_ref, wo_ref, y_ref):
    u = u_ref[...]
    zg = zg_ref[...]
    y_ref[...] = _merge_out(x_ref[...], u[:, 0:D_A], u[:, D_A + D_M:], u[:, D_A:D_A + D_M],
                            zg[:, 0:D_MODEL], zg[:, D_MODEL:2 * D_MODEL], zg[:, 2 * D_MODEL:],
                            wa_ref, wb_ref, wm_ref, wo_ref, nf_ref[...])


def _sample_out_call(x, u, zg, nf, wa, wb, wm, wo):
    db = x.shape[0]

    def full(a):
        return pl.BlockSpec(a.shape, lambda i: (0,) * a.ndim)

    args = (x, u, zg, nf, wa, wb, wm, wo)
    return pl.pallas_call(
        _sample_out_kernel,
        out_shape=jax.ShapeDtypeStruct((db, D_MODEL), F32),
        grid=(1,),
        in_specs=[full(a) for a in args],
        out_specs=pl.BlockSpec((db, D_MODEL), lambda i: (0, 0)),
        compiler_params=pltpu.CompilerParams(vmem_limit_bytes=VMEM_LIMIT),
        name="sample_out",
    )(*args)


def _cache_rows(kv, length):
    outs = []
    for g in range(N_GROUPS):
        k = kv[:, -length[g]:, g * D_A:(g + 1) * D_A]
        v = kv[:, -length[g]:, (N_GROUPS + g) * D_A:(N_GROUPS + g + 1) * D_A]
        outs.append(jnp.stack([k, v], axis=2).reshape(kv.shape[0], length[g], 2, A_HEADS, A_HEAD_DIM)[None])
    return outs


def kernel(x_prompt, x_sample, mem_prompt, cache_w1_kv, cache_w2_kv, cache_w3_kv, cache_mem_kv, state_hgrn,
           norm_in, w_in, lb_logits, norm_hgrn, norm_mem, w_mem_kv, w_branch_a, w_branch_b, w_branch_m,
           w_out, norm_final):
    b, s, _ = x_prompt.shape
    db = x_sample.shape[0]
    w_in_bf = w_in[0].astype(BF16)
    wa, wb, wm, wo = (w[0].astype(BF16) for w in (w_branch_a, w_branch_b, w_branch_m, w_out))
    gain_in = norm_in[0][None]
    nh = norm_hgrn[0][None]
    nf = norm_final[None]
    lbl = lb_logits.astype(F32)

    xs = x_sample.reshape(db, D_MODEL)
    zs = _sample_inproj_call(xs, gain_in, w_in_bf)
    mkv = _mem_kv_call(mem_prompt, norm_mem[0][None], w_mem_kv[0].astype(BF16))
    sample_ops = (zs.reshape(db, Z_ROWS, LANES),
                  *[_stored_view(c[0]) for c in (cache_w1_kv, cache_w2_kv, cache_w3_kv, cache_mem_kv)],
                  state_hgrn[0].reshape(db, HG_HEADS * HG_DIM, HG_DIM), lbl.T, nh.reshape(HG_HEADS, HG_DIM))
    qkv0, qkv1, qkv2, gq, hb, zg, fb, kv_tail, u_first, nst_first = _inproj_call(
        x_prompt, gain_in, w_in_bf, sample_ops)
    og = [_dilated_call(qkv, dil, g) for g, (qkv, (_, dil)) in enumerate(zip((qkv0, qkv1, qkv2), WIN_GROUPS))]
    y_prompt, hg_prompt, u, nst = _prompt_tail_call(
        x_prompt, gq, hb, zg, fb, og[0], og[1], og[2], mkv, lbl, nh, nf, wa, wb, wm, wo,
        sample_ops, u_first, nst_first)
    pw = _cache_rows(kv_tail, [min(w, s) for w, _ in WIN_GROUPS])
    new_mem = mkv.reshape(1, b, MEM_LEN, 2, A_HEADS, A_HEAD_DIM)

    y_sample = _sample_out_call(xs, u, zs[:, OFF_ZG:], nf, wa, wb, wm, wo).reshape(db, 1, D_MODEL)
    sw = _cache_rows(zs[:, None, OFF_KA:OFF_GA], [1, 1, 1])
    new_hg_sample = nst.reshape(1, db, HG_HEADS, HG_DIM, HG_DIM)

    return (y_prompt, y_sample, pw[0], pw[1], pw[2], new_mem, hg_prompt[None],
            sw[0], sw[1], sw[2], new_hg_sample)
```

```python
import jax
import jax.numpy as jnp
from jax import lax
from jax.experimental import pallas as pl
from jax.experimental.pallas import tpu as pltpu

F32 = jnp.float32
BF16 = jnp.bfloat16

D_MODEL = 1024
WIN_GROUPS = ((128, 1), (512, 4), (2048, 16))
N_GROUPS = 3
A_HEADS = 4
A_HEAD_DIM = 64
D_A = 256
QBLK = 128
HG_HEADS = 4
HG_DIM = 128
D_B = 512
HG_CHUNK = 64
HG_SUB = 16
MEM_LEN = 256
D_M = 256
EPS = 1e-6
D_IN = 8192
LANES = 128
OFF_QA, OFF_KA, OFF_VA, OFF_GA = 0, 768, 1536, 2304
OFF_QB, OFF_FB, OFF_IB, OFF_GB = 2560, 3072, 3584, 4096
OFF_QM, OFF_GM, OFF_ZG = 4608, 4864, 5120
ATT_SCALE = A_HEAD_DIM ** -0.5

VMEM_LIMIT = 56 * 1024 * 1024

NT_DIMS = (((1,), (1,)), ((), ()))
TN_DIMS = (((0,), (0,)), ((), ()))


def _sigmoid(x):
    return 0.5 * jnp.tanh(0.5 * x) + 0.5


def _silu(x):
    h = 0.5 * x
    return h * jnp.tanh(h) + h


def _rms(x, gain):
    return x * lax.rsqrt(jnp.mean(x * x, axis=-1, keepdims=True) + EPS) * gain


def _head_masks(width, head_dim, n_heads):
    lane = lax.broadcasted_iota(jnp.int32, (1, width), 1)
    return [(lane >= h * head_dim) & (lane < (h + 1) * head_dim) for h in range(n_heads)]


def _stack_heads(q, masks):
    zero = jnp.zeros_like(q)
    return jnp.concatenate([jnp.where(m, q, zero) for m in masks], axis=0)


def _unstack_heads(o_all, masks, t):
    n = len(masks)
    out = o_all[(n - 1) * t:n * t]
    for h in range(n - 2, -1, -1):
        out = jnp.where(masks[h], o_all[h * t:(h + 1) * t], out)
    return out


def _expand_cols(cols, masks):
    out = jnp.broadcast_to(cols[-1], (cols[-1].shape[0], masks[0].shape[1]))
    for h in range(len(masks) - 2, -1, -1):
        out = jnp.where(masks[h], cols[h], out)
    return out


def _mem_kv_kernel(mem_ref, gain_ref, w_ref, out_ref):
    hm = _rms(mem_ref[0], gain_ref[...]).astype(BF16)
    out_ref[0] = jnp.dot(hm, w_ref[...], preferred_element_type=F32)


def _mem_kv_call(mem, gain, w_bf):
    b = mem.shape[0]
    return pl.pallas_call(
        _mem_kv_kernel,
        out_shape=jax.ShapeDtypeStruct((b, MEM_LEN, 2 * D_M), F32),
        grid=(b,),
        in_specs=[pl.BlockSpec((1, MEM_LEN, D_MODEL), lambda i: (i, 0, 0)),
                  pl.BlockSpec((1, D_MODEL), lambda i: (0, 0)),
                  pl.BlockSpec((D_MODEL, 2 * D_M), lambda i: (0, 0))],
        out_specs=pl.BlockSpec((1, MEM_LEN, 2 * D_M), lambda i: (i, 0, 0)),
        name="mem_kv",
    )(mem, gain, w_bf)


IN_TILE = 256
IN_CHUNK = 256
KV_TAIL = 2048
D_QKV = 3 * D_A
D_GQ = 3 * D_A
D_HB = 3 * D_B
D_ZG = 3 * D_MODEL


def _inproj_kernel(x_ref, gain_ref, w_ref, zr_ref, c1_ref, c2_ref, c3_ref, cm_ref, sst_ref, lblt_ref, nh4_ref,
                   qkv0_ref, qkv1_ref, qkv2_ref, gq_ref, hb_ref, zg_ref, fb_ref, kv_ref, u_ref, nst_ref,
                   scr_ref):
    _sample_rows(zr_ref, c1_ref, c2_ref, c3_ref, cm_ref, sst_ref, lblt_ref, nh4_ref, u_ref, nst_ref)
    qkv_refs = (qkv0_ref, qkv1_ref, qkv2_ref)
    h = _rms(x_ref[0], gain_ref[...]).astype(BF16)
    t = IN_TILE
    for c in range(D_IN // IN_CHUNK):
        lo = c * IN_CHUNK
        zc = jnp.dot(h, w_ref[:, lo:lo + IN_CHUNK], preferred_element_type=F32)
        if lo < OFF_GA:
            sec, g = divmod(c, N_GROUPS)
            dil = WIN_GROUPS[g][1]
            dst = slice(sec * D_A, (sec + 1) * D_A)
            if dil == 1:
                qkv_refs[g][0, 0, :, dst] = zc.astype(BF16)
            else:
                for k in range(IN_CHUNK // LANES):
                    scr_ref[k] = zc[:, k * LANES:(k + 1) * LANES]
                for r in range(dil):
                    qkv_refs[g][0, r, :, dst] = jnp.concatenate(
                        [scr_ref[k, pl.ds(r, t // dil, stride=dil), :] for k in range(IN_CHUNK // LANES)],
                        axis=1).astype(BF16)
            if sec > 0:
                kv_ref[0, :, lo - OFF_KA:lo - OFF_KA + IN_CHUNK] = zc
        elif lo < OFF_QB:
            gq_ref[0, :, 0:D_A] = zc.astype(BF16)
        elif lo < OFF_FB:
            hb_ref[0, :, lo - OFF_QB:lo - OFF_QB + IN_CHUNK] = zc.astype(BF16)
        elif lo < OFF_IB:
            fb_ref[0, :, lo - OFF_FB:lo - OFF_FB + IN_CHUNK] = zc
        elif lo < OFF_QM:
            hb_ref[0, :, lo - OFF_IB + D_B:lo - OFF_IB + D_B + IN_CHUNK] = zc.astype(BF16)
        elif lo < OFF_ZG:
            gq_ref[0, :, lo - OFF_QM + D_A:lo - OFF_QM + D_A + IN_CHUNK] = zc.astype(BF16)
        else:
            zg_ref[0, :, lo - OFF_ZG:lo - OFF_ZG + IN_CHUNK] = zc.astype(BF16)


def _sample_specs(ops, nt, first_row):
    first_blk = first_row // S_TILE

    def rows(a):
        return pl.BlockSpec((S_TILE,) + a.shape[1:], lambda i, j: (first_blk + i * nt + j, 0, 0))

    def const(a):
        return pl.BlockSpec(a.shape, lambda i, j: (0, 0), pipeline_mode=pl.Buffered(1))

    return [rows(a) for a in ops[:6]] + [const(a) for a in ops[6:]]


def _inproj_call(x, gain, w_bf, sample_ops):
    b, s, _ = x.shape
    t = IN_TILE
    nt = s // t
    first_tail = (s - KV_TAIL) // t
    n_rows = b * nt * S_TILE

    def tok(width):
        return pl.BlockSpec((1, t, width), lambda i, j: (i, j, 0))

    def srow(shape):
        return pl.BlockSpec((S_TILE,) + shape, lambda i, j: (i * nt + j, 0, 0))

    qkv_shapes, qkv_specs = [], []
    for _, dil in WIN_GROUPS:
        qkv_shapes.append(jax.ShapeDtypeStruct((b, dil, s // dil, D_QKV), BF16))
        qkv_specs.append(pl.BlockSpec((1, dil, t // dil, D_QKV), lambda i, j: (i, 0, j, 0)))
    return pl.pallas_call(
        _inproj_kernel,
        out_shape=(*qkv_shapes,
                   jax.ShapeDtypeStruct((b, s, D_GQ), BF16),
                   jax.ShapeDtypeStruct((b, s, D_HB), BF16),
                   jax.ShapeDtypeStruct((b, s, D_ZG), BF16),
                   jax.ShapeDtypeStruct((b, s, D_B), F32),
                   jax.ShapeDtypeStruct((b, KV_TAIL, 2 * N_GROUPS * D_A), F32),
                   jax.ShapeDtypeStruct((n_rows, U_ROWS, LANES), F32),
                   jax.ShapeDtypeStruct((n_rows, HG_HEADS * HG_DIM, HG_DIM), F32)),
        grid=(b, nt),
        in_specs=[tok(D_MODEL),
                  pl.BlockSpec((1, D_MODEL), lambda i, j: (0, 0)),
                  pl.BlockSpec((D_MODEL, D_IN), lambda i, j: (0, 0), pipeline_mode=pl.Buffered(1)),
                  *_sample_specs(sample_ops, nt, 0)],
        out_specs=(*qkv_specs, tok(D_GQ), tok(D_HB), tok(D_ZG), tok(D_B),
                   pl.BlockSpec((1, t, 2 * N_GROUPS * D_A),
                                lambda i, j: (i, jnp.maximum(j - first_tail, 0), 0)),
                   srow((U_ROWS, LANES)), srow((HG_HEADS * HG_DIM, HG_DIM))),
        scratch_shapes=[pltpu.VMEM((IN_CHUNK // LANES, t, LANES), F32)],
        compiler_params=pltpu.CompilerParams(
            dimension_semantics=("arbitrary", "arbitrary"), vmem_limit_bytes=VMEM_LIMIT),
        name="inproj",
    )(x, gain, w_bf, *sample_ops)


def _dilated_block(q, kk, vv, neg, masks):
    qs = _stack_heads(q * jnp.asarray(ATT_SCALE, BF16), masks)
    s = lax.dot_general(qs, kk, NT_DIMS, preferred_element_type=F32) + neg
    mx = jnp.max(s, axis=-1, keepdims=True)
    p = jnp.exp(s - mx)
    l = jnp.sum(p, axis=-1, keepdims=True)
    o_all = jnp.dot(p.astype(BF16), vv, preferred_element_type=F32)
    l_e = _expand_cols([l[h * QBLK:(h + 1) * QBLK] for h in range(A_HEADS)], masks)
    mx_e = _expand_cols([mx[h * QBLK:(h + 1) * QBLK] for h in range(A_HEADS)], masks)
    return _unstack_heads(o_all, masks, QBLK) / l_e, mx_e + jnp.log(l_e)


DIL_STEP = 2048
OG_SLABS = 2 * D_A // LANES


def _make_dilated_kernel(dil):
    nblk = DIL_STEP // (dil * QBLK)

    def body(prev_ref, cur_ref, out_ref):
        first_step = pl.program_id(1) == 0
        masks = _head_masks(D_A, A_HEAD_DIM, A_HEADS)
        qi = lax.broadcasted_iota(jnp.int32, (A_HEADS * QBLK, 2 * QBLK), 0) % QBLK
        kj = lax.broadcasted_iota(jnp.int32, (A_HEADS * QBLK, 2 * QBLK), 1)
        band = (kj >= qi) & (kj <= qi + QBLK)
        neg_band = jnp.where(band, 0.0, -jnp.inf)
        neg_first = jnp.where(first_step, jnp.where(band & (kj >= QBLK), 0.0, -jnp.inf), neg_band)

        for r in range(dil):
            for jb in range(nblk):
                rows = slice(jb * QBLK, (jb + 1) * QBLK)
                q = cur_ref[0, r, rows, 0:D_A]
                if jb == 0:
                    kv = [jnp.concatenate([prev_ref[0, r, :, c * D_A:(c + 1) * D_A],
                                           cur_ref[0, r, rows, c * D_A:(c + 1) * D_A]], axis=0) for c in (1, 2)]
                else:
                    both = slice((jb - 1) * QBLK, (jb + 1) * QBLK)
                    kv = [cur_ref[0, r, both, c * D_A:(c + 1) * D_A] for c in (1, 2)]
                o, lse = _dilated_block(q, kv[0], kv[1], neg_first if jb == 0 else neg_band, masks)
                res = (o, lse)
                for k in range(OG_SLABS):
                    src = res[k // 2][:, (k % 2) * LANES:(k % 2 + 1) * LANES]
                    if dil == 1:
                        out_ref[0, k, rows, :] = src
                    else:
                        out_ref[0, k, pl.ds(jb * QBLK * dil + r, QBLK, stride=dil), :] = src
    return body


def _dilated_call(qkv, dil, g):
    b, _, n, _ = qkv.shape
    rows = DIL_STEP // dil
    nb = n // rows
    return pl.pallas_call(
        _make_dilated_kernel(dil),
        out_shape=jax.ShapeDtypeStruct((b, OG_SLABS, n * dil, LANES), F32),
        grid=(b, nb),
        in_specs=[pl.BlockSpec((1, dil, QBLK, D_QKV),
                               lambda i, u: (i, 0, jnp.maximum(u * (rows // QBLK) - 1, 0), 0)),
                  pl.BlockSpec((1, dil, rows, D_QKV), lambda i, u: (i, 0, u, 0))],
        out_specs=pl.BlockSpec((1, OG_SLABS, DIL_STEP, LANES), lambda i, u: (i, 0, u, 0)),
        compiler_params=pltpu.CompilerParams(
            dimension_semantics=("arbitrary", "arbitrary"), vmem_limit_bytes=VMEM_LIMIT),
        name=f"dilated_attn_g{g}",
    )(qkv, qkv)


def _lower_bound(l0, l1):
    m = jnp.maximum(l0, l1)
    e0, e1 = jnp.exp(l0 - m), jnp.exp(l1 - m)
    return e0 / (e0 + e1)


def _head_rms(ob, gain):
    parts = []
    for h in range(HG_HEADS):
        sl = slice(h * HG_DIM, (h + 1) * HG_DIM)
        parts.append(_rms(ob[:, sl], gain[:, sl]))
    return jnp.concatenate(parts, axis=1)


def _merge_out(x, ua, ub, um, zg, wa_ref, wb_ref, wm_ref, wo_ref, nf):
    gates = _sigmoid(zg.astype(BF16))
    merged = None
    for k, (u, w_ref) in enumerate(((ua, wa_ref), (ub, wb_ref), (um, wm_ref))):
        proj = jnp.dot(u.astype(BF16), w_ref[...], preferred_element_type=F32).astype(BF16)
        term = gates[:, k * D_MODEL:(k + 1) * D_MODEL] * proj
        merged = term if merged is None else merged + term
    y = x + jnp.dot(merged, wo_ref[...], preferred_element_type=F32)
    return _rms(y, nf)


TAIL_TILE = 256


def _cumsum_rows(g, tri):
    g1 = g.astype(BF16)
    g2 = (g - g1.astype(F32)).astype(BF16)
    return jnp.dot(tri, g1, preferred_element_type=F32) + jnp.dot(tri, g2, preferred_element_type=F32)


def _hgrn_tile(fb, qb, v, lb, st_ref):
    t = fb.shape[0]
    c, sub = HG_CHUNK, HG_SUB
    n_sub = c // sub
    f = lb + (1.0 - lb) * _sigmoid(fb)
    kk = 1.0 - f
    row = lax.broadcasted_iota(jnp.int32, (t, t), 0)
    col = lax.broadcasted_iota(jnp.int32, (t, t), 1)
    same_chunk = (row // c) == (col // c)
    gcs = _cumsum_rows(jnp.log(f), (same_chunk & (col <= row)).astype(BF16))
    q = _silu(qb.astype(F32))

    loc = lax.broadcasted_iota(jnp.int32, (c, 1), 0)
    qt_c, kt_ci, qe_c, kd_c, dec_c = [], [[] for _ in range(n_sub)], [], [], []
    for ci in range(t // c):
        rows = slice(ci * c, (ci + 1) * c)
        g_c, q_c, k_c = gcs[rows], q[rows], kk[rows]
        refs = [jnp.zeros((1, D_B), F32)] + [g_c[i * sub - 1:i * sub] for i in range(1, n_sub)]
        own = refs[n_sub - 1]
        for i in range(n_sub - 2, -1, -1):
            own = jnp.where(loc < (i + 1) * sub, refs[i], own)
        qt_c.append((q_c * jnp.exp(g_c - own)).astype(BF16))
        k_own = k_c * jnp.exp(own - g_c)
        for i in range(n_sub):
            pieces = []
            for jb in range(n_sub):
                piece = k_own[jb * sub:(jb + 1) * sub]
                if jb < i:
                    piece = piece * jnp.exp(refs[i] - refs[jb])
                elif jb > i:
                    piece = jnp.zeros_like(piece)
                pieces.append(piece)
            kt_ci[i].append(jnp.concatenate(pieces, axis=0).astype(BF16))
        g_last = g_c[c - 1:c]
        qe_c.append((q_c * jnp.exp(g_c)).astype(BF16))
        kd_c.append((k_c * jnp.exp(g_last - g_c)).astype(BF16))
        dec_c.append(jnp.exp(g_last))
    qt = jnp.concatenate(qt_c, axis=0)
    kts = [jnp.concatenate(parts, axis=0) for parts in kt_ci]
    sub_of_row = (lax.broadcasted_iota(jnp.int32, (t, 1), 0) % c) // sub
    causal = same_chunk & (col <= row)

    outs = []
    for h in range(HG_HEADS):
        sl = slice(h * HG_DIM, (h + 1) * HG_DIM)
        zero = jnp.zeros((t, HG_DIM), BF16)
        q_big = jnp.concatenate([jnp.where(sub_of_row == i, qt[:, sl], zero) for i in range(n_sub)], axis=1)
        k_big = jnp.concatenate([kts[i][:, sl] for i in range(n_sub)], axis=1)
        att = lax.dot_general(q_big, k_big, NT_DIMS, preferred_element_type=F32)
        att = jnp.where(causal, att, 0.0).astype(BF16)
        o_intra = jnp.dot(att, v[:, sl], preferred_element_type=F32)
        st = st_ref[h]
        inter = []
        for ci in range(t // c):
            rows = slice(ci * c, (ci + 1) * c)
            inter.append(lax.dot_general(qe_c[ci][:, sl], st.astype(BF16), NT_DIMS, preferred_element_type=F32))
            st = st * dec_c[ci][:, sl] + lax.dot_general(
                v[rows, sl], kd_c[ci][:, sl], TN_DIMS, preferred_element_type=F32)
        st_ref[h] = st
        outs.append(o_intra + jnp.concatenate(inter, axis=0))
    return jnp.concatenate(outs, axis=1)


def _prompt_tail_kernel(x_ref, gq_ref, hb_ref, zg_ref, fb_ref, o1_ref, o2_ref, o3_ref, mkv_ref, lbl_ref,
                        nh_ref, nf_ref, wa_ref, wb_ref, wm_ref, wo_ref,
                        zr_ref, c1_ref, c2_ref, c3_ref, cm_ref, sst_ref, lblt_ref, nh4_ref, ua_ref, nsta_ref,
                        y_ref, hst_ref, u_ref, nst_ref, st_ref):
    j = pl.program_id(1)
    t = TAIL_TILE

    @pl.when(j == 0)
    def _():
        st_ref[...] = jnp.zeros_like(st_ref)

    u_ref[0] = ua_ref[...]
    nst_ref[0] = nsta_ref[...]
    _sample_rows(zr_ref, c1_ref, c2_ref, c3_ref, cm_ref, sst_ref, lblt_ref, nh4_ref, u_ref.at[1], nst_ref.at[1])

    os_ = [jnp.concatenate([r[0, 0], r[0, 1]], axis=1) for r in (o1_ref, o2_ref, o3_ref)]
    ls_ = [jnp.concatenate([r[0, 2], r[0, 3]], axis=1) for r in (o1_ref, o2_ref, o3_ref)]
    mx = jnp.maximum(jnp.maximum(ls_[0], ls_[1]), ls_[2])
    es_ = [jnp.exp(l - mx) for l in ls_]
    oa = (es_[0] * os_[0] + es_[1] * os_[1] + es_[2] * os_[2]) / (es_[0] + es_[1] + es_[2])
    ua = oa * _silu(gq_ref[0, :, 0:D_A].astype(F32))

    masks = _head_masks(D_M, A_HEAD_DIM, A_HEADS)
    qm = gq_ref[0, :, D_A:2 * D_A] * jnp.asarray(ATT_SCALE, BF16)
    qs = _stack_heads(qm, masks)
    mk = mkv_ref[0, :, 0:D_M].astype(BF16)
    mv = mkv_ref[0, :, D_M:2 * D_M].astype(BF16)
    s = lax.dot_general(qs, mk, NT_DIMS, preferred_element_type=F32)
    p = jnp.exp(s - jnp.max(s, axis=-1, keepdims=True))
    l = jnp.sum(p, axis=-1, keepdims=True)
    om = _unstack_heads(jnp.dot(p.astype(BF16), mv, preferred_element_type=F32) / l, masks, t)
    um = om * _silu(gq_ref[0, :, 2 * D_A:3 * D_A].astype(F32))

    lb = _lower_bound(lbl_ref[0:1], lbl_ref[1:2])
    ob = _hgrn_tile(fb_ref[0], hb_ref[0, :, 0:D_B], hb_ref[0, :, D_B:2 * D_B], lb, st_ref)
    ub = _head_rms(ob, nh_ref[...]) * _silu(hb_ref[0, :, 2 * D_B:3 * D_B].astype(F32))

    y_ref[0] = _merge_out(x_ref[0], ua, ub, um, zg_ref[0], wa_ref, wb_ref, wm_ref, wo_ref, nf_ref[...])

    @pl.when(j == pl.num_programs(1) - 1)
    def _():
        for h in range(HG_HEADS):
            hst_ref[0, h] = st_ref[h].T


def _prompt_tail_call(x, gq, hb, zg, fb, o1, o2, o3, mkv, lbl, nh, nf, wa, wb, wm, wo,
                      sample_ops, u_first, nst_first):
    b, s, _ = x.shape
    t = TAIL_TILE
    nt = s // t
    db = sample_ops[0].shape[0]
    half = u_first.shape[0]
    assert half + b * nt * S_TILE == db and 2 * half == db, "sample rows must split evenly over both grids"

    def tok(width):
        return pl.BlockSpec((1, t, width), lambda i, j: (i, j, 0))

    def const(shape):
        return pl.BlockSpec(shape, lambda i, j: (0,) * len(shape), pipeline_mode=pl.Buffered(1))

    def first(a):
        return pl.BlockSpec((S_TILE,) + a.shape[1:], lambda i, j: (i * nt + j, 0, 0))

    def halves(shape):
        return pl.BlockSpec((2, S_TILE) + shape, lambda i, j: (0, i * nt + j, 0, 0))

    y, hst, u, nst = pl.pallas_call(
        _prompt_tail_kernel,
        out_shape=(jax.ShapeDtypeStruct((b, s, D_MODEL), F32),
                   jax.ShapeDtypeStruct((b, HG_HEADS, HG_DIM, HG_DIM), F32),
                   jax.ShapeDtypeStruct((2, half, U_ROWS, LANES), F32),
                   jax.ShapeDtypeStruct((2, half, HG_HEADS * HG_DIM, HG_DIM), F32)),
        grid=(b, nt),
        in_specs=[tok(D_MODEL), tok(D_GQ), tok(D_HB), tok(D_ZG), tok(D_B),
                  *[pl.BlockSpec((1, OG_SLABS, t, LANES), lambda i, j: (i, 0, j, 0))] * N_GROUPS,
                  pl.BlockSpec((1, MEM_LEN, 2 * D_M), lambda i, j: (i, 0, 0)),
                  const((2, D_B)), const((1, D_B)), const((1, D_MODEL)),
                  const((D_A, D_MODEL)), const((D_B, D_MODEL)), const((D_M, D_MODEL)),
                  const((D_MODEL, D_MODEL)),
                  *_sample_specs(sample_ops, nt, half), first(u_first), first(nst_first)],
        out_specs=(tok(D_MODEL),
                   pl.BlockSpec((1, HG_HEADS, HG_DIM, HG_DIM), lambda i, j: (i, 0, 0, 0)),
                   halves((U_ROWS, LANES)), halves((HG_HEADS * HG_DIM, HG_DIM))),
        scratch_shapes=[pltpu.VMEM((HG_HEADS, HG_DIM, HG_DIM), F32)],
        compiler_params=pltpu.CompilerParams(
            dimension_semantics=("arbitrary", "arbitrary"), vmem_limit_bytes=VMEM_LIMIT),
        name="prompt_tail",
    )(x, gq, hb, zg, fb, o1, o2, o3, mkv, lbl, nh, nf, wa, wb, wm, wo, *sample_ops, u_first, nst_first)
    return y, hst, u.reshape(db, U_ROWS * LANES), nst.reshape(db, HG_HEADS * HG_DIM, HG_DIM)


S_COLS = 1024


def _sample_inproj_kernel(x_ref, gain_ref, w_ref, z_ref, wbf_ref):
    h = _rms(x_ref[...], gain_ref[...]).astype(BF16)
    w = w_ref[...].astype(BF16)
    wbf_ref[...] = w
    z_ref[...] = jnp.dot(h, w, preferred_element_type=F32)


def _sample_inproj_call(x, gain, w):
    db = x.shape[0]
    return pl.pallas_call(
        _sample_inproj_kernel,
        out_shape=(jax.ShapeDtypeStruct((db, D_IN), F32),
                   jax.ShapeDtypeStruct((D_MODEL, D_IN), BF16)),
        grid=(D_IN // S_COLS,),
        in_specs=[pl.BlockSpec((db, D_MODEL), lambda c: (0, 0)),
                  pl.BlockSpec((1, D_MODEL), lambda c: (0, 0)),
                  pl.BlockSpec((D_MODEL, S_COLS), lambda c: (0, c))],
        out_specs=(pl.BlockSpec((db, S_COLS), lambda c: (0, c)),
                   pl.BlockSpec((D_MODEL, S_COLS), lambda c: (0, c))),
        name="sample_inproj",
    )(x, gain, w)


S_TILE = 1
Z_ROWS = D_IN // LANES
U_ROWS = 8


def _sample_rows(zr_ref, c1_ref, c2_ref, c3_ref, cm_ref, st_ref, lblt_ref, nh_ref, u_ref, nst_ref):
    bt = S_TILE
    pad = [jnp.zeros((LANES - bt * Z_ROWS, LANES), F32)] if bt * Z_ROWS < LANES else []
    zt = jnp.concatenate([zr_ref[b] for b in range(bt)] + pad, axis=0).T

    def col(b, off, n=HG_DIM):
        j, l = divmod(off, LANES)
        return zt[l:l + n, Z_ROWS * b + j:Z_ROWS * b + j + 1]

    lbt = lblt_ref[...]
    lb_col = _lower_bound(lbt[:, 0:1], lbt[:, 1:2])
    nh = nh_ref[...]

    def col2(b, off):
        return jnp.concatenate([col(b, off), col(b, off + LANES)], axis=0)

    def per_head(x):
        return jnp.concatenate([jnp.sum(x[h * A_HEAD_DIM:(h + 1) * A_HEAD_DIM], axis=0, keepdims=True)
                                for h in range(A_HEADS)], axis=0)

    def spread(x):
        return jnp.concatenate([jnp.broadcast_to(x[h:h + 1], (A_HEAD_DIM, 1)) for h in range(A_HEADS)], axis=0)

    def attend(b, srcs):
        parts, new_scores, qs = [], [], []
        for c_ref, dil, q_off, k_off, _ in srcs:
            q = col2(b, q_off) * ATT_SCALE
            qs.append(q)
            s = per_head(c_ref[b, 0:D_A, :] * q)
            if dil > 1:
                lane = lax.broadcasted_iota(jnp.int32, s.shape, 1)
                s = jnp.where(lane % dil == 0, s, -jnp.inf)
            parts.append(s)
            if k_off is not None:
                new_scores.append(per_head(col2(b, k_off) * q))
        if new_scores:
            lane = lax.broadcasted_iota(jnp.int32, (A_HEADS, LANES), 1)
            slab = jnp.full((A_HEADS, LANES), -jnp.inf, F32)
            for k, sn in enumerate(new_scores):
                slab = jnp.where(lane == k, sn, slab)
            parts.append(slab)
        s_all = jnp.concatenate(parts, axis=1)
        p_all = jnp.exp(s_all - jnp.max(s_all, axis=1, keepdims=True))
        den = jnp.sum(p_all, axis=1, keepdims=True)
        acc = [jnp.zeros((A_HEAD_DIM, LANES), F32) for _ in range(A_HEADS)]
        lo = 0
        for c_ref, _, _, _, _ in srcs:
            length = c_ref.shape[2]
            for k in range(length // LANES):
                ls = slice(k * LANES, (k + 1) * LANES)
                for h in range(A_HEADS):
                    vt = c_ref[b, D_A + h * A_HEAD_DIM:D_A + (h + 1) * A_HEAD_DIM, ls]
                    acc[h] = acc[h] + vt * p_all[h:h + 1, lo + k * LANES:lo + (k + 1) * LANES]
            lo += length
        num = jnp.sum(jnp.concatenate(acc, axis=0), axis=1, keepdims=True)
        k = 0
        for _, _, _, k_off, v_off in srcs:
            if k_off is not None:
                num = num + spread(p_all[:, lo + k:lo + k + 1]) * col2(b, v_off)
                k += 1
        return num / spread(den)

    window = [(c_ref, dil, OFF_QA + g * D_A, OFF_KA + g * D_A, OFF_VA + g * D_A)
              for g, (c_ref, (_, dil)) in enumerate(zip((c1_ref, c2_ref, c3_ref), WIN_GROUPS))]
    memory = [(cm_ref, 1, OFF_QM, None, None)]

    for b in range(bt):
        ua = attend(b, window) * _silu(col2(b, OFF_GA))
        um = attend(b, memory) * _silu(col2(b, OFF_GM))
        branch_cols = [ua[0:LANES], ua[LANES:2 * LANES], um[0:LANES], um[LANES:2 * LANES]]

        lane = lax.broadcasted_iota(jnp.int32, (LANES, LANES), 1)
        ut = jnp.zeros((LANES, LANES), F32)
        for k, cvec in enumerate(branch_cols):
            ut = ut + jnp.where(lane == k, cvec, 0.0)
        u_ref[b, 0:4, :] = ut.T[0:4, :]

        orows = []
        for h in range(HG_HEADS):
            sl = slice(h * HG_DIM, (h + 1) * HG_DIM)
            fcol = lb_col[sl] + (1.0 - lb_col[sl]) * _sigmoid(col(b, OFF_FB + h * HG_DIM))
            qcol = _silu(col(b, OFF_QB + h * HG_DIM))
            vrow = zr_ref[b, OFF_IB // LANES + h:OFF_IB // LANES + h + 1, :]
            sn = fcol * st_ref[b, sl, :] + (1.0 - fcol) * vrow
            nst_ref[b, sl, :] = sn
            orows.append(jnp.sum(sn * qcol, axis=0, keepdims=True))
        gate = zr_ref[b, OFF_GB // LANES:OFF_GB // LANES + HG_HEADS, :]
        u_ref[b, 4:4 + HG_HEADS, :] = _rms(jnp.concatenate(orows, axis=0), nh) * _silu(gate)


def _stored_view(cache):
    db, length = cache.shape[0], cache.shape[1]
    return jnp.transpose(cache, (0, 2, 3, 4, 1)).reshape(db, 2 * D_A, length)


def _sample_out_kernel(x_ref, u_ref, zg_ref, nf_ref, wa_ref, wb_ref, wm_ref, wo_ref, y_ref):
    u = u_ref[...]
    y_ref[...] = _merge_out(x_ref[...], u[:, 0:D_A], u[:, D_A + D_M:], u[:, D_A:D_A + D_M], zg_ref[...],
                            wa_ref, wb_ref, wm_ref, wo_ref, nf_ref[...])


def _sample_out_call(x, u, zg, nf, wa, wb, wm, wo):
    db = x.shape[0]

    def full(a):
        return pl.BlockSpec(a.shape, lambda i: (0,) * a.ndim)

    args = (x, u, zg, nf, wa, wb, wm, wo)
    return pl.pallas_call(
        _sample_out_kernel,
        out_shape=jax.ShapeDtypeStruct((db, D_MODEL), F32),
        grid=(1,),
        in_specs=[full(a) for a in args],
        out_specs=pl.BlockSpec((db, D_MODEL), lambda i: (0, 0)),
        compiler_params=pltpu.CompilerParams(vmem_limit_bytes=VMEM_LIMIT),
        name="sample_out",
    )(*args)


def _cache_rows(kv, length):
    outs = []
    for g in range(N_GROUPS):
        k = kv[:, -length[g]:, g * D_A:(g + 1) * D_A]
        v = kv[:, -length[g]:, (N_GROUPS + g) * D_A:(N_GROUPS + g + 1) * D_A]
        outs.append(jnp.stack([k, v], axis=2).reshape(kv.shape[0], length[g], 2, A_HEADS, A_HEAD_DIM)[None])
    return outs


def kernel(x_prompt, x_sample, mem_prompt, cache_w1_kv, cache_w2_kv, cache_w3_kv, cache_mem_kv, state_hgrn,
           norm_in, w_in, lb_logits, norm_hgrn, norm_mem, w_mem_kv, w_branch_a, w_branch_b, w_branch_m,
           w_out, norm_final):
    b, s, _ = x_prompt.shape
    db = x_sample.shape[0]
    wa, wb, wm, wo = (w[0].astype(BF16) for w in (w_branch_a, w_branch_b, w_branch_m, w_out))
    gain_in = norm_in[0][None]
    nh = norm_hgrn[0][None]
    nf = norm_final[None]
    lbl = lb_logits.astype(F32)

    xs = x_sample.reshape(db, D_MODEL)
    zs, w_in_bf = _sample_inproj_call(xs, gain_in, w_in[0])
    mkv = _mem_kv_call(mem_prompt, norm_mem[0][None], w_mem_kv[0].astype(BF16))
    sample_ops = (zs.reshape(db, Z_ROWS, LANES),
                  *[_stored_view(c[0]) for c in (cache_w1_kv, cache_w2_kv, cache_w3_kv, cache_mem_kv)],
                  state_hgrn[0].reshape(db, HG_HEADS * HG_DIM, HG_DIM), lbl.T, nh.reshape(HG_HEADS, HG_DIM))
    qkv0, qkv1, qkv2, gq, hb, zg, fb, kv_tail, u_first, nst_first = _inproj_call(
        x_prompt, gain_in, w_in_bf, sample_ops)
    og = [_dilated_call(qkv, dil, g) for g, (qkv, (_, dil)) in enumerate(zip((qkv0, qkv1, qkv2), WIN_GROUPS))]
    y_prompt, hg_prompt, u, nst = _prompt_tail_call(
        x_prompt, gq, hb, zg, fb, og[0], og[1], og[2], mkv, lbl, nh, nf, wa, wb, wm, wo,
        sample_ops, u_first, nst_first)
    pw = _cache_rows(kv_tail, [min(w, s) for w, _ in WIN_GROUPS])
    new_mem = mkv.reshape(1, b, MEM_LEN, 2, A_HEADS, A_HEAD_DIM)

    y_sample = _sample_out_call(xs, u, zs[:, OFF_ZG:], nf, wa, wb, wm, wo).reshape(db, 1, D_MODEL)
    sw = _cache_rows(zs[:, None, OFF_KA:OFF_GA], [1, 1, 1])
    new_hg_sample = nst.reshape(1, db, HG_HEADS, HG_DIM, HG_DIM)

    return (y_prompt, y_sample, pw[0], pw[1], pw[2], new_mem, hg_prompt[None],
            sw[0], sw[1], sw[2], new_hg_sample)
```

```python
import jax
import jax.numpy as jnp
from jax import lax
from jax.experimental import pallas as pl
from jax.experimental.pallas import tpu as pltpu

F32 = jnp.float32
BF16 = jnp.bfloat16

D_MODEL = 1024
WIN_GROUPS = ((128, 1), (512, 4), (2048, 16))
N_GROUPS = 3
A_HEADS = 4
A_HEAD_DIM = 64
D_A = 256
QBLK = 128
HG_HEADS = 4
HG_DIM = 128
D_B = 512
HG_CHUNK = 64
HG_SUB = 16
MEM_LEN = 256
D_M = 256
EPS = 1e-6
D_IN = 8192
LANES = 128
OFF_QA, OFF_KA, OFF_VA, OFF_GA = 0, 768, 1536, 2304
OFF_QB, OFF_FB, OFF_IB, OFF_GB = 2560, 3072, 3584, 4096
OFF_QM, OFF_GM, OFF_ZG = 4608, 4864, 5120
ATT_SCALE = A_HEAD_DIM ** -0.5

VMEM_LIMIT = 56 * 1024 * 1024

NT_DIMS = (((1,), (1,)), ((), ()))
TN_DIMS = (((0,), (0,)), ((), ()))


def _sigmoid(x):
    return 0.5 * jnp.tanh(0.5 * x) + 0.5


def _silu(x):
    h = 0.5 * x
    return h * jnp.tanh(h) + h


def _rms(x, gain):
    return x * lax.rsqrt(jnp.mean(x * x, axis=-1, keepdims=True) + EPS) * gain


def _head_masks(width, head_dim, n_heads):
    lane = lax.broadcasted_iota(jnp.int32, (1, width), 1)
    return [(lane >= h * head_dim) & (lane < (h + 1) * head_dim) for h in range(n_heads)]


def _stack_heads(q, masks):
    zero = jnp.zeros_like(q)
    return jnp.concatenate([jnp.where(m, q, zero) for m in masks], axis=0)


def _unstack_heads(o_all, masks, t):
    n = len(masks)
    out = o_all[(n - 1) * t:n * t]
    for h in range(n - 2, -1, -1):
        out = jnp.where(masks[h], o_all[h * t:(h + 1) * t], out)
    return out


def _expand_cols(cols, masks):
    out = jnp.broadcast_to(cols[-1], (cols[-1].shape[0], masks[0].shape[1]))
    for h in range(len(masks) - 2, -1, -1):
        out = jnp.where(masks[h], cols[h], out)
    return out


def _mem_kv_kernel(mem_ref, gain_ref, w_ref, out_ref, out_t_ref):
    hm = _rms(mem_ref[0], gain_ref[...]).astype(BF16)
    kv = jnp.dot(hm, w_ref[...], preferred_element_type=F32)
    out_ref[0] = kv
    out_t_ref[0] = kv.T


def _mem_kv_call(mem, gain, w_bf):
    b = mem.shape[0]
    return pl.pallas_call(
        _mem_kv_kernel,
        out_shape=(jax.ShapeDtypeStruct((b, MEM_LEN, 2 * D_M), F32),
                   jax.ShapeDtypeStruct((b, 2 * D_M, MEM_LEN), F32)),
        grid=(b,),
        in_specs=[pl.BlockSpec((1, MEM_LEN, D_MODEL), lambda i: (i, 0, 0)),
                  pl.BlockSpec((1, D_MODEL), lambda i: (0, 0)),
                  pl.BlockSpec((D_MODEL, 2 * D_M), lambda i: (0, 0))],
        out_specs=(pl.BlockSpec((1, MEM_LEN, 2 * D_M), lambda i: (i, 0, 0)),
                   pl.BlockSpec((1, 2 * D_M, MEM_LEN), lambda i: (i, 0, 0))),
        name="mem_kv",
    )(mem, gain, w_bf)


IN_TILE = 256
IN_CHUNK = 256
D_QKV = 3 * D_A
D_GQ = 3 * D_A
D_HB = 3 * D_B
D_ZG = 3 * D_MODEL


def _inproj_kernel(x_ref, gain_ref, w_ref, zr_ref, c1_ref, c2_ref, c3_ref, cm_ref, sst_ref, lblt_ref, nh4_ref,
                   qkv0_ref, qkv1_ref, qkv2_ref, gq_ref, hb_ref, zg_ref, fb_ref, kvt0_ref, kvt1_ref, kvt2_ref,
                   u_ref, nst_ref, scr_ref):
    _sample_rows(zr_ref, c1_ref, c2_ref, c3_ref, cm_ref, sst_ref, lblt_ref, nh4_ref, u_ref, nst_ref)
    qkv_refs = (qkv0_ref, qkv1_ref, qkv2_ref)
    kvt_refs = (kvt0_ref, kvt1_ref, kvt2_ref)
    h = _rms(x_ref[0], gain_ref[...]).astype(BF16)
    t = IN_TILE
    for c in range(D_IN // IN_CHUNK):
        lo = c * IN_CHUNK
        zc = jnp.dot(h, w_ref[:, lo:lo + IN_CHUNK], preferred_element_type=F32)
        if lo < OFF_GA:
            sec, g = divmod(c, N_GROUPS)
            dil = WIN_GROUPS[g][1]
            dst = slice(sec * D_A, (sec + 1) * D_A)
            if dil == 1:
                qkv_refs[g][0, 0, :, dst] = zc.astype(BF16)
            else:
                for k in range(IN_CHUNK // LANES):
                    scr_ref[k] = zc[:, k * LANES:(k + 1) * LANES]
                for r in range(dil):
                    qkv_refs[g][0, r, :, dst] = jnp.concatenate(
                        [scr_ref[k, pl.ds(r, t // dil, stride=dil), :] for k in range(IN_CHUNK // LANES)],
                        axis=1).astype(BF16)
            if sec > 0:
                width = kvt_refs[g].shape[2]
                kvt_refs[g][0, (sec - 1) * D_A:sec * D_A, :] = zc.T[:, t - width:]
        elif lo < OFF_QB:
            gq_ref[0, :, 0:D_A] = zc.astype(BF16)
        elif lo < OFF_FB:
            hb_ref[0, :, lo - OFF_QB:lo - OFF_QB + IN_CHUNK] = zc.astype(BF16)
        elif lo < OFF_IB:
            fb_ref[0, :, lo - OFF_FB:lo - OFF_FB + IN_CHUNK] = zc
        elif lo < OFF_QM:
            hb_ref[0, :, lo - OFF_IB + D_B:lo - OFF_IB + D_B + IN_CHUNK] = zc.astype(BF16)
        elif lo < OFF_ZG:
            gq_ref[0, :, lo - OFF_QM + D_A:lo - OFF_QM + D_A + IN_CHUNK] = zc.astype(BF16)
        else:
            zg_ref[0, :, lo - OFF_ZG:lo - OFF_ZG + IN_CHUNK] = zc.astype(BF16)


def _sample_specs(ops, nt, first_row):
    first_blk = first_row // S_TILE

    def rows(a):
        return pl.BlockSpec((S_TILE,) + a.shape[1:], lambda i, j: (first_blk + i * nt + j, 0, 0))

    def const(a):
        return pl.BlockSpec(a.shape, lambda i, j: (0, 0), pipeline_mode=pl.Buffered(1))

    return [rows(a) for a in ops[:6]] + [const(a) for a in ops[6:]]


def _inproj_call(x, gain, w_bf, sample_ops):
    b, s, _ = x.shape
    t = IN_TILE
    nt = s // t
    n_rows = b * nt * S_TILE

    def tok(width):
        return pl.BlockSpec((1, t, width), lambda i, j: (i, j, 0))

    def srow(shape):
        return pl.BlockSpec((S_TILE,) + shape, lambda i, j: (i * nt + j, 0, 0))

    qkv_shapes, qkv_specs = [], []
    for _, dil in WIN_GROUPS:
        qkv_shapes.append(jax.ShapeDtypeStruct((b, dil, s // dil, D_QKV), BF16))
        qkv_specs.append(pl.BlockSpec((1, dil, t // dil, D_QKV), lambda i, j: (i, 0, j, 0)))
    kvt_shapes, kvt_specs = [], []
    for window, _ in WIN_GROUPS:
        length = min(window, s)
        width = min(t, length)
        first = nt - length // width
        kvt_shapes.append(jax.ShapeDtypeStruct((b, 2 * D_A, length), F32))
        kvt_specs.append(pl.BlockSpec((1, 2 * D_A, width),
                                      lambda i, j, first=first: (i, 0, jnp.maximum(j - first, 0))))
    return pl.pallas_call(
        _inproj_kernel,
        out_shape=(*qkv_shapes,
                   jax.ShapeDtypeStruct((b, s, D_GQ), BF16),
                   jax.ShapeDtypeStruct((b, s, D_HB), BF16),
                   jax.ShapeDtypeStruct((b, s, D_ZG), BF16),
                   jax.ShapeDtypeStruct((b, s, D_B), F32),
                   *kvt_shapes,
                   jax.ShapeDtypeStruct((n_rows, U_ROWS, LANES), F32),
                   jax.ShapeDtypeStruct((n_rows, HG_HEADS * HG_DIM, HG_DIM), F32)),
        grid=(b, nt),
        in_specs=[tok(D_MODEL),
                  pl.BlockSpec((1, D_MODEL), lambda i, j: (0, 0)),
                  pl.BlockSpec((D_MODEL, D_IN), lambda i, j: (0, 0), pipeline_mode=pl.Buffered(1)),
                  *_sample_specs(sample_ops, nt, 0)],
        out_specs=(*qkv_specs, tok(D_GQ), tok(D_HB), tok(D_ZG), tok(D_B), *kvt_specs,
                   srow((U_ROWS, LANES)), srow((HG_HEADS * HG_DIM, HG_DIM))),
        scratch_shapes=[pltpu.VMEM((IN_CHUNK // LANES, t, LANES), F32)],
        compiler_params=pltpu.CompilerParams(
            dimension_semantics=("arbitrary", "arbitrary"), vmem_limit_bytes=VMEM_LIMIT),
        name="inproj",
    )(x, gain, w_bf, *sample_ops)


def _dilated_block(q, kk, vv, neg, masks):
    qs = _stack_heads(q * jnp.asarray(ATT_SCALE, BF16), masks)
    s = lax.dot_general(qs, kk, NT_DIMS, preferred_element_type=F32) + neg
    mx = jnp.max(s, axis=-1, keepdims=True)
    p = jnp.exp(s - mx)
    l = jnp.sum(p, axis=-1, keepdims=True)
    o_all = jnp.dot(p.astype(BF16), vv, preferred_element_type=F32)
    l_e = _expand_cols([l[h * QBLK:(h + 1) * QBLK] for h in range(A_HEADS)], masks)
    mx_e = _expand_cols([mx[h * QBLK:(h + 1) * QBLK] for h in range(A_HEADS)], masks)
    return _unstack_heads(o_all, masks, QBLK) / l_e, mx_e + jnp.log(l_e)


DIL_STEP = 2048
OG_SLABS = 2 * D_A // LANES


def _make_dilated_kernel(dil):
    nblk = DIL_STEP // (dil * QBLK)

    def body(prev_ref, cur_ref, out_ref):
        first_step = pl.program_id(1) == 0
        masks = _head_masks(D_A, A_HEAD_DIM, A_HEADS)
        qi = lax.broadcasted_iota(jnp.int32, (A_HEADS * QBLK, 2 * QBLK), 0) % QBLK
        kj = lax.broadcasted_iota(jnp.int32, (A_HEADS * QBLK, 2 * QBLK), 1)
        band = (kj >= qi) & (kj <= qi + QBLK)
        neg_band = jnp.where(band, 0.0, -jnp.inf)
        neg_first = jnp.where(first_step, jnp.where(band & (kj >= QBLK), 0.0, -jnp.inf), neg_band)

        for r in range(dil):
            for jb in range(nblk):
                rows = slice(jb * QBLK, (jb + 1) * QBLK)
                q = cur_ref[0, r, rows, 0:D_A]
                if jb == 0:
                    kv = [jnp.concatenate([prev_ref[0, r, :, c * D_A:(c + 1) * D_A],
                                           cur_ref[0, r, rows, c * D_A:(c + 1) * D_A]], axis=0) for c in (1, 2)]
                else:
                    both = slice((jb - 1) * QBLK, (jb + 1) * QBLK)
                    kv = [cur_ref[0, r, both, c * D_A:(c + 1) * D_A] for c in (1, 2)]
                o, lse = _dilated_block(q, kv[0], kv[1], neg_first if jb == 0 else neg_band, masks)
                res = (o, lse)
                for k in range(OG_SLABS):
                    src = res[k // 2][:, (k % 2) * LANES:(k % 2 + 1) * LANES]
                    if dil == 1:
                        out_ref[0, k, rows, :] = src
                    else:
                        out_ref[0, k, pl.ds(jb * QBLK * dil + r, QBLK, stride=dil), :] = src
    return body


def _dilated_call(qkv, dil, g):
    b, _, n, _ = qkv.shape
    rows = DIL_STEP // dil
    nb = n // rows
    return pl.pallas_call(
        _make_dilated_kernel(dil),
        out_shape=jax.ShapeDtypeStruct((b, OG_SLABS, n * dil, LANES), F32),
        grid=(b, nb),
        in_specs=[pl.BlockSpec((1, dil, QBLK, D_QKV),
                               lambda i, u: (i, 0, jnp.maximum(u * (rows // QBLK) - 1, 0), 0)),
                  pl.BlockSpec((1, dil, rows, D_QKV), lambda i, u: (i, 0, u, 0))],
        out_specs=pl.BlockSpec((1, OG_SLABS, DIL_STEP, LANES), lambda i, u: (i, 0, u, 0)),
        compiler_params=pltpu.CompilerParams(
            dimension_semantics=("arbitrary", "arbitrary"), vmem_limit_bytes=VMEM_LIMIT),
        name=f"dilated_attn_g{g}",
    )(qkv, qkv)


def _lower_bound(l0, l1):
    m = jnp.maximum(l0, l1)
    e0, e1 = jnp.exp(l0 - m), jnp.exp(l1 - m)
    return e0 / (e0 + e1)


def _head_rms(ob, gain):
    parts = []
    for h in range(HG_HEADS):
        sl = slice(h * HG_DIM, (h + 1) * HG_DIM)
        parts.append(_rms(ob[:, sl], gain[:, sl]))
    return jnp.concatenate(parts, axis=1)


def _merge_out(x, ua, ub, um, zg, wa_ref, wb_ref, wm_ref, wo_ref, nf):
    gates = _sigmoid(zg.astype(BF16))
    merged = None
    for k, (u, w_ref) in enumerate(((ua, wa_ref), (ub, wb_ref), (um, wm_ref))):
        proj = jnp.dot(u.astype(BF16), w_ref[...], preferred_element_type=F32).astype(BF16)
        term = gates[:, k * D_MODEL:(k + 1) * D_MODEL] * proj
        merged = term if merged is None else merged + term
    y = x + jnp.dot(merged, wo_ref[...], preferred_element_type=F32)
    return _rms(y, nf)


TAIL_TILE = 256


def _cumsum_rows(g, tri):
    g1 = g.astype(BF16)
    g2 = (g - g1.astype(F32)).astype(BF16)
    return jnp.dot(tri, g1, preferred_element_type=F32) + jnp.dot(tri, g2, preferred_element_type=F32)


def _hgrn_tile(fb, qb, v, lb, st_ref):
    t = fb.shape[0]
    c, sub = HG_CHUNK, HG_SUB
    n_sub = c // sub
    f = lb + (1.0 - lb) * _sigmoid(fb)
    kk = 1.0 - f
    row = lax.broadcasted_iota(jnp.int32, (t, t), 0)
    col = lax.broadcasted_iota(jnp.int32, (t, t), 1)
    same_chunk = (row // c) == (col // c)
    gcs = _cumsum_rows(jnp.log(f), (same_chunk & (col <= row)).astype(BF16))
    q = _silu(qb.astype(F32))

    loc = lax.broadcasted_iota(jnp.int32, (c, 1), 0)
    qt_c, kt_ci, qe_c, kd_c, dec_c = [], [[] for _ in range(n_sub)], [], [], []
    for ci in range(t // c):
        rows = slice(ci * c, (ci + 1) * c)
        g_c, q_c, k_c = gcs[rows], q[rows], kk[rows]
        refs = [jnp.zeros((1, D_B), F32)] + [g_c[i * sub - 1:i * sub] for i in range(1, n_sub)]
        own = refs[n_sub - 1]
        for i in range(n_sub - 2, -1, -1):
            own = jnp.where(loc < (i + 1) * sub, refs[i], own)
        qt_c.append((q_c * jnp.exp(g_c - own)).astype(BF16))
        k_own = k_c * jnp.exp(own - g_c)
        for i in range(n_sub):
            pieces = []
            for jb in range(n_sub):
                piece = k_own[jb * sub:(jb + 1) * sub]
                if jb < i:
                    piece = piece * jnp.exp(refs[i] - refs[jb])
                elif jb > i:
                    piece = jnp.zeros_like(piece)
                pieces.append(piece)
            kt_ci[i].append(jnp.concatenate(pieces, axis=0).astype(BF16))
        g_last = g_c[c - 1:c]
        qe_c.append((q_c * jnp.exp(g_c)).astype(BF16))
        kd_c.append((k_c * jnp.exp(g_last - g_c)).astype(BF16))
        dec_c.append(jnp.exp(g_last))
    qt = jnp.concatenate(qt_c, axis=0)
    kts = [jnp.concatenate(parts, axis=0) for parts in kt_ci]
    sub_of_row = (lax.broadcasted_iota(jnp.int32, (t, 1), 0) % c) // sub
    causal = same_chunk & (col <= row)

    outs = []
    for h in range(HG_HEADS):
        sl = slice(h * HG_DIM, (h + 1) * HG_DIM)
        zero = jnp.zeros((t, HG_DIM), BF16)
        q_big = jnp.concatenate([jnp.where(sub_of_row == i, qt[:, sl], zero) for i in range(n_sub)], axis=1)
        k_big = jnp.concatenate([kts[i][:, sl] for i in range(n_sub)], axis=1)
        att = lax.dot_general(q_big, k_big, NT_DIMS, preferred_element_type=F32)
        att = jnp.where(causal, att, 0.0).astype(BF16)
        o_intra = jnp.dot(att, v[:, sl], preferred_element_type=F32)
        st = st_ref[h]
        inter = []
        for ci in range(t // c):
            rows = slice(ci * c, (ci + 1) * c)
            inter.append(lax.dot_general(qe_c[ci][:, sl], st.astype(BF16), NT_DIMS, preferred_element_type=F32))
            st = st * dec_c[ci][:, sl] + lax.dot_general(
                v[rows, sl], kd_c[ci][:, sl], TN_DIMS, preferred_element_type=F32)
        st_ref[h] = st
        outs.append(o_intra + jnp.concatenate(inter, axis=0))
    return jnp.concatenate(outs, axis=1)


def _prompt_tail_kernel(x_ref, gq_ref, hb_ref, zg_ref, fb_ref, o1_ref, o2_ref, o3_ref, mkv_ref, lbl_ref,
                        nh_ref, nf_ref, wa_ref, wb_ref, wm_ref, wo_ref,
                        zr_ref, c1_ref, c2_ref, c3_ref, cm_ref, sst_ref, lblt_ref, nh4_ref, ua_ref, nsta_ref,
                        y_ref, hst_ref, u_ref, nst_ref, st_ref):
    j = pl.program_id(1)
    t = TAIL_TILE

    @pl.when(j == 0)
    def _():
        st_ref[...] = jnp.zeros_like(st_ref)

    u_ref[0] = ua_ref[...]
    nst_ref[0] = nsta_ref[...]
    _sample_rows(zr_ref, c1_ref, c2_ref, c3_ref, cm_ref, sst_ref, lblt_ref, nh4_ref, u_ref.at[1], nst_ref.at[1])

    os_ = [jnp.concatenate([r[0, 0], r[0, 1]], axis=1) for r in (o1_ref, o2_ref, o3_ref)]
    ls_ = [jnp.concatenate([r[0, 2], r[0, 3]], axis=1) for r in (o1_ref, o2_ref, o3_ref)]
    mx = jnp.maximum(jnp.maximum(ls_[0], ls_[1]), ls_[2])
    es_ = [jnp.exp(l - mx) for l in ls_]
    oa = (es_[0] * os_[0] + es_[1] * os_[1] + es_[2] * os_[2]) / (es_[0] + es_[1] + es_[2])
    ua = oa * _silu(gq_ref[0, :, 0:D_A].astype(F32))

    masks = _head_masks(D_M, A_HEAD_DIM, A_HEADS)
    qm = gq_ref[0, :, D_A:2 * D_A] * jnp.asarray(ATT_SCALE, BF16)
    qs = _stack_heads(qm, masks)
    mk = mkv_ref[0, :, 0:D_M].astype(BF16)
    mv = mkv_ref[0, :, D_M:2 * D_M].astype(BF16)
    s = lax.dot_general(qs, mk, NT_DIMS, preferred_element_type=F32)
    p = jnp.exp(s - jnp.max(s, axis=-1, keepdims=True))
    l = jnp.sum(p, axis=-1, keepdims=True)
    om = _unstack_heads(jnp.dot(p.astype(BF16), mv, preferred_element_type=F32) / l, masks, t)
    um = om * _silu(gq_ref[0, :, 2 * D_A:3 * D_A].astype(F32))

    lb = _lower_bound(lbl_ref[0:1], lbl_ref[1:2])
    ob = _hgrn_tile(fb_ref[0], hb_ref[0, :, 0:D_B], hb_ref[0, :, D_B:2 * D_B], lb, st_ref)
    ub = _head_rms(ob, nh_ref[...]) * _silu(hb_ref[0, :, 2 * D_B:3 * D_B].astype(F32))

    y_ref[0] = _merge_out(x_ref[0], ua, ub, um, zg_ref[0], wa_ref, wb_ref, wm_ref, wo_ref, nf_ref[...])

    @pl.when(j == pl.num_programs(1) - 1)
    def _():
        for h in range(HG_HEADS):
            hst_ref[0, h] = st_ref[h].T


def _prompt_tail_call(x, gq, hb, zg, fb, o1, o2, o3, mkv, lbl, nh, nf, wa, wb, wm, wo,
                      sample_ops, u_first, nst_first):
    b, s, _ = x.shape
    t = TAIL_TILE
    nt = s // t
    db = sample_ops[0].shape[0]
    half = u_first.shape[0]
    assert half + b * nt * S_TILE == db and 2 * half == db, "sample rows must split evenly over both grids"

    def tok(width):
        return pl.BlockSpec((1, t, width), lambda i, j: (i, j, 0))

    def const(shape):
        return pl.BlockSpec(shape, lambda i, j: (0,) * len(shape), pipeline_mode=pl.Buffered(1))

    def first(a):
        return pl.BlockSpec((S_TILE,) + a.shape[1:], lambda i, j: (i * nt + j, 0, 0))

    def halves(shape):
        return pl.BlockSpec((2, S_TILE) + shape, lambda i, j: (0, i * nt + j, 0, 0))

    y, hst, u, nst = pl.pallas_call(
        _prompt_tail_kernel,
        out_shape=(jax.ShapeDtypeStruct((b, s, D_MODEL), F32),
                   jax.ShapeDtypeStruct((b, HG_HEADS, HG_DIM, HG_DIM), F32),
                   jax.ShapeDtypeStruct((2, half, U_ROWS, LANES), F32),
                   jax.ShapeDtypeStruct((2, half, HG_HEADS * HG_DIM, HG_DIM), F32)),
        grid=(b, nt),
        in_specs=[tok(D_MODEL), tok(D_GQ), tok(D_HB), tok(D_ZG), tok(D_B),
                  *[pl.BlockSpec((1, OG_SLABS, t, LANES), lambda i, j: (i, 0, j, 0))] * N_GROUPS,
                  pl.BlockSpec((1, MEM_LEN, 2 * D_M), lambda i, j: (i, 0, 0)),
                  const((2, D_B)), const((1, D_B)), const((1, D_MODEL)),
                  const((D_A, D_MODEL)), const((D_B, D_MODEL)), const((D_M, D_MODEL)),
                  const((D_MODEL, D_MODEL)),
                  *_sample_specs(sample_ops, nt, half), first(u_first), first(nst_first)],
        out_specs=(tok(D_MODEL),
                   pl.BlockSpec((1, HG_HEADS, HG_DIM, HG_DIM), lambda i, j: (i, 0, 0, 0)),
                   halves((U_ROWS, LANES)), halves((HG_HEADS * HG_DIM, HG_DIM))),
        scratch_shapes=[pltpu.VMEM((HG_HEADS, HG_DIM, HG_DIM), F32)],
        compiler_params=pltpu.CompilerParams(
            dimension_semantics=("arbitrary", "arbitrary"), vmem_limit_bytes=VMEM_LIMIT),
        name="prompt_tail",
    )(x, gq, hb, zg, fb, o1, o2, o3, mkv, lbl, nh, nf, wa, wb, wm, wo, *sample_ops, u_first, nst_first)
    return y, hst, u.reshape(db, U_ROWS * LANES), nst.reshape(db, HG_HEADS * HG_DIM, HG_DIM)


S_COLS = 1024


def _sample_inproj_kernel(x_ref, gain_ref, w_ref, zr_ref, zg_ref, kvt_ref, wbf_ref):
    c = pl.program_id(0)
    h = _rms(x_ref[...], gain_ref[...]).astype(BF16)
    w = w_ref[...].astype(BF16)
    wbf_ref[...] = w
    z = jnp.dot(h, w, preferred_element_type=F32)
    for k in range(S_COLS // LANES):
        zr_ref[:, k, :] = z[:, k * LANES:(k + 1) * LANES]
    zg_ref[...] = z
    for step in range(OFF_GA // S_COLS + 1):
        pieces = [(g, kv, (OFF_KA + (kv * N_GROUPS + g) * D_A) % S_COLS)
                  for g in range(N_GROUPS) for kv in range(2)
                  if (OFF_KA + (kv * N_GROUPS + g) * D_A) // S_COLS == step]

        @pl.when(c == step)
        def _(pieces=pieces):
            zt = z.T
            for g, kv, row0 in pieces:
                kvt_ref[g, kv * D_A:(kv + 1) * D_A, :] = zt[row0:row0 + D_A]


def _sample_inproj_call(x, gain, w):
    db = x.shape[0]
    first_zg = OFF_ZG // S_COLS
    return pl.pallas_call(
        _sample_inproj_kernel,
        out_shape=(jax.ShapeDtypeStruct((db, Z_ROWS, LANES), F32),
                   jax.ShapeDtypeStruct((db, D_ZG), F32),
                   jax.ShapeDtypeStruct((N_GROUPS, 2 * D_A, db), F32),
                   jax.ShapeDtypeStruct((D_MODEL, D_IN), BF16)),
        grid=(D_IN // S_COLS,),
        in_specs=[pl.BlockSpec((db, D_MODEL), lambda c: (0, 0)),
                  pl.BlockSpec((1, D_MODEL), lambda c: (0, 0)),
                  pl.BlockSpec((D_MODEL, S_COLS), lambda c: (0, c))],
        out_specs=(pl.BlockSpec((db, S_COLS // LANES, LANES), lambda c: (0, c, 0)),
                   pl.BlockSpec((db, S_COLS), lambda c: (0, jnp.maximum(c - first_zg, 0))),
                   pl.BlockSpec((N_GROUPS, 2 * D_A, db), lambda c: (0, 0, 0)),
                   pl.BlockSpec((D_MODEL, S_COLS), lambda c: (0, c))),
        compiler_params=pltpu.CompilerParams(dimension_semantics=("arbitrary",)),
        name="sample_inproj",
    )(x, gain, w)


S_TILE = 1
Z_ROWS = D_IN // LANES
U_ROWS = 8


def _sample_rows(zr_ref, c1_ref, c2_ref, c3_ref, cm_ref, st_ref, lblt_ref, nh_ref, u_ref, nst_ref):
    bt = S_TILE
    pad = [jnp.zeros((LANES - bt * Z_ROWS, LANES), F32)] if bt * Z_ROWS < LANES else []
    zt = jnp.concatenate([zr_ref[b] for b in range(bt)] + pad, axis=0).T

    def col(b, off, n=HG_DIM):
        j, l = divmod(off, LANES)
        return zt[l:l + n, Z_ROWS * b + j:Z_ROWS * b + j + 1]

    lbt = lblt_ref[...]
    lb_col = _lower_bound(lbt[:, 0:1], lbt[:, 1:2])
    nh = nh_ref[...]

    def col2(b, off):
        return jnp.concatenate([col(b, off), col(b, off + LANES)], axis=0)

    def per_head(x):
        return jnp.concatenate([jnp.sum(x[h * A_HEAD_DIM:(h + 1) * A_HEAD_DIM], axis=0, keepdims=True)
                                for h in range(A_HEADS)], axis=0)

    def spread(x):
        return jnp.concatenate([jnp.broadcast_to(x[h:h + 1], (A_HEAD_DIM, 1)) for h in range(A_HEADS)], axis=0)

    def attend(b, srcs):
        parts, new_scores, qs = [], [], []
        for c_ref, dil, q_off, k_off, _ in srcs:
            q = col2(b, q_off) * ATT_SCALE
            qs.append(q)
            s = per_head(c_ref[b, 0:D_A, :] * q)
            if dil > 1:
                lane = lax.broadcasted_iota(jnp.int32, s.shape, 1)
                s = jnp.where(lane % dil == 0, s, -jnp.inf)
            parts.append(s)
            if k_off is not None:
                new_scores.append(per_head(col2(b, k_off) * q))
        if new_scores:
            lane = lax.broadcasted_iota(jnp.int32, (A_HEADS, LANES), 1)
            slab = jnp.full((A_HEADS, LANES), -jnp.inf, F32)
            for k, sn in enumerate(new_scores):
                slab = jnp.where(lane == k, sn, slab)
            parts.append(slab)
        s_all = jnp.concatenate(parts, axis=1)
        p_all = jnp.exp(s_all - jnp.max(s_all, axis=1, keepdims=True))
        den = jnp.sum(p_all, axis=1, keepdims=True)
        acc = [jnp.zeros((A_HEAD_DIM, LANES), F32) for _ in range(A_HEADS)]
        lo = 0
        for c_ref, _, _, _, _ in srcs:
            length = c_ref.shape[2]
            for k in range(length // LANES):
                ls = slice(k * LANES, (k + 1) * LANES)
                for h in range(A_HEADS):
                    vt = c_ref[b, D_A + h * A_HEAD_DIM:D_A + (h + 1) * A_HEAD_DIM, ls]
                    acc[h] = acc[h] + vt * p_all[h:h + 1, lo + k * LANES:lo + (k + 1) * LANES]
            lo += length
        num = jnp.sum(jnp.concatenate(acc, axis=0), axis=1, keepdims=True)
        k = 0
        for _, _, _, k_off, v_off in srcs:
            if k_off is not None:
                num = num + spread(p_all[:, lo + k:lo + k + 1]) * col2(b, v_off)
                k += 1
        return num / spread(den)

    window = [(c_ref, dil, OFF_QA + g * D_A, OFF_KA + g * D_A, OFF_VA + g * D_A)
              for g, (c_ref, (_, dil)) in enumerate(zip((c1_ref, c2_ref, c3_ref), WIN_GROUPS))]
    memory = [(cm_ref, 1, OFF_QM, None, None)]

    for b in range(bt):
        ua = attend(b, window) * _silu(col2(b, OFF_GA))
        um = attend(b, memory) * _silu(col2(b, OFF_GM))
        branch_cols = [ua[0:LANES], ua[LANES:2 * LANES], um[0:LANES], um[LANES:2 * LANES]]

        lane = lax.broadcasted_iota(jnp.int32, (LANES, LANES), 1)
        ut = jnp.zeros((LANES, LANES), F32)
        for k, cvec in enumerate(branch_cols):
            ut = ut + jnp.where(lane == k, cvec, 0.0)
        u_ref[b, 0:4, :] = ut.T[0:4, :]

        orows = []
        for h in range(HG_HEADS):
            sl = slice(h * HG_DIM, (h + 1) * HG_DIM)
            fcol = lb_col[sl] + (1.0 - lb_col[sl]) * _sigmoid(col(b, OFF_FB + h * HG_DIM))
            qcol = _silu(col(b, OFF_QB + h * HG_DIM))
            vrow = zr_ref[b, OFF_IB // LANES + h:OFF_IB // LANES + h + 1, :]
            sn = fcol * st_ref[b, sl, :] + (1.0 - fcol) * vrow
            nst_ref[b, sl, :] = sn
            orows.append(jnp.sum(sn * qcol, axis=0, keepdims=True))
        gate = zr_ref[b, OFF_GB // LANES:OFF_GB // LANES + HG_HEADS, :]
        u_ref[b, 4:4 + HG_HEADS, :] = _rms(jnp.concatenate(orows, axis=0), nh) * _silu(gate)


def _stored_view(cache):
    db, length = cache.shape[0], cache.shape[1]
    return jnp.transpose(cache, (0, 2, 3, 4, 1)).reshape(db, 2 * D_A, length)


def _sample_out_kernel(x_ref, u_ref, zg_ref, nf_ref, wa_ref, wb_ref, wm_ref, wo_ref, y_ref):
    u = u_ref[...]
    y_ref[...] = _merge_out(x_ref[...], u[:, 0:D_A], u[:, D_A + D_M:], u[:, D_A:D_A + D_M], zg_ref[...],
                            wa_ref, wb_ref, wm_ref, wo_ref, nf_ref[...])


def _sample_out_call(x, u, zg, nf, wa, wb, wm, wo):
    db = x.shape[0]

    def full(a):
        return pl.BlockSpec(a.shape, lambda i: (0,) * a.ndim)

    args = (x, u, zg, nf, wa, wb, wm, wo)
    return pl.pallas_call(
        _sample_out_kernel,
        out_shape=jax.ShapeDtypeStruct((db, D_MODEL), F32),
        grid=(1,),
        in_specs=[full(a) for a in args],
        out_specs=pl.BlockSpec((db, D_MODEL), lambda i: (0, 0)),
        compiler_params=pltpu.CompilerParams(vmem_limit_bytes=VMEM_LIMIT),
        name="sample_out",
    )(*args)


def _cache_from_stored(kvt):
    b, _, length = kvt.shape
    return jnp.transpose(kvt.reshape(b, 2, A_HEADS, A_HEAD_DIM, length), (0, 4, 1, 2, 3))[None]


def kernel(x_prompt, x_sample, mem_prompt, cache_w1_kv, cache_w2_kv, cache_w3_kv, cache_mem_kv, state_hgrn,
           norm_in, w_in, lb_logits, norm_hgrn, norm_mem, w_mem_kv, w_branch_a, w_branch_b, w_branch_m,
           w_out, norm_final):
    b, s, _ = x_prompt.shape
    db = x_sample.shape[0]
    wa, wb, wm, wo = (w[0].astype(BF16) for w in (w_branch_a, w_branch_b, w_branch_m, w_out))
    gain_in = norm_in[0][None]
    nh = norm_hgrn[0][None]
    nf = norm_final[None]
    lbl = lb_logits.astype(F32)

    xs = x_sample.reshape(db, D_MODEL)
    zr, zg_s, kvt_s, w_in_bf = _sample_inproj_call(xs, gain_in, w_in[0])
    mkv, mkv_t = _mem_kv_call(mem_prompt, norm_mem[0][None], w_mem_kv[0].astype(BF16))
    sample_ops = (zr,
                  *[_stored_view(c[0]) for c in (cache_w1_kv, cache_w2_kv, cache_w3_kv, cache_mem_kv)],
                  state_hgrn[0].reshape(db, HG_HEADS * HG_DIM, HG_DIM), lbl.T, nh.reshape(HG_HEADS, HG_DIM))
    qkv0, qkv1, qkv2, gq, hb, zg, fb, kvt0, kvt1, kvt2, u_first, nst_first = _inproj_call(
        x_prompt, gain_in, w_in_bf, sample_ops)
    og = [_dilated_call(qkv, dil, g) for g, (qkv, (_, dil)) in enumerate(zip((qkv0, qkv1, qkv2), WIN_GROUPS))]
    y_prompt, hg_prompt, u, nst = _prompt_tail_call(
        x_prompt, gq, hb, zg, fb, og[0], og[1], og[2], mkv, lbl, nh, nf, wa, wb, wm, wo,
        sample_ops, u_first, nst_first)
    pw = [_cache_from_stored(kvt) for kvt in (kvt0, kvt1, kvt2)]
    new_mem = _cache_from_stored(mkv_t)

    y_sample = _sample_out_call(xs, u, zg_s, nf, wa, wb, wm, wo).reshape(db, 1, D_MODEL)
    sw = [jnp.transpose(kvt_s[g].reshape(2, A_HEADS, A_HEAD_DIM, db), (3, 0, 1, 2))[None, :, None]
          for g in range(N_GROUPS)]
    new_hg_sample = nst.reshape(1, db, HG_HEADS, HG_DIM, HG_DIM)

    return (y_prompt, y_sample, pw[0], pw[1], pw[2], new_mem, hg_prompt[None],
            sw[0], sw[1], sw[2], new_hg_sample)
```

```python
import jax
import jax.numpy as jnp
from jax import lax
from jax.experimental import pallas as pl
from jax.experimental.pallas import tpu as pltpu

F32 = jnp.float32
BF16 = jnp.bfloat16

D_MODEL = 1024
WIN_GROUPS = ((128, 1), (512, 4), (2048, 16))
N_GROUPS = 3
A_HEADS = 4
A_HEAD_DIM = 64
D_A = 256
QBLK = 128
HG_HEADS = 4
HG_DIM = 128
D_B = 512
HG_CHUNK = 64
HG_SUB = 16
MEM_LEN = 256
D_M = 256
EPS = 1e-6
D_IN = 8192
LANES = 128
OFF_QA, OFF_KA, OFF_VA, OFF_GA = 0, 768, 1536, 2304
OFF_QB, OFF_FB, OFF_IB, OFF_GB = 2560, 3072, 3584, 4096
OFF_QM, OFF_GM, OFF_ZG = 4608, 4864, 5120
ATT_SCALE = A_HEAD_DIM ** -0.5

VMEM_LIMIT = 56 * 1024 * 1024

NT_DIMS = (((1,), (1,)), ((), ()))
TN_DIMS = (((0,), (0,)), ((), ()))


def _sigmoid(x):
    return 0.5 * jnp.tanh(0.5 * x) + 0.5


def _silu(x):
    h = 0.5 * x
    return h * jnp.tanh(h) + h


def _rms(x, gain):
    return x * lax.rsqrt(jnp.mean(x * x, axis=-1, keepdims=True) + EPS) * gain


def _head_masks(width, head_dim, n_heads):
    lane = lax.broadcasted_iota(jnp.int32, (1, width), 1)
    return [(lane >= h * head_dim) & (lane < (h + 1) * head_dim) for h in range(n_heads)]


def _stack_heads(q, masks):
    zero = jnp.zeros_like(q)
    return jnp.concatenate([jnp.where(m, q, zero) for m in masks], axis=0)


def _unstack_heads(o_all, masks, t):
    n = len(masks)
    out = o_all[(n - 1) * t:n * t]
    for h in range(n - 2, -1, -1):
        out = jnp.where(masks[h], o_all[h * t:(h + 1) * t], out)
    return out


def _expand_cols(cols, masks):
    out = jnp.broadcast_to(cols[-1], (cols[-1].shape[0], masks[0].shape[1]))
    for h in range(len(masks) - 2, -1, -1):
        out = jnp.where(masks[h], cols[h], out)
    return out


def _mem_kv_kernel(mem_ref, gain_ref, w_ref, out_ref, out_t_ref):
    hm = _rms(mem_ref[0], gain_ref[...]).astype(BF16)
    kv = jnp.dot(hm, w_ref[...], preferred_element_type=F32)
    out_ref[0] = kv
    out_t_ref[0] = kv.T


def _mem_kv_call(mem, gain, w_bf):
    b = mem.shape[0]
    return pl.pallas_call(
        _mem_kv_kernel,
        out_shape=(jax.ShapeDtypeStruct((b, MEM_LEN, 2 * D_M), F32),
                   jax.ShapeDtypeStruct((b, 2 * D_M, MEM_LEN), F32)),
        grid=(b,),
        in_specs=[pl.BlockSpec((1, MEM_LEN, D_MODEL), lambda i: (i, 0, 0)),
                  pl.BlockSpec((1, D_MODEL), lambda i: (0, 0)),
                  pl.BlockSpec((D_MODEL, 2 * D_M), lambda i: (0, 0))],
        out_specs=(pl.BlockSpec((1, MEM_LEN, 2 * D_M), lambda i: (i, 0, 0)),
                   pl.BlockSpec((1, 2 * D_M, MEM_LEN), lambda i: (i, 0, 0))),
        name="mem_kv",
    )(mem, gain, w_bf)


IN_TILE = 256
IN_CHUNK = 256
D_QKV = 3 * D_A
D_GQ = 3 * D_A
D_HB = 3 * D_B
D_ZG = 3 * D_MODEL


def _inproj_kernel(x_ref, gain_ref, w_ref, zr_ref, c1_ref, c2_ref, c3_ref, cm_ref, sst_ref, lblt_ref, nh4_ref,
                   qkv0_ref, qkv1_ref, qkv2_ref, gq_ref, hb_ref, zg_ref, fb_ref, kvt0_ref, kvt1_ref, kvt2_ref,
                   u_ref, nst_ref, scr_ref):
    qkv_refs = (qkv0_ref, qkv1_ref, qkv2_ref)
    kvt_refs = (kvt0_ref, kvt1_ref, kvt2_ref)
    h = _rms(x_ref[0], gain_ref[...]).astype(BF16)
    t = IN_TILE
    for c in range(D_IN // IN_CHUNK):
        lo = c * IN_CHUNK
        zc = jnp.dot(h, w_ref[:, lo:lo + IN_CHUNK], preferred_element_type=F32)
        if lo < OFF_GA:
            sec, g = divmod(c, N_GROUPS)
            dil = WIN_GROUPS[g][1]
            dst = slice(sec * D_A, (sec + 1) * D_A)
            if dil == 1:
                qkv_refs[g][0, 0, :, dst] = zc.astype(BF16)
            else:
                for k in range(IN_CHUNK // LANES):
                    scr_ref[k] = zc[:, k * LANES:(k + 1) * LANES]
                for r in range(dil):
                    qkv_refs[g][0, r, :, dst] = jnp.concatenate(
                        [scr_ref[k, pl.ds(r, t // dil, stride=dil), :] for k in range(IN_CHUNK // LANES)],
                        axis=1).astype(BF16)
            if sec > 0:
                width = kvt_refs[g].shape[2]
                kvt_refs[g][0, (sec - 1) * D_A:sec * D_A, :] = zc.T[:, t - width:]
        elif lo < OFF_QB:
            gq_ref[0, :, 0:D_A] = zc.astype(BF16)
        elif lo < OFF_FB:
            hb_ref[0, :, lo - OFF_QB:lo - OFF_QB + IN_CHUNK] = zc.astype(BF16)
        elif lo < OFF_IB:
            fb_ref[0, :, lo - OFF_FB:lo - OFF_FB + IN_CHUNK] = zc
        elif lo < OFF_QM:
            hb_ref[0, :, lo - OFF_IB + D_B:lo - OFF_IB + D_B + IN_CHUNK] = zc.astype(BF16)
        elif lo < OFF_ZG:
            gq_ref[0, :, lo - OFF_QM + D_A:lo - OFF_QM + D_A + IN_CHUNK] = zc.astype(BF16)
        else:
            zg_ref[0, :, lo - OFF_ZG:lo - OFF_ZG + IN_CHUNK] = zc.astype(BF16)
    _sample_rows(zr_ref, c1_ref, c2_ref, c3_ref, cm_ref, sst_ref, lblt_ref, nh4_ref, u_ref, nst_ref)


def _sample_specs(ops, nt, first_row):
    first_blk = first_row // S_TILE

    def rows(a):
        return pl.BlockSpec((S_TILE,) + a.shape[1:], lambda i, j: (first_blk + i * nt + j, 0, 0))

    def const(a):
        return pl.BlockSpec(a.shape, lambda i, j: (0, 0), pipeline_mode=pl.Buffered(1))

    return [rows(a) for a in ops[:6]] + [const(a) for a in ops[6:]]


def _inproj_call(x, gain, w_bf, sample_ops):
    b, s, _ = x.shape
    t = IN_TILE
    nt = s // t
    n_rows = b * nt * S_TILE

    def tok(width):
        return pl.BlockSpec((1, t, width), lambda i, j: (i, j, 0))

    def srow(shape):
        return pl.BlockSpec((S_TILE,) + shape, lambda i, j: (i * nt + j, 0, 0))

    qkv_shapes, qkv_specs = [], []
    for _, dil in WIN_GROUPS:
        qkv_shapes.append(jax.ShapeDtypeStruct((b, dil, s // dil, D_QKV), BF16))
        qkv_specs.append(pl.BlockSpec((1, dil, t // dil, D_QKV), lambda i, j: (i, 0, j, 0)))
    kvt_shapes, kvt_specs = [], []
    for window, _ in WIN_GROUPS:
        length = min(window, s)
        width = min(t, length)
        first = nt - length // width
        kvt_shapes.append(jax.ShapeDtypeStruct((b, 2 * D_A, length), F32))
        kvt_specs.append(pl.BlockSpec((1, 2 * D_A, width),
                                      lambda i, j, first=first: (i, 0, jnp.maximum(j - first, 0))))
    return pl.pallas_call(
        _inproj_kernel,
        out_shape=(*qkv_shapes,
                   jax.ShapeDtypeStruct((b, s, D_GQ), BF16),
                   jax.ShapeDtypeStruct((b, s, D_HB), BF16),
                   jax.ShapeDtypeStruct((b, s, D_ZG), BF16),
                   jax.ShapeDtypeStruct((b, s, D_B), F32),
                   *kvt_shapes,
                   jax.ShapeDtypeStruct((n_rows, U_ROWS, LANES), F32),
                   jax.ShapeDtypeStruct((n_rows, HG_HEADS * HG_DIM, HG_DIM), F32)),
        grid=(b, nt),
        in_specs=[tok(D_MODEL),
                  pl.BlockSpec((1, D_MODEL), lambda i, j: (0, 0)),
                  pl.BlockSpec((D_MODEL, D_IN), lambda i, j: (0, 0), pipeline_mode=pl.Buffered(1)),
                  *_sample_specs(sample_ops, nt, 0)],
        out_specs=(*qkv_specs, tok(D_GQ), tok(D_HB), tok(D_ZG), tok(D_B), *kvt_specs,
                   srow((U_ROWS, LANES)), srow((HG_HEADS * HG_DIM, HG_DIM))),
        scratch_shapes=[pltpu.VMEM((IN_CHUNK // LANES, t, LANES), F32)],
        compiler_params=pltpu.CompilerParams(
            dimension_semantics=("arbitrary", "arbitrary"), vmem_limit_bytes=VMEM_LIMIT),
        name="inproj",
    )(x, gain, w_bf, *sample_ops)


def _dilated_block(q, kk, vv, neg, masks):
    qs = _stack_heads(q * jnp.asarray(ATT_SCALE, BF16), masks)
    s = lax.dot_general(qs, kk, NT_DIMS, preferred_element_type=F32) + neg
    mx = jnp.max(s, axis=-1, keepdims=True)
    p = jnp.exp(s - mx)
    l = jnp.sum(p, axis=-1, keepdims=True)
    o_all = jnp.dot(p.astype(BF16), vv, preferred_element_type=F32)
    l_e = _expand_cols([l[h * QBLK:(h + 1) * QBLK] for h in range(A_HEADS)], masks)
    mx_e = _expand_cols([mx[h * QBLK:(h + 1) * QBLK] for h in range(A_HEADS)], masks)
    return _unstack_heads(o_all, masks, QBLK) / l_e, mx_e + jnp.log(l_e)


DIL_STEP = 2048
OG_SLABS = 2 * D_A // LANES


def _make_dilated_kernel(dil):
    nblk = DIL_STEP // (dil * QBLK)

    def body(prev_ref, cur_ref, out_ref):
        first_step = pl.program_id(1) == 0
        masks = _head_masks(D_A, A_HEAD_DIM, A_HEADS)
        qi = lax.broadcasted_iota(jnp.int32, (A_HEADS * QBLK, 2 * QBLK), 0) % QBLK
        kj = lax.broadcasted_iota(jnp.int32, (A_HEADS * QBLK, 2 * QBLK), 1)
        band = (kj >= qi) & (kj <= qi + QBLK)
        neg_band = jnp.where(band, 0.0, -jnp.inf)
        neg_first = jnp.where(first_step, jnp.where(band & (kj >= QBLK), 0.0, -jnp.inf), neg_band)

        for r in range(dil):
            for jb in range(nblk):
                rows = slice(jb * QBLK, (jb + 1) * QBLK)
                q = cur_ref[0, r, rows, 0:D_A]
                if jb == 0:
                    kv = [jnp.concatenate([prev_ref[0, r, :, c * D_A:(c + 1) * D_A],
                                           cur_ref[0, r, rows, c * D_A:(c + 1) * D_A]], axis=0) for c in (1, 2)]
                else:
                    both = slice((jb - 1) * QBLK, (jb + 1) * QBLK)
                    kv = [cur_ref[0, r, both, c * D_A:(c + 1) * D_A] for c in (1, 2)]
                o, lse = _dilated_block(q, kv[0], kv[1], neg_first if jb == 0 else neg_band, masks)
                res = (o, lse)
                for k in range(OG_SLABS):
                    src = res[k // 2][:, (k % 2) * LANES:(k % 2 + 1) * LANES]
                    if dil == 1:
                        out_ref[0, k, rows, :] = src
                    else:
                        out_ref[0, k, pl.ds(jb * QBLK * dil + r, QBLK, stride=dil), :] = src
    return body


def _dilated_call(qkv, dil, g):
    b, _, n, _ = qkv.shape
    rows = DIL_STEP // dil
    nb = n // rows
    return pl.pallas_call(
        _make_dilated_kernel(dil),
        out_shape=jax.ShapeDtypeStruct((b, OG_SLABS, n * dil, LANES), F32),
        grid=(b, nb),
        in_specs=[pl.BlockSpec((1, dil, QBLK, D_QKV),
                               lambda i, u: (i, 0, jnp.maximum(u * (rows // QBLK) - 1, 0), 0)),
                  pl.BlockSpec((1, dil, rows, D_QKV), lambda i, u: (i, 0, u, 0))],
        out_specs=pl.BlockSpec((1, OG_SLABS, DIL_STEP, LANES), lambda i, u: (i, 0, u, 0)),
        compiler_params=pltpu.CompilerParams(
            dimension_semantics=("arbitrary", "arbitrary"), vmem_limit_bytes=VMEM_LIMIT),
        name=f"dilated_attn_g{g}",
    )(qkv, qkv)


def _lower_bound(l0, l1):
    m = jnp.maximum(l0, l1)
    e0, e1 = jnp.exp(l0 - m), jnp.exp(l1 - m)
    return e0 / (e0 + e1)


def _head_rms(ob, gain):
    parts = []
    for h in range(HG_HEADS):
        sl = slice(h * HG_DIM, (h + 1) * HG_DIM)
        parts.append(_rms(ob[:, sl], gain[:, sl]))
    return jnp.concatenate(parts, axis=1)


def _merge_out(x, ua, ub, um, zg, wa_ref, wb_ref, wm_ref, wo_ref, nf):
    gates = _sigmoid(zg.astype(BF16))
    merged = None
    for k, (u, w_ref) in enumerate(((ua, wa_ref), (ub, wb_ref), (um, wm_ref))):
        proj = jnp.dot(u.astype(BF16), w_ref[...], preferred_element_type=F32).astype(BF16)
        term = gates[:, k * D_MODEL:(k + 1) * D_MODEL] * proj
        merged = term if merged is None else merged + term
    y = x + jnp.dot(merged, wo_ref[...], preferred_element_type=F32)
    return _rms(y, nf)


TAIL_TILE = 256


def _cumsum_rows(g, tri):
    g1 = g.astype(BF16)
    g2 = (g - g1.astype(F32)).astype(BF16)
    return jnp.dot(tri, g1, preferred_element_type=F32) + jnp.dot(tri, g2, preferred_element_type=F32)


def _hgrn_tile(fb, qb, v, lb, st_ref):
    t = fb.shape[0]
    c, sub = HG_CHUNK, HG_SUB
    n_sub = c // sub
    f = lb + (1.0 - lb) * _sigmoid(fb)
    kk = 1.0 - f
    row = lax.broadcasted_iota(jnp.int32, (t, t), 0)
    col = lax.broadcasted_iota(jnp.int32, (t, t), 1)
    same_chunk = (row // c) == (col // c)
    gcs = _cumsum_rows(jnp.log(f), (same_chunk & (col <= row)).astype(BF16))
    q = _silu(qb.astype(F32))

    loc = lax.broadcasted_iota(jnp.int32, (c, 1), 0)
    qt_c, kt_ci, qe_c, kd_c, dec_c = [], [[] for _ in range(n_sub)], [], [], []
    for ci in range(t // c):
        rows = slice(ci * c, (ci + 1) * c)
        g_c, q_c, k_c = gcs[rows], q[rows], kk[rows]
        refs = [jnp.zeros((1, D_B), F32)] + [g_c[i * sub - 1:i * sub] for i in range(1, n_sub)]
        own = refs[n_sub - 1]
        for i in range(n_sub - 2, -1, -1):
            own = jnp.where(loc < (i + 1) * sub, refs[i], own)
        qt_c.append((q_c * jnp.exp(g_c - own)).astype(BF16))
        k_own = k_c * jnp.exp(own - g_c)
        for i in range(n_sub):
            pieces = []
            for jb in range(n_sub):
                piece = k_own[jb * sub:(jb + 1) * sub]
                if jb < i:
                    piece = piece * jnp.exp(refs[i] - refs[jb])
                elif jb > i:
                    piece = jnp.zeros_like(piece)
                pieces.append(piece)
            kt_ci[i].append(jnp.concatenate(pieces, axis=0).astype(BF16))
        g_last = g_c[c - 1:c]
        qe_c.append((q_c * jnp.exp(g_c)).astype(BF16))
        kd_c.append((k_c * jnp.exp(g_last - g_c)).astype(BF16))
        dec_c.append(jnp.exp(g_last))
    qt = jnp.concatenate(qt_c, axis=0)
    kts = [jnp.concatenate(parts, axis=0) for parts in kt_ci]
    sub_of_row = (lax.broadcasted_iota(jnp.int32, (t, 1), 0) % c) // sub
    causal = same_chunk & (col <= row)

    outs = []
    for h in range(HG_HEADS):
        sl = slice(h * HG_DIM, (h + 1) * HG_DIM)
        zero = jnp.zeros((t, HG_DIM), BF16)
        q_big = jnp.concatenate([jnp.where(sub_of_row == i, qt[:, sl], zero) for i in range(n_sub)], axis=1)
        k_big = jnp.concatenate([kts[i][:, sl] for i in range(n_sub)], axis=1)
        att = lax.dot_general(q_big, k_big, NT_DIMS, preferred_element_type=F32)
        att = jnp.where(causal, att, 0.0).astype(BF16)
        o_intra = jnp.dot(att, v[:, sl], preferred_element_type=F32)
        st = st_ref[h]
        inter = []
        for ci in range(t // c):
            rows = slice(ci * c, (ci + 1) * c)
            inter.append(lax.dot_general(qe_c[ci][:, sl], st.astype(BF16), NT_DIMS, preferred_element_type=F32))
            st = st * dec_c[ci][:, sl] + lax.dot_general(
                v[rows, sl], kd_c[ci][:, sl], TN_DIMS, preferred_element_type=F32)
        st_ref[h] = st
        outs.append(o_intra + jnp.concatenate(inter, axis=0))
    return jnp.concatenate(outs, axis=1)


def _prompt_tail_kernel(x_ref, gq_ref, hb_ref, zg_ref, fb_ref, o1_ref, o2_ref, o3_ref, mkv_ref, lbl_ref,
                        nh_ref, nf_ref, wa_ref, wb_ref, wm_ref, wo_ref,
                        zr_ref, c1_ref, c2_ref, c3_ref, cm_ref, sst_ref, lblt_ref, nh4_ref, ua_ref, nsta_ref,
                        y_ref, hst_ref, u_ref, nst_ref, st_ref):
    j = pl.program_id(1)
    t = TAIL_TILE

    @pl.when(j == 0)
    def _():
        st_ref[...] = jnp.zeros_like(st_ref)

    u_ref[0] = ua_ref[...]
    nst_ref[0] = nsta_ref[...]
    _sample_rows(zr_ref, c1_ref, c2_ref, c3_ref, cm_ref, sst_ref, lblt_ref, nh4_ref, u_ref.at[1], nst_ref.at[1])

    os_ = [jnp.concatenate([r[0, 0], r[0, 1]], axis=1) for r in (o1_ref, o2_ref, o3_ref)]
    ls_ = [jnp.concatenate([r[0, 2], r[0, 3]], axis=1) for r in (o1_ref, o2_ref, o3_ref)]
    mx = jnp.maximum(jnp.maximum(ls_[0], ls_[1]), ls_[2])
    es_ = [jnp.exp(l - mx) for l in ls_]
    oa = (es_[0] * os_[0] + es_[1] * os_[1] + es_[2] * os_[2]) / (es_[0] + es_[1] + es_[2])
    ua = oa.astype(BF16) * _silu(gq_ref[0, :, 0:D_A])

    masks = _head_masks(D_M, A_HEAD_DIM, A_HEADS)
    qm = gq_ref[0, :, D_A:2 * D_A] * jnp.asarray(ATT_SCALE, BF16)
    qs = _stack_heads(qm, masks)
    mk = mkv_ref[0, :, 0:D_M].astype(BF16)
    mv = mkv_ref[0, :, D_M:2 * D_M].astype(BF16)
    s = lax.dot_general(qs, mk, NT_DIMS, preferred_element_type=F32)
    p = jnp.exp(s - jnp.max(s, axis=-1, keepdims=True))
    l = jnp.sum(p, axis=-1, keepdims=True)
    om = _unstack_heads(jnp.dot(p.astype(BF16), mv, preferred_element_type=F32) / l, masks, t)
    um = om.astype(BF16) * _silu(gq_ref[0, :, 2 * D_A:3 * D_A])

    lb = _lower_bound(lbl_ref[0:1], lbl_ref[1:2])
    ob = _hgrn_tile(fb_ref[0], hb_ref[0, :, 0:D_B], hb_ref[0, :, D_B:2 * D_B], lb, st_ref)
    ub = _head_rms(ob, nh_ref[...]).astype(BF16) * _silu(hb_ref[0, :, 2 * D_B:3 * D_B])

    y_ref[0] = _merge_out(x_ref[0], ua, ub, um, zg_ref[0], wa_ref, wb_ref, wm_ref, wo_ref, nf_ref[...])

    @pl.when(j == pl.num_programs(1) - 1)
    def _():
        for h in range(HG_HEADS):
            hst_ref[0, h] = st_ref[h].T


def _prompt_tail_call(x, gq, hb, zg, fb, o1, o2, o3, mkv, lbl, nh, nf, wa, wb, wm, wo,
                      sample_ops, u_first, nst_first):
    b, s, _ = x.shape
    t = TAIL_TILE
    nt = s // t
    db = sample_ops[0].shape[0]
    half = u_first.shape[0]
    assert half + b * nt * S_TILE == db and 2 * half == db, "sample rows must split evenly over both grids"

    def tok(width):
        return pl.BlockSpec((1, t, width), lambda i, j: (i, j, 0))

    def const(shape):
        return pl.BlockSpec(shape, lambda i, j: (0,) * len(shape), pipeline_mode=pl.Buffered(1))

    def first(a):
        return pl.BlockSpec((S_TILE,) + a.shape[1:], lambda i, j: (i * nt + j, 0, 0))

    def halves(shape):
        return pl.BlockSpec((2, S_TILE) + shape, lambda i, j: (0, i * nt + j, 0, 0))

    y, hst, u, nst = pl.pallas_call(
        _prompt_tail_kernel,
        out_shape=(jax.ShapeDtypeStruct((b, s, D_MODEL), F32),
                   jax.ShapeDtypeStruct((b, HG_HEADS, HG_DIM, HG_DIM), F32),
                   jax.ShapeDtypeStruct((2, half, U_ROWS, LANES), F32),
                   jax.ShapeDtypeStruct((2, half, HG_HEADS * HG_DIM, HG_DIM), F32)),
        grid=(b, nt),
        in_specs=[tok(D_MODEL), tok(D_GQ), tok(D_HB), tok(D_ZG), tok(D_B),
                  *[pl.BlockSpec((1, OG_SLABS, t, LANES), lambda i, j: (i, 0, j, 0))] * N_GROUPS,
                  pl.BlockSpec((1, MEM_LEN, 2 * D_M), lambda i, j: (i, 0, 0)),
                  const((2, D_B)), const((1, D_B)), const((1, D_MODEL)),
                  const((D_A, D_MODEL)), const((D_B, D_MODEL)), const((D_M, D_MODEL)),
                  const((D_MODEL, D_MODEL)),
                  *_sample_specs(sample_ops, nt, half), first(u_first), first(nst_first)],
        out_specs=(tok(D_MODEL),
                   pl.BlockSpec((1, HG_HEADS, HG_DIM, HG_DIM), lambda i, j: (i, 0, 0, 0)),
                   halves((U_ROWS, LANES)), halves((HG_HEADS * HG_DIM, HG_DIM))),
        scratch_shapes=[pltpu.VMEM((HG_HEADS, HG_DIM, HG_DIM), F32)],
        compiler_params=pltpu.CompilerParams(
            dimension_semantics=("arbitrary", "arbitrary"), vmem_limit_bytes=VMEM_LIMIT),
        name="prompt_tail",
    )(x, gq, hb, zg, fb, o1, o2, o3, mkv, lbl, nh, nf, wa, wb, wm, wo, *sample_ops, u_first, nst_first)
    return y, hst, u.reshape(db, U_ROWS * LANES), nst.reshape(db, HG_HEADS * HG_DIM, HG_DIM)


S_COLS = 1024


def _sample_inproj_kernel(x_ref, gain_ref, w_ref, zr_ref, zg_ref, kvt_ref, wbf_ref):
    c = pl.program_id(0)
    h = _rms(x_ref[...], gain_ref[...]).astype(BF16)
    w = w_ref[...].astype(BF16)
    wbf_ref[...] = w
    z = jnp.dot(h, w, preferred_element_type=F32)
    for k in range(S_COLS // LANES):
        zr_ref[:, k, :] = z[:, k * LANES:(k + 1) * LANES]
    zg_ref[...] = z
    for step in range(OFF_GA // S_COLS + 1):
        pieces = [(g, kv, (OFF_KA + (kv * N_GROUPS + g) * D_A) % S_COLS)
                  for g in range(N_GROUPS) for kv in range(2)
                  if (OFF_KA + (kv * N_GROUPS + g) * D_A) // S_COLS == step]

        @pl.when(c == step)
        def _(pieces=pieces):
            zt = z.T
            for g, kv, row0 in pieces:
                kvt_ref[g, kv * D_A:(kv + 1) * D_A, :] = zt[row0:row0 + D_A]


def _sample_inproj_call(x, gain, w):
    db = x.shape[0]
    first_zg = OFF_ZG // S_COLS
    return pl.pallas_call(
        _sample_inproj_kernel,
        out_shape=(jax.ShapeDtypeStruct((db, Z_ROWS, LANES), F32),
                   jax.ShapeDtypeStruct((db, D_ZG), F32),
                   jax.ShapeDtypeStruct((N_GROUPS, 2 * D_A, db), F32),
                   jax.ShapeDtypeStruct((D_MODEL, D_IN), BF16)),
        grid=(D_IN // S_COLS,),
        in_specs=[pl.BlockSpec((db, D_MODEL), lambda c: (0, 0)),
                  pl.BlockSpec((1, D_MODEL), lambda c: (0, 0)),
                  pl.BlockSpec((D_MODEL, S_COLS), lambda c: (0, c))],
        out_specs=(pl.BlockSpec((db, S_COLS // LANES, LANES), lambda c: (0, c, 0)),
                   pl.BlockSpec((db, S_COLS), lambda c: (0, jnp.maximum(c - first_zg, 0))),
                   pl.BlockSpec((N_GROUPS, 2 * D_A, db), lambda c: (0, 0, 0)),
                   pl.BlockSpec((D_MODEL, S_COLS), lambda c: (0, c))),
        compiler_params=pltpu.CompilerParams(dimension_semantics=("arbitrary",)),
        name="sample_inproj",
    )(x, gain, w)


S_TILE = 1
Z_ROWS = D_IN // LANES
U_ROWS = 8


def _sample_rows(zr_ref, c1_ref, c2_ref, c3_ref, cm_ref, st_ref, lblt_ref, nh_ref, u_ref, nst_ref):
    bt = S_TILE
    pad = [jnp.zeros((LANES - bt * Z_ROWS, LANES), F32)] if bt * Z_ROWS < LANES else []
    zt = jnp.concatenate([zr_ref[b] for b in range(bt)] + pad, axis=0).T

    def col(b, off, n=HG_DIM):
        j, l = divmod(off, LANES)
        return zt[l:l + n, Z_ROWS * b + j:Z_ROWS * b + j + 1]

    lbt = lblt_ref[...]
    lb_col = _lower_bound(lbt[:, 0:1], lbt[:, 1:2])
    nh = nh_ref[...]

    def col2(b, off):
        return jnp.concatenate([col(b, off), col(b, off + LANES)], axis=0)

    def per_head(x):
        return jnp.concatenate([jnp.sum(x[h * A_HEAD_DIM:(h + 1) * A_HEAD_DIM], axis=0, keepdims=True)
                                for h in range(A_HEADS)], axis=0)

    def spread(x):
        return jnp.concatenate([jnp.broadcast_to(x[h:h + 1], (A_HEAD_DIM, 1)) for h in range(A_HEADS)], axis=0)

    def attend(b, srcs):
        parts, new_scores, qs = [], [], []
        for c_ref, dil, q_off, k_off, _ in srcs:
            q = col2(b, q_off) * ATT_SCALE
            qs.append(q)
            s = per_head(c_ref[b, 0:D_A, :] * q)
            if dil > 1:
                lane = lax.broadcasted_iota(jnp.int32, s.shape, 1)
                s = jnp.where(lane % dil == 0, s, -jnp.inf)
            parts.append(s)
            if k_off is not None:
                new_scores.append(per_head(col2(b, k_off) * q))
        if new_scores:
            lane = lax.broadcasted_iota(jnp.int32, (A_HEADS, LANES), 1)
            slab = jnp.full((A_HEADS, LANES), -jnp.inf, F32)
            for k, sn in enumerate(new_scores):
                slab = jnp.where(lane == k, sn, slab)
            parts.append(slab)
        s_all = jnp.concatenate(parts, axis=1)
        p_all = jnp.exp(s_all - jnp.max(s_all, axis=1, keepdims=True))
        den = jnp.sum(p_all, axis=1, keepdims=True)
        acc = [jnp.zeros((A_HEAD_DIM, LANES), F32) for _ in range(A_HEADS)]
        lo = 0
        for c_ref, _, _, _, _ in srcs:
            length = c_ref.shape[2]
            for k in range(length // LANES):
                ls = slice(k * LANES, (k + 1) * LANES)
                for h in range(A_HEADS):
                    vt = c_ref[b, D_A + h * A_HEAD_DIM:D_A + (h + 1) * A_HEAD_DIM, ls]
                    acc[h] = acc[h] + vt * p_all[h:h + 1, lo + k * LANES:lo + (k + 1) * LANES]
            lo += length
        num = jnp.sum(jnp.concatenate(acc, axis=0), axis=1, keepdims=True)
        k = 0
        for _, _, _, k_off, v_off in srcs:
            if k_off is not None:
                num = num + spread(p_all[:, lo + k:lo + k + 1]) * col2(b, v_off)
                k += 1
        return num / spread(den)

    window = [(c_ref, dil, OFF_QA + g * D_A, OFF_KA + g * D_A, OFF_VA + g * D_A)
              for g, (c_ref, (_, dil)) in enumerate(zip((c1_ref, c2_ref, c3_ref), WIN_GROUPS))]
    memory = [(cm_ref, 1, OFF_QM, None, None)]

    for b in range(bt):
        ua = attend(b, window) * _silu(col2(b, OFF_GA))
        um = attend(b, memory) * _silu(col2(b, OFF_GM))
        branch_cols = [ua[0:LANES], ua[LANES:2 * LANES], um[0:LANES], um[LANES:2 * LANES]]

        lane = lax.broadcasted_iota(jnp.int32, (LANES, LANES), 1)
        ut = jnp.zeros((LANES, LANES), F32)
        for k, cvec in enumerate(branch_cols):
            ut = ut + jnp.where(lane == k, cvec, 0.0)
        u_ref[b, 0:4, :] = ut.T[0:4, :]

        orows = []
        for h in range(HG_HEADS):
            sl = slice(h * HG_DIM, (h + 1) * HG_DIM)
            fcol = lb_col[sl] + (1.0 - lb_col[sl]) * _sigmoid(col(b, OFF_FB + h * HG_DIM))
            qcol = _silu(col(b, OFF_QB + h * HG_DIM))
            vrow = zr_ref[b, OFF_IB // LANES + h:OFF_IB // LANES + h + 1, :]
            sn = fcol * st_ref[b, sl, :] + (1.0 - fcol) * vrow
            nst_ref[b, sl, :] = sn
            orows.append(jnp.sum(sn * qcol, axis=0, keepdims=True))
        gate = zr_ref[b, OFF_GB // LANES:OFF_GB // LANES + HG_HEADS, :]
        u_ref[b, 4:4 + HG_HEADS, :] = _rms(jnp.concatenate(orows, axis=0), nh) * _silu(gate)


def _stored_view(cache):
    db, length = cache.shape[0], cache.shape[1]
    return jnp.transpose(cache, (0, 2, 3, 4, 1)).reshape(db, 2 * D_A, length)


def _sample_out_kernel(x_ref, u_ref, zg_ref, nf_ref, wa_ref, wb_ref, wm_ref, wo_ref, y_ref):
    u = u_ref[...]
    y_ref[...] = _merge_out(x_ref[...], u[:, 0:D_A], u[:, D_A + D_M:], u[:, D_A:D_A + D_M], zg_ref[...],
                            wa_ref, wb_ref, wm_ref, wo_ref, nf_ref[...])


def _sample_out_call(x, u, zg, nf, wa, wb, wm, wo):
    db = x.shape[0]

    def full(a):
        return pl.BlockSpec(a.shape, lambda i: (0,) * a.ndim)

    args = (x, u, zg, nf, wa, wb, wm, wo)
    return pl.pallas_call(
        _sample_out_kernel,
        out_shape=jax.ShapeDtypeStruct((db, D_MODEL), F32),
        grid=(1,),
        in_specs=[full(a) for a in args],
        out_specs=pl.BlockSpec((db, D_MODEL), lambda i: (0, 0)),
        compiler_params=pltpu.CompilerParams(vmem_limit_bytes=VMEM_LIMIT),
        name="sample_out",
    )(*args)


def _cache_from_stored(kvt):
    b, _, length = kvt.shape
    return jnp.transpose(kvt.reshape(b, 2, A_HEADS, A_HEAD_DIM, length), (0, 4, 1, 2, 3))[None]


def kernel(x_prompt, x_sample, mem_prompt, cache_w1_kv, cache_w2_kv, cache_w3_kv, cache_mem_kv, state_hgrn,
           norm_in, w_in, lb_logits, norm_hgrn, norm_mem, w_mem_kv, w_branch_a, w_branch_b, w_branch_m,
           w_out, norm_final):
    b, s, _ = x_prompt.shape
    db = x_sample.shape[0]
    wa, wb, wm, wo = (w[0].astype(BF16) for w in (w_branch_a, w_branch_b, w_branch_m, w_out))
    gain_in = norm_in[0][None]
    nh = norm_hgrn[0][None]
    nf = norm_final[None]
    lbl = lb_logits.astype(F32)

    xs = x_sample.reshape(db, D_MODEL)
    zr, zg_s, kvt_s, w_in_bf = _sample_inproj_call(xs, gain_in, w_in[0])
    mkv, mkv_t = _mem_kv_call(mem_prompt, norm_mem[0][None], w_mem_kv[0].astype(BF16))
    sample_ops = (zr,
                  *[_stored_view(c[0]) for c in (cache_w1_kv, cache_w2_kv, cache_w3_kv, cache_mem_kv)],
                  state_hgrn[0].reshape(db, HG_HEADS * HG_DIM, HG_DIM), lbl.T, nh.reshape(HG_HEADS, HG_DIM))
    qkv0, qkv1, qkv2, gq, hb, zg, fb, kvt0, kvt1, kvt2, u_first, nst_first = _inproj_call(
        x_prompt, gain_in, w_in_bf, sample_ops)
    og = [_dilated_call(qkv, dil, g) for g, (qkv, (_, dil)) in enumerate(zip((qkv0, qkv1, qkv2), WIN_GROUPS))]
    y_prompt, hg_prompt, u, nst = _prompt_tail_call(
        x_prompt, gq, hb, zg, fb, og[0], og[1], og[2], mkv, lbl, nh, nf, wa, wb, wm, wo,
        sample_ops, u_first, nst_first)
    pw = [_cache_from_stored(kvt) for kvt in (kvt0, kvt1, kvt2)]
    new_mem = _cache_from_stored(mkv_t)

    y_sample = _sample_out_call(xs, u, zg_s, nf, wa, wb, wm, wo).reshape(db, 1, D_MODEL)
    sw = [jnp.transpose(kvt_s[g].reshape(2, A_HEADS, A_HEAD_DIM, db), (3, 0, 1, 2))[None, :, None]
          for g in range(N_GROUPS)]
    new_hg_sample = nst.reshape(1, db, HG_HEADS, HG_DIM, HG_DIM)

    return (y_prompt, y_sample, pw[0], pw[1], pw[2], new_mem, hg_prompt[None],
            sw[0], sw[1], sw[2], new_hg_sample)
```

```python
import jax
import jax.numpy as jnp
from jax import lax
from jax.experimental import pallas as pl
from jax.experimental.pallas import tpu as pltpu

F32 = jnp.float32
BF16 = jnp.bfloat16

D_MODEL = 1024
WIN_GROUPS = ((128, 1), (512, 4), (2048, 16))
N_GROUPS = 3
A_HEADS = 4
A_HEAD_DIM = 64
D_A = 256
QBLK = 128
HG_HEADS = 4
HG_DIM = 128
D_B = 512
HG_CHUNK = 64
HG_SUB = 16
MEM_LEN = 256
D_M = 256
EPS = 1e-6
D_IN = 8192
LANES = 128
OFF_QA, OFF_KA, OFF_VA, OFF_GA = 0, 768, 1536, 2304
OFF_QB, OFF_FB, OFF_IB, OFF_GB = 2560, 3072, 3584, 4096
OFF_QM, OFF_GM, OFF_ZG = 4608, 4864, 5120
ATT_SCALE = A_HEAD_DIM ** -0.5

VMEM_LIMIT = 56 * 1024 * 1024

NT_DIMS = (((1,), (1,)), ((), ()))
TN_DIMS = (((0,), (0,)), ((), ()))


def _sigmoid(x):
    return 0.5 * jnp.tanh(0.5 * x) + 0.5


def _silu(x):
    h = 0.5 * x
    return h * jnp.tanh(h) + h


def _rms(x, gain):
    return x * lax.rsqrt(jnp.mean(x * x, axis=-1, keepdims=True) + EPS) * gain


def _head_masks(width, head_dim, n_heads):
    lane = lax.broadcasted_iota(jnp.int32, (1, width), 1)
    return [(lane >= h * head_dim) & (lane < (h + 1) * head_dim) for h in range(n_heads)]


def _stack_heads(q, masks):
    zero = jnp.zeros_like(q)
    return jnp.concatenate([jnp.where(m, q, zero) for m in masks], axis=0)


def _unstack_heads(o_all, masks, t):
    n = len(masks)
    out = o_all[(n - 1) * t:n * t]
    for h in range(n - 2, -1, -1):
        out = jnp.where(masks[h], o_all[h * t:(h + 1) * t], out)
    return out


def _expand_cols(cols, masks):
    out = jnp.broadcast_to(cols[-1], (cols[-1].shape[0], masks[0].shape[1]))
    for h in range(len(masks) - 2, -1, -1):
        out = jnp.where(masks[h], cols[h], out)
    return out


def _mem_kv_kernel(mem_ref, gain_ref, w_ref, out_ref, out_t_ref):
    hm = _rms(mem_ref[0], gain_ref[...]).astype(BF16)
    kv = jnp.dot(hm, w_ref[...], preferred_element_type=F32)
    out_ref[0] = kv
    out_t_ref[0] = kv.T


def _mem_kv_call(mem, gain, w_bf):
    b = mem.shape[0]
    return pl.pallas_call(
        _mem_kv_kernel,
        out_shape=(jax.ShapeDtypeStruct((b, MEM_LEN, 2 * D_M), F32),
                   jax.ShapeDtypeStruct((b, 2 * D_M, MEM_LEN), F32)),
        grid=(b,),
        in_specs=[pl.BlockSpec((1, MEM_LEN, D_MODEL), lambda i: (i, 0, 0)),
                  pl.BlockSpec((1, D_MODEL), lambda i: (0, 0)),
                  pl.BlockSpec((D_MODEL, 2 * D_M), lambda i: (0, 0))],
        out_specs=(pl.BlockSpec((1, MEM_LEN, 2 * D_M), lambda i: (i, 0, 0)),
                   pl.BlockSpec((1, 2 * D_M, MEM_LEN), lambda i: (i, 0, 0))),
        name="mem_kv",
    )(mem, gain, w_bf)


IN_TILE = 256
IN_CHUNK = 256
D_QKV = 3 * D_A
D_GQ = 3 * D_A
D_HB = 3 * D_B
D_ZG = 3 * D_MODEL


def _inproj_kernel(x_ref, gain_ref, w_ref, zr_ref, c1_ref, c2_ref, c3_ref, cm_ref, sst_ref, lblt_ref, nh4_ref,
                   qkv0_ref, qkv1_ref, qkv2_ref, gq_ref, hb_ref, zg_ref, fb_ref, kvt0_ref, kvt1_ref, kvt2_ref,
                   u_ref, nst_ref, scr_ref):
    qkv_refs = (qkv0_ref, qkv1_ref, qkv2_ref)
    kvt_refs = (kvt0_ref, kvt1_ref, kvt2_ref)
    h = _rms(x_ref[0], gain_ref[...]).astype(BF16)
    t = IN_TILE
    for c in range(D_IN // IN_CHUNK):
        lo = c * IN_CHUNK
        zc = jnp.dot(h, w_ref[:, lo:lo + IN_CHUNK], preferred_element_type=F32)
        if lo < OFF_GA:
            sec, g = divmod(c, N_GROUPS)
            dil = WIN_GROUPS[g][1]
            dst = slice(sec * D_A, (sec + 1) * D_A)
            if dil == 1:
                qkv_refs[g][0, 0, :, dst] = zc.astype(BF16)
            else:
                for k in range(IN_CHUNK // LANES):
                    scr_ref[k] = zc[:, k * LANES:(k + 1) * LANES]
                for r in range(dil):
                    qkv_refs[g][0, r, :, dst] = jnp.concatenate(
                        [scr_ref[k, pl.ds(r, t // dil, stride=dil), :] for k in range(IN_CHUNK // LANES)],
                        axis=1).astype(BF16)
            if sec > 0:
                width = kvt_refs[g].shape[2]
                kvt_refs[g][0, (sec - 1) * D_A:sec * D_A, :] = zc.T[:, t - width:]
        elif lo < OFF_QB:
            gq_ref[0, :, 0:D_A] = zc.astype(BF16)
        elif lo < OFF_FB:
            hb_ref[0, :, lo - OFF_QB:lo - OFF_QB + IN_CHUNK] = zc.astype(BF16)
        elif lo < OFF_IB:
            fb_ref[0, :, lo - OFF_FB:lo - OFF_FB + IN_CHUNK] = zc
        elif lo < OFF_QM:
            hb_ref[0, :, lo - OFF_IB + D_B:lo - OFF_IB + D_B + IN_CHUNK] = zc.astype(BF16)
        elif lo < OFF_ZG:
            gq_ref[0, :, lo - OFF_QM + D_A:lo - OFF_QM + D_A + IN_CHUNK] = zc.astype(BF16)
        else:
            zg_ref[0, :, lo - OFF_ZG:lo - OFF_ZG + IN_CHUNK] = zc.astype(BF16)
    _sample_rows(zr_ref, c1_ref, c2_ref, c3_ref, cm_ref, sst_ref, lblt_ref, nh4_ref, u_ref, nst_ref)


def _sample_specs(ops, nt, first_row):
    first_blk = first_row // S_TILE

    def rows(a):
        return pl.BlockSpec((S_TILE,) + a.shape[1:], lambda i, j: (first_blk + i * nt + j, 0, 0))

    def const(a):
        return pl.BlockSpec(a.shape, lambda i, j: (0, 0), pipeline_mode=pl.Buffered(1))

    return [rows(a) for a in ops[:6]] + [const(a) for a in ops[6:]]


def _inproj_call(x, gain, w_bf, sample_ops):
    b, s, _ = x.shape
    t = IN_TILE
    nt = s // t
    n_rows = b * nt * S_TILE

    def tok(width):
        return pl.BlockSpec((1, t, width), lambda i, j: (i, j, 0))

    def srow(shape):
        return pl.BlockSpec((S_TILE,) + shape, lambda i, j: (i * nt + j, 0, 0))

    qkv_shapes, qkv_specs = [], []
    for _, dil in WIN_GROUPS:
        qkv_shapes.append(jax.ShapeDtypeStruct((b, dil, s // dil, D_QKV), BF16))
        qkv_specs.append(pl.BlockSpec((1, dil, t // dil, D_QKV), lambda i, j: (i, 0, j, 0)))
    kvt_shapes, kvt_specs = [], []
    for window, _ in WIN_GROUPS:
        length = min(window, s)
        width = min(t, length)
        first = nt - length // width
        kvt_shapes.append(jax.ShapeDtypeStruct((b, 2 * D_A, length), F32))
        kvt_specs.append(pl.BlockSpec((1, 2 * D_A, width),
                                      lambda i, j, first=first: (i, 0, jnp.maximum(j - first, 0))))
    return pl.pallas_call(
        _inproj_kernel,
        out_shape=(*qkv_shapes,
                   jax.ShapeDtypeStruct((b, s, D_GQ), BF16),
                   jax.ShapeDtypeStruct((b, s, D_HB), BF16),
                   jax.ShapeDtypeStruct((b, s, D_ZG), BF16),
                   jax.ShapeDtypeStruct((b, s, D_B), F32),
                   *kvt_shapes,
                   jax.ShapeDtypeStruct((n_rows, U_ROWS, LANES), F32),
                   jax.ShapeDtypeStruct((n_rows, HG_HEADS * HG_DIM, HG_DIM), F32)),
        grid=(b, nt),
        in_specs=[tok(D_MODEL),
                  pl.BlockSpec((1, D_MODEL), lambda i, j: (0, 0)),
                  pl.BlockSpec((D_MODEL, D_IN), lambda i, j: (0, 0), pipeline_mode=pl.Buffered(1)),
                  *_sample_specs(sample_ops, nt, 0)],
        out_specs=(*qkv_specs, tok(D_GQ), tok(D_HB), tok(D_ZG), tok(D_B), *kvt_specs,
                   srow((U_ROWS, LANES)), srow((HG_HEADS * HG_DIM, HG_DIM))),
        scratch_shapes=[pltpu.VMEM((IN_CHUNK // LANES, t, LANES), F32)],
        compiler_params=pltpu.CompilerParams(
            dimension_semantics=("arbitrary", "arbitrary"), vmem_limit_bytes=VMEM_LIMIT),
        name="inproj",
    )(x, gain, w_bf, *sample_ops)


DIL_STEP = 2048
DIL_GROUP = 2
OG_SLABS = 2 * D_A // LANES


def _make_dilated_kernel(dil):
    nblk = DIL_STEP // (dil * QBLK)

    def body(prev_ref, cur_ref, out_ref):
        first_step = pl.program_id(1) == 0
        masks = _head_masks(D_A, A_HEAD_DIM, A_HEADS)
        qi = lax.broadcasted_iota(jnp.int32, (A_HEADS * QBLK, 2 * QBLK), 0) % QBLK
        kj = lax.broadcasted_iota(jnp.int32, (A_HEADS * QBLK, 2 * QBLK), 1)
        band = (kj >= qi) & (kj <= qi + QBLK)
        neg_band = jnp.where(band, 0.0, -jnp.inf)
        neg_first = jnp.where(first_step, jnp.where(band & (kj >= QBLK), 0.0, -jnp.inf), neg_band)

        def keys_values(r, jb, c):
            rows = slice(jb * QBLK, (jb + 1) * QBLK)
            if jb == 0:
                return jnp.concatenate([prev_ref[0, r, :, c * D_A:(c + 1) * D_A],
                                        cur_ref[0, r, rows, c * D_A:(c + 1) * D_A]], axis=0)
            return cur_ref[0, r, (jb - 1) * QBLK:(jb + 1) * QBLK, c * D_A:(c + 1) * D_A]

        blocks = [(r, jb) for r in range(dil) for jb in range(nblk)]
        for g0 in range(0, len(blocks), DIL_GROUP):
            group = blocks[g0:g0 + DIL_GROUP]
            scores = []
            for r, jb in group:
                q = cur_ref[0, r, jb * QBLK:(jb + 1) * QBLK, 0:D_A]
                qs = _stack_heads(q * jnp.asarray(ATT_SCALE, BF16), masks)
                s = lax.dot_general(qs, keys_values(r, jb, 1), NT_DIMS, preferred_element_type=F32)
                scores.append(s + (neg_first if jb == 0 else neg_band))
            probs = []
            for s in scores:
                mx = jnp.max(s, axis=-1, keepdims=True)
                p = jnp.exp(s - mx)
                probs.append((p.astype(BF16), mx, jnp.sum(p, axis=-1, keepdims=True)))
            outs = [jnp.dot(p, keys_values(r, jb, 2), preferred_element_type=F32)
                    for (p, _, _), (r, jb) in zip(probs, group)]
            for o_all, (_, mx, l), (r, jb) in zip(outs, probs, group):
                l_e = _expand_cols([l[h * QBLK:(h + 1) * QBLK] for h in range(A_HEADS)], masks)
                mx_e = _expand_cols([mx[h * QBLK:(h + 1) * QBLK] for h in range(A_HEADS)], masks)
                res = (_unstack_heads(o_all, masks, QBLK) / l_e, mx_e + jnp.log(l_e))
                for k in range(OG_SLABS):
                    src = res[k // 2][:, (k % 2) * LANES:(k % 2 + 1) * LANES]
                    if dil == 1:
                        out_ref[0, k, jb * QBLK:(jb + 1) * QBLK, :] = src
                    else:
                        out_ref[0, k, pl.ds(jb * QBLK * dil + r, QBLK, stride=dil), :] = src
    return body


def _dilated_call(qkv, dil, g):
    b, _, n, _ = qkv.shape
    rows = DIL_STEP // dil
    nb = n // rows
    return pl.pallas_call(
        _make_dilated_kernel(dil),
        out_shape=jax.ShapeDtypeStruct((b, OG_SLABS, n * dil, LANES), F32),
        grid=(b, nb),
        in_specs=[pl.BlockSpec((1, dil, QBLK, D_QKV),
                               lambda i, u: (i, 0, jnp.maximum(u * (rows // QBLK) - 1, 0), 0)),
                  pl.BlockSpec((1, dil, rows, D_QKV), lambda i, u: (i, 0, u, 0))],
        out_specs=pl.BlockSpec((1, OG_SLABS, DIL_STEP, LANES), lambda i, u: (i, 0, u, 0)),
        compiler_params=pltpu.CompilerParams(
            dimension_semantics=("arbitrary", "arbitrary"), vmem_limit_bytes=VMEM_LIMIT),
        name=f"dilated_attn_g{g}",
    )(qkv, qkv)


def _lower_bound(l0, l1):
    m = jnp.maximum(l0, l1)
    e0, e1 = jnp.exp(l0 - m), jnp.exp(l1 - m)
    return e0 / (e0 + e1)


def _head_rms(ob, gain):
    parts = []
    for h in range(HG_HEADS):
        sl = slice(h * HG_DIM, (h + 1) * HG_DIM)
        parts.append(_rms(ob[:, sl], gain[:, sl]))
    return jnp.concatenate(parts, axis=1)


def _merge_gates(zg):
    return _sigmoid(zg.astype(BF16))


def _gated_proj(u, w_ref, gates, k):
    proj = jnp.dot(u.astype(BF16), w_ref[...], preferred_element_type=F32).astype(BF16)
    return gates[:, k * D_MODEL:(k + 1) * D_MODEL] * proj


def _residual_out(x, merged, wo_ref, nf):
    y = x + jnp.dot(merged, wo_ref[...], preferred_element_type=F32)
    return _rms(y, nf)


def _merge_out(x, ua, ub, um, zg, wa_ref, wb_ref, wm_ref, wo_ref, nf):
    gates = _merge_gates(zg)
    merged = (_gated_proj(ua, wa_ref, gates, 0) + _gated_proj(ub, wb_ref, gates, 1)
              + _gated_proj(um, wm_ref, gates, 2))
    return _residual_out(x, merged, wo_ref, nf)


TAIL_TILE = 256


def _cumsum_rows(g, tri):
    g1 = g.astype(BF16)
    g2 = (g - g1.astype(F32)).astype(BF16)
    return jnp.dot(tri, g1, preferred_element_type=F32) + jnp.dot(tri, g2, preferred_element_type=F32)


def _hgrn_tile(fb, qb, v, lb, st_ref):
    t = fb.shape[0]
    c, sub = HG_CHUNK, HG_SUB
    n_sub = c // sub
    f = lb + (1.0 - lb) * _sigmoid(fb)
    kk = 1.0 - f
    row = lax.broadcasted_iota(jnp.int32, (t, t), 0)
    col = lax.broadcasted_iota(jnp.int32, (t, t), 1)
    same_chunk = (row // c) == (col // c)
    gcs = _cumsum_rows(jnp.log(f), (same_chunk & (col <= row)).astype(BF16))
    q = _silu(qb.astype(F32))

    loc = lax.broadcasted_iota(jnp.int32, (c, 1), 0)
    qt_c, kt_ci, qe_c, kd_c, dec_c = [], [[] for _ in range(n_sub)], [], [], []
    for ci in range(t // c):
        rows = slice(ci * c, (ci + 1) * c)
        g_c, q_c, k_c = gcs[rows], q[rows], kk[rows]
        refs = [jnp.zeros((1, D_B), F32)] + [g_c[i * sub - 1:i * sub] for i in range(1, n_sub)]
        own = refs[n_sub - 1]
        for i in range(n_sub - 2, -1, -1):
            own = jnp.where(loc < (i + 1) * sub, refs[i], own)
        qt_c.append((q_c * jnp.exp(g_c - own)).astype(BF16))
        k_own = k_c * jnp.exp(own - g_c)
        for i in range(n_sub):
            pieces = []
            for jb in range(n_sub):
                piece = k_own[jb * sub:(jb + 1) * sub]
                if jb < i:
                    piece = piece * jnp.exp(refs[i] - refs[jb])
                elif jb > i:
                    piece = jnp.zeros_like(piece)
                pieces.append(piece)
            kt_ci[i].append(jnp.concatenate(pieces, axis=0).astype(BF16))
        g_last = g_c[c - 1:c]
        qe_c.append((q_c * jnp.exp(g_c)).astype(BF16))
        kd_c.append((k_c * jnp.exp(g_last - g_c)).astype(BF16))
        dec_c.append(jnp.exp(g_last))
    qt = jnp.concatenate(qt_c, axis=0)
    kts = [jnp.concatenate(parts, axis=0) for parts in kt_ci]
    sub_of_row = (lax.broadcasted_iota(jnp.int32, (t, 1), 0) % c) // sub
    causal = same_chunk & (col <= row)

    heads = [slice(h * HG_DIM, (h + 1) * HG_DIM) for h in range(HG_HEADS)]
    chunks = [slice(ci * c, (ci + 1) * c) for ci in range(t // c)]
    zero = jnp.zeros((t, HG_DIM), BF16)
    atts, incs = [], []
    for sl in heads:
        q_big = jnp.concatenate([jnp.where(sub_of_row == i, qt[:, sl], zero) for i in range(n_sub)], axis=1)
        k_big = jnp.concatenate([kts[i][:, sl] for i in range(n_sub)], axis=1)
        atts.append(lax.dot_general(q_big, k_big, NT_DIMS, preferred_element_type=F32))
        incs.append([lax.dot_general(v[rows, sl], kd_c[ci][:, sl], TN_DIMS, preferred_element_type=F32)
                     for ci, rows in enumerate(chunks)])
    states = []
    for h, sl in enumerate(heads):
        atts[h] = jnp.where(causal, atts[h], 0.0).astype(BF16)
        sts = [st_ref[h]]
        for ci in range(len(chunks)):
            sts.append(sts[-1] * dec_c[ci][:, sl] + incs[h][ci])
        st_ref[h] = sts[-1]
        states.append(sts)
    outs = []
    for h, sl in enumerate(heads):
        o_intra = jnp.dot(atts[h], v[:, sl], preferred_element_type=F32)
        inter = [lax.dot_general(qe_c[ci][:, sl], states[h][ci].astype(BF16), NT_DIMS, preferred_element_type=F32)
                 for ci in range(len(chunks))]
        outs.append(o_intra + jnp.concatenate(inter, axis=0))
    return jnp.concatenate(outs, axis=1)


def _prompt_tail_kernel(x_ref, gq_ref, hb_ref, zg_ref, fb_ref, o1_ref, o2_ref, o3_ref, mkv_ref, lbl_ref,
                        nh_ref, nf_ref, wa_ref, wb_ref, wm_ref, wo_ref,
                        zr_ref, c1_ref, c2_ref, c3_ref, cm_ref, sst_ref, lblt_ref, nh4_ref, ua_ref, nsta_ref,
                        y_ref, hst_ref, u_ref, nst_ref, st_ref):
    j = pl.program_id(1)
    t = TAIL_TILE

    @pl.when(j == 0)
    def _():
        st_ref[...] = jnp.zeros_like(st_ref)

    u_ref[0] = ua_ref[...]
    nst_ref[0] = nsta_ref[...]
    _sample_rows(zr_ref, c1_ref, c2_ref, c3_ref, cm_ref, sst_ref, lblt_ref, nh4_ref, u_ref.at[1], nst_ref.at[1])

    masks = _head_masks(D_M, A_HEAD_DIM, A_HEADS)
    qm = gq_ref[0, :, D_A:2 * D_A] * jnp.asarray(ATT_SCALE, BF16)
    qs = _stack_heads(qm, masks)
    mk = mkv_ref[0, :, 0:D_M].astype(BF16)
    mv = mkv_ref[0, :, D_M:2 * D_M].astype(BF16)
    s = lax.dot_general(qs, mk, NT_DIMS, preferred_element_type=F32)

    os_ = [jnp.concatenate([r[0, 0], r[0, 1]], axis=1) for r in (o1_ref, o2_ref, o3_ref)]
    ls_ = [jnp.concatenate([r[0, 2], r[0, 3]], axis=1) for r in (o1_ref, o2_ref, o3_ref)]
    mx = jnp.maximum(jnp.maximum(ls_[0], ls_[1]), ls_[2])
    es_ = [jnp.exp(l - mx) for l in ls_]
    oa = (es_[0] * os_[0] + es_[1] * os_[1] + es_[2] * os_[2]) / (es_[0] + es_[1] + es_[2])
    ua = oa.astype(BF16) * _silu(gq_ref[0, :, 0:D_A])

    p = jnp.exp(s - jnp.max(s, axis=-1, keepdims=True))
    l = jnp.sum(p, axis=-1, keepdims=True)
    om = _unstack_heads(jnp.dot(p.astype(BF16), mv, preferred_element_type=F32) / l, masks, t)
    um = om.astype(BF16) * _silu(gq_ref[0, :, 2 * D_A:3 * D_A])

    lb = _lower_bound(lbl_ref[0:1], lbl_ref[1:2])
    ob = _hgrn_tile(fb_ref[0], hb_ref[0, :, 0:D_B], hb_ref[0, :, D_B:2 * D_B], lb, st_ref)
    ub = _head_rms(ob, nh_ref[...]).astype(BF16) * _silu(hb_ref[0, :, 2 * D_B:3 * D_B])

    y_ref[0] = _merge_out(x_ref[0], ua, ub, um, zg_ref[0], wa_ref, wb_ref, wm_ref, wo_ref, nf_ref[...])

    @pl.when(j == pl.num_programs(1) - 1)
    def _():
        for h in range(HG_HEADS):
            hst_ref[0, h] = st_ref[h].T


def _prompt_tail_call(x, gq, hb, zg, fb, o1, o2, o3, mkv, lbl, nh, nf, wa, wb, wm, wo,
                      sample_ops, u_first, nst_first):
    b, s, _ = x.shape
    t = TAIL_TILE
    nt = s // t
    db = sample_ops[0].shape[0]
    half = u_first.shape[0]
    assert half + b * nt * S_TILE == db and 2 * half == db, "sample rows must split evenly over both grids"

    def tok(width):
        return pl.BlockSpec((1, t, width), lambda i, j: (i, j, 0))

    def const(shape):
        return pl.BlockSpec(shape, lambda i, j: (0,) * len(shape), pipeline_mode=pl.Buffered(1))

    def first(a):
        return pl.BlockSpec((S_TILE,) + a.shape[1:], lambda i, j: (i * nt + j, 0, 0))

    def halves(shape):
        return pl.BlockSpec((2, S_TILE) + shape, lambda i, j: (0, i * nt + j, 0, 0))

    y, hst, u, nst = pl.pallas_call(
        _prompt_tail_kernel,
        out_shape=(jax.ShapeDtypeStruct((b, s, D_MODEL), F32),
                   jax.ShapeDtypeStruct((b, HG_HEADS, HG_DIM, HG_DIM), F32),
                   jax.ShapeDtypeStruct((2, half, U_ROWS, LANES), F32),
                   jax.ShapeDtypeStruct((2, half, HG_HEADS * HG_DIM, HG_DIM), F32)),
        grid=(b, nt),
        in_specs=[tok(D_MODEL), tok(D_GQ), tok(D_HB), tok(D_ZG), tok(D_B),
                  *[pl.BlockSpec((1, OG_SLABS, t, LANES), lambda i, j: (i, 0, j, 0))] * N_GROUPS,
                  pl.BlockSpec((1, MEM_LEN, 2 * D_M), lambda i, j: (i, 0, 0)),
                  const((2, D_B)), const((1, D_B)), const((1, D_MODEL)),
                  const((D_A, D_MODEL)), const((D_B, D_MODEL)), const((D_M, D_MODEL)),
                  const((D_MODEL, D_MODEL)),
                  *_sample_specs(sample_ops, nt, half), first(u_first), first(nst_first)],
        out_specs=(tok(D_MODEL),
                   pl.BlockSpec((1, HG_HEADS, HG_DIM, HG_DIM), lambda i, j: (i, 0, 0, 0)),
                   halves((U_ROWS, LANES)), halves((HG_HEADS * HG_DIM, HG_DIM))),
        scratch_shapes=[pltpu.VMEM((HG_HEADS, HG_DIM, HG_DIM), F32)],
        compiler_params=pltpu.CompilerParams(
            dimension_semantics=("arbitrary", "arbitrary"), vmem_limit_bytes=VMEM_LIMIT),
        name="prompt_tail",
    )(x, gq, hb, zg, fb, o1, o2, o3, mkv, lbl, nh, nf, wa, wb, wm, wo, *sample_ops, u_first, nst_first)
    return y, hst, u.reshape(db, U_ROWS * LANES), nst.reshape(db, HG_HEADS * HG_DIM, HG_DIM)


S_COLS = 1024


def _sample_inproj_kernel(x_ref, gain_ref, w_ref, zr_ref, zg_ref, kvt_ref, wbf_ref):
    c = pl.program_id(0)
    h = _rms(x_ref[...], gain_ref[...]).astype(BF16)
    w = w_ref[...].astype(BF16)
    wbf_ref[...] = w
    z = jnp.dot(h, w, preferred_element_type=F32)
    for k in range(S_COLS // LANES):
        zr_ref[:, k, :] = z[:, k * LANES:(k + 1) * LANES]
    zg_ref[...] = z
    for step in range(OFF_GA // S_COLS + 1):
        pieces = [(g, kv, (OFF_KA + (kv * N_GROUPS + g) * D_A) % S_COLS)
                  for g in range(N_GROUPS) for kv in range(2)
                  if (OFF_KA + (kv * N_GROUPS + g) * D_A) // S_COLS == step]

        @pl.when(c == step)
        def _(pieces=pieces):
            zt = z.T
            for g, kv, row0 in pieces:
                kvt_ref[g, kv * D_A:(kv + 1) * D_A, :] = zt[row0:row0 + D_A]


def _sample_inproj_call(x, gain, w):
    db = x.shape[0]
    first_zg = OFF_ZG // S_COLS
    return pl.pallas_call(
        _sample_inproj_kernel,
        out_shape=(jax.ShapeDtypeStruct((db, Z_ROWS, LANES), F32),
                   jax.ShapeDtypeStruct((db, D_ZG), F32),
                   jax.ShapeDtypeStruct((N_GROUPS, 2 * D_A, db), F32),
                   jax.ShapeDtypeStruct((D_MODEL, D_IN), BF16)),
        grid=(D_IN // S_COLS,),
        in_specs=[pl.BlockSpec((db, D_MODEL), lambda c: (0, 0)),
                  pl.BlockSpec((1, D_MODEL), lambda c: (0, 0)),
                  pl.BlockSpec((D_MODEL, S_COLS), lambda c: (0, c))],
        out_specs=(pl.BlockSpec((db, S_COLS // LANES, LANES), lambda c: (0, c, 0)),
                   pl.BlockSpec((db, S_COLS), lambda c: (0, jnp.maximum(c - first_zg, 0))),
                   pl.BlockSpec((N_GROUPS, 2 * D_A, db), lambda c: (0, 0, 0)),
                   pl.BlockSpec((D_MODEL, S_COLS), lambda c: (0, c))),
        compiler_params=pltpu.CompilerParams(dimension_semantics=("arbitrary",)),
        name="sample_inproj",
    )(x, gain, w)


S_TILE = 1
Z_ROWS = D_IN // LANES
U_ROWS = 8


def _sample_rows(zr_ref, c1_ref, c2_ref, c3_ref, cm_ref, st_ref, lblt_ref, nh_ref, u_ref, nst_ref):
    bt = S_TILE
    pad = [jnp.zeros((LANES - bt * Z_ROWS, LANES), F32)] if bt * Z_ROWS < LANES else []
    zt = jnp.concatenate([zr_ref[b] for b in range(bt)] + pad, axis=0).T

    def col(b, off, n=HG_DIM):
        j, l = divmod(off, LANES)
        return zt[l:l + n, Z_ROWS * b + j:Z_ROWS * b + j + 1]

    lbt = lblt_ref[...]
    lb_col = _lower_bound(lbt[:, 0:1], lbt[:, 1:2])
    nh = nh_ref[...]

    def col2(b, off):
        return jnp.concatenate([col(b, off), col(b, off + LANES)], axis=0)

    def per_head(x):
        return jnp.concatenate([jnp.sum(x[h * A_HEAD_DIM:(h + 1) * A_HEAD_DIM], axis=0, keepdims=True)
                                for h in range(A_HEADS)], axis=0)

    def spread(x):
        return jnp.concatenate([jnp.broadcast_to(x[h:h + 1], (A_HEAD_DIM, 1)) for h in range(A_HEADS)], axis=0)

    def attend(b, srcs):
        parts, new_scores, qs = [], [], []
        for c_ref, dil, q_off, k_off, _ in srcs:
            q = col2(b, q_off) * ATT_SCALE
            qs.append(q)
            s = per_head(c_ref[b, 0:D_A, :] * q)
            if dil > 1:
                lane = lax.broadcasted_iota(jnp.int32, s.shape, 1)
                s = jnp.where(lane % dil == 0, s, -jnp.inf)
            parts.append(s)
            if k_off is not None:
                new_scores.append(per_head(col2(b, k_off) * q))
        if new_scores:
            lane = lax.broadcasted_iota(jnp.int32, (A_HEADS, LANES), 1)
            slab = jnp.full((A_HEADS, LANES), -jnp.inf, F32)
            for k, sn in enumerate(new_scores):
                slab = jnp.where(lane == k, sn, slab)
            parts.append(slab)
        s_all = jnp.concatenate(parts, axis=1)
        p_all = jnp.exp(s_all - jnp.max(s_all, axis=1, keepdims=True))
        den = jnp.sum(p_all, axis=1, keepdims=True)
        acc = [jnp.zeros((A_HEAD_DIM, LANES), F32) for _ in range(A_HEADS)]
        lo = 0
        for c_ref, _, _, _, _ in srcs:
            length = c_ref.shape[2]
            for k in range(length // LANES):
                ls = slice(k * LANES, (k + 1) * LANES)
                for h in range(A_HEADS):
                    vt = c_ref[b, D_A + h * A_HEAD_DIM:D_A + (h + 1) * A_HEAD_DIM, ls]
                    acc[h] = acc[h] + vt * p_all[h:h + 1, lo + k * LANES:lo + (k + 1) * LANES]
            lo += length
        num = jnp.sum(jnp.concatenate(acc, axis=0), axis=1, keepdims=True)
        k = 0
        for _, _, _, k_off, v_off in srcs:
            if k_off is not None:
                num = num + spread(p_all[:, lo + k:lo + k + 1]) * col2(b, v_off)
                k += 1
        return num / spread(den)

    window = [(c_ref, dil, OFF_QA + g * D_A, OFF_KA + g * D_A, OFF_VA + g * D_A)
              for g, (c_ref, (_, dil)) in enumerate(zip((c1_ref, c2_ref, c3_ref), WIN_GROUPS))]
    memory = [(cm_ref, 1, OFF_QM, None, None)]

    for b in range(bt):
        ua = attend(b, window) * _silu(col2(b, OFF_GA))
        um = attend(b, memory) * _silu(col2(b, OFF_GM))
        branch_cols = [ua[0:LANES], ua[LANES:2 * LANES], um[0:LANES], um[LANES:2 * LANES]]

        lane = lax.broadcasted_iota(jnp.int32, (LANES, LANES), 1)
        ut = jnp.zeros((LANES, LANES), F32)
        for k, cvec in enumerate(branch_cols):
            ut = ut + jnp.where(lane == k, cvec, 0.0)
        u_ref[b, 0:4, :] = ut.T[0:4, :]

        orows = []
        for h in range(HG_HEADS):
            sl = slice(h * HG_DIM, (h + 1) * HG_DIM)
            fcol = lb_col[sl] + (1.0 - lb_col[sl]) * _sigmoid(col(b, OFF_FB + h * HG_DIM))
            qcol = _silu(col(b, OFF_QB + h * HG_DIM))
            vrow = zr_ref[b, OFF_IB // LANES + h:OFF_IB // LANES + h + 1, :]
            sn = fcol * st_ref[b, sl, :] + (1.0 - fcol) * vrow
            nst_ref[b, sl, :] = sn
            orows.append(jnp.sum(sn * qcol, axis=0, keepdims=True))
        gate = zr_ref[b, OFF_GB // LANES:OFF_GB // LANES + HG_HEADS, :]
        u_ref[b, 4:4 + HG_HEADS, :] = _rms(jnp.concatenate(orows, axis=0), nh) * _silu(gate)


def _stored_view(cache):
    db, length = cache.shape[0], cache.shape[1]
    return jnp.transpose(cache, (0, 2, 3, 4, 1)).reshape(db, 2 * D_A, length)


def _sample_out_kernel(x_ref, u_ref, zg_ref, nf_ref, wa_ref, wb_ref, wm_ref, wo_ref, y_ref):
    u = u_ref[...]
    y_ref[...] = _merge_out(x_ref[...], u[:, 0:D_A], u[:, D_A + D_M:], u[:, D_A:D_A + D_M], zg_ref[...],
                            wa_ref, wb_ref, wm_ref, wo_ref, nf_ref[...])


def _sample_out_call(x, u, zg, nf, wa, wb, wm, wo):
    db = x.shape[0]

    def full(a):
        return pl.BlockSpec(a.shape, lambda i: (0,) * a.ndim)

    args = (x, u, zg, nf, wa, wb, wm, wo)
    return pl.pallas_call(
        _sample_out_kernel,
        out_shape=jax.ShapeDtypeStruct((db, D_MODEL), F32),
        grid=(1,),
        in_specs=[full(a) for a in args],
        out_specs=pl.BlockSpec((db, D_MODEL), lambda i: (0, 0)),
        compiler_params=pltpu.CompilerParams(vmem_limit_bytes=VMEM_LIMIT),
        name="sample_out",
    )(*args)


def _cache_from_stored(kvt):
    b, _, length = kvt.shape
    return jnp.transpose(kvt.reshape(b, 2, A_HEADS, A_HEAD_DIM, length), (0, 4, 1, 2, 3))[None]


def kernel(x_prompt, x_sample, mem_prompt, cache_w1_kv, cache_w2_kv, cache_w3_kv, cache_mem_kv, state_hgrn,
           norm_in, w_in, lb_logits, norm_hgrn, norm_mem, w_mem_kv, w_branch_a, w_branch_b, w_branch_m,
           w_out, norm_final):
    b, s, _ = x_prompt.shape
    db = x_sample.shape[0]
    wa, wb, wm, wo = (w[0].astype(BF16) for w in (w_branch_a, w_branch_b, w_branch_m, w_out))
    gain_in = norm_in[0][None]
    nh = norm_hgrn[0][None]
    nf = norm_final[None]
    lbl = lb_logits.astype(F32)

    xs = x_sample.reshape(db, D_MODEL)
    zr, zg_s, kvt_s, w_in_bf = _sample_inproj_call(xs, gain_in, w_in[0])
    mkv, mkv_t = _mem_kv_call(mem_prompt, norm_mem[0][None], w_mem_kv[0].astype(BF16))
    sample_ops = (zr,
                  *[_stored_view(c[0]) for c in (cache_w1_kv, cache_w2_kv, cache_w3_kv, cache_mem_kv)],
                  state_hgrn[0].reshape(db, HG_HEADS * HG_DIM, HG_DIM), lbl.T, nh.reshape(HG_HEADS, HG_DIM))
    qkv0, qkv1, qkv2, gq, hb, zg, fb, kvt0, kvt1, kvt2, u_first, nst_first = _inproj_call(
        x_prompt, gain_in, w_in_bf, sample_ops)
    og = [_dilated_call(qkv, dil, g) for g, (qkv, (_, dil)) in enumerate(zip((qkv0, qkv1, qkv2), WIN_GROUPS))]
    y_prompt, hg_prompt, u, nst = _prompt_tail_call(
        x_prompt, gq, hb, zg, fb, og[0], og[1], og[2], mkv, lbl, nh, nf, wa, wb, wm, wo,
        sample_ops, u_first, nst_first)
    pw = [_cache_from_stored(kvt) for kvt in (kvt0, kvt1, kvt2)]
    new_mem = _cache_from_stored(mkv_t)

    y_sample = _sample_out_call(xs, u, zg_s, nf, wa, wb, wm, wo).reshape(db, 1, D_MODEL)
    sw = [jnp.transpose(kvt_s[g].reshape(2, A_HEADS, A_HEAD_DIM, db), (3, 0, 1, 2))[None, :, None]
          for g in range(N_GROUPS)]
    new_hg_sample = nst.reshape(1, db, HG_HEADS, HG_DIM, HG_DIM)

    return (y_prompt, y_sample, pw[0], pw[1], pw[2], new_mem, hg_prompt[None],
            sw[0], sw[1], sw[2], new_hg_sample)
```

```python
import jax
import jax.numpy as jnp
from jax import lax
from jax.experimental import pallas as pl
from jax.experimental.pallas import tpu as pltpu

F32 = jnp.float32
BF16 = jnp.bfloat16

D_MODEL = 1024
WIN_GROUPS = ((128, 1), (512, 4), (2048, 16))
N_GROUPS = 3
A_HEADS = 4
A_HEAD_DIM = 64
D_A = 256
QBLK = 128
HG_HEADS = 4
HG_DIM = 128
D_B = 512
HG_CHUNK = 64
HG_SUB = 16
MEM_LEN = 256
D_M = 256
EPS = 1e-6
D_IN = 8192
LANES = 128
OFF_QA, OFF_KA, OFF_VA, OFF_GA = 0, 768, 1536, 2304
OFF_QB, OFF_FB, OFF_IB, OFF_GB = 2560, 3072, 3584, 4096
OFF_QM, OFF_GM, OFF_ZG = 4608, 4864, 5120
ATT_SCALE = A_HEAD_DIM ** -0.5

VMEM_LIMIT = 56 * 1024 * 1024

NT_DIMS = (((1,), (1,)), ((), ()))
TN_DIMS = (((0,), (0,)), ((), ()))


def _sigmoid(x):
    return 0.5 * jnp.tanh(0.5 * x) + 0.5


def _silu(x):
    h = 0.5 * x
    return h * jnp.tanh(h) + h


def _rms(x, gain):
    return x * lax.rsqrt(jnp.mean(x * x, axis=-1, keepdims=True) + EPS) * gain


def _head_masks(width, head_dim, n_heads):
    lane = lax.broadcasted_iota(jnp.int32, (1, width), 1)
    return [(lane >= h * head_dim) & (lane < (h + 1) * head_dim) for h in range(n_heads)]


def _stack_heads(q, masks):
    zero = jnp.zeros_like(q)
    return jnp.concatenate([jnp.where(m, q, zero) for m in masks], axis=0)


def _unstack_heads(o_all, masks, t):
    n = len(masks)
    out = o_all[(n - 1) * t:n * t]
    for h in range(n - 2, -1, -1):
        out = jnp.where(masks[h], o_all[h * t:(h + 1) * t], out)
    return out


def _expand_cols(cols, masks):
    out = jnp.broadcast_to(cols[-1], (cols[-1].shape[0], masks[0].shape[1]))
    for h in range(len(masks) - 2, -1, -1):
        out = jnp.where(masks[h], cols[h], out)
    return out


def _mem_kv_kernel(mem_ref, gain_ref, w_ref, out_ref, out_t_ref):
    hm = _rms(mem_ref[0], gain_ref[...]).astype(BF16)
    kv = jnp.dot(hm, w_ref[...], preferred_element_type=F32)
    out_ref[0] = kv
    out_t_ref[0] = kv.T


def _mem_kv_call(mem, gain, w_bf):
    b = mem.shape[0]
    return pl.pallas_call(
        _mem_kv_kernel,
        out_shape=(jax.ShapeDtypeStruct((b, MEM_LEN, 2 * D_M), F32),
                   jax.ShapeDtypeStruct((b, 2 * D_M, MEM_LEN), F32)),
        grid=(b,),
        in_specs=[pl.BlockSpec((1, MEM_LEN, D_MODEL), lambda i: (i, 0, 0)),
                  pl.BlockSpec((1, D_MODEL), lambda i: (0, 0)),
                  pl.BlockSpec((D_MODEL, 2 * D_M), lambda i: (0, 0))],
        out_specs=(pl.BlockSpec((1, MEM_LEN, 2 * D_M), lambda i: (i, 0, 0)),
                   pl.BlockSpec((1, 2 * D_M, MEM_LEN), lambda i: (i, 0, 0))),
        name="mem_kv",
    )(mem, gain, w_bf)


IN_TILE = 256
IN_CHUNK = 256
D_QKV = 3 * D_A
D_GQ = 3 * D_A
D_HB = 3 * D_B
D_ZG = 3 * D_MODEL


def _inproj_kernel(x_ref, gain_ref, w_ref, zr_ref, c1_ref, c2_ref, c3_ref, cm_ref, sst_ref, lblt_ref, nh4_ref,
                   qkv0_ref, qkv1_ref, qkv2_ref, gq_ref, hb_ref, zg_ref, fb_ref, kvt0_ref, kvt1_ref, kvt2_ref,
                   u_ref, nst_ref, scr_ref):
    qkv_refs = (qkv0_ref, qkv1_ref, qkv2_ref)
    kvt_refs = (kvt0_ref, kvt1_ref, kvt2_ref)
    t = IN_TILE
    h = _rms(x_ref[0], gain_ref[...]).astype(BF16)
    for c in range(D_IN // IN_CHUNK):
        lo = c * IN_CHUNK
        zc = jnp.dot(h, w_ref[:, lo:lo + IN_CHUNK], preferred_element_type=F32)
        if lo < OFF_GA:
            sec, g = divmod(c, N_GROUPS)
            dil = WIN_GROUPS[g][1]
            dst = slice(sec * D_A, (sec + 1) * D_A)
            if dil == 1:
                qkv_refs[g][0, 0, :, dst] = zc.astype(BF16)
            else:
                for k in range(IN_CHUNK // LANES):
                    scr_ref[k] = zc[:, k * LANES:(k + 1) * LANES]
                for r in range(dil):
                    qkv_refs[g][0, r, :, dst] = jnp.concatenate(
                        [scr_ref[k, pl.ds(r, t // dil, stride=dil), :] for k in range(IN_CHUNK // LANES)],
                        axis=1).astype(BF16)
            if sec > 0:
                width = kvt_refs[g].shape[2]
                kvt_refs[g][0, (sec - 1) * D_A:sec * D_A, :] = zc.T[:, t - width:]
        elif lo < OFF_QB:
            gq_ref[0, :, 0:D_A] = zc.astype(BF16)
        elif lo < OFF_FB:
            hb_ref[0, :, lo - OFF_QB:lo - OFF_QB + IN_CHUNK] = zc.astype(BF16)
        elif lo < OFF_IB:
            fb_ref[0, :, lo - OFF_FB:lo - OFF_FB + IN_CHUNK] = zc
        elif lo < OFF_QM:
            hb_ref[0, :, lo - OFF_IB + D_B:lo - OFF_IB + D_B + IN_CHUNK] = zc.astype(BF16)
        elif lo < OFF_ZG:
            gq_ref[0, :, lo - OFF_QM + D_A:lo - OFF_QM + D_A + IN_CHUNK] = zc.astype(BF16)
        else:
            zg_ref[0, :, lo - OFF_ZG:lo - OFF_ZG + IN_CHUNK] = zc.astype(BF16)
    _sample_rows(zr_ref, c1_ref, c2_ref, c3_ref, cm_ref, sst_ref, lblt_ref, nh4_ref, u_ref, nst_ref)


def _sample_specs(ops, nt, first_row):
    first_blk = first_row // S_TILE

    def rows(a):
        return pl.BlockSpec((S_TILE,) + a.shape[1:], lambda i, j: (first_blk + i * nt + j, 0, 0))

    def const(a):
        return pl.BlockSpec(a.shape, lambda i, j: (0, 0), pipeline_mode=pl.Buffered(1))

    return [rows(a) for a in ops[:6]] + [const(a) for a in ops[6:]]


def _inproj_call(x, gain, w_bf, sample_ops):
    b, s, _ = x.shape
    t = IN_TILE
    nt = s // t
    n_rows = b * nt * S_TILE

    def tok(width):
        return pl.BlockSpec((1, t, width), lambda i, j: (i, j, 0))

    def srow(shape):
        return pl.BlockSpec((S_TILE,) + shape, lambda i, j: (i * nt + j, 0, 0))

    qkv_shapes, qkv_specs = [], []
    for _, dil in WIN_GROUPS:
        qkv_shapes.append(jax.ShapeDtypeStruct((b, dil, s // dil, D_QKV), BF16))
        qkv_specs.append(pl.BlockSpec((1, dil, t // dil, D_QKV), lambda i, j: (i, 0, j, 0)))
    kvt_shapes, kvt_specs = [], []
    for window, _ in WIN_GROUPS:
        length = min(window, s)
        width = min(t, length)
        first = nt - length // width
        kvt_shapes.append(jax.ShapeDtypeStruct((b, 2 * D_A, length), F32))
        kvt_specs.append(pl.BlockSpec((1, 2 * D_A, width),
                                      lambda i, j, first=first: (i, 0, jnp.maximum(j - first, 0))))
    return pl.pallas_call(
        _inproj_kernel,
        out_shape=(*qkv_shapes,
                   jax.ShapeDtypeStruct((b, s, D_GQ), BF16),
                   jax.ShapeDtypeStruct((b, s, D_HB), BF16),
                   jax.ShapeDtypeStruct((b, s, D_ZG), BF16),
                   jax.ShapeDtypeStruct((b, s, D_B), F32),
                   *kvt_shapes,
                   jax.ShapeDtypeStruct((n_rows, U_ROWS, LANES), F32),
                   jax.ShapeDtypeStruct((n_rows, HG_HEADS * HG_DIM, HG_DIM), F32)),
        grid=(b, nt),
        in_specs=[tok(D_MODEL),
                  pl.BlockSpec((1, D_MODEL), lambda i, j: (0, 0)),
                  pl.BlockSpec((D_MODEL, D_IN), lambda i, j: (0, 0), pipeline_mode=pl.Buffered(1)),
                  *_sample_specs(sample_ops, nt, 0)],
        out_specs=(*qkv_specs, tok(D_GQ), tok(D_HB), tok(D_ZG), tok(D_B), *kvt_specs,
                   srow((U_ROWS, LANES)), srow((HG_HEADS * HG_DIM, HG_DIM))),
        scratch_shapes=[pltpu.VMEM((IN_CHUNK // LANES, t, LANES), F32)],
        compiler_params=pltpu.CompilerParams(
            dimension_semantics=("arbitrary", "arbitrary"), vmem_limit_bytes=VMEM_LIMIT),
        name="inproj",
    )(x, gain, w_bf, *sample_ops)


DIL_STEP = 2048
DIL_GROUP = 1
OG_SLABS = 2 * D_A // LANES


def _make_dilated_kernel(dil):
    nblk = DIL_STEP // (dil * QBLK)

    def body(prev_ref, cur_ref, out_ref):
        first_step = pl.program_id(1) == 0
        masks = _head_masks(D_A, A_HEAD_DIM, A_HEADS)
        qi = lax.broadcasted_iota(jnp.int32, (A_HEADS * QBLK, 2 * QBLK), 0) % QBLK
        kj = lax.broadcasted_iota(jnp.int32, (A_HEADS * QBLK, 2 * QBLK), 1)
        band = (kj >= qi) & (kj <= qi + QBLK)
        neg_band = jnp.where(band, 0.0, -jnp.inf)
        neg_first = jnp.where(first_step, jnp.where(band & (kj >= QBLK), 0.0, -jnp.inf), neg_band)

        def keys_values(r, jb, c):
            rows = slice(jb * QBLK, (jb + 1) * QBLK)
            if jb == 0:
                return jnp.concatenate([prev_ref[0, r, :, c * D_A:(c + 1) * D_A],
                                        cur_ref[0, r, rows, c * D_A:(c + 1) * D_A]], axis=0)
            return cur_ref[0, r, (jb - 1) * QBLK:(jb + 1) * QBLK, c * D_A:(c + 1) * D_A]

        blocks = [(r, jb) for r in range(dil) for jb in range(nblk)]
        for g0 in range(0, len(blocks), DIL_GROUP):
            group = blocks[g0:g0 + DIL_GROUP]
            scores = []
            for r, jb in group:
                q = cur_ref[0, r, jb * QBLK:(jb + 1) * QBLK, 0:D_A]
                qs = _stack_heads(q * jnp.asarray(ATT_SCALE, BF16), masks)
                s = lax.dot_general(qs, keys_values(r, jb, 1), NT_DIMS, preferred_element_type=F32)
                scores.append(s + (neg_first if jb == 0 else neg_band))
            probs = []
            for s in scores:
                mx = jnp.max(s, axis=-1, keepdims=True)
                p = jnp.exp(s - mx)
                probs.append((p.astype(BF16), mx, jnp.sum(p, axis=-1, keepdims=True)))
            outs = [jnp.dot(p, keys_values(r, jb, 2), preferred_element_type=F32)
                    for (p, _, _), (r, jb) in zip(probs, group)]
            for o_all, (_, mx, l), (r, jb) in zip(outs, probs, group):
                l_e = _expand_cols([l[h * QBLK:(h + 1) * QBLK] for h in range(A_HEADS)], masks)
                mx_e = _expand_cols([mx[h * QBLK:(h + 1) * QBLK] for h in range(A_HEADS)], masks)
                res = (_unstack_heads(o_all, masks, QBLK) / l_e, mx_e + jnp.log(l_e))
                for k in range(OG_SLABS):
                    src = res[k // 2][:, (k % 2) * LANES:(k % 2 + 1) * LANES]
                    if dil == 1:
                        out_ref[0, k, jb * QBLK:(jb + 1) * QBLK, :] = src
                    else:
                        out_ref[0, k, pl.ds(jb * QBLK * dil + r, QBLK, stride=dil), :] = src
    return body


def _dilated_call(qkv, dil, g):
    b, _, n, _ = qkv.shape
    rows = DIL_STEP // dil
    nb = n // rows
    return pl.pallas_call(
        _make_dilated_kernel(dil),
        out_shape=jax.ShapeDtypeStruct((b, OG_SLABS, n * dil, LANES), F32),
        grid=(b, nb),
        in_specs=[pl.BlockSpec((1, dil, QBLK, D_QKV),
                               lambda i, u: (i, 0, jnp.maximum(u * (rows // QBLK) - 1, 0), 0)),
                  pl.BlockSpec((1, dil, rows, D_QKV), lambda i, u: (i, 0, u, 0))],
        out_specs=pl.BlockSpec((1, OG_SLABS, DIL_STEP, LANES), lambda i, u: (i, 0, u, 0)),
        compiler_params=pltpu.CompilerParams(
            dimension_semantics=("arbitrary", "arbitrary"), vmem_limit_bytes=VMEM_LIMIT),
        name=f"dilated_attn_g{g}",
    )(qkv, qkv)


def _lower_bound(l0, l1):
    m = jnp.maximum(l0, l1)
    e0, e1 = jnp.exp(l0 - m), jnp.exp(l1 - m)
    return e0 / (e0 + e1)


def _head_rms(ob, gain):
    parts = []
    for h in range(HG_HEADS):
        sl = slice(h * HG_DIM, (h + 1) * HG_DIM)
        parts.append(_rms(ob[:, sl], gain[:, sl]))
    return jnp.concatenate(parts, axis=1)


def _merge_gates(zg):
    return _sigmoid(zg.astype(BF16))


def _gated_proj(u, w_ref, gates, k):
    proj = jnp.dot(u.astype(BF16), w_ref[...], preferred_element_type=F32).astype(BF16)
    return gates[:, k * D_MODEL:(k + 1) * D_MODEL] * proj


def _residual_out(x, merged, wo_ref, nf):
    y = x + jnp.dot(merged, wo_ref[...], preferred_element_type=F32)
    return _rms(y, nf)


def _merge_out(x, ua, ub, um, zg, wa_ref, wb_ref, wm_ref, wo_ref, nf):
    gates = _merge_gates(zg)
    merged = (_gated_proj(ua, wa_ref, gates, 0) + _gated_proj(ub, wb_ref, gates, 1)
              + _gated_proj(um, wm_ref, gates, 2))
    return _residual_out(x, merged, wo_ref, nf)


TAIL_TILE = 256


def _cumsum_rows(g, tri):
    g1 = g.astype(BF16)
    g2 = (g - g1.astype(F32)).astype(BF16)
    return jnp.dot(tri, g1, preferred_element_type=F32) + jnp.dot(tri, g2, preferred_element_type=F32)


def _hgrn_tile(fb, qb, v, lb, st_ref):
    t = fb.shape[0]
    c, sub = HG_CHUNK, HG_SUB
    n_sub = c // sub
    f = lb + (1.0 - lb) * _sigmoid(fb)
    kk = 1.0 - f
    row = lax.broadcasted_iota(jnp.int32, (t, t), 0)
    col = lax.broadcasted_iota(jnp.int32, (t, t), 1)
    same_chunk = (row // c) == (col // c)
    gcs = _cumsum_rows(jnp.log2(f), (same_chunk & (col <= row)).astype(BF16))
    q = _silu(qb).astype(F32)

    loc = lax.broadcasted_iota(jnp.int32, (c, 1), 0)
    qt_c, kt_ci, qe_c, kd_c, dec_c = [], [[] for _ in range(n_sub)], [], [], []
    for ci in range(t // c):
        rows = slice(ci * c, (ci + 1) * c)
        g_c, q_c, k_c = gcs[rows], q[rows], kk[rows]
        refs = [jnp.zeros((1, D_B), F32)] + [g_c[i * sub - 1:i * sub] for i in range(1, n_sub)]
        own = refs[n_sub - 1]
        for i in range(n_sub - 2, -1, -1):
            own = jnp.where(loc < (i + 1) * sub, refs[i], own)
        qt_c.append((q_c * jnp.exp2(g_c - own)).astype(BF16))
        k_own = k_c * jnp.exp2(own - g_c)
        for i in range(n_sub):
            pieces = []
            for jb in range(n_sub):
                piece = k_own[jb * sub:(jb + 1) * sub]
                if jb < i:
                    piece = piece * jnp.exp2(refs[i] - refs[jb])
                elif jb > i:
                    piece = jnp.zeros_like(piece)
                pieces.append(piece)
            kt_ci[i].append(jnp.concatenate(pieces, axis=0).astype(BF16))
        g_last = g_c[c - 1:c]
        qe_c.append((q_c * jnp.exp2(g_c)).astype(BF16))
        kd_c.append((k_c * jnp.exp2(g_last - g_c)).astype(BF16))
        dec_c.append(jnp.exp2(g_last))
    qt = jnp.concatenate(qt_c, axis=0)
    kts = [jnp.concatenate(parts, axis=0) for parts in kt_ci]
    sub_of_row = (lax.broadcasted_iota(jnp.int32, (t, 1), 0) % c) // sub
    causal = same_chunk & (col <= row)

    heads = [slice(h * HG_DIM, (h + 1) * HG_DIM) for h in range(HG_HEADS)]
    chunks = [slice(ci * c, (ci + 1) * c) for ci in range(t // c)]
    zero = jnp.zeros((t, HG_DIM), BF16)
    atts, incs = [], []
    for sl in heads:
        q_big = jnp.concatenate([jnp.where(sub_of_row == i, qt[:, sl], zero) for i in range(n_sub)], axis=1)
        k_big = jnp.concatenate([kts[i][:, sl] for i in range(n_sub)], axis=1)
        atts.append(lax.dot_general(q_big, k_big, NT_DIMS, preferred_element_type=F32))
        incs.append([lax.dot_general(v[rows, sl], kd_c[ci][:, sl], TN_DIMS, preferred_element_type=F32)
                     for ci, rows in enumerate(chunks)])
    states = []
    for h, sl in enumerate(heads):
        atts[h] = jnp.where(causal, atts[h], 0.0).astype(BF16)
        sts = [st_ref[h]]
        for ci in range(len(chunks)):
            sts.append(sts[-1] * dec_c[ci][:, sl] + incs[h][ci])
        st_ref[h] = sts[-1]
        states.append(sts)
    outs = []
    for h, sl in enumerate(heads):
        o_intra = jnp.dot(atts[h], v[:, sl], preferred_element_type=F32)
        inter = [lax.dot_general(qe_c[ci][:, sl], states[h][ci].astype(BF16), NT_DIMS, preferred_element_type=F32)
                 for ci in range(len(chunks))]
        outs.append(o_intra + jnp.concatenate(inter, axis=0))
    return jnp.concatenate(outs, axis=1)


def _prompt_tail_kernel(x_ref, gq_ref, hb_ref, zg_ref, fb_ref, o1_ref, o2_ref, o3_ref, mkv_ref, lbl_ref,
                        nh_ref, nf_ref, wa_ref, wb_ref, wm_ref, wo_ref,
                        zr_ref, c1_ref, c2_ref, c3_ref, cm_ref, sst_ref, lblt_ref, nh4_ref, ua_ref, nsta_ref,
                        y_ref, hst_ref, u_ref, nst_ref, st_ref):
    j = pl.program_id(1)
    t = TAIL_TILE

    @pl.when(j == 0)
    def _():
        st_ref[...] = jnp.zeros_like(st_ref)

    u_ref[0] = ua_ref[...]
    nst_ref[0] = nsta_ref[...]
    _sample_rows(zr_ref, c1_ref, c2_ref, c3_ref, cm_ref, sst_ref, lblt_ref, nh4_ref, u_ref.at[1], nst_ref.at[1])

    masks = _head_masks(D_M, A_HEAD_DIM, A_HEADS)
    qm = gq_ref[0, :, D_A:2 * D_A] * jnp.asarray(ATT_SCALE, BF16)
    qs = _stack_heads(qm, masks)
    mk = mkv_ref[0, :, 0:D_M].astype(BF16)
    mv = mkv_ref[0, :, D_M:2 * D_M].astype(BF16)
    s = lax.dot_general(qs, mk, NT_DIMS, preferred_element_type=F32)

    lb = _lower_bound(lbl_ref[0:1], lbl_ref[1:2])
    ob = _hgrn_tile(fb_ref[0], hb_ref[0, :, 0:D_B], hb_ref[0, :, D_B:2 * D_B], lb, st_ref)
    ub = _head_rms(ob, nh_ref[...]).astype(BF16) * _silu(hb_ref[0, :, 2 * D_B:3 * D_B])

    os_ = [jnp.concatenate([r[0, 0], r[0, 1]], axis=1) for r in (o1_ref, o2_ref, o3_ref)]
    ls_ = [jnp.concatenate([r[0, 2], r[0, 3]], axis=1) for r in (o1_ref, o2_ref, o3_ref)]
    mx = jnp.maximum(jnp.maximum(ls_[0], ls_[1]), ls_[2])
    es_ = [jnp.exp(l - mx) for l in ls_]
    oa = (es_[0] * os_[0] + es_[1] * os_[1] + es_[2] * os_[2]) / (es_[0] + es_[1] + es_[2])
    ua = oa.astype(BF16) * _silu(gq_ref[0, :, 0:D_A])

    p = jnp.exp(s - jnp.max(s, axis=-1, keepdims=True))
    l = jnp.sum(p, axis=-1, keepdims=True)
    om = _unstack_heads(jnp.dot(p.astype(BF16), mv, preferred_element_type=F32) / l, masks, t)
    um = om.astype(BF16) * _silu(gq_ref[0, :, 2 * D_A:3 * D_A])

    y_ref[0] = _merge_out(x_ref[0], ua, ub, um, zg_ref[0], wa_ref, wb_ref, wm_ref, wo_ref, nf_ref[...])

    @pl.when(j == pl.num_programs(1) - 1)
    def _():
        for h in range(HG_HEADS):
            hst_ref[0, h] = st_ref[h].T


def _prompt_tail_call(x, gq, hb, zg, fb, o1, o2, o3, mkv, lbl, nh, nf, wa, wb, wm, wo,
                      sample_ops, u_first, nst_first):
    b, s, _ = x.shape
    t = TAIL_TILE
    nt = s // t
    db = sample_ops[0].shape[0]
    half = u_first.shape[0]
    assert half + b * nt * S_TILE == db and 2 * half == db, "sample rows must split evenly over both grids"

    def tok(width):
        return pl.BlockSpec((1, t, width), lambda i, j: (i, j, 0))

    def const(shape):
        return pl.BlockSpec(shape, lambda i, j: (0,) * len(shape), pipeline_mode=pl.Buffered(1))

    def first(a):
        return pl.BlockSpec((S_TILE,) + a.shape[1:], lambda i, j: (i * nt + j, 0, 0))

    def halves(shape):
        return pl.BlockSpec((2, S_TILE) + shape, lambda i, j: (0, i * nt + j, 0, 0))

    y, hst, u, nst = pl.pallas_call(
        _prompt_tail_kernel,
        out_shape=(jax.ShapeDtypeStruct((b, s, D_MODEL), F32),
                   jax.ShapeDtypeStruct((b, HG_HEADS, HG_DIM, HG_DIM), F32),
                   jax.ShapeDtypeStruct((2, half, U_ROWS, LANES), F32),
                   jax.ShapeDtypeStruct((2, half, HG_HEADS * HG_DIM, HG_DIM), F32)),
        grid=(b, nt),
        in_specs=[tok(D_MODEL), tok(D_GQ), tok(D_HB), tok(D_ZG), tok(D_B),
                  *[pl.BlockSpec((1, OG_SLABS, t, LANES), lambda i, j: (i, 0, j, 0))] * N_GROUPS,
                  pl.BlockSpec((1, MEM_LEN, 2 * D_M), lambda i, j: (i, 0, 0)),
                  const((2, D_B)), const((1, D_B)), const((1, D_MODEL)),
                  const((D_A, D_MODEL)), const((D_B, D_MODEL)), const((D_M, D_MODEL)),
                  const((D_MODEL, D_MODEL)),
                  *_sample_specs(sample_ops, nt, half), first(u_first), first(nst_first)],
        out_specs=(tok(D_MODEL),
                   pl.BlockSpec((1, HG_HEADS, HG_DIM, HG_DIM), lambda i, j: (i, 0, 0, 0)),
                   halves((U_ROWS, LANES)), halves((HG_HEADS * HG_DIM, HG_DIM))),
        scratch_shapes=[pltpu.VMEM((HG_HEADS, HG_DIM, HG_DIM), F32)],
        compiler_params=pltpu.CompilerParams(
            dimension_semantics=("arbitrary", "arbitrary"), vmem_limit_bytes=VMEM_LIMIT),
        name="prompt_tail",
    )(x, gq, hb, zg, fb, o1, o2, o3, mkv, lbl, nh, nf, wa, wb, wm, wo, *sample_ops, u_first, nst_first)
    return y, hst, u.reshape(db, U_ROWS * LANES), nst.reshape(db, HG_HEADS * HG_DIM, HG_DIM)


S_COLS = 1024


def _sample_inproj_kernel(x_ref, gain_ref, w_ref, zr_ref, zg_ref, kvt_ref, wbf_ref):
    c = pl.program_id(0)
    h = _rms(x_ref[...], gain_ref[...]).astype(BF16)
    w = w_ref[...].astype(BF16)
    wbf_ref[...] = w
    z = jnp.dot(h, w, preferred_element_type=F32)
    for k in range(S_COLS // LANES):
        zr_ref[:, k, :] = z[:, k * LANES:(k + 1) * LANES]
    zg_ref[...] = z
    for step in range(OFF_GA // S_COLS + 1):
        pieces = [(g, kv, (OFF_KA + (kv * N_GROUPS + g) * D_A) % S_COLS)
                  for g in range(N_GROUPS) for kv in range(2)
                  if (OFF_KA + (kv * N_GROUPS + g) * D_A) // S_COLS == step]

        @pl.when(c == step)
        def _(pieces=pieces):
            zt = z.T
            for g, kv, row0 in pieces:
                kvt_ref[g, kv * D_A:(kv + 1) * D_A, :] = zt[row0:row0 + D_A]


def _sample_inproj_call(x, gain, w):
    db = x.shape[0]
    first_zg = OFF_ZG // S_COLS
    return pl.pallas_call(
        _sample_inproj_kernel,
        out_shape=(jax.ShapeDtypeStruct((db, Z_ROWS, LANES), F32),
                   jax.ShapeDtypeStruct((db, D_ZG), F32),
                   jax.ShapeDtypeStruct((N_GROUPS, 2 * D_A, db), F32),
                   jax.ShapeDtypeStruct((D_MODEL, D_IN), BF16)),
        grid=(D_IN // S_COLS,),
        in_specs=[pl.BlockSpec((db, D_MODEL), lambda c: (0, 0)),
                  pl.BlockSpec((1, D_MODEL), lambda c: (0, 0)),
                  pl.BlockSpec((D_MODEL, S_COLS), lambda c: (0, c))],
        out_specs=(pl.BlockSpec((db, S_COLS // LANES, LANES), lambda c: (0, c, 0)),
                   pl.BlockSpec((db, S_COLS), lambda c: (0, jnp.maximum(c - first_zg, 0))),
                   pl.BlockSpec((N_GROUPS, 2 * D_A, db), lambda c: (0, 0, 0)),
                   pl.BlockSpec((D_MODEL, S_COLS), lambda c: (0, c))),
        compiler_params=pltpu.CompilerParams(dimension_semantics=("arbitrary",)),
        name="sample_inproj",
    )(x, gain, w)


S_TILE = 1
Z_ROWS = D_IN // LANES
U_ROWS = 8


def _sample_rows(zr_ref, c1_ref, c2_ref, c3_ref, cm_ref, st_ref, lblt_ref, nh_ref, u_ref, nst_ref):
    bt = S_TILE
    pad = [jnp.zeros((LANES - bt * Z_ROWS, LANES), F32)] if bt * Z_ROWS < LANES else []
    zt = jnp.concatenate([zr_ref[b] for b in range(bt)] + pad, axis=0).T

    def col(b, off, n=HG_DIM):
        j, l = divmod(off, LANES)
        return zt[l:l + n, Z_ROWS * b + j:Z_ROWS * b + j + 1]

    lbt = lblt_ref[...]
    lb_col = _lower_bound(lbt[:, 0:1], lbt[:, 1:2])
    nh = nh_ref[...]

    def col2(b, off):
        return jnp.concatenate([col(b, off), col(b, off + LANES)], axis=0)

    def per_head(x):
        return jnp.concatenate([jnp.sum(x[h * A_HEAD_DIM:(h + 1) * A_HEAD_DIM], axis=0, keepdims=True)
                                for h in range(A_HEADS)], axis=0)

    def spread(x):
        return jnp.concatenate([jnp.broadcast_to(x[h:h + 1], (A_HEAD_DIM, 1)) for h in range(A_HEADS)], axis=0)

    def attend(b, srcs):
        parts, new_scores, qs = [], [], []
        for c_ref, dil, q_off, k_off, _ in srcs:
            q = col2(b, q_off) * ATT_SCALE
            qs.append(q)
            s = per_head(c_ref[b, 0:D_A, :] * q)
            if dil > 1:
                lane = lax.broadcasted_iota(jnp.int32, s.shape, 1)
                s = jnp.where(lane % dil == 0, s, -jnp.inf)
            parts.append(s)
            if k_off is not None:
                new_scores.append(per_head(col2(b, k_off) * q))
        if new_scores:
            lane = lax.broadcasted_iota(jnp.int32, (A_HEADS, LANES), 1)
            slab = jnp.full((A_HEADS, LANES), -jnp.inf, F32)
            for k, sn in enumerate(new_scores):
                slab = jnp.where(lane == k, sn, slab)
            parts.append(slab)
        s_all = jnp.concatenate(parts, axis=1)
        p_all = jnp.exp(s_all - jnp.max(s_all, axis=1, keepdims=True))
        den = jnp.sum(p_all, axis=1, keepdims=True)
        acc = [jnp.zeros((A_HEAD_DIM, LANES), F32) for _ in range(A_HEADS)]
        lo = 0
        for c_ref, _, _, _, _ in srcs:
            length = c_ref.shape[2]
            for k in range(length // LANES):
                ls = slice(k * LANES, (k + 1) * LANES)
                for h in range(A_HEADS):
                    vt = c_ref[b, D_A + h * A_HEAD_DIM:D_A + (h + 1) * A_HEAD_DIM, ls]
                    acc[h] = acc[h] + vt * p_all[h:h + 1, lo + k * LANES:lo + (k + 1) * LANES]
            lo += length
        num = jnp.sum(jnp.concatenate(acc, axis=0), axis=1, keepdims=True)
        k = 0
        for _, _, _, k_off, v_off in srcs:
            if k_off is not None:
                num = num + spread(p_all[:, lo + k:lo + k + 1]) * col2(b, v_off)
                k += 1
        return num / spread(den)

    window = [(c_ref, dil, OFF_QA + g * D_A, OFF_KA + g * D_A, OFF_VA + g * D_A)
              for g, (c_ref, (_, dil)) in enumerate(zip((c1_ref, c2_ref, c3_ref), WIN_GROUPS))]
    memory = [(cm_ref, 1, OFF_QM, None, None)]

    for b in range(bt):
        ua = attend(b, window) * _silu(col2(b, OFF_GA))
        um = attend(b, memory) * _silu(col2(b, OFF_GM))
        branch_cols = [ua[0:LANES], ua[LANES:2 * LANES], um[0:LANES], um[LANES:2 * LANES]]

        lane = lax.broadcasted_iota(jnp.int32, (LANES, LANES), 1)
        ut = jnp.zeros((LANES, LANES), F32)
        for k, cvec in enumerate(branch_cols):
            ut = ut + jnp.where(lane == k, cvec, 0.0)
        u_ref[b, 0:4, :] = ut.T[0:4, :]

        orows = []
        for h in range(HG_HEADS):
            sl = slice(h * HG_DIM, (h + 1) * HG_DIM)
            fcol = lb_col[sl] + (1.0 - lb_col[sl]) * _sigmoid(col(b, OFF_FB + h * HG_DIM))
            qcol = _silu(col(b, OFF_QB + h * HG_DIM))
            vrow = zr_ref[b, OFF_IB // LANES + h:OFF_IB // LANES + h + 1, :]
            sn = fcol * st_ref[b, sl, :] + (1.0 - fcol) * vrow
            nst_ref[b, sl, :] = sn
            orows.append(jnp.sum(sn * qcol, axis=0, keepdims=True))
        gate = zr_ref[b, OFF_GB // LANES:OFF_GB // LANES + HG_HEADS, :]
        u_ref[b, 4:4 + HG_HEADS, :] = _rms(jnp.concatenate(orows, axis=0), nh) * _silu(gate)


def _stored_view(cache):
    db, length = cache.shape[0], cache.shape[1]
    return jnp.transpose(cache, (0, 2, 3, 4, 1)).reshape(db, 2 * D_A, length)


def _sample_out_kernel(x_ref, u_ref, zg_ref, nf_ref, wa_ref, wb_ref, wm_ref, wo_ref, y_ref):
    u = u_ref[...]
    y_ref[...] = _merge_out(x_ref[...], u[:, 0:D_A], u[:, D_A + D_M:], u[:, D_A:D_A + D_M], zg_ref[...],
                            wa_ref, wb_ref, wm_ref, wo_ref, nf_ref[...])


def _sample_out_call(x, u, zg, nf, wa, wb, wm, wo):
    db = x.shape[0]

    def full(a):
        return pl.BlockSpec(a.shape, lambda i: (0,) * a.ndim)

    args = (x, u, zg, nf, wa, wb, wm, wo)
    return pl.pallas_call(
        _sample_out_kernel,
        out_shape=jax.ShapeDtypeStruct((db, D_MODEL), F32),
        grid=(1,),
        in_specs=[full(a) for a in args],
        out_specs=pl.BlockSpec((db, D_MODEL), lambda i: (0, 0)),
        compiler_params=pltpu.CompilerParams(vmem_limit_bytes=VMEM_LIMIT),
        name="sample_out",
    )(*args)


def _cache_from_stored(kvt):
    b, _, length = kvt.shape
    return jnp.transpose(kvt.reshape(b, 2, A_HEADS, A_HEAD_DIM, length), (0, 4, 1, 2, 3))[None]


def kernel(x_prompt, x_sample, mem_prompt, cache_w1_kv, cache_w2_kv, cache_w3_kv, cache_mem_kv, state_hgrn,
           norm_in, w_in, lb_logits, norm_hgrn, norm_mem, w_mem_kv, w_branch_a, w_branch_b, w_branch_m,
           w_out, norm_final):
    b, s, _ = x_prompt.shape
    db = x_sample.shape[0]
    wa, wb, wm, wo = (w[0].astype(BF16) for w in (w_branch_a, w_branch_b, w_branch_m, w_out))
    gain_in = norm_in[0][None]
    nh = norm_hgrn[0][None]
    nf = norm_final[None]
    lbl = lb_logits.astype(F32)

    xs = x_sample.reshape(db, D_MODEL)
    zr, zg_s, kvt_s, w_in_bf = _sample_inproj_call(xs, gain_in, w_in[0])
    mkv, mkv_t = _mem_kv_call(mem_prompt, norm_mem[0][None], w_mem_kv[0].astype(BF16))
    sample_ops = (zr,
                  *[_stored_view(c[0]) for c in (cache_w1_kv, cache_w2_kv, cache_w3_kv, cache_mem_kv)],
                  state_hgrn[0].reshape(db, HG_HEADS * HG_DIM, HG_DIM), lbl.T, nh.reshape(HG_HEADS, HG_DIM))
    qkv0, qkv1, qkv2, gq, hb, zg, fb, kvt0, kvt1, kvt2, u_first, nst_first = _inproj_call(
        x_prompt, gain_in, w_in_bf, sample_ops)
    og = [_dilated_call(qkv, dil, g) for g, (qkv, (_, dil)) in enumerate(zip((qkv0, qkv1, qkv2), WIN_GROUPS))]
    y_prompt, hg_prompt, u, nst = _prompt_tail_call(
        x_prompt, gq, hb, zg, fb, og[0], og[1], og[2], mkv, lbl, nh, nf, wa, wb, wm, wo,
        sample_ops, u_first, nst_first)
    pw = [_cache_from_stored(kvt) for kvt in (kvt0, kvt1, kvt2)]
    new_mem = _cache_from_stored(mkv_t)

    y_sample = _sample_out_call(xs, u, zg_s, nf, wa, wb, wm, wo).reshape(db, 1, D_MODEL)
    sw = [jnp.transpose(kvt_s[g].reshape(2, A_HEADS, A_HEAD_DIM, db), (3, 0, 1, 2))[None, :, None]
          for g in range(N_GROUPS)]
    new_hg_sample = nst.reshape(1, db, HG_HEADS, HG_DIM, HG_DIM)

    return (y_prompt, y_sample, pw[0], pw[1], pw[2], new_mem, hg_prompt[None],
            sw[0], sw[1], sw[2], new_hg_sample)
```

```python
import jax
import jax.numpy as jnp
from jax import lax
from jax.experimental import pallas as pl
from jax.experimental.pallas import tpu as pltpu

F32 = jnp.float32
BF16 = jnp.bfloat16

D_MODEL = 1024
WIN_GROUPS = ((128, 1), (512, 4), (2048, 16))
N_GROUPS = 3
A_HEADS = 4
A_HEAD_DIM = 64
D_A = 256
QBLK = 128
HG_HEADS = 4
HG_DIM = 128
D_B = 512
HG_CHUNK = 64
HG_SUB = 16
MEM_LEN = 256
D_M = 256
EPS = 1e-6
D_IN = 8192
LANES = 128
OFF_QA, OFF_KA, OFF_VA, OFF_GA = 0, 768, 1536, 2304
OFF_QB, OFF_FB, OFF_IB, OFF_GB = 2560, 3072, 3584, 4096
OFF_QM, OFF_GM, OFF_ZG = 4608, 4864, 5120
ATT_SCALE = A_HEAD_DIM ** -0.5

VMEM_LIMIT = 56 * 1024 * 1024

NT_DIMS = (((1,), (1,)), ((), ()))
TN_DIMS = (((0,), (0,)), ((), ()))


def _sigmoid(x):
    return 0.5 * jnp.tanh(0.5 * x) + 0.5


def _silu(x):
    h = 0.5 * x
    return h * jnp.tanh(h) + h


def _rms(x, gain):
    return x * lax.rsqrt(jnp.mean(x * x, axis=-1, keepdims=True) + EPS) * gain


def _head_masks(width, head_dim, n_heads):
    lane = lax.broadcasted_iota(jnp.int32, (1, width), 1)
    return [(lane >= h * head_dim) & (lane < (h + 1) * head_dim) for h in range(n_heads)]


def _stack_heads(q, masks):
    zero = jnp.zeros_like(q)
    return jnp.concatenate([jnp.where(m, q, zero) for m in masks], axis=0)


def _unstack_heads(o_all, masks, t):
    n = len(masks)
    out = o_all[(n - 1) * t:n * t]
    for h in range(n - 2, -1, -1):
        out = jnp.where(masks[h], o_all[h * t:(h + 1) * t], out)
    return out


def _expand_cols(cols, masks):
    out = jnp.broadcast_to(cols[-1], (cols[-1].shape[0], masks[0].shape[1]))
    for h in range(len(masks) - 2, -1, -1):
        out = jnp.where(masks[h], cols[h], out)
    return out


def _mem_kv_kernel(mem_ref, gain_ref, w_ref, out_ref, out_t_ref):
    hm = _rms(mem_ref[0], gain_ref[...]).astype(BF16)
    kv = jnp.dot(hm, w_ref[...], preferred_element_type=F32)
    out_ref[0] = kv
    out_t_ref[0] = kv.T


def _mem_kv_call(mem, gain, w_bf):
    b = mem.shape[0]
    return pl.pallas_call(
        _mem_kv_kernel,
        out_shape=(jax.ShapeDtypeStruct((b, MEM_LEN, 2 * D_M), F32),
                   jax.ShapeDtypeStruct((b, 2 * D_M, MEM_LEN), F32)),
        grid=(b,),
        in_specs=[pl.BlockSpec((1, MEM_LEN, D_MODEL), lambda i: (i, 0, 0)),
                  pl.BlockSpec((1, D_MODEL), lambda i: (0, 0)),
                  pl.BlockSpec((D_MODEL, 2 * D_M), lambda i: (0, 0))],
        out_specs=(pl.BlockSpec((1, MEM_LEN, 2 * D_M), lambda i: (i, 0, 0)),
                   pl.BlockSpec((1, 2 * D_M, MEM_LEN), lambda i: (i, 0, 0))),
        name="mem_kv",
    )(mem, gain, w_bf)


IN_TILE = 256
IN_CHUNK = 256
D_QKV = 3 * D_A
D_GQ = 3 * D_A
D_HB = 3 * D_B
D_ZG = 3 * D_MODEL


def _inproj_kernel(x_ref, gain_ref, w_ref, zr_ref, c1_ref, c2_ref, c3_ref, cm_ref, sst_ref, lblt_ref, nh4_ref,
                   qkv0_ref, qkv1_ref, qkv2_ref, gq_ref, hb_ref, zg_ref, fb_ref, kvt0_ref, kvt1_ref, kvt2_ref,
                   u_ref, nst_ref, scr_ref):
    qkv_refs = (qkv0_ref, qkv1_ref, qkv2_ref)
    kvt_refs = (kvt0_ref, kvt1_ref, kvt2_ref)
    t = IN_TILE
    h = _rms(x_ref[0], gain_ref[...]).astype(BF16)
    for c in range(D_IN // IN_CHUNK):
        lo = c * IN_CHUNK
        zc = jnp.dot(h, w_ref[:, lo:lo + IN_CHUNK], preferred_element_type=F32)
        if lo < OFF_GA:
            sec, g = divmod(c, N_GROUPS)
            dil = WIN_GROUPS[g][1]
            dst = slice(sec * D_A, (sec + 1) * D_A)
            if dil == 1:
                qkv_refs[g][0, 0, :, dst] = zc.astype(BF16)
            else:
                for k in range(IN_CHUNK // LANES):
                    scr_ref[k] = zc[:, k * LANES:(k + 1) * LANES]
                for r in range(dil):
                    qkv_refs[g][0, r, :, dst] = jnp.concatenate(
                        [scr_ref[k, pl.ds(r, t // dil, stride=dil), :] for k in range(IN_CHUNK // LANES)],
                        axis=1).astype(BF16)
            if sec > 0:
                width = kvt_refs[g].shape[2]
                kvt_refs[g][0, (sec - 1) * D_A:sec * D_A, :] = zc.T[:, t - width:]
        elif lo < OFF_QB:
            gq_ref[0, :, 0:D_A] = zc.astype(BF16)
        elif lo < OFF_FB:
            hb_ref[0, :, lo - OFF_QB:lo - OFF_QB + IN_CHUNK] = zc.astype(BF16)
        elif lo < OFF_IB:
            fb_ref[0, :, lo - OFF_FB:lo - OFF_FB + IN_CHUNK] = zc
        elif lo < OFF_QM:
            hb_ref[0, :, lo - OFF_IB + D_B:lo - OFF_IB + D_B + IN_CHUNK] = zc.astype(BF16)
        elif lo < OFF_ZG:
            gq_ref[0, :, lo - OFF_QM + D_A:lo - OFF_QM + D_A + IN_CHUNK] = zc.astype(BF16)
        else:
            zg_ref[0, :, lo - OFF_ZG:lo - OFF_ZG + IN_CHUNK] = zc.astype(BF16)
    _sample_rows(zr_ref, c1_ref, c2_ref, c3_ref, cm_ref, sst_ref, lblt_ref, nh4_ref, u_ref, nst_ref)


def _sample_specs(ops, nt, first_row):
    first_blk = first_row // S_TILE

    def rows(a):
        return pl.BlockSpec((S_TILE,) + a.shape[1:], lambda i, j: (first_blk + i * nt + j, 0, 0))

    def const(a):
        return pl.BlockSpec(a.shape, lambda i, j: (0, 0), pipeline_mode=pl.Buffered(1))

    return [rows(a) for a in ops[:6]] + [const(a) for a in ops[6:]]


def _inproj_call(x, gain, w_bf, sample_ops):
    b, s, _ = x.shape
    t = IN_TILE
    nt = s // t
    n_rows = b * nt * S_TILE

    def tok(width):
        return pl.BlockSpec((1, t, width), lambda i, j: (i, j, 0))

    def srow(shape):
        return pl.BlockSpec((S_TILE,) + shape, lambda i, j: (i * nt + j, 0, 0))

    qkv_shapes, qkv_specs = [], []
    for _, dil in WIN_GROUPS:
        qkv_shapes.append(jax.ShapeDtypeStruct((b, dil, s // dil, D_QKV), BF16))
        qkv_specs.append(pl.BlockSpec((1, dil, t // dil, D_QKV), lambda i, j: (i, 0, j, 0)))
    kvt_shapes, kvt_specs = [], []
    for window, _ in WIN_GROUPS:
        length = min(window, s)
        width = min(t, length)
        first = nt - length // width
        kvt_shapes.append(jax.ShapeDtypeStruct((b, 2 * D_A, length), F32))
        kvt_specs.append(pl.BlockSpec((1, 2 * D_A, width),
                                      lambda i, j, first=first: (i, 0, jnp.maximum(j - first, 0))))
    return pl.pallas_call(
        _inproj_kernel,
        out_shape=(*qkv_shapes,
                   jax.ShapeDtypeStruct((b, s, D_GQ), BF16),
                   jax.ShapeDtypeStruct((b, s, D_HB), BF16),
                   jax.ShapeDtypeStruct((b, s, D_ZG), BF16),
                   jax.ShapeDtypeStruct((b, s, D_B), F32),
                   *kvt_shapes,
                   jax.ShapeDtypeStruct((n_rows, U_ROWS, LANES), F32),
                   jax.ShapeDtypeStruct((n_rows, HG_HEADS * HG_DIM, HG_DIM), F32)),
        grid=(b, nt),
        in_specs=[tok(D_MODEL),
                  pl.BlockSpec((1, D_MODEL), lambda i, j: (0, 0)),
                  pl.BlockSpec((D_MODEL, D_IN), lambda i, j: (0, 0), pipeline_mode=pl.Buffered(1)),
                  *_sample_specs(sample_ops, nt, 0)],
        out_specs=(*qkv_specs, tok(D_GQ), tok(D_HB), tok(D_ZG), tok(D_B), *kvt_specs,
                   srow((U_ROWS, LANES)), srow((HG_HEADS * HG_DIM, HG_DIM))),
        scratch_shapes=[pltpu.VMEM((IN_CHUNK // LANES, t, LANES), F32)],
        compiler_params=pltpu.CompilerParams(
            dimension_semantics=("arbitrary", "arbitrary"), vmem_limit_bytes=VMEM_LIMIT),
        name="inproj",
    )(x, gain, w_bf, *sample_ops)


DIL_STEP = 2048
OG_SLABS = 2 * D_A // LANES


def _make_dilated_kernel(dil):
    nblk = DIL_STEP // (dil * QBLK)

    def body(prev_ref, cur_ref, out_ref):
        first_step = pl.program_id(1) == 0
        masks = _head_masks(D_A, A_HEAD_DIM, A_HEADS)
        qi = lax.broadcasted_iota(jnp.int32, (A_HEADS * QBLK, 2 * QBLK), 0) % QBLK
        kj = lax.broadcasted_iota(jnp.int32, (A_HEADS * QBLK, 2 * QBLK), 1)
        band = (kj >= qi) & (kj <= qi + QBLK)
        neg_band = jnp.where(band, 0.0, -jnp.inf)
        neg_first = jnp.where(first_step, jnp.where(band & (kj >= QBLK), 0.0, -jnp.inf), neg_band)

        def keys_values(r, jb, c):
            rows = slice(jb * QBLK, (jb + 1) * QBLK)
            if jb == 0:
                return jnp.concatenate([prev_ref[0, r, :, c * D_A:(c + 1) * D_A],
                                        cur_ref[0, r, rows, c * D_A:(c + 1) * D_A]], axis=0)
            return cur_ref[0, r, (jb - 1) * QBLK:(jb + 1) * QBLK, c * D_A:(c + 1) * D_A]

        for r, jb in [(r, jb) for r in range(dil) for jb in range(nblk)]:
            q = cur_ref[0, r, jb * QBLK:(jb + 1) * QBLK, 0:D_A]
            qs = _stack_heads(q * jnp.asarray(ATT_SCALE, BF16), masks)
            s = lax.dot_general(qs, keys_values(r, jb, 1), NT_DIMS, preferred_element_type=F32)
            s = s + (neg_first if jb == 0 else neg_band)
            mx = jnp.max(s, axis=-1, keepdims=True)
            p = jnp.exp(s - mx)
            l = jnp.sum(p, axis=-1, keepdims=True)
            o_all = jnp.dot(p.astype(BF16), keys_values(r, jb, 2), preferred_element_type=F32)
            l_e = _expand_cols([l[h * QBLK:(h + 1) * QBLK] for h in range(A_HEADS)], masks)
            mx_e = _expand_cols([mx[h * QBLK:(h + 1) * QBLK] for h in range(A_HEADS)], masks)
            res = (_unstack_heads(o_all, masks, QBLK) / l_e, mx_e + jnp.log(l_e))
            for k in range(OG_SLABS):
                src = res[k // 2][:, (k % 2) * LANES:(k % 2 + 1) * LANES]
                if dil == 1:
                    out_ref[0, k, jb * QBLK:(jb + 1) * QBLK, :] = src
                else:
                    out_ref[0, k, pl.ds(jb * QBLK * dil + r, QBLK, stride=dil), :] = src
    return body


def _dilated_call(qkv, dil, g):
    b, _, n, _ = qkv.shape
    rows = DIL_STEP // dil
    nb = n // rows
    return pl.pallas_call(
        _make_dilated_kernel(dil),
        out_shape=jax.ShapeDtypeStruct((b, OG_SLABS, n * dil, LANES), F32),
        grid=(b, nb),
        in_specs=[pl.BlockSpec((1, dil, QBLK, D_QKV),
                               lambda i, u: (i, 0, jnp.maximum(u * (rows // QBLK) - 1, 0), 0)),
                  pl.BlockSpec((1, dil, rows, D_QKV), lambda i, u: (i, 0, u, 0))],
        out_specs=pl.BlockSpec((1, OG_SLABS, DIL_STEP, LANES), lambda i, u: (i, 0, u, 0)),
        compiler_params=pltpu.CompilerParams(
            dimension_semantics=("arbitrary", "arbitrary"), vmem_limit_bytes=VMEM_LIMIT),
        name=f"dilated_attn_g{g}",
    )(qkv, qkv)


def _lower_bound(l0, l1):
    m = jnp.maximum(l0, l1)
    e0, e1 = jnp.exp(l0 - m), jnp.exp(l1 - m)
    return e0 / (e0 + e1)


def _head_rms(ob, gain):
    parts = []
    for h in range(HG_HEADS):
        sl = slice(h * HG_DIM, (h + 1) * HG_DIM)
        parts.append(_rms(ob[:, sl], gain[:, sl]))
    return jnp.concatenate(parts, axis=1)


def _merge_out(x, ua, ub, um, zg, wa_ref, wb_ref, wm_ref, wo_ref, nf):
    projs = [jnp.dot(u.astype(BF16), w_ref[...], preferred_element_type=F32)
             for u, w_ref in ((ua, wa_ref), (ub, wb_ref), (um, wm_ref))]
    gates = _sigmoid(zg.astype(BF16))
    merged = None
    for k, proj in enumerate(projs):
        term = gates[:, k * D_MODEL:(k + 1) * D_MODEL] * proj.astype(BF16)
        merged = term if merged is None else merged + term
    y = x + jnp.dot(merged, wo_ref[...], preferred_element_type=F32)
    return _rms(y, nf)


TAIL_TILE = 256


def _cumsum_rows(g, tri):
    g1 = g.astype(BF16)
    g2 = (g - g1.astype(F32)).astype(BF16)
    return jnp.dot(tri, g1, preferred_element_type=F32) + jnp.dot(tri, g2, preferred_element_type=F32)


def _hgrn_tile(fb, qb, v, lb, st_ref):
    t = fb.shape[0]
    c, sub = HG_CHUNK, HG_SUB
    n_sub = c // sub
    f = lb + (1.0 - lb) * _sigmoid(fb)
    kk = 1.0 - f
    row = lax.broadcasted_iota(jnp.int32, (t, t), 0)
    col = lax.broadcasted_iota(jnp.int32, (t, t), 1)
    same_chunk = (row // c) == (col // c)
    gcs = _cumsum_rows(jnp.log2(f), (same_chunk & (col <= row)).astype(BF16))
    q = _silu(qb).astype(F32)

    loc = lax.broadcasted_iota(jnp.int32, (c, 1), 0)
    qt_c, kt_ci, qe_c, kd_c, dec_c = [], [[] for _ in range(n_sub)], [], [], []
    for ci in range(t // c):
        rows = slice(ci * c, (ci + 1) * c)
        g_c, q_c, k_c = gcs[rows], q[rows], kk[rows]
        refs = [jnp.zeros((1, D_B), F32)] + [g_c[i * sub - 1:i * sub] for i in range(1, n_sub)]
        own = refs[n_sub - 1]
        for i in range(n_sub - 2, -1, -1):
            own = jnp.where(loc < (i + 1) * sub, refs[i], own)
        qt_c.append((q_c * jnp.exp2(g_c - own)).astype(BF16))
        k_own = k_c * jnp.exp2(own - g_c)
        for i in range(n_sub):
            pieces = []
            for jb in range(n_sub):
                piece = k_own[jb * sub:(jb + 1) * sub]
                if jb < i:
                    piece = piece * jnp.exp2(refs[i] - refs[jb])
                elif jb > i:
                    piece = jnp.zeros_like(piece)
                pieces.append(piece)
            kt_ci[i].append(jnp.concatenate(pieces, axis=0).astype(BF16))
        g_last = g_c[c - 1:c]
        qe_c.append((q_c * jnp.exp2(g_c)).astype(BF16))
        kd_c.append((k_c * jnp.exp2(g_last - g_c)).astype(BF16))
        dec_c.append(jnp.exp2(g_last))
    qt = jnp.concatenate(qt_c, axis=0)
    kts = [jnp.concatenate(parts, axis=0) for parts in kt_ci]
    sub_of_row = (lax.broadcasted_iota(jnp.int32, (t, 1), 0) % c) // sub
    causal = same_chunk & (col <= row)

    heads = [slice(h * HG_DIM, (h + 1) * HG_DIM) for h in range(HG_HEADS)]
    chunks = [slice(ci * c, (ci + 1) * c) for ci in range(t // c)]
    zero = jnp.zeros((t, HG_DIM), BF16)
    atts, incs = [], []
    for sl in heads:
        q_big = jnp.concatenate([jnp.where(sub_of_row == i, qt[:, sl], zero) for i in range(n_sub)], axis=1)
        k_big = jnp.concatenate([kts[i][:, sl] for i in range(n_sub)], axis=1)
        atts.append(lax.dot_general(q_big, k_big, NT_DIMS, preferred_element_type=F32))
        incs.append([lax.dot_general(v[rows, sl], kd_c[ci][:, sl], TN_DIMS, preferred_element_type=F32)
                     for ci, rows in enumerate(chunks)])
    states = []
    for h, sl in enumerate(heads):
        atts[h] = jnp.where(causal, atts[h], 0.0).astype(BF16)
        sts = [st_ref[h]]
        for ci in range(len(chunks)):
            sts.append(sts[-1] * dec_c[ci][:, sl] + incs[h][ci])
        st_ref[h] = sts[-1]
        states.append(sts)
    outs = []
    for h, sl in enumerate(heads):
        o_intra = jnp.dot(atts[h], v[:, sl], preferred_element_type=F32)
        inter = [lax.dot_general(qe_c[ci][:, sl], states[h][ci].astype(BF16), NT_DIMS, preferred_element_type=F32)
                 for ci in range(len(chunks))]
        outs.append(o_intra + jnp.concatenate(inter, axis=0))
    return jnp.concatenate(outs, axis=1)


def _prompt_tail_kernel(x_ref, gq_ref, hb_ref, zg_ref, fb_ref, o1_ref, o2_ref, o3_ref, mkv_ref, lbl_ref,
                        nh_ref, nf_ref, wa_ref, wb_ref, wm_ref, wo_ref,
                        zr_ref, c1_ref, c2_ref, c3_ref, cm_ref, sst_ref, lblt_ref, nh4_ref, ua_ref, nsta_ref,
                        y_ref, hst_ref, u_ref, nst_ref, st_ref):
    j = pl.program_id(1)
    t = TAIL_TILE

    @pl.when(j == 0)
    def _():
        st_ref[...] = jnp.zeros_like(st_ref)

    u_ref[0] = ua_ref[...]
    nst_ref[0] = nsta_ref[...]
    _sample_rows(zr_ref, c1_ref, c2_ref, c3_ref, cm_ref, sst_ref, lblt_ref, nh4_ref, u_ref.at[1], nst_ref.at[1])

    masks = _head_masks(D_M, A_HEAD_DIM, A_HEADS)
    qm = gq_ref[0, :, D_A:2 * D_A] * jnp.asarray(ATT_SCALE, BF16)
    qs = _stack_heads(qm, masks)
    mk = mkv_ref[0, :, 0:D_M].astype(BF16)
    mv = mkv_ref[0, :, D_M:2 * D_M].astype(BF16)
    s = lax.dot_general(qs, mk, NT_DIMS, preferred_element_type=F32)

    lb = _lower_bound(lbl_ref[0:1], lbl_ref[1:2])
    ob = _hgrn_tile(fb_ref[0], hb_ref[0, :, 0:D_B], hb_ref[0, :, D_B:2 * D_B], lb, st_ref)
    ub = _head_rms(ob, nh_ref[...]).astype(BF16) * _silu(hb_ref[0, :, 2 * D_B:3 * D_B])

    os_ = [jnp.concatenate([r[0, 0], r[0, 1]], axis=1) for r in (o1_ref, o2_ref, o3_ref)]
    ls_ = [jnp.concatenate([r[0, 2], r[0, 3]], axis=1) for r in (o1_ref, o2_ref, o3_ref)]
    mx = jnp.maximum(jnp.maximum(ls_[0], ls_[1]), ls_[2])
    es_ = [jnp.exp(l - mx) for l in ls_]
    oa = (es_[0] * os_[0] + es_[1] * os_[1] + es_[2] * os_[2]) / (es_[0] + es_[1] + es_[2])
    ua = oa.astype(BF16) * _silu(gq_ref[0, :, 0:D_A])

    p = jnp.exp(s - jnp.max(s, axis=-1, keepdims=True))
    l = jnp.sum(p, axis=-1, keepdims=True)
    om = _unstack_heads(jnp.dot(p.astype(BF16), mv, preferred_element_type=F32) / l, masks, t)
    um = om.astype(BF16) * _silu(gq_ref[0, :, 2 * D_A:3 * D_A])

    y_ref[0] = _merge_out(x_ref[0], ua, ub, um, zg_ref[0], wa_ref, wb_ref, wm_ref, wo_ref, nf_ref[...])

    @pl.when(j == pl.num_programs(1) - 1)
    def _():
        for h in range(HG_HEADS):
            hst_ref[0, h] = st_ref[h].T


def _prompt_tail_call(x, gq, hb, zg, fb, o1, o2, o3, mkv, lbl, nh, nf, wa, wb, wm, wo,
                      sample_ops, u_first, nst_first):
    b, s, _ = x.shape
    t = TAIL_TILE
    nt = s // t
    db = sample_ops[0].shape[0]
    half = u_first.shape[0]
    assert half + b * nt * S_TILE == db and 2 * half == db, "sample rows must split evenly over both grids"

    def tok(width):
        return pl.BlockSpec((1, t, width), lambda i, j: (i, j, 0))

    def const(shape):
        return pl.BlockSpec(shape, lambda i, j: (0,) * len(shape), pipeline_mode=pl.Buffered(1))

    def first(a):
        return pl.BlockSpec((S_TILE,) + a.shape[1:], lambda i, j: (i * nt + j, 0, 0))

    def halves(shape):
        return pl.BlockSpec((2, S_TILE) + shape, lambda i, j: (0, i * nt + j, 0, 0))

    y, hst, u, nst = pl.pallas_call(
        _prompt_tail_kernel,
        out_shape=(jax.ShapeDtypeStruct((b, s, D_MODEL), F32),
                   jax.ShapeDtypeStruct((b, HG_HEADS, HG_DIM, HG_DIM), F32),
                   jax.ShapeDtypeStruct((2, half, U_ROWS, LANES), F32),
                   jax.ShapeDtypeStruct((2, half, HG_HEADS * HG_DIM, HG_DIM), F32)),
        grid=(b, nt),
        in_specs=[tok(D_MODEL), tok(D_GQ), tok(D_HB), tok(D_ZG), tok(D_B),
                  *[pl.BlockSpec((1, OG_SLABS, t, LANES), lambda i, j: (i, 0, j, 0))] * N_GROUPS,
                  pl.BlockSpec((1, MEM_LEN, 2 * D_M), lambda i, j: (i, 0, 0)),
                  const((2, D_B)), const((1, D_B)), const((1, D_MODEL)),
                  const((D_A, D_MODEL)), const((D_B, D_MODEL)), const((D_M, D_MODEL)),
                  const((D_MODEL, D_MODEL)),
                  *_sample_specs(sample_ops, nt, half), first(u_first), first(nst_first)],
        out_specs=(tok(D_MODEL),
                   pl.BlockSpec((1, HG_HEADS, HG_DIM, HG_DIM), lambda i, j: (i, 0, 0, 0)),
                   halves((U_ROWS, LANES)), halves((HG_HEADS * HG_DIM, HG_DIM))),
        scratch_shapes=[pltpu.VMEM((HG_HEADS, HG_DIM, HG_DIM), F32)],
        compiler_params=pltpu.CompilerParams(
            dimension_semantics=("arbitrary", "arbitrary"), vmem_limit_bytes=VMEM_LIMIT),
        name="prompt_tail",
    )(x, gq, hb, zg, fb, o1, o2, o3, mkv, lbl, nh, nf, wa, wb, wm, wo, *sample_ops, u_first, nst_first)
    return y, hst, u.reshape(db, U_ROWS * LANES), nst.reshape(db, HG_HEADS * HG_DIM, HG_DIM)


S_COLS = 1024


def _sample_inproj_kernel(x_ref, gain_ref, w_ref, zr_ref, zg_ref, kvt_ref, wbf_ref):
    c = pl.program_id(0)
    h = _rms(x_ref[...], gain_ref[...]).astype(BF16)
    w = w_ref[...].astype(BF16)
    wbf_ref[...] = w
    z = jnp.dot(h, w, preferred_element_type=F32)
    for k in range(S_COLS // LANES):
        zr_ref[:, k, :] = z[:, k * LANES:(k + 1) * LANES]
    zg_ref[...] = z
    for step in range(OFF_GA // S_COLS + 1):
        pieces = [(g, kv, (OFF_KA + (kv * N_GROUPS + g) * D_A) % S_COLS)
                  for g in range(N_GROUPS) for kv in range(2)
                  if (OFF_KA + (kv * N_GROUPS + g) * D_A) // S_COLS == step]

        @pl.when(c == step)
        def _(pieces=pieces):
            zt = z.T
            for g, kv, row0 in pieces:
                kvt_ref[g, kv * D_A:(kv + 1) * D_A, :] = zt[row0:row0 + D_A]


def _sample_inproj_call(x, gain, w):
    db = x.shape[0]
    first_zg = OFF_ZG // S_COLS
    return pl.pallas_call(
        _sample_inproj_kernel,
        out_shape=(jax.ShapeDtypeStruct((db, Z_ROWS, LANES), F32),
                   jax.ShapeDtypeStruct((db, D_ZG), F32),
                   jax.ShapeDtypeStruct((N_GROUPS, 2 * D_A, db), F32),
                   jax.ShapeDtypeStruct((D_MODEL, D_IN), BF16)),
        grid=(D_IN // S_COLS,),
        in_specs=[pl.BlockSpec((db, D_MODEL), lambda c: (0, 0)),
                  pl.BlockSpec((1, D_MODEL), lambda c: (0, 0)),
                  pl.BlockSpec((D_MODEL, S_COLS), lambda c: (0, c))],
        out_specs=(pl.BlockSpec((db, S_COLS // LANES, LANES), lambda c: (0, c, 0)),
                   pl.BlockSpec((db, S_COLS), lambda c: (0, jnp.maximum(c - first_zg, 0))),
                   pl.BlockSpec((N_GROUPS, 2 * D_A, db), lambda c: (0, 0, 0)),
                   pl.BlockSpec((D_MODEL, S_COLS), lambda c: (0, c))),
        compiler_params=pltpu.CompilerParams(dimension_semantics=("arbitrary",)),
        name="sample_inproj",
    )(x, gain, w)


S_TILE = 1
Z_ROWS = D_IN // LANES
U_ROWS = 8


def _sample_rows(zr_ref, c1_ref, c2_ref, c3_ref, cm_ref, st_ref, lblt_ref, nh_ref, u_ref, nst_ref):
    bt = S_TILE
    pad = [jnp.zeros((LANES - bt * Z_ROWS, LANES), F32)] if bt * Z_ROWS < LANES else []
    zt = jnp.concatenate([zr_ref[b] for b in range(bt)] + pad, axis=0).T

    def col(b, off, n=HG_DIM):
        j, l = divmod(off, LANES)
        return zt[l:l + n, Z_ROWS * b + j:Z_ROWS * b + j + 1]

    lbt = lblt_ref[...]
    lb_col = _lower_bound(lbt[:, 0:1], lbt[:, 1:2])
    nh = nh_ref[...]

    def col2(b, off):
        return jnp.concatenate([col(b, off), col(b, off + LANES)], axis=0)

    def per_head(x):
        return jnp.concatenate([jnp.sum(x[h * A_HEAD_DIM:(h + 1) * A_HEAD_DIM], axis=0, keepdims=True)
                                for h in range(A_HEADS)], axis=0)

    def spread(x):
        return jnp.concatenate([jnp.broadcast_to(x[h:h + 1], (A_HEAD_DIM, 1)) for h in range(A_HEADS)], axis=0)

    def attend(b, srcs):
        parts, new_scores, qs = [], [], []
        for c_ref, dil, q_off, k_off, _ in srcs:
            q = col2(b, q_off) * ATT_SCALE
            qs.append(q)
            s = per_head(c_ref[b, 0:D_A, :] * q)
            if dil > 1:
                lane = lax.broadcasted_iota(jnp.int32, s.shape, 1)
                s = jnp.where(lane % dil == 0, s, -jnp.inf)
            parts.append(s)
            if k_off is not None:
                new_scores.append(per_head(col2(b, k_off) * q))
        if new_scores:
            lane = lax.broadcasted_iota(jnp.int32, (A_HEADS, LANES), 1)
            slab = jnp.full((A_HEADS, LANES), -jnp.inf, F32)
            for k, sn in enumerate(new_scores):
                slab = jnp.where(lane == k, sn, slab)
            parts.append(slab)
        s_all = jnp.concatenate(parts, axis=1)
        p_all = jnp.exp(s_all - jnp.max(s_all, axis=1, keepdims=True))
        den = jnp.sum(p_all, axis=1, keepdims=True)
        acc = [jnp.zeros((A_HEAD_DIM, LANES), F32) for _ in range(A_HEADS)]
        lo = 0
        for c_ref, _, _, _, _ in srcs:
            length = c_ref.shape[2]
            for k in range(length // LANES):
                ls = slice(k * LANES, (k + 1) * LANES)
                for h in range(A_HEADS):
                    vt = c_ref[b, D_A + h * A_HEAD_DIM:D_A + (h + 1) * A_HEAD_DIM, ls]
                    acc[h] = acc[h] + vt * p_all[h:h + 1, lo + k * LANES:lo + (k + 1) * LANES]
            lo += length
        num = jnp.sum(jnp.concatenate(acc, axis=0), axis=1, keepdims=True)
        k = 0
        for _, _, _, k_off, v_off in srcs:
            if k_off is not None:
                num = num + spread(p_all[:, lo + k:lo + k + 1]) * col2(b, v_off)
                k += 1
        return num / spread(den)

    window = [(c_ref, dil, OFF_QA + g * D_A, OFF_KA + g * D_A, OFF_VA + g * D_A)
              for g, (c_ref, (_, dil)) in enumerate(zip((c1_ref, c2_ref, c3_ref), WIN_GROUPS))]
    memory = [(cm_ref, 1, OFF_QM, None, None)]

    for b in range(bt):
        ua = attend(b, window) * _silu(col2(b, OFF_GA))
        um = attend(b, memory) * _silu(col2(b, OFF_GM))
        branch_cols = [ua[0:LANES], ua[LANES:2 * LANES], um[0:LANES], um[LANES:2 * LANES]]

        lane = lax.broadcasted_iota(jnp.int32, (LANES, LANES), 1)
        ut = jnp.zeros((LANES, LANES), F32)
        for k, cvec in enumerate(branch_cols):
            ut = ut + jnp.where(lane == k, cvec, 0.0)
        u_ref[b, 0:4, :] = ut.T[0:4, :]

        orows = []
        for h in range(HG_HEADS):
            sl = slice(h * HG_DIM, (h + 1) * HG_DIM)
            fcol = lb_col[sl] + (1.0 - lb_col[sl]) * _sigmoid(col(b, OFF_FB + h * HG_DIM))
            qcol = _silu(col(b, OFF_QB + h * HG_DIM))
            vrow = zr_ref[b, OFF_IB // LANES + h:OFF_IB // LANES + h + 1, :]
            sn = fcol * st_ref[b, sl, :] + (1.0 - fcol) * vrow
            nst_ref[b, sl, :] = sn
            orows.append(jnp.sum(sn * qcol, axis=0, keepdims=True))
        gate = zr_ref[b, OFF_GB // LANES:OFF_GB // LANES + HG_HEADS, :]
        u_ref[b, 4:4 + HG_HEADS, :] = _rms(jnp.concatenate(orows, axis=0), nh) * _silu(gate)


def _stored_view(cache):
    db, length = cache.shape[0], cache.shape[1]
    return jnp.transpose(cache, (0, 2, 3, 4, 1)).reshape(db, 2 * D_A, length)


def _sample_out_kernel(x_ref, u_ref, zg_ref, nf_ref, wa_ref, wb_ref, wm_ref, wo_ref, y_ref):
    u = u_ref[...]
    y_ref[...] = _merge_out(x_ref[...], u[:, 0:D_A], u[:, D_A + D_M:], u[:, D_A:D_A + D_M], zg_ref[...],
                            wa_ref, wb_ref, wm_ref, wo_ref, nf_ref[...])


def _sample_out_call(x, u, zg, nf, wa, wb, wm, wo):
    db = x.shape[0]

    def full(a):
        return pl.BlockSpec(a.shape, lambda i: (0,) * a.ndim)

    args = (x, u, zg, nf, wa, wb, wm, wo)
    return pl.pallas_call(
        _sample_out_kernel,
        out_shape=jax.ShapeDtypeStruct((db, D_MODEL), F32),
        grid=(1,),
        in_specs=[full(a) for a in args],
        out_specs=pl.BlockSpec((db, D_MODEL), lambda i: (0, 0)),
        compiler_params=pltpu.CompilerParams(vmem_limit_bytes=VMEM_LIMIT),
        name="sample_out",
    )(*args)


def _cache_from_stored(kvt):
    b, _, length = kvt.shape
    return jnp.transpose(kvt.reshape(b, 2, A_HEADS, A_HEAD_DIM, length), (0, 4, 1, 2, 3))[None]


def kernel(x_prompt, x_sample, mem_prompt, cache_w1_kv, cache_w2_kv, cache_w3_kv, cache_mem_kv, state_hgrn,
           norm_in, w_in, lb_logits, norm_hgrn, norm_mem, w_mem_kv, w_branch_a, w_branch_b, w_branch_m,
           w_out, norm_final):
    b, s, _ = x_prompt.shape
    db = x_sample.shape[0]
    wa, wb, wm, wo = (w[0].astype(BF16) for w in (w_branch_a, w_branch_b, w_branch_m, w_out))
    gain_in = norm_in[0][None]
    nh = norm_hgrn[0][None]
    nf = norm_final[None]
    lbl = lb_logits.astype(F32)

    xs = x_sample.reshape(db, D_MODEL)
    zr, zg_s, kvt_s, w_in_bf = _sample_inproj_call(xs, gain_in, w_in[0])
    mkv, mkv_t = _mem_kv_call(mem_prompt, norm_mem[0][None], w_mem_kv[0].astype(BF16))
    sample_ops = (zr,
                  *[_stored_view(c[0]) for c in (cache_w1_kv, cache_w2_kv, cache_w3_kv, cache_mem_kv)],
                  state_hgrn[0].reshape(db, HG_HEADS * HG_DIM, HG_DIM), lbl.T, nh.reshape(HG_HEADS, HG_DIM))
    qkv0, qkv1, qkv2, gq, hb, zg, fb, kvt0, kvt1, kvt2, u_first, nst_first = _inproj_call(
        x_prompt, gain_in, w_in_bf, sample_ops)
    og = [_dilated_call(qkv, dil, g) for g, (qkv, (_, dil)) in enumerate(zip((qkv0, qkv1, qkv2), WIN_GROUPS))]
    y_prompt, hg_prompt, u, nst = _prompt_tail_call(
        x_prompt, gq, hb, zg, fb, og[0], og[1], og[2], mkv, lbl, nh, nf, wa, wb, wm, wo,
        sample_ops, u_first, nst_first)
    pw = [_cache_from_stored(kvt) for kvt in (kvt0, kvt1, kvt2)]
    new_mem = _cache_from_stored(mkv_t)

    y_sample = _sample_out_call(xs, u, zg_s, nf, wa, wb, wm, wo).reshape(db, 1, D_MODEL)
    sw = [jnp.transpose(kvt_s[g].reshape(2, A_HEADS, A_HEAD_DIM, db), (3, 0, 1, 2))[None, :, None]
          for g in range(N_GROUPS)]
    new_hg_sample = nst.reshape(1, db, HG_HEADS, HG_DIM, HG_DIM)

    return (y_prompt, y_sample, pw[0], pw[1], pw[2], new_mem, hg_prompt[None],
            sw[0], sw[1], sw[2], new_hg_sample)
```

```python
import jax
import jax.numpy as jnp
from jax import lax
from jax.experimental import pallas as pl
from jax.experimental.pallas import tpu as pltpu

F32 = jnp.float32
BF16 = jnp.bfloat16

D_MODEL = 1024
WIN_GROUPS = ((128, 1), (512, 4), (2048, 16))
N_GROUPS = 3
A_HEADS = 4
A_HEAD_DIM = 64
D_A = 256
QBLK = 128
HG_HEADS = 4
HG_DIM = 128
D_B = 512
HG_CHUNK = 64
HG_SUB = 16
MEM_LEN = 256
D_M = 256
EPS = 1e-6
D_IN = 8192
LANES = 128
OFF_QA, OFF_KA, OFF_VA, OFF_GA = 0, 768, 1536, 2304
OFF_QB, OFF_FB, OFF_IB, OFF_GB = 2560, 3072, 3584, 4096
OFF_QM, OFF_GM, OFF_ZG = 4608, 4864, 5120
ATT_SCALE = A_HEAD_DIM ** -0.5

VMEM_LIMIT = 56 * 1024 * 1024

NT_DIMS = (((1,), (1,)), ((), ()))
TN_DIMS = (((0,), (0,)), ((), ()))


def _sigmoid(x):
    return 0.5 * jnp.tanh(0.5 * x) + 0.5


def _silu(x):
    h = 0.5 * x
    return h * jnp.tanh(h) + h


def _rms(x, gain):
    return x * lax.rsqrt(jnp.mean(x * x, axis=-1, keepdims=True) + EPS) * gain


def _head_masks(width, head_dim, n_heads):
    lane = lax.broadcasted_iota(jnp.int32, (1, width), 1)
    return [(lane >= h * head_dim) & (lane < (h + 1) * head_dim) for h in range(n_heads)]


def _stack_heads(q, masks):
    zero = jnp.zeros_like(q)
    return jnp.concatenate([jnp.where(m, q, zero) for m in masks], axis=0)


def _unstack_heads(o_all, masks, t):
    n = len(masks)
    out = o_all[(n - 1) * t:n * t]
    for h in range(n - 2, -1, -1):
        out = jnp.where(masks[h], o_all[h * t:(h + 1) * t], out)
    return out


def _expand_cols(cols, masks):
    out = jnp.broadcast_to(cols[-1], (cols[-1].shape[0], masks[0].shape[1]))
    for h in range(len(masks) - 2, -1, -1):
        out = jnp.where(masks[h], cols[h], out)
    return out


def _mem_kv_kernel(mem_ref, gain_ref, w_ref, out_ref, out_t_ref):
    hm = _rms(mem_ref[0], gain_ref[...]).astype(BF16)
    kv = jnp.dot(hm, w_ref[...], preferred_element_type=F32)
    out_ref[0] = kv
    out_t_ref[0] = kv.T


def _mem_kv_call(mem, gain, w_bf):
    b = mem.shape[0]
    return pl.pallas_call(
        _mem_kv_kernel,
        out_shape=(jax.ShapeDtypeStruct((b, MEM_LEN, 2 * D_M), F32),
                   jax.ShapeDtypeStruct((b, 2 * D_M, MEM_LEN), F32)),
        grid=(b,),
        in_specs=[pl.BlockSpec((1, MEM_LEN, D_MODEL), lambda i: (i, 0, 0)),
                  pl.BlockSpec((1, D_MODEL), lambda i: (0, 0)),
                  pl.BlockSpec((D_MODEL, 2 * D_M), lambda i: (0, 0))],
        out_specs=(pl.BlockSpec((1, MEM_LEN, 2 * D_M), lambda i: (i, 0, 0)),
                   pl.BlockSpec((1, 2 * D_M, MEM_LEN), lambda i: (i, 0, 0))),
        name="mem_kv",
    )(mem, gain, w_bf)


IN_TILE = 256
IN_CHUNK = 256
D_QKV = 3 * D_A
D_GQ = 3 * D_A
D_HB = 3 * D_B
D_ZG = 3 * D_MODEL


def _inproj_kernel(x_ref, gain_ref, w_ref, zr_ref, c1_ref, c2_ref, c3_ref, cm_ref, sst_ref, lblt_ref, nh4_ref,
                   qkv0_ref, qkv1_ref, qkv2_ref, gq_ref, hb_ref, zg_ref, fb_ref, kvt0_ref, kvt1_ref, kvt2_ref,
                   u_ref, nst_ref, scr_ref):
    qkv_refs = (qkv0_ref, qkv1_ref, qkv2_ref)
    kvt_refs = (kvt0_ref, kvt1_ref, kvt2_ref)
    t = IN_TILE
    h = _rms(x_ref[0], gain_ref[...]).astype(BF16)
    for c in range(D_IN // IN_CHUNK):
        lo = c * IN_CHUNK
        zc = jnp.dot(h, w_ref[:, lo:lo + IN_CHUNK], preferred_element_type=F32)
        if lo < OFF_GA:
            sec, g = divmod(c, N_GROUPS)
            dil = WIN_GROUPS[g][1]
            dst = slice(sec * D_A, (sec + 1) * D_A)
            if dil == 1:
                qkv_refs[g][0, 0, :, dst] = zc.astype(BF16)
            else:
                for k in range(IN_CHUNK // LANES):
                    scr_ref[k] = zc[:, k * LANES:(k + 1) * LANES]
                for r in range(dil):
                    qkv_refs[g][0, r, :, dst] = jnp.concatenate(
                        [scr_ref[k, pl.ds(r, t // dil, stride=dil), :] for k in range(IN_CHUNK // LANES)],
                        axis=1).astype(BF16)
            if sec > 0:
                width = kvt_refs[g].shape[2]
                kvt_refs[g][0, (sec - 1) * D_A:sec * D_A, :] = zc.T[:, t - width:]
        elif lo < OFF_QB:
            gq_ref[0, :, 0:D_A] = zc.astype(BF16)
        elif lo < OFF_FB:
            hb_ref[0, :, lo - OFF_QB:lo - OFF_QB + IN_CHUNK] = zc.astype(BF16)
        elif lo < OFF_IB:
            fb_ref[0, :, lo - OFF_FB:lo - OFF_FB + IN_CHUNK] = zc
        elif lo < OFF_QM:
            hb_ref[0, :, lo - OFF_IB + D_B:lo - OFF_IB + D_B + IN_CHUNK] = zc.astype(BF16)
        elif lo < OFF_ZG:
            gq_ref[0, :, lo - OFF_QM + D_A:lo - OFF_QM + D_A + IN_CHUNK] = zc.astype(BF16)
        else:
            zg_ref[0, :, lo - OFF_ZG:lo - OFF_ZG + IN_CHUNK] = zc.astype(BF16)
    _sample_rows(zr_ref, c1_ref, c2_ref, c3_ref, cm_ref, sst_ref, lblt_ref, nh4_ref, u_ref, nst_ref)


def _sample_specs(ops, nt, first_row):
    first_blk = first_row // S_TILE

    def rows(a):
        return pl.BlockSpec((S_TILE,) + a.shape[1:], lambda i, j: (first_blk + i * nt + j, 0, 0))

    def const(a):
        return pl.BlockSpec(a.shape, lambda i, j: (0, 0), pipeline_mode=pl.Buffered(1))

    return [rows(a) for a in ops[:6]] + [const(a) for a in ops[6:]]


def _inproj_call(x, gain, w_bf, sample_ops):
    b, s, _ = x.shape
    t = IN_TILE
    nt = s // t
    n_rows = b * nt * S_TILE

    def tok(width):
        return pl.BlockSpec((1, t, width), lambda i, j: (i, j, 0))

    def srow(shape):
        return pl.BlockSpec((S_TILE,) + shape, lambda i, j: (i * nt + j, 0, 0))

    qkv_shapes, qkv_specs = [], []
    for _, dil in WIN_GROUPS:
        qkv_shapes.append(jax.ShapeDtypeStruct((b, dil, s // dil, D_QKV), BF16))
        qkv_specs.append(pl.BlockSpec((1, dil, t // dil, D_QKV), lambda i, j: (i, 0, j, 0)))
    kvt_shapes, kvt_specs = [], []
    for window, _ in WIN_GROUPS:
        length = min(window, s)
        width = min(t, length)
        first = nt - length // width
        kvt_shapes.append(jax.ShapeDtypeStruct((b, 2 * D_A, length), F32))
        kvt_specs.append(pl.BlockSpec((1, 2 * D_A, width),
                                      lambda i, j, first=first: (i, 0, jnp.maximum(j - first, 0))))
    return pl.pallas_call(
        _inproj_kernel,
        out_shape=(*qkv_shapes,
                   jax.ShapeDtypeStruct((b, s, D_GQ), BF16),
                   jax.ShapeDtypeStruct((b, s, D_HB), BF16),
                   jax.ShapeDtypeStruct((b, s, D_ZG), BF16),
                   jax.ShapeDtypeStruct((b, s, D_B), F32),
                   *kvt_shapes,
                   jax.ShapeDtypeStruct((n_rows, U_ROWS, LANES), F32),
                   jax.ShapeDtypeStruct((n_rows, HG_HEADS * HG_DIM, HG_DIM), F32)),
        grid=(b, nt),
        in_specs=[tok(D_MODEL),
                  pl.BlockSpec((1, D_MODEL), lambda i, j: (0, 0)),
                  pl.BlockSpec((D_MODEL, D_IN), lambda i, j: (0, 0), pipeline_mode=pl.Buffered(1)),
                  *_sample_specs(sample_ops, nt, 0)],
        out_specs=(*qkv_specs, tok(D_GQ), tok(D_HB), tok(D_ZG), tok(D_B), *kvt_specs,
                   srow((U_ROWS, LANES)), srow((HG_HEADS * HG_DIM, HG_DIM))),
        scratch_shapes=[pltpu.VMEM((IN_CHUNK // LANES, t, LANES), F32)],
        compiler_params=pltpu.CompilerParams(
            dimension_semantics=("arbitrary", "arbitrary"), vmem_limit_bytes=VMEM_LIMIT),
        name="inproj",
    )(x, gain, w_bf, *sample_ops)


DIL_STEP = 2048
OG_SLABS = 2 * D_A // LANES


def _make_dilated_kernel(dil):
    nblk = DIL_STEP // (dil * QBLK)

    def body(prev_ref, cur_ref, out_ref):
        first_step = pl.program_id(1) == 0
        masks = _head_masks(D_A, A_HEAD_DIM, A_HEADS)
        qi = lax.broadcasted_iota(jnp.int32, (A_HEADS * QBLK, 2 * QBLK), 0) % QBLK
        kj = lax.broadcasted_iota(jnp.int32, (A_HEADS * QBLK, 2 * QBLK), 1)
        band = (kj >= qi) & (kj <= qi + QBLK)
        neg_band = jnp.where(band, 0.0, -jnp.inf)
        neg_first = jnp.where(first_step, jnp.where(band & (kj >= QBLK), 0.0, -jnp.inf), neg_band)

        def keys_values(r, jb, c):
            rows = slice(jb * QBLK, (jb + 1) * QBLK)
            if jb == 0:
                return jnp.concatenate([prev_ref[0, r, :, c * D_A:(c + 1) * D_A],
                                        cur_ref[0, r, rows, c * D_A:(c + 1) * D_A]], axis=0)
            return cur_ref[0, r, (jb - 1) * QBLK:(jb + 1) * QBLK, c * D_A:(c + 1) * D_A]

        for r, jb in [(r, jb) for r in range(dil) for jb in range(nblk)]:
            q = cur_ref[0, r, jb * QBLK:(jb + 1) * QBLK, 0:D_A]
            qs = _stack_heads(q * jnp.asarray(ATT_SCALE, BF16), masks)
            s = lax.dot_general(qs, keys_values(r, jb, 1), NT_DIMS, preferred_element_type=F32)
            s = s + (neg_first if jb == 0 else neg_band)
            mx = jnp.max(s, axis=-1, keepdims=True)
            p = jnp.exp(s - mx)
            l = jnp.sum(p, axis=-1, keepdims=True)
            o_all = jnp.dot(p.astype(BF16), keys_values(r, jb, 2), preferred_element_type=F32)
            l_e = _expand_cols([l[h * QBLK:(h + 1) * QBLK] for h in range(A_HEADS)], masks)
            mx_e = _expand_cols([mx[h * QBLK:(h + 1) * QBLK] for h in range(A_HEADS)], masks)
            res = (_unstack_heads(o_all, masks, QBLK) / l_e, mx_e + jnp.log(l_e))
            for k in range(OG_SLABS):
                src = res[k // 2][:, (k % 2) * LANES:(k % 2 + 1) * LANES]
                if dil == 1:
                    out_ref[0, k, jb * QBLK:(jb + 1) * QBLK, :] = src
                else:
                    out_ref[0, k, pl.ds(jb * QBLK * dil + r, QBLK, stride=dil), :] = src
    return body


def _dilated_call(qkv, dil, g):
    b, _, n, _ = qkv.shape
    rows = DIL_STEP // dil
    nb = n // rows
    return pl.pallas_call(
        _make_dilated_kernel(dil),
        out_shape=jax.ShapeDtypeStruct((b, OG_SLABS, n * dil, LANES), F32),
        grid=(b, nb),
        in_specs=[pl.BlockSpec((1, dil, QBLK, D_QKV),
                               lambda i, u: (i, 0, jnp.maximum(u * (rows // QBLK) - 1, 0), 0)),
                  pl.BlockSpec((1, dil, rows, D_QKV), lambda i, u: (i, 0, u, 0))],
        out_specs=pl.BlockSpec((1, OG_SLABS, DIL_STEP, LANES), lambda i, u: (i, 0, u, 0)),
        compiler_params=pltpu.CompilerParams(
            dimension_semantics=("arbitrary", "arbitrary"), vmem_limit_bytes=VMEM_LIMIT),
        name=f"dilated_attn_g{g}",
    )(qkv, qkv)


def _lower_bound(l0, l1):
    m = jnp.maximum(l0, l1)
    e0, e1 = jnp.exp(l0 - m), jnp.exp(l1 - m)
    return e0 / (e0 + e1)


def _head_rms(ob, gain):
    parts = []
    for h in range(HG_HEADS):
        sl = slice(h * HG_DIM, (h + 1) * HG_DIM)
        parts.append(_rms(ob[:, sl], gain[:, sl]))
    return jnp.concatenate(parts, axis=1)


def _merge_out(x, ua, ub, um, zg, wa_ref, wb_ref, wm_ref, wo_ref, nf):
    projs = [jnp.dot(u.astype(BF16), w_ref[...], preferred_element_type=F32)
             for u, w_ref in ((ua, wa_ref), (ub, wb_ref), (um, wm_ref))]
    gates = _sigmoid(zg.astype(BF16))
    merged = None
    for k, proj in enumerate(projs):
        term = gates[:, k * D_MODEL:(k + 1) * D_MODEL] * proj.astype(BF16)
        merged = term if merged is None else merged + term
    y = x + jnp.dot(merged, wo_ref[...], preferred_element_type=F32)
    return _rms(y, nf)


TAIL_TILE = 256


def _cumsum_rows(g, tri):
    g1 = g.astype(BF16)
    g2 = (g - g1.astype(F32)).astype(BF16)
    return jnp.dot(tri, g1, preferred_element_type=F32) + jnp.dot(tri, g2, preferred_element_type=F32)


def _hgrn_tile(fb, qb, v, lb, st_ref):
    t = fb.shape[0]
    c, sub = HG_CHUNK, HG_SUB
    n_sub = c // sub
    f = lb + (1.0 - lb) * _sigmoid(fb)
    kk = 1.0 - f
    row = lax.broadcasted_iota(jnp.int32, (t, t), 0)
    col = lax.broadcasted_iota(jnp.int32, (t, t), 1)
    same_chunk = (row // c) == (col // c)
    gcs = _cumsum_rows(jnp.log2(f), (same_chunk & (col <= row)).astype(BF16))
    q = _silu(qb).astype(F32)

    loc = lax.broadcasted_iota(jnp.int32, (c, 1), 0)
    qt_c, kt_ci, qe_c, kd_c, dec_c = [], [[] for _ in range(n_sub)], [], [], []
    for ci in range(t // c):
        rows = slice(ci * c, (ci + 1) * c)
        g_c, q_c, k_c = gcs[rows], q[rows], kk[rows]
        refs = [jnp.zeros((1, D_B), F32)] + [g_c[i * sub - 1:i * sub] for i in range(1, n_sub)]
        own = refs[n_sub - 1]
        for i in range(n_sub - 2, -1, -1):
            own = jnp.where(loc < (i + 1) * sub, refs[i], own)
        qt_c.append((q_c * jnp.exp2(g_c - own)).astype(BF16))
        k_own = k_c * jnp.exp2(own - g_c)
        for i in range(n_sub):
            pieces = []
            for jb in range(n_sub):
                piece = k_own[jb * sub:(jb + 1) * sub]
                if jb < i:
                    piece = piece * jnp.exp2(refs[i] - refs[jb])
                elif jb > i:
                    piece = jnp.zeros_like(piece)
                pieces.append(piece)
            kt_ci[i].append(jnp.concatenate(pieces, axis=0).astype(BF16))
        g_last = g_c[c - 1:c]
        qe_c.append((q_c * jnp.exp2(g_c)).astype(BF16))
        kd_c.append((k_c * jnp.exp2(g_last - g_c)).astype(BF16))
        dec_c.append(jnp.exp2(g_last))
    qt = jnp.concatenate(qt_c, axis=0)
    kts = [jnp.concatenate(parts, axis=0) for parts in kt_ci]
    sub_of_row = (lax.broadcasted_iota(jnp.int32, (t, 1), 0) % c) // sub
    causal = same_chunk & (col <= row)

    heads = [slice(h * HG_DIM, (h + 1) * HG_DIM) for h in range(HG_HEADS)]
    chunks = [slice(ci * c, (ci + 1) * c) for ci in range(t // c)]
    zero = jnp.zeros((t, HG_DIM), BF16)
    atts, incs = [], []
    for sl in heads:
        q_big = jnp.concatenate([jnp.where(sub_of_row == i, qt[:, sl], zero) for i in range(n_sub)], axis=1)
        k_big = jnp.concatenate([kts[i][:, sl] for i in range(n_sub)], axis=1)
        atts.append(lax.dot_general(q_big, k_big, NT_DIMS, preferred_element_type=F32))
        incs.append([lax.dot_general(v[rows, sl], kd_c[ci][:, sl], TN_DIMS, preferred_element_type=F32)
                     for ci, rows in enumerate(chunks)])
    states = []
    for h, sl in enumerate(heads):
        atts[h] = jnp.where(causal, atts[h], 0.0).astype(BF16)
        sts = [st_ref[h]]
        for ci in range(len(chunks)):
            sts.append(sts[-1] * dec_c[ci][:, sl] + incs[h][ci])
        st_ref[h] = sts[-1]
        states.append(sts)
    outs = []
    for h, sl in enumerate(heads):
        o_intra = jnp.dot(atts[h], v[:, sl], preferred_element_type=F32)
        inter = [lax.dot_general(qe_c[ci][:, sl], states[h][ci].astype(BF16), NT_DIMS, preferred_element_type=F32)
                 for ci in range(len(chunks))]
        outs.append(o_intra + jnp.concatenate(inter, axis=0))
    return jnp.concatenate(outs, axis=1)


def _prompt_tail_kernel(x_ref, gq_ref, hb_ref, zg_ref, fb_ref, o1_ref, o2_ref, o3_ref, mkv_ref, lbl_ref,
                        nh_ref, nf_ref, wa_ref, wb_ref, wm_ref, wo_ref,
                        zr_ref, c1_ref, c2_ref, c3_ref, cm_ref, sst_ref, lblt_ref, nh4_ref, ua_ref, nsta_ref,
                        y_ref, hst_ref, u_ref, nst_ref, st_ref):
    j = pl.program_id(1)
    t = TAIL_TILE

    @pl.when(j == 0)
    def _():
        st_ref[...] = jnp.zeros_like(st_ref)

    u_ref[0] = ua_ref[...]
    nst_ref[0] = nsta_ref[...]
    _sample_rows(zr_ref, c1_ref, c2_ref, c3_ref, cm_ref, sst_ref, lblt_ref, nh4_ref, u_ref.at[1], nst_ref.at[1])

    masks = _head_masks(D_M, A_HEAD_DIM, A_HEADS)
    qm = gq_ref[0, :, D_A:2 * D_A] * jnp.asarray(ATT_SCALE, BF16)
    qs = _stack_heads(qm, masks)
    mk = mkv_ref[0, :, 0:D_M].astype(BF16)
    mv = mkv_ref[0, :, D_M:2 * D_M].astype(BF16)
    s = lax.dot_general(qs, mk, NT_DIMS, preferred_element_type=F32)

    lb = _lower_bound(lbl_ref[0:1], lbl_ref[1:2])
    ob = _hgrn_tile(fb_ref[0], hb_ref[0, :, 0:D_B], hb_ref[0, :, D_B:2 * D_B], lb, st_ref)
    ub = _head_rms(ob, nh_ref[...]).astype(BF16) * _silu(hb_ref[0, :, 2 * D_B:3 * D_B])

    os_ = [jnp.concatenate([r[0, 0], r[0, 1]], axis=1) for r in (o1_ref, o2_ref, o3_ref)]
    ls_ = [jnp.concatenate([r[0, 2], r[0, 3]], axis=1) for r in (o1_ref, o2_ref, o3_ref)]
    mx = jnp.maximum(jnp.maximum(ls_[0], ls_[1]), ls_[2])
    es_ = [jnp.exp(l - mx) for l in ls_]
    oa = (es_[0] * os_[0] + es_[1] * os_[1] + es_[2] * os_[2]) / (es_[0] + es_[1] + es_[2])
    ua = oa.astype(BF16) * _silu(gq_ref[0, :, 0:D_A])

    p = jnp.exp(s - jnp.max(s, axis=-1, keepdims=True))
    l = jnp.sum(p, axis=-1, keepdims=True)
    om = _unstack_heads(jnp.dot(p.astype(BF16), mv, preferred_element_type=F32) / l, masks, t)
    um = om.astype(BF16) * _silu(gq_ref[0, :, 2 * D_A:3 * D_A])

    y_ref[0] = _merge_out(x_ref[0], ua, ub, um, zg_ref[0], wa_ref, wb_ref, wm_ref, wo_ref, nf_ref[...])

    @pl.when(j == pl.num_programs(1) - 1)
    def _():
        for h in range(HG_HEADS):
            hst_ref[0, h] = st_ref[h].T


def _prompt_tail_call(x, gq, hb, zg, fb, o1, o2, o3, mkv, lbl, nh, nf, wa, wb, wm, wo,
                      sample_ops, u_first, nst_first):
    b, s, _ = x.shape
    t = TAIL_TILE
    nt = s // t
    db = sample_ops[0].shape[0]
    half = u_first.shape[0]
    assert half + b * nt * S_TILE == db and 2 * half == db, "sample rows must split evenly over both grids"

    def tok(width):
        return pl.BlockSpec((1, t, width), lambda i, j: (i, j, 0))

    def const(shape):
        return pl.BlockSpec(shape, lambda i, j: (0,) * len(shape), pipeline_mode=pl.Buffered(1))

    def first(a):
        return pl.BlockSpec((S_TILE,) + a.shape[1:], lambda i, j: (i * nt + j, 0, 0))

    def halves(shape):
        return pl.BlockSpec((2, S_TILE) + shape, lambda i, j: (0, i * nt + j, 0, 0))

    y, hst, u, nst = pl.pallas_call(
        _prompt_tail_kernel,
        out_shape=(jax.ShapeDtypeStruct((b, s, D_MODEL), F32),
                   jax.ShapeDtypeStruct((b, HG_HEADS, HG_DIM, HG_DIM), F32),
                   jax.ShapeDtypeStruct((2, half, U_ROWS, LANES), F32),
                   jax.ShapeDtypeStruct((2, half, HG_HEADS * HG_DIM, HG_DIM), F32)),
        grid=(b, nt),
        in_specs=[tok(D_MODEL), tok(D_GQ), tok(D_HB), tok(D_ZG), tok(D_B),
                  *[pl.BlockSpec((1, OG_SLABS, t, LANES), lambda i, j: (i, 0, j, 0))] * N_GROUPS,
                  pl.BlockSpec((1, MEM_LEN, 2 * D_M), lambda i, j: (i, 0, 0)),
                  const((2, D_B)), const((1, D_B)), const((1, D_MODEL)),
                  const((D_A, D_MODEL)), const((D_B, D_MODEL)), const((D_M, D_MODEL)),
                  const((D_MODEL, D_MODEL)),
                  *_sample_specs(sample_ops, nt, half), first(u_first), first(nst_first)],
        out_specs=(tok(D_MODEL),
                   pl.BlockSpec((1, HG_HEADS, HG_DIM, HG_DIM), lambda i, j: (i, 0, 0, 0)),
                   halves((U_ROWS, LANES)), halves((HG_HEADS * HG_DIM, HG_DIM))),
        scratch_shapes=[pltpu.VMEM((HG_HEADS, HG_DIM, HG_DIM), F32)],
        compiler_params=pltpu.CompilerParams(
            dimension_semantics=("arbitrary", "arbitrary"), vmem_limit_bytes=VMEM_LIMIT),
        name="prompt_tail",
    )(x, gq, hb, zg, fb, o1, o2, o3, mkv, lbl, nh, nf, wa, wb, wm, wo, *sample_ops, u_first, nst_first)
    return y, hst, u.reshape(db, U_ROWS * LANES), nst.reshape(db, HG_HEADS * HG_DIM, HG_DIM)


S_COLS = 1024


def _sample_inproj_kernel(x_ref, gain_ref, w_ref, *refs):
    n_small = (len(refs) - 4) // 2
    small_in, (zr_ref, zg_ref, kvt_ref, wbf_ref), small_out = refs[:n_small], refs[n_small:n_small + 4], refs[n_small + 4:]
    for src, dst in zip(small_in, small_out):
        dst[...] = src[...].astype(BF16)
    c = pl.program_id(0)
    h = _rms(x_ref[...], gain_ref[...]).astype(BF16)
    w = w_ref[...].astype(BF16)
    wbf_ref[...] = w
    z = jnp.dot(h, w, preferred_element_type=F32)
    for k in range(S_COLS // LANES):
        zr_ref[:, k, :] = z[:, k * LANES:(k + 1) * LANES]
    zg_ref[...] = z
    for step in range(OFF_GA // S_COLS + 1):
        pieces = [(g, kv, (OFF_KA + (kv * N_GROUPS + g) * D_A) % S_COLS)
                  for g in range(N_GROUPS) for kv in range(2)
                  if (OFF_KA + (kv * N_GROUPS + g) * D_A) // S_COLS == step]

        @pl.when(c == step)
        def _(pieces=pieces):
            zt = z.T
            for g, kv, row0 in pieces:
                kvt_ref[g, kv * D_A:(kv + 1) * D_A, :] = zt[row0:row0 + D_A]


def _sample_inproj_call(x, gain, w, small_weights):
    db = x.shape[0]
    steps = D_IN // S_COLS
    first_zg = OFF_ZG // S_COLS
    small_specs = [pl.BlockSpec((a.shape[0] // steps, a.shape[1]), lambda c: (c, 0)) for a in small_weights]
    return pl.pallas_call(
        _sample_inproj_kernel,
        out_shape=(jax.ShapeDtypeStruct((db, Z_ROWS, LANES), F32),
                   jax.ShapeDtypeStruct((db, D_ZG), F32),
                   jax.ShapeDtypeStruct((N_GROUPS, 2 * D_A, db), F32),
                   jax.ShapeDtypeStruct((D_MODEL, D_IN), BF16),
                   *[jax.ShapeDtypeStruct(a.shape, BF16) for a in small_weights]),
        grid=(steps,),
        in_specs=[pl.BlockSpec((db, D_MODEL), lambda c: (0, 0)),
                  pl.BlockSpec((1, D_MODEL), lambda c: (0, 0)),
                  pl.BlockSpec((D_MODEL, S_COLS), lambda c: (0, c)),
                  *small_specs],
        out_specs=(pl.BlockSpec((db, S_COLS // LANES, LANES), lambda c: (0, c, 0)),
                   pl.BlockSpec((db, S_COLS), lambda c: (0, jnp.maximum(c - first_zg, 0))),
                   pl.BlockSpec((N_GROUPS, 2 * D_A, db), lambda c: (0, 0, 0)),
                   pl.BlockSpec((D_MODEL, S_COLS), lambda c: (0, c)),
                   *small_specs),
        compiler_params=pltpu.CompilerParams(dimension_semantics=("arbitrary",)),
        name="sample_inproj",
    )(x, gain, w, *small_weights)


S_TILE = 1
Z_ROWS = D_IN // LANES
U_ROWS = 8


def _sample_rows(zr_ref, c1_ref, c2_ref, c3_ref, cm_ref, st_ref, lblt_ref, nh_ref, u_ref, nst_ref):
    bt = S_TILE
    pad = [jnp.zeros((LANES - bt * Z_ROWS, LANES), F32)] if bt * Z_ROWS < LANES else []
    zt = jnp.concatenate([zr_ref[b] for b in range(bt)] + pad, axis=0).T

    def col(b, off, n=HG_DIM):
        j, l = divmod(off, LANES)
        return zt[l:l + n, Z_ROWS * b + j:Z_ROWS * b + j + 1]

    lbt = lblt_ref[...]
    lb_col = _lower_bound(lbt[:, 0:1], lbt[:, 1:2])
    nh = nh_ref[...]

    def col2(b, off):
        return jnp.concatenate([col(b, off), col(b, off + LANES)], axis=0)

    def per_head(x):
        return jnp.concatenate([jnp.sum(x[h * A_HEAD_DIM:(h + 1) * A_HEAD_DIM], axis=0, keepdims=True)
                                for h in range(A_HEADS)], axis=0)

    def spread(x):
        return jnp.concatenate([jnp.broadcast_to(x[h:h + 1], (A_HEAD_DIM, 1)) for h in range(A_HEADS)], axis=0)

    def attend(b, srcs):
        parts, new_scores, qs = [], [], []
        for c_ref, dil, q_off, k_off, _ in srcs:
            q = col2(b, q_off) * ATT_SCALE
            qs.append(q)
            s = per_head(c_ref[b, 0:D_A, :] * q)
            if dil > 1:
                lane = lax.broadcasted_iota(jnp.int32, s.shape, 1)
                s = jnp.where(lane % dil == 0, s, -jnp.inf)
            parts.append(s)
            if k_off is not None:
                new_scores.append(per_head(col2(b, k_off) * q))
        if new_scores:
            lane = lax.broadcasted_iota(jnp.int32, (A_HEADS, LANES), 1)
            slab = jnp.full((A_HEADS, LANES), -jnp.inf, F32)
            for k, sn in enumerate(new_scores):
                slab = jnp.where(lane == k, sn, slab)
            parts.append(slab)
        s_all = jnp.concatenate(parts, axis=1)
        p_all = jnp.exp(s_all - jnp.max(s_all, axis=1, keepdims=True))
        den = jnp.sum(p_all, axis=1, keepdims=True)
        acc = [jnp.zeros((A_HEAD_DIM, LANES), F32) for _ in range(A_HEADS)]
        lo = 0
        for c_ref, _, _, _, _ in srcs:
            length = c_ref.shape[2]
            for k in range(length // LANES):
                ls = slice(k * LANES, (k + 1) * LANES)
                for h in range(A_HEADS):
                    vt = c_ref[b, D_A + h * A_HEAD_DIM:D_A + (h + 1) * A_HEAD_DIM, ls]
                    acc[h] = acc[h] + vt * p_all[h:h + 1, lo + k * LANES:lo + (k + 1) * LANES]
            lo += length
        num = jnp.sum(jnp.concatenate(acc, axis=0), axis=1, keepdims=True)
        k = 0
        for _, _, _, k_off, v_off in srcs:
            if k_off is not None:
                num = num + spread(p_all[:, lo + k:lo + k + 1]) * col2(b, v_off)
                k += 1
        return num / spread(den)

    window = [(c_ref, dil, OFF_QA + g * D_A, OFF_KA + g * D_A, OFF_VA + g * D_A)
              for g, (c_ref, (_, dil)) in enumerate(zip((c1_ref, c2_ref, c3_ref), WIN_GROUPS))]
    memory = [(cm_ref, 1, OFF_QM, None, None)]

    for b in range(bt):
        ua = attend(b, window) * _silu(col2(b, OFF_GA))
        um = attend(b, memory) * _silu(col2(b, OFF_GM))
        branch_cols = [ua[0:LANES], ua[LANES:2 * LANES], um[0:LANES], um[LANES:2 * LANES]]

        lane = lax.broadcasted_iota(jnp.int32, (LANES, LANES), 1)
        ut = jnp.zeros((LANES, LANES), F32)
        for k, cvec in enumerate(branch_cols):
            ut = ut + jnp.where(lane == k, cvec, 0.0)
        u_ref[b, 0:4, :] = ut.T[0:4, :]

        orows = []
        for h in range(HG_HEADS):
            sl = slice(h * HG_DIM, (h + 1) * HG_DIM)
            fcol = lb_col[sl] + (1.0 - lb_col[sl]) * _sigmoid(col(b, OFF_FB + h * HG_DIM))
            qcol = _silu(col(b, OFF_QB + h * HG_DIM))
            vrow = zr_ref[b, OFF_IB // LANES + h:OFF_IB // LANES + h + 1, :]
            sn = fcol * st_ref[b, sl, :] + (1.0 - fcol) * vrow
            nst_ref[b, sl, :] = sn
            orows.append(jnp.sum(sn * qcol, axis=0, keepdims=True))
        gate = zr_ref[b, OFF_GB // LANES:OFF_GB // LANES + HG_HEADS, :]
        u_ref[b, 4:4 + HG_HEADS, :] = _rms(jnp.concatenate(orows, axis=0), nh) * _silu(gate)


def _stored_view(cache):
    db, length = cache.shape[0], cache.shape[1]
    return jnp.transpose(cache, (0, 2, 3, 4, 1)).reshape(db, 2 * D_A, length)


def _sample_out_kernel(x_ref, u_ref, zg_ref, nf_ref, wa_ref, wb_ref, wm_ref, wo_ref, y_ref):
    u = u_ref[...]
    y_ref[...] = _merge_out(x_ref[...], u[:, 0:D_A], u[:, D_A + D_M:], u[:, D_A:D_A + D_M], zg_ref[...],
                            wa_ref, wb_ref, wm_ref, wo_ref, nf_ref[...])


def _sample_out_call(x, u, zg, nf, wa, wb, wm, wo):
    db = x.shape[0]

    def full(a):
        return pl.BlockSpec(a.shape, lambda i: (0,) * a.ndim)

    args = (x, u, zg, nf, wa, wb, wm, wo)
    return pl.pallas_call(
        _sample_out_kernel,
        out_shape=jax.ShapeDtypeStruct((db, D_MODEL), F32),
        grid=(1,),
        in_specs=[full(a) for a in args],
        out_specs=pl.BlockSpec((db, D_MODEL), lambda i: (0, 0)),
        compiler_params=pltpu.CompilerParams(vmem_limit_bytes=VMEM_LIMIT),
        name="sample_out",
    )(*args)


def _cache_from_stored(kvt):
    b, _, length = kvt.shape
    return jnp.transpose(kvt.reshape(b, 2, A_HEADS, A_HEAD_DIM, length), (0, 4, 1, 2, 3))[None]


def kernel(x_prompt, x_sample, mem_prompt, cache_w1_kv, cache_w2_kv, cache_w3_kv, cache_mem_kv, state_hgrn,
           norm_in, w_in, lb_logits, norm_hgrn, norm_mem, w_mem_kv, w_branch_a, w_branch_b, w_branch_m,
           w_out, norm_final):
    b, s, _ = x_prompt.shape
    db = x_sample.shape[0]
    gain_in = norm_in[0][None]
    nh = norm_hgrn[0][None]
    nf = norm_final[None]
    lbl = lb_logits.astype(F32)

    xs = x_sample.reshape(db, D_MODEL)
    zr, zg_s, kvt_s, w_in_bf, wa, wb, wm, wo, w_mem_bf = _sample_inproj_call(
        xs, gain_in, w_in[0], [w[0] for w in (w_branch_a, w_branch_b, w_branch_m, w_out, w_mem_kv)])
    mkv, mkv_t = _mem_kv_call(mem_prompt, norm_mem[0][None], w_mem_bf)
    sample_ops = (zr,
                  *[_stored_view(c[0]) for c in (cache_w1_kv, cache_w2_kv, cache_w3_kv, cache_mem_kv)],
                  state_hgrn[0].reshape(db, HG_HEADS * HG_DIM, HG_DIM), lbl.T, nh.reshape(HG_HEADS, HG_DIM))
    qkv0, qkv1, qkv2, gq, hb, zg, fb, kvt0, kvt1, kvt2, u_first, nst_first = _inproj_call(
        x_prompt, gain_in, w_in_bf, sample_ops)
    og = [_dilated_call(qkv, dil, g) for g, (qkv, (_, dil)) in enumerate(zip((qkv0, qkv1, qkv2), WIN_GROUPS))]
    y_prompt, hg_prompt, u, nst = _prompt_tail_call(
        x_prompt, gq, hb, zg, fb, og[0], og[1], og[2], mkv, lbl, nh, nf, wa, wb, wm, wo,
        sample_ops, u_first, nst_first)
    pw = [_cache_from_stored(kvt) for kvt in (kvt0, kvt1, kvt2)]
    new_mem = _cache_from_stored(mkv_t)

    y_sample = _sample_out_call(xs, u, zg_s, nf, wa, wb, wm, wo).reshape(db, 1, D_MODEL)
    sw = [jnp.transpose(kvt_s[g].reshape(2, A_HEADS, A_HEAD_DIM, db), (3, 0, 1, 2))[None, :, None]
          for g in range(N_GROUPS)]
    new_hg_sample = nst.reshape(1, db, HG_HEADS, HG_DIM, HG_DIM)

    return (y_prompt, y_sample, pw[0], pw[1], pw[2], new_mem, hg_prompt[None],
            sw[0], sw[1], sw[2], new_hg_sample)
```

```python
import jax
import jax.numpy as jnp
from jax import lax
from jax.experimental import pallas as pl
from jax.experimental.pallas import tpu as pltpu

F32 = jnp.float32
BF16 = jnp.bfloat16

D_MODEL = 1024
WIN_GROUPS = ((128, 1), (512, 4), (2048, 16))
N_GROUPS = 3
A_HEADS = 4
A_HEAD_DIM = 64
D_A = 256
QBLK = 128
HG_HEADS = 4
HG_DIM = 128
D_B = 512
HG_CHUNK = 64
HG_SUB = 16
MEM_LEN = 256
D_M = 256
EPS = 1e-6
D_IN = 8192
LANES = 128
OFF_QA, OFF_KA, OFF_VA, OFF_GA = 0, 768, 1536, 2304
OFF_QB, OFF_FB, OFF_IB, OFF_GB = 2560, 3072, 3584, 4096
OFF_QM, OFF_GM, OFF_ZG = 4608, 4864, 5120
ATT_SCALE = A_HEAD_DIM ** -0.5

VMEM_LIMIT = 56 * 1024 * 1024

NT_DIMS = (((1,), (1,)), ((), ()))
TN_DIMS = (((0,), (0,)), ((), ()))


def _sigmoid(x):
    return 0.5 * jnp.tanh(0.5 * x) + 0.5


def _silu(x):
    h = 0.5 * x
    return h * jnp.tanh(h) + h


def _rms(x, gain):
    return x * lax.rsqrt(jnp.mean(x * x, axis=-1, keepdims=True) + EPS) * gain


def _head_masks(width, head_dim, n_heads):
    lane = lax.broadcasted_iota(jnp.int32, (1, width), 1)
    return [(lane >= h * head_dim) & (lane < (h + 1) * head_dim) for h in range(n_heads)]


def _stack_heads(q, masks):
    zero = jnp.zeros_like(q)
    return jnp.concatenate([jnp.where(m, q, zero) for m in masks], axis=0)


def _unstack_heads(o_all, masks, t):
    n = len(masks)
    out = o_all[(n - 1) * t:n * t]
    for h in range(n - 2, -1, -1):
        out = jnp.where(masks[h], o_all[h * t:(h + 1) * t], out)
    return out


def _expand_cols(cols, masks):
    out = jnp.broadcast_to(cols[-1], (cols[-1].shape[0], masks[0].shape[1]))
    for h in range(len(masks) - 2, -1, -1):
        out = jnp.where(masks[h], cols[h], out)
    return out


def _mem_kv_kernel(mem_ref, gain_ref, w_ref, out_ref, out_t_ref):
    hm = _rms(mem_ref[0], gain_ref[...]).astype(BF16)
    kv = jnp.dot(hm, w_ref[...], preferred_element_type=F32)
    out_ref[0] = kv
    out_t_ref[0] = kv.T


def _mem_kv_call(mem, gain, w_bf):
    b = mem.shape[0]
    return pl.pallas_call(
        _mem_kv_kernel,
        out_shape=(jax.ShapeDtypeStruct((b, MEM_LEN, 2 * D_M), F32),
                   jax.ShapeDtypeStruct((b, 2 * D_M, MEM_LEN), F32)),
        grid=(b,),
        in_specs=[pl.BlockSpec((1, MEM_LEN, D_MODEL), lambda i: (i, 0, 0)),
                  pl.BlockSpec((1, D_MODEL), lambda i: (0, 0)),
                  pl.BlockSpec((D_MODEL, 2 * D_M), lambda i: (0, 0))],
        out_specs=(pl.BlockSpec((1, MEM_LEN, 2 * D_M), lambda i: (i, 0, 0)),
                   pl.BlockSpec((1, 2 * D_M, MEM_LEN), lambda i: (i, 0, 0))),
        name="mem_kv",
    )(mem, gain, w_bf)


IN_TILE = 256
IN_CHUNK = 256
D_QKV = 3 * D_A
D_GQ = 3 * D_A
D_HB = 3 * D_B
D_ZG = 3 * D_MODEL


def _inproj_kernel(x_ref, gain_ref, w_ref, zr_ref, c1_ref, c2_ref, c3_ref, cm_ref, sst_ref, slbl_ref, nh4_ref,
                   qkv0_ref, qkv1_ref, qkv2_ref, gq_ref, hb_ref, zg_ref, fb_ref, kvt0_ref, kvt1_ref, kvt2_ref,
                   u_ref, nst_ref, scr_ref):
    qkv_refs = (qkv0_ref, qkv1_ref, qkv2_ref)
    kvt_refs = (kvt0_ref, kvt1_ref, kvt2_ref)
    t = IN_TILE
    h = _rms(x_ref[0], gain_ref[...]).astype(BF16)
    for c in range(D_IN // IN_CHUNK):
        lo = c * IN_CHUNK
        zc = jnp.dot(h, w_ref[:, lo:lo + IN_CHUNK], preferred_element_type=F32)
        if lo < OFF_GA:
            sec, g = divmod(c, N_GROUPS)
            dil = WIN_GROUPS[g][1]
            dst = slice(sec * D_A, (sec + 1) * D_A)
            if dil == 1:
                qkv_refs[g][0, 0, :, dst] = zc.astype(BF16)
            else:
                for k in range(IN_CHUNK // LANES):
                    scr_ref[k] = zc[:, k * LANES:(k + 1) * LANES]
                for r in range(dil):
                    qkv_refs[g][0, r, :, dst] = jnp.concatenate(
                        [scr_ref[k, pl.ds(r, t // dil, stride=dil), :] for k in range(IN_CHUNK // LANES)],
                        axis=1).astype(BF16)
            if sec > 0:
                width = kvt_refs[g].shape[2]
                kvt_refs[g][0, (sec - 1) * D_A:sec * D_A, :] = zc.T[:, t - width:]
        elif lo < OFF_QB:
            gq_ref[0, :, 0:D_A] = zc.astype(BF16)
        elif lo < OFF_FB:
            hb_ref[0, :, lo - OFF_QB:lo - OFF_QB + IN_CHUNK] = zc.astype(BF16)
        elif lo < OFF_IB:
            fb_ref[0, :, lo - OFF_FB:lo - OFF_FB + IN_CHUNK] = zc
        elif lo < OFF_QM:
            hb_ref[0, :, lo - OFF_IB + D_B:lo - OFF_IB + D_B + IN_CHUNK] = zc.astype(BF16)
        elif lo < OFF_ZG:
            gq_ref[0, :, lo - OFF_QM + D_A:lo - OFF_QM + D_A + IN_CHUNK] = zc.astype(BF16)
        else:
            zg_ref[0, :, lo - OFF_ZG:lo - OFF_ZG + IN_CHUNK] = zc.astype(BF16)
    _sample_rows(zr_ref, c1_ref, c2_ref, c3_ref, cm_ref, sst_ref, slbl_ref, nh4_ref, u_ref, nst_ref)


def _sample_specs(ops, nt, first_row):
    first_blk = first_row // S_TILE

    def rows(a):
        return pl.BlockSpec((S_TILE,) + a.shape[1:], lambda i, j: (first_blk + i * nt + j, 0, 0))

    def const(a):
        return pl.BlockSpec(a.shape, lambda i, j: (0, 0), pipeline_mode=pl.Buffered(1))

    return [rows(a) for a in ops[:6]] + [const(a) for a in ops[6:]]


def _inproj_call(x, gain, w_bf, sample_ops):
    b, s, _ = x.shape
    t = IN_TILE
    nt = s // t
    n_rows = b * nt * S_TILE

    def tok(width):
        return pl.BlockSpec((1, t, width), lambda i, j: (i, j, 0))

    def srow(shape):
        return pl.BlockSpec((S_TILE,) + shape, lambda i, j: (i * nt + j, 0, 0))

    qkv_shapes, qkv_specs = [], []
    for _, dil in WIN_GROUPS:
        qkv_shapes.append(jax.ShapeDtypeStruct((b, dil, s // dil, D_QKV), BF16))
        qkv_specs.append(pl.BlockSpec((1, dil, t // dil, D_QKV), lambda i, j: (i, 0, j, 0)))
    kvt_shapes, kvt_specs = [], []
    for window, _ in WIN_GROUPS:
        length = min(window, s)
        width = min(t, length)
        first = nt - length // width
        kvt_shapes.append(jax.ShapeDtypeStruct((b, 2 * D_A, length), F32))
        kvt_specs.append(pl.BlockSpec((1, 2 * D_A, width),
                                      lambda i, j, first=first: (i, 0, jnp.maximum(j - first, 0))))
    return pl.pallas_call(
        _inproj_kernel,
        out_shape=(*qkv_shapes,
                   jax.ShapeDtypeStruct((b, s, D_GQ), BF16),
                   jax.ShapeDtypeStruct((b, s, D_HB), BF16),
                   jax.ShapeDtypeStruct((b, s, D_ZG), BF16),
                   jax.ShapeDtypeStruct((b, s, D_B), F32),
                   *kvt_shapes,
                   jax.ShapeDtypeStruct((n_rows, U_ROWS, LANES), F32),
                   jax.ShapeDtypeStruct((n_rows, HG_HEADS * HG_DIM, HG_DIM), F32)),
        grid=(b, nt),
        in_specs=[tok(D_MODEL),
                  pl.BlockSpec((1, D_MODEL), lambda i, j: (0, 0)),
                  pl.BlockSpec((D_MODEL, D_IN), lambda i, j: (0, 0), pipeline_mode=pl.Buffered(1)),
                  *_sample_specs(sample_ops, nt, 0)],
        out_specs=(*qkv_specs, tok(D_GQ), tok(D_HB), tok(D_ZG), tok(D_B), *kvt_specs,
                   srow((U_ROWS, LANES)), srow((HG_HEADS * HG_DIM, HG_DIM))),
        scratch_shapes=[pltpu.VMEM((IN_CHUNK // LANES, t, LANES), F32)],
        compiler_params=pltpu.CompilerParams(
            dimension_semantics=("arbitrary", "arbitrary"), vmem_limit_bytes=VMEM_LIMIT),
        name="inproj",
    )(x, gain, w_bf, *sample_ops)


DIL_STEP = 2048
OG_SLABS = 2 * D_A // LANES


def _make_dilated_kernel(dil):
    nblk = DIL_STEP // (dil * QBLK)

    def body(prev_ref, cur_ref, out_ref):
        first_step = pl.program_id(1) == 0
        masks = _head_masks(D_A, A_HEAD_DIM, A_HEADS)
        qi = lax.broadcasted_iota(jnp.int32, (A_HEADS * QBLK, 2 * QBLK), 0) % QBLK
        kj = lax.broadcasted_iota(jnp.int32, (A_HEADS * QBLK, 2 * QBLK), 1)
        band = (kj >= qi) & (kj <= qi + QBLK)
        neg_band = jnp.where(band, 0.0, -jnp.inf)
        neg_first = jnp.where(first_step, jnp.where(band & (kj >= QBLK), 0.0, -jnp.inf), neg_band)

        def keys_values(r, jb, c):
            rows = slice(jb * QBLK, (jb + 1) * QBLK)
            if jb == 0:
                return jnp.concatenate([prev_ref[0, r, :, c * D_A:(c + 1) * D_A],
                                        cur_ref[0, r, rows, c * D_A:(c + 1) * D_A]], axis=0)
            return cur_ref[0, r, (jb - 1) * QBLK:(jb + 1) * QBLK, c * D_A:(c + 1) * D_A]

        for r, jb in [(r, jb) for r in range(dil) for jb in range(nblk)]:
            q = cur_ref[0, r, jb * QBLK:(jb + 1) * QBLK, 0:D_A]
            qs = _stack_heads(q * jnp.asarray(ATT_SCALE, BF16), masks)
            s = lax.dot_general(qs, keys_values(r, jb, 1), NT_DIMS, preferred_element_type=F32)
            s = s + (neg_first if jb == 0 else neg_band)
            mx = jnp.max(s, axis=-1, keepdims=True)
            p = jnp.exp(s - mx)
            l = jnp.sum(p, axis=-1, keepdims=True)
            o_all = jnp.dot(p.astype(BF16), keys_values(r, jb, 2), preferred_element_type=F32)
            l_e = _expand_cols([l[h * QBLK:(h + 1) * QBLK] for h in range(A_HEADS)], masks)
            mx_e = _expand_cols([mx[h * QBLK:(h + 1) * QBLK] for h in range(A_HEADS)], masks)
            res = (_unstack_heads(o_all, masks, QBLK) / l_e, mx_e + jnp.log(l_e))
            for k in range(OG_SLABS):
                src = res[k // 2][:, (k % 2) * LANES:(k % 2 + 1) * LANES]
                if dil == 1:
                    out_ref[0, k, jb * QBLK:(jb + 1) * QBLK, :] = src
                else:
                    out_ref[0, k, pl.ds(jb * QBLK * dil + r, QBLK, stride=dil), :] = src
    return body


def _dilated_call(qkv, dil, g):
    b, _, n, _ = qkv.shape
    rows = DIL_STEP // dil
    nb = n // rows
    return pl.pallas_call(
        _make_dilated_kernel(dil),
        out_shape=jax.ShapeDtypeStruct((b, OG_SLABS, n * dil, LANES), F32),
        grid=(b, nb),
        in_specs=[pl.BlockSpec((1, dil, QBLK, D_QKV),
                               lambda i, u: (i, 0, jnp.maximum(u * (rows // QBLK) - 1, 0), 0)),
                  pl.BlockSpec((1, dil, rows, D_QKV), lambda i, u: (i, 0, u, 0))],
        out_specs=pl.BlockSpec((1, OG_SLABS, DIL_STEP, LANES), lambda i, u: (i, 0, u, 0)),
        compiler_params=pltpu.CompilerParams(
            dimension_semantics=("arbitrary", "arbitrary"), vmem_limit_bytes=VMEM_LIMIT),
        name=f"dilated_attn_g{g}",
    )(qkv, qkv)


def _lower_bound(l0, l1):
    m = jnp.maximum(l0, l1)
    e0, e1 = jnp.exp(l0 - m), jnp.exp(l1 - m)
    return e0 / (e0 + e1)


def _head_rms(ob, gain):
    parts = []
    for h in range(HG_HEADS):
        sl = slice(h * HG_DIM, (h + 1) * HG_DIM)
        parts.append(_rms(ob[:, sl], gain[:, sl]))
    return jnp.concatenate(parts, axis=1)


def _merge_out(x, ua, ub, um, zg, wa_ref, wb_ref, wm_ref, wo_ref, nf):
    projs = [jnp.dot(u.astype(BF16), w_ref[...], preferred_element_type=F32)
             for u, w_ref in ((ua, wa_ref), (ub, wb_ref), (um, wm_ref))]
    gates = _sigmoid(zg.astype(BF16))
    merged = None
    for k, proj in enumerate(projs):
        term = gates[:, k * D_MODEL:(k + 1) * D_MODEL] * proj.astype(BF16)
        merged = term if merged is None else merged + term
    y = x + jnp.dot(merged, wo_ref[...], preferred_element_type=F32)
    return _rms(y, nf)


TAIL_TILE = 256


def _cumsum_rows(g, tri):
    g1 = g.astype(BF16)
    g2 = (g - g1.astype(F32)).astype(BF16)
    return jnp.dot(tri, g1, preferred_element_type=F32) + jnp.dot(tri, g2, preferred_element_type=F32)


def _hgrn_tile(fb, qb, v, lb, st_ref):
    t = fb.shape[0]
    c, sub = HG_CHUNK, HG_SUB
    n_sub = c // sub
    f = lb + (1.0 - lb) * _sigmoid(fb)
    kk = 1.0 - f
    row = lax.broadcasted_iota(jnp.int32, (t, t), 0)
    col = lax.broadcasted_iota(jnp.int32, (t, t), 1)
    same_chunk = (row // c) == (col // c)
    gcs = _cumsum_rows(jnp.log2(f), (same_chunk & (col <= row)).astype(BF16))
    q = _silu(qb).astype(F32)

    loc = lax.broadcasted_iota(jnp.int32, (c, 1), 0)
    qt_c, kt_ci, qe_c, kd_c, dec_c = [], [[] for _ in range(n_sub)], [], [], []
    for ci in range(t // c):
        rows = slice(ci * c, (ci + 1) * c)
        g_c, q_c, k_c = gcs[rows], q[rows], kk[rows]
        refs = [jnp.zeros((1, D_B), F32)] + [g_c[i * sub - 1:i * sub] for i in range(1, n_sub)]
        own = refs[n_sub - 1]
        for i in range(n_sub - 2, -1, -1):
            own = jnp.where(loc < (i + 1) * sub, refs[i], own)
        qt_c.append((q_c * jnp.exp2(g_c - own)).astype(BF16))
        k_own = k_c * jnp.exp2(own - g_c)
        for i in range(n_sub):
            pieces = []
            for jb in range(n_sub):
                piece = k_own[jb * sub:(jb + 1) * sub]
                if jb < i:
                    piece = piece * jnp.exp2(refs[i] - refs[jb])
                elif jb > i:
                    piece = jnp.zeros_like(piece)
                pieces.append(piece)
            kt_ci[i].append(jnp.concatenate(pieces, axis=0).astype(BF16))
        g_last = g_c[c - 1:c]
        qe_c.append((q_c * jnp.exp2(g_c)).astype(BF16))
        kd_c.append((k_c * jnp.exp2(g_last - g_c)).astype(BF16))
        dec_c.append(jnp.exp2(g_last))
    qt = jnp.concatenate(qt_c, axis=0)
    kts = [jnp.concatenate(parts, axis=0) for parts in kt_ci]
    sub_of_row = (lax.broadcasted_iota(jnp.int32, (t, 1), 0) % c) // sub
    causal = same_chunk & (col <= row)

    heads = [slice(h * HG_DIM, (h + 1) * HG_DIM) for h in range(HG_HEADS)]
    chunks = [slice(ci * c, (ci + 1) * c) for ci in range(t // c)]
    zero = jnp.zeros((t, HG_DIM), BF16)
    atts, incs = [], []
    for sl in heads:
        q_big = jnp.concatenate([jnp.where(sub_of_row == i, qt[:, sl], zero) for i in range(n_sub)], axis=1)
        k_big = jnp.concatenate([kts[i][:, sl] for i in range(n_sub)], axis=1)
        atts.append(lax.dot_general(q_big, k_big, NT_DIMS, preferred_element_type=F32))
        incs.append([lax.dot_general(v[rows, sl], kd_c[ci][:, sl], TN_DIMS, preferred_element_type=F32)
                     for ci, rows in enumerate(chunks)])
    states = []
    for h, sl in enumerate(heads):
        atts[h] = jnp.where(causal, atts[h], 0.0).astype(BF16)
        sts = [st_ref[h]]
        for ci in range(len(chunks)):
            sts.append(sts[-1] * dec_c[ci][:, sl] + incs[h][ci])
        st_ref[h] = sts[-1]
        states.append(sts)
    outs = []
    for h, sl in enumerate(heads):
        o_intra = jnp.dot(atts[h], v[:, sl], preferred_element_type=F32)
        inter = [lax.dot_general(qe_c[ci][:, sl], states[h][ci].astype(BF16), NT_DIMS, preferred_element_type=F32)
                 for ci in range(len(chunks))]
        outs.append(o_intra + jnp.concatenate(inter, axis=0))
    return jnp.concatenate(outs, axis=1)


def _prompt_tail_kernel(x_ref, gq_ref, hb_ref, zg_ref, fb_ref, o1_ref, o2_ref, o3_ref, mkv_ref, lbl_ref,
                        nh_ref, nf_ref, wa_ref, wb_ref, wm_ref, wo_ref,
                        zr_ref, c1_ref, c2_ref, c3_ref, cm_ref, sst_ref, slbl_ref, nh4_ref, ua_ref, nsta_ref,
                        y_ref, hst_ref, u_ref, nst_ref, st_ref):
    j = pl.program_id(1)
    t = TAIL_TILE

    @pl.when(j == 0)
    def _():
        st_ref[...] = jnp.zeros_like(st_ref)

    u_ref[0] = ua_ref[...]
    nst_ref[0] = nsta_ref[...]
    _sample_rows(zr_ref, c1_ref, c2_ref, c3_ref, cm_ref, sst_ref, slbl_ref, nh4_ref, u_ref.at[1], nst_ref.at[1])

    masks = _head_masks(D_M, A_HEAD_DIM, A_HEADS)
    qm = gq_ref[0, :, D_A:2 * D_A] * jnp.asarray(ATT_SCALE, BF16)
    qs = _stack_heads(qm, masks)
    mk = mkv_ref[0, :, 0:D_M].astype(BF16)
    mv = mkv_ref[0, :, D_M:2 * D_M].astype(BF16)
    s = lax.dot_general(qs, mk, NT_DIMS, preferred_element_type=F32)

    lb = _lower_bound(lbl_ref[0:1], lbl_ref[1:2])
    ob = _hgrn_tile(fb_ref[0], hb_ref[0, :, 0:D_B], hb_ref[0, :, D_B:2 * D_B], lb, st_ref)
    ub = _head_rms(ob, nh_ref[...]).astype(BF16) * _silu(hb_ref[0, :, 2 * D_B:3 * D_B])

    os_ = [jnp.concatenate([r[0, 0], r[0, 1]], axis=1) for r in (o1_ref, o2_ref, o3_ref)]
    ls_ = [jnp.concatenate([r[0, 2], r[0, 3]], axis=1) for r in (o1_ref, o2_ref, o3_ref)]
    mx = jnp.maximum(jnp.maximum(ls_[0], ls_[1]), ls_[2])
    es_ = [jnp.exp(l - mx) for l in ls_]
    oa = (es_[0] * os_[0] + es_[1] * os_[1] + es_[2] * os_[2]) / (es_[0] + es_[1] + es_[2])
    ua = oa.astype(BF16) * _silu(gq_ref[0, :, 0:D_A])

    p = jnp.exp(s - jnp.max(s, axis=-1, keepdims=True))
    l = jnp.sum(p, axis=-1, keepdims=True)
    om = _unstack_heads(jnp.dot(p.astype(BF16), mv, preferred_element_type=F32) / l, masks, t)
    um = om.astype(BF16) * _silu(gq_ref[0, :, 2 * D_A:3 * D_A])

    y_ref[0] = _merge_out(x_ref[0], ua, ub, um, zg_ref[0], wa_ref, wb_ref, wm_ref, wo_ref, nf_ref[...])

    @pl.when(j == pl.num_programs(1) - 1)
    def _():
        for h in range(HG_HEADS):
            hst_ref[0, h] = st_ref[h].T


def _prompt_tail_call(x, gq, hb, zg, fb, o1, o2, o3, mkv, lbl, nh, nf, wa, wb, wm, wo,
                      sample_ops, u_first, nst_first):
    b, s, _ = x.shape
    t = TAIL_TILE
    nt = s // t
    db = sample_ops[0].shape[0]
    half = u_first.shape[0]
    assert half + b * nt * S_TILE == db and 2 * half == db, "sample rows must split evenly over both grids"

    def tok(width):
        return pl.BlockSpec((1, t, width), lambda i, j: (i, j, 0))

    def const(shape):
        return pl.BlockSpec(shape, lambda i, j: (0,) * len(shape), pipeline_mode=pl.Buffered(1))

    def first(a):
        return pl.BlockSpec((S_TILE,) + a.shape[1:], lambda i, j: (i * nt + j, 0, 0))

    def halves(shape):
        return pl.BlockSpec((2, S_TILE) + shape, lambda i, j: (0, i * nt + j, 0, 0))

    y, hst, u, nst = pl.pallas_call(
        _prompt_tail_kernel,
        out_shape=(jax.ShapeDtypeStruct((b, s, D_MODEL), F32),
                   jax.ShapeDtypeStruct((b, HG_HEADS, HG_DIM, HG_DIM), F32),
                   jax.ShapeDtypeStruct((2, half, U_ROWS, LANES), F32),
                   jax.ShapeDtypeStruct((2, half, HG_HEADS * HG_DIM, HG_DIM), F32)),
        grid=(b, nt),
        in_specs=[tok(D_MODEL), tok(D_GQ), tok(D_HB), tok(D_ZG), tok(D_B),
                  *[pl.BlockSpec((1, OG_SLABS, t, LANES), lambda i, j: (i, 0, j, 0))] * N_GROUPS,
                  pl.BlockSpec((1, MEM_LEN, 2 * D_M), lambda i, j: (i, 0, 0)),
                  const((2, D_B)), const((1, D_B)), const((1, D_MODEL)),
                  const((D_A, D_MODEL)), const((D_B, D_MODEL)), const((D_M, D_MODEL)),
                  const((D_MODEL, D_MODEL)),
                  *_sample_specs(sample_ops, nt, half), first(u_first), first(nst_first)],
        out_specs=(tok(D_MODEL),
                   pl.BlockSpec((1, HG_HEADS, HG_DIM, HG_DIM), lambda i, j: (i, 0, 0, 0)),
                   halves((U_ROWS, LANES)), halves((HG_HEADS * HG_DIM, HG_DIM))),
        scratch_shapes=[pltpu.VMEM((HG_HEADS, HG_DIM, HG_DIM), F32)],
        compiler_params=pltpu.CompilerParams(
            dimension_semantics=("arbitrary", "arbitrary"), vmem_limit_bytes=VMEM_LIMIT),
        name="prompt_tail",
    )(x, gq, hb, zg, fb, o1, o2, o3, mkv, lbl, nh, nf, wa, wb, wm, wo, *sample_ops, u_first, nst_first)
    return y, hst, u.reshape(db, U_ROWS, LANES), nst.reshape(db, HG_HEADS * HG_DIM, HG_DIM)


S_COLS = 1024


def _sample_inproj_kernel(x_ref, gain_ref, w_ref, *refs):
    n_small = (len(refs) - 4) // 2
    small_in, (zr_ref, zg_ref, kvt_ref, wbf_ref), small_out = refs[:n_small], refs[n_small:n_small + 4], refs[n_small + 4:]
    for src, dst in zip(small_in, small_out):
        dst[...] = src[...].astype(BF16)
    c = pl.program_id(0)
    h = _rms(x_ref[:, 0, :], gain_ref[...]).astype(BF16)
    w = w_ref[...].astype(BF16)
    wbf_ref[...] = w
    z = jnp.dot(h, w, preferred_element_type=F32)
    for k in range(S_COLS // LANES):
        zr_ref[:, k, :] = z[:, k * LANES:(k + 1) * LANES]
    zg_ref[...] = z
    for step in range(OFF_GA // S_COLS + 1):
        pieces = [(g, kv, (OFF_KA + (kv * N_GROUPS + g) * D_A) % S_COLS)
                  for g in range(N_GROUPS) for kv in range(2)
                  if (OFF_KA + (kv * N_GROUPS + g) * D_A) // S_COLS == step]

        @pl.when(c == step)
        def _(pieces=pieces):
            zt = z.T
            for g, kv, row0 in pieces:
                kvt_ref[g, kv * D_A:(kv + 1) * D_A, :] = zt[row0:row0 + D_A]


def _sample_inproj_call(x, gain, w, small_weights):
    db = x.shape[0]
    steps = D_IN // S_COLS
    first_zg = OFF_ZG // S_COLS
    small_specs = [pl.BlockSpec((a.shape[0] // steps, a.shape[1]), lambda c: (c, 0)) for a in small_weights]
    return pl.pallas_call(
        _sample_inproj_kernel,
        out_shape=(jax.ShapeDtypeStruct((db, Z_ROWS, LANES), F32),
                   jax.ShapeDtypeStruct((db, D_ZG), F32),
                   jax.ShapeDtypeStruct((N_GROUPS, 2 * D_A, db), F32),
                   jax.ShapeDtypeStruct((D_MODEL, D_IN), BF16),
                   *[jax.ShapeDtypeStruct(a.shape, BF16) for a in small_weights]),
        grid=(steps,),
        in_specs=[pl.BlockSpec(x.shape, lambda c: (0, 0, 0)),
                  pl.BlockSpec((1, D_MODEL), lambda c: (0, 0)),
                  pl.BlockSpec((D_MODEL, S_COLS), lambda c: (0, c)),
                  *small_specs],
        out_specs=(pl.BlockSpec((db, S_COLS // LANES, LANES), lambda c: (0, c, 0)),
                   pl.BlockSpec((db, S_COLS), lambda c: (0, jnp.maximum(c - first_zg, 0))),
                   pl.BlockSpec((N_GROUPS, 2 * D_A, db), lambda c: (0, 0, 0)),
                   pl.BlockSpec((D_MODEL, S_COLS), lambda c: (0, c)),
                   *small_specs),
        compiler_params=pltpu.CompilerParams(dimension_semantics=("arbitrary",)),
        name="sample_inproj",
    )(x, gain, w, *small_weights)


S_TILE = 1
Z_ROWS = D_IN // LANES
U_ROWS = 8


def _sample_rows(zr_ref, c1_ref, c2_ref, c3_ref, cm_ref, st_ref, slbl_ref, nh_ref, u_ref, nst_ref):
    bt = S_TILE
    assert bt * Z_ROWS + HG_HEADS <= LANES
    lb_row = _lower_bound(slbl_ref[0:1], slbl_ref[1:2])
    lb_rows = [lb_row[:, h * HG_DIM:(h + 1) * HG_DIM] for h in range(HG_HEADS)]
    pad = jnp.zeros((LANES - bt * Z_ROWS - HG_HEADS, LANES), F32)
    zt = jnp.concatenate([zr_ref[b] for b in range(bt)] + lb_rows + [pad], axis=0).T

    def col(b, off, n=HG_DIM):
        j, l = divmod(off, LANES)
        return zt[l:l + n, Z_ROWS * b + j:Z_ROWS * b + j + 1]

    nh = nh_ref[...]

    def col2(b, off):
        return jnp.concatenate([col(b, off), col(b, off + LANES)], axis=0)

    def per_head(x):
        return jnp.concatenate([jnp.sum(x[h * A_HEAD_DIM:(h + 1) * A_HEAD_DIM], axis=0, keepdims=True)
                                for h in range(A_HEADS)], axis=0)

    def spread(x):
        return jnp.concatenate([jnp.broadcast_to(x[h:h + 1], (A_HEAD_DIM, 1)) for h in range(A_HEADS)], axis=0)

    def attend(b, srcs):
        parts, new_scores, qs = [], [], []
        for c_ref, dil, q_off, k_off, _ in srcs:
            q = col2(b, q_off) * ATT_SCALE
            qs.append(q)
            s = per_head(c_ref[b, 0:D_A, :] * q)
            if dil > 1:
                lane = lax.broadcasted_iota(jnp.int32, s.shape, 1)
                s = jnp.where(lane % dil == 0, s, -jnp.inf)
            parts.append(s)
            if k_off is not None:
                new_scores.append(per_head(col2(b, k_off) * q))
        if new_scores:
            lane = lax.broadcasted_iota(jnp.int32, (A_HEADS, LANES), 1)
            slab = jnp.full((A_HEADS, LANES), -jnp.inf, F32)
            for k, sn in enumerate(new_scores):
                slab = jnp.where(lane == k, sn, slab)
            parts.append(slab)
        s_all = jnp.concatenate(parts, axis=1)
        p_all = jnp.exp(s_all - jnp.max(s_all, axis=1, keepdims=True))
        den = jnp.sum(p_all, axis=1, keepdims=True)
        acc = [jnp.zeros((A_HEAD_DIM, LANES), F32) for _ in range(A_HEADS)]
        lo = 0
        for c_ref, _, _, _, _ in srcs:
            length = c_ref.shape[2]
            for k in range(length // LANES):
                ls = slice(k * LANES, (k + 1) * LANES)
                for h in range(A_HEADS):
                    vt = c_ref[b, D_A + h * A_HEAD_DIM:D_A + (h + 1) * A_HEAD_DIM, ls]
                    acc[h] = acc[h] + vt * p_all[h:h + 1, lo + k * LANES:lo + (k + 1) * LANES]
            lo += length
        num = jnp.sum(jnp.concatenate(acc, axis=0), axis=1, keepdims=True)
        k = 0
        for _, _, _, k_off, v_off in srcs:
            if k_off is not None:
                num = num + spread(p_all[:, lo + k:lo + k + 1]) * col2(b, v_off)
                k += 1
        return num / spread(den)

    window = [(c_ref, dil, OFF_QA + g * D_A, OFF_KA + g * D_A, OFF_VA + g * D_A)
              for g, (c_ref, (_, dil)) in enumerate(zip((c1_ref, c2_ref, c3_ref), WIN_GROUPS))]
    memory = [(cm_ref, 1, OFF_QM, None, None)]

    for b in range(bt):
        ua = attend(b, window) * _silu(col2(b, OFF_GA))
        um = attend(b, memory) * _silu(col2(b, OFF_GM))
        branch_cols = [ua[0:LANES], ua[LANES:2 * LANES], um[0:LANES], um[LANES:2 * LANES]]

        lane = lax.broadcasted_iota(jnp.int32, (LANES, LANES), 1)
        ut = jnp.zeros((LANES, LANES), F32)
        for k, cvec in enumerate(branch_cols):
            ut = ut + jnp.where(lane == k, cvec, 0.0)
        u_ref[b, 0:4, :] = ut.T[0:4, :]

        orows = []
        for h in range(HG_HEADS):
            sl = slice(h * HG_DIM, (h + 1) * HG_DIM)
            lb_col = zt[:, bt * Z_ROWS + h:bt * Z_ROWS + h + 1]
            fcol = lb_col + (1.0 - lb_col) * _sigmoid(col(b, OFF_FB + h * HG_DIM))
            qcol = _silu(col(b, OFF_QB + h * HG_DIM))
            vrow = zr_ref[b, OFF_IB // LANES + h:OFF_IB // LANES + h + 1, :]
            sn = fcol * st_ref[b, sl, :] + (1.0 - fcol) * vrow
            nst_ref[b, sl, :] = sn
            orows.append(jnp.sum(sn * qcol, axis=0, keepdims=True))
        gate = zr_ref[b, OFF_GB // LANES:OFF_GB // LANES + HG_HEADS, :]
        u_ref[b, 4:4 + HG_HEADS, :] = _rms(jnp.concatenate(orows, axis=0), nh) * _silu(gate)


def _stored_view(cache):
    db, length = cache.shape[0], cache.shape[1]
    return jnp.transpose(cache, (0, 2, 3, 4, 1)).reshape(db, 2 * D_A, length)


def _sample_out_kernel(x_ref, u_ref, zg_ref, nf_ref, wa_ref, wb_ref, wm_ref, wo_ref, y_ref):
    def rows(lo, hi):
        return jnp.concatenate([u_ref[:, k, :] for k in range(lo, hi)], axis=1)

    ua, um, ub = rows(0, 2), rows(2, 4), rows(4, U_ROWS)
    y_ref[:, 0, :] = _merge_out(x_ref[:, 0, :], ua, ub, um, zg_ref[...],
                                wa_ref, wb_ref, wm_ref, wo_ref, nf_ref[...])


def _sample_out_call(x, u, zg, nf, wa, wb, wm, wo):
    def full(a):
        return pl.BlockSpec(a.shape, lambda i: (0,) * a.ndim)

    args = (x, u, zg, nf, wa, wb, wm, wo)
    return pl.pallas_call(
        _sample_out_kernel,
        out_shape=jax.ShapeDtypeStruct(x.shape, F32),
        grid=(1,),
        in_specs=[full(a) for a in args],
        out_specs=full(x),
        compiler_params=pltpu.CompilerParams(vmem_limit_bytes=VMEM_LIMIT),
        name="sample_out",
    )(*args)


def _cache_from_stored(kvt):
    b, _, length = kvt.shape
    return jnp.transpose(kvt.reshape(b, 2, A_HEADS, A_HEAD_DIM, length), (0, 4, 1, 2, 3))[None]


def kernel(x_prompt, x_sample, mem_prompt, cache_w1_kv, cache_w2_kv, cache_w3_kv, cache_mem_kv, state_hgrn,
           norm_in, w_in, lb_logits, norm_hgrn, norm_mem, w_mem_kv, w_branch_a, w_branch_b, w_branch_m,
           w_out, norm_final):
    b, s, _ = x_prompt.shape
    db = x_sample.shape[0]
    gain_in = norm_in[0][None]
    nh = norm_hgrn[0][None]
    nf = norm_final[None]
    lbl = lb_logits.astype(F32)

    zr, zg_s, kvt_s, w_in_bf, wa, wb, wm, wo, w_mem_bf = _sample_inproj_call(
        x_sample, gain_in, w_in[0], [w[0] for w in (w_branch_a, w_branch_b, w_branch_m, w_out, w_mem_kv)])
    mkv, mkv_t = _mem_kv_call(mem_prompt, norm_mem[0][None], w_mem_bf)
    sample_ops = (zr,
                  *[_stored_view(c[0]) for c in (cache_w1_kv, cache_w2_kv, cache_w3_kv, cache_mem_kv)],
                  state_hgrn[0].reshape(db, HG_HEADS * HG_DIM, HG_DIM), lbl, nh.reshape(HG_HEADS, HG_DIM))
    qkv0, qkv1, qkv2, gq, hb, zg, fb, kvt0, kvt1, kvt2, u_first, nst_first = _inproj_call(
        x_prompt, gain_in, w_in_bf, sample_ops)
    og = [_dilated_call(qkv, dil, g) for g, (qkv, (_, dil)) in enumerate(zip((qkv0, qkv1, qkv2), WIN_GROUPS))]
    y_prompt, hg_prompt, u, nst = _prompt_tail_call(
        x_prompt, gq, hb, zg, fb, og[0], og[1], og[2], mkv, lbl, nh, nf, wa, wb, wm, wo,
        sample_ops, u_first, nst_first)
    pw = [_cache_from_stored(kvt) for kvt in (kvt0, kvt1, kvt2)]
    new_mem = _cache_from_stored(mkv_t)

    y_sample = _sample_out_call(x_sample, u, zg_s, nf, wa, wb, wm, wo)
    sw = [jnp.transpose(kvt_s[g].reshape(2, A_HEADS, A_HEAD_DIM, db), (3, 0, 1, 2))[None, :, None]
          for g in range(N_GROUPS)]
    new_hg_sample = nst.reshape(1, db, HG_HEADS, HG_DIM, HG_DIM)

    return (y_prompt, y_sample, pw[0], pw[1], pw[2], new_mem, hg_prompt[None],
            sw[0], sw[1], sw[2], new_hg_sample)
```

```python
import jax
import jax.numpy as jnp
from jax import lax
from jax.experimental import pallas as pl
from jax.experimental.pallas import tpu as pltpu

F32 = jnp.float32
BF16 = jnp.bfloat16

D_MODEL = 1024
WIN_GROUPS = ((128, 1), (512, 4), (2048, 16))
N_GROUPS = 3
A_HEADS = 4
A_HEAD_DIM = 64
D_A = 256
QBLK = 128
HG_HEADS = 4
HG_DIM = 128
D_B = 512
HG_CHUNK = 64
HG_SUB = 16
MEM_LEN = 256
D_M = 256
EPS = 1e-6
D_IN = 8192
LANES = 128
OFF_QA, OFF_KA, OFF_VA, OFF_GA = 0, 768, 1536, 2304
OFF_QB, OFF_FB, OFF_IB, OFF_GB = 2560, 3072, 3584, 4096
OFF_QM, OFF_GM, OFF_ZG = 4608, 4864, 5120
ATT_SCALE = A_HEAD_DIM ** -0.5

VMEM_LIMIT = 56 * 1024 * 1024

NT_DIMS = (((1,), (1,)), ((), ()))
TN_DIMS = (((0,), (0,)), ((), ()))


def _sigmoid(x):
    return 0.5 * jnp.tanh(0.5 * x) + 0.5


def _silu(x):
    h = 0.5 * x
    return h * jnp.tanh(h) + h


def _rms(x, gain):
    return x * lax.rsqrt(jnp.mean(x * x, axis=-1, keepdims=True) + EPS) * gain


def _head_masks(width, head_dim, n_heads):
    lane = lax.broadcasted_iota(jnp.int32, (1, width), 1)
    return [(lane >= h * head_dim) & (lane < (h + 1) * head_dim) for h in range(n_heads)]


def _stack_heads(q, masks):
    zero = jnp.zeros_like(q)
    return jnp.concatenate([jnp.where(m, q, zero) for m in masks], axis=0)


def _unstack_heads(o_all, masks, t):
    n = len(masks)
    out = o_all[(n - 1) * t:n * t]
    for h in range(n - 2, -1, -1):
        out = jnp.where(masks[h], o_all[h * t:(h + 1) * t], out)
    return out


def _expand_cols(cols, masks):
    out = jnp.broadcast_to(cols[-1], (cols[-1].shape[0], masks[0].shape[1]))
    for h in range(len(masks) - 2, -1, -1):
        out = jnp.where(masks[h], cols[h], out)
    return out


def _mem_kv_kernel(mem_ref, gain_ref, w_ref, out_ref, out_t_ref):
    hm = _rms(mem_ref[0], gain_ref[...]).astype(BF16)
    kv = jnp.dot(hm, w_ref[...], preferred_element_type=F32)
    out_ref[0] = kv
    out_t_ref[0] = kv.T


def _mem_kv_call(mem, gain, w_bf):
    b = mem.shape[0]
    return pl.pallas_call(
        _mem_kv_kernel,
        out_shape=(jax.ShapeDtypeStruct((b, MEM_LEN, 2 * D_M), F32),
                   jax.ShapeDtypeStruct((b, 2 * D_M, MEM_LEN), F32)),
        grid=(b,),
        in_specs=[pl.BlockSpec((1, MEM_LEN, D_MODEL), lambda i: (i, 0, 0)),
                  pl.BlockSpec((1, D_MODEL), lambda i: (0, 0)),
                  pl.BlockSpec((D_MODEL, 2 * D_M), lambda i: (0, 0))],
        out_specs=(pl.BlockSpec((1, MEM_LEN, 2 * D_M), lambda i: (i, 0, 0)),
                   pl.BlockSpec((1, 2 * D_M, MEM_LEN), lambda i: (i, 0, 0))),
        name="mem_kv",
    )(mem, gain, w_bf)


IN_TILE = 256
IN_CHUNK = 256
D_QKV = 3 * D_A
D_GQ = 3 * D_A
D_HB = 3 * D_B
D_ZG = 3 * D_MODEL


def _inproj_kernel(x_ref, gain_ref, w_ref, zr_ref, c1_ref, c2_ref, c3_ref, cm_ref, sst_ref, slbl_ref, nh4_ref,
                   qkv0_ref, qkv1_ref, qkv2_ref, gq_ref, hb_ref, zg_ref, fb_ref, kvt0_ref, kvt1_ref, kvt2_ref,
                   u_ref, nst_ref, scr_ref):
    qkv_refs = (qkv0_ref, qkv1_ref, qkv2_ref)
    kvt_refs = (kvt0_ref, kvt1_ref, kvt2_ref)
    t = IN_TILE
    h = _rms(x_ref[0], gain_ref[...]).astype(BF16)
    for c in range(D_IN // IN_CHUNK):
        lo = c * IN_CHUNK
        zc = jnp.dot(h, w_ref[:, lo:lo + IN_CHUNK], preferred_element_type=F32)
        if lo < OFF_GA:
            sec, g = divmod(c, N_GROUPS)
            dil = WIN_GROUPS[g][1]
            dst = slice(sec * D_A, (sec + 1) * D_A)
            if dil == 1:
                qkv_refs[g][0, 0, :, dst] = zc.astype(BF16)
            else:
                for k in range(IN_CHUNK // LANES):
                    scr_ref[k] = zc[:, k * LANES:(k + 1) * LANES]
                for r in range(dil):
                    qkv_refs[g][0, r, :, dst] = jnp.concatenate(
                        [scr_ref[k, pl.ds(r, t // dil, stride=dil), :] for k in range(IN_CHUNK // LANES)],
                        axis=1).astype(BF16)
            if sec > 0:
                width = kvt_refs[g].shape[2]
                kvt_refs[g][0, (sec - 1) * D_A:sec * D_A, :] = zc.T[:, t - width:]
        elif lo < OFF_QB:
            gq_ref[0, :, 0:D_A] = zc.astype(BF16)
        elif lo < OFF_FB:
            hb_ref[0, :, lo - OFF_QB:lo - OFF_QB + IN_CHUNK] = zc.astype(BF16)
        elif lo < OFF_IB:
            fb_ref[0, :, lo - OFF_FB:lo - OFF_FB + IN_CHUNK] = zc
        elif lo < OFF_QM:
            hb_ref[0, :, lo - OFF_IB + D_B:lo - OFF_IB + D_B + IN_CHUNK] = zc.astype(BF16)
        elif lo < OFF_ZG:
            gq_ref[0, :, lo - OFF_QM + D_A:lo - OFF_QM + D_A + IN_CHUNK] = zc.astype(BF16)
        else:
            zg_ref[0, :, lo - OFF_ZG:lo - OFF_ZG + IN_CHUNK] = zc.astype(BF16)
    _sample_rows(zr_ref, c1_ref, c2_ref, c3_ref, cm_ref, sst_ref, slbl_ref, nh4_ref, u_ref, nst_ref)


def _sample_specs(ops, nt, first_row):
    first_blk = first_row // S_TILE

    def rows(a):
        return pl.BlockSpec((S_TILE,) + a.shape[1:], lambda i, j: (first_blk + i * nt + j, 0, 0))

    def const(a):
        return pl.BlockSpec(a.shape, lambda i, j: (0, 0), pipeline_mode=pl.Buffered(1))

    return [rows(a) for a in ops[:6]] + [const(a) for a in ops[6:]]


def _inproj_call(x, gain, w_bf, sample_ops):
    b, s, _ = x.shape
    t = IN_TILE
    nt = s // t
    n_rows = b * nt * S_TILE

    def tok(width):
        return pl.BlockSpec((1, t, width), lambda i, j: (i, j, 0))

    def srow(shape):
        return pl.BlockSpec((S_TILE,) + shape, lambda i, j: (i * nt + j, 0, 0))

    qkv_shapes, qkv_specs = [], []
    for _, dil in WIN_GROUPS:
        qkv_shapes.append(jax.ShapeDtypeStruct((b, dil, s // dil, D_QKV), BF16))
        qkv_specs.append(pl.BlockSpec((1, dil, t // dil, D_QKV), lambda i, j: (i, 0, j, 0)))
    kvt_shapes, kvt_specs = [], []
    for window, _ in WIN_GROUPS:
        length = min(window, s)
        width = min(t, length)
        first = nt - length // width
        kvt_shapes.append(jax.ShapeDtypeStruct((b, 2 * D_A, length), F32))
        kvt_specs.append(pl.BlockSpec((1, 2 * D_A, width),
                                      lambda i, j, first=first: (i, 0, jnp.maximum(j - first, 0))))
    return pl.pallas_call(
        _inproj_kernel,
        out_shape=(*qkv_shapes,
                   jax.ShapeDtypeStruct((b, s, D_GQ), BF16),
                   jax.ShapeDtypeStruct((b, s, D_HB), BF16),
                   jax.ShapeDtypeStruct((b, s, D_ZG), BF16),
                   jax.ShapeDtypeStruct((b, s, D_B), F32),
                   *kvt_shapes,
                   jax.ShapeDtypeStruct((n_rows, U_ROWS, LANES), F32),
                   jax.ShapeDtypeStruct((n_rows, HG_HEADS * HG_DIM, HG_DIM), F32)),
        grid=(b, nt),
        in_specs=[tok(D_MODEL),
                  pl.BlockSpec((1, D_MODEL), lambda i, j: (0, 0)),
                  pl.BlockSpec((D_MODEL, D_IN), lambda i, j: (0, 0), pipeline_mode=pl.Buffered(1)),
                  *_sample_specs(sample_ops, nt, 0)],
        out_specs=(*qkv_specs, tok(D_GQ), tok(D_HB), tok(D_ZG), tok(D_B), *kvt_specs,
                   srow((U_ROWS, LANES)), srow((HG_HEADS * HG_DIM, HG_DIM))),
        scratch_shapes=[pltpu.VMEM((IN_CHUNK // LANES, t, LANES), F32)],
        compiler_params=pltpu.CompilerParams(
            dimension_semantics=("arbitrary", "arbitrary"), vmem_limit_bytes=VMEM_LIMIT),
        name="inproj",
    )(x, gain, w_bf, *sample_ops)


DIL_STEP = 2048
OG_SLABS = 2 * D_A // LANES


def _make_dilated_kernel(dil):
    nblk = DIL_STEP // (dil * QBLK)

    def body(prev_ref, cur_ref, out_ref):
        first_step = pl.program_id(1) == 0
        masks = _head_masks(D_A, A_HEAD_DIM, A_HEADS)
        qi = lax.broadcasted_iota(jnp.int32, (A_HEADS * QBLK, 2 * QBLK), 0) % QBLK
        kj = lax.broadcasted_iota(jnp.int32, (A_HEADS * QBLK, 2 * QBLK), 1)
        band = (kj >= qi) & (kj <= qi + QBLK)
        neg_band = jnp.where(band, 0.0, -jnp.inf)
        neg_first = jnp.where(first_step, jnp.where(band & (kj >= QBLK), 0.0, -jnp.inf), neg_band)

        def keys_values(r, jb, c):
            rows = slice(jb * QBLK, (jb + 1) * QBLK)
            if jb == 0:
                return jnp.concatenate([prev_ref[0, r, :, c * D_A:(c + 1) * D_A],
                                        cur_ref[0, r, rows, c * D_A:(c + 1) * D_A]], axis=0)
            return cur_ref[0, r, (jb - 1) * QBLK:(jb + 1) * QBLK, c * D_A:(c + 1) * D_A]

        for r, jb in [(r, jb) for r in range(dil) for jb in range(nblk)]:
            q = cur_ref[0, r, jb * QBLK:(jb + 1) * QBLK, 0:D_A]
            qs = _stack_heads(q * jnp.asarray(ATT_SCALE, BF16), masks)
            s = lax.dot_general(qs, keys_values(r, jb, 1), NT_DIMS, preferred_element_type=F32)
            s = s + (neg_first if jb == 0 else neg_band)
            mx = jnp.max(s, axis=-1, keepdims=True)
            p = jnp.exp(s - mx)
            l = jnp.sum(p, axis=-1, keepdims=True)
            o_all = jnp.dot(p.astype(BF16), keys_values(r, jb, 2), preferred_element_type=F32)
            l_e = _expand_cols([l[h * QBLK:(h + 1) * QBLK] for h in range(A_HEADS)], masks)
            mx_e = _expand_cols([mx[h * QBLK:(h + 1) * QBLK] for h in range(A_HEADS)], masks)
            res = (_unstack_heads(o_all, masks, QBLK) / l_e, mx_e + jnp.log(l_e))
            for k in range(OG_SLABS):
                src = res[k // 2][:, (k % 2) * LANES:(k % 2 + 1) * LANES]
                if dil == 1:
                    out_ref[0, k, jb * QBLK:(jb + 1) * QBLK, :] = src
                else:
                    out_ref[0, k, pl.ds(jb * QBLK * dil + r, QBLK, stride=dil), :] = src
    return body


def _dilated_call(qkv, dil, g):
    b, _, n, _ = qkv.shape
    rows = DIL_STEP // dil
    nb = n // rows
    return pl.pallas_call(
        _make_dilated_kernel(dil),
        out_shape=jax.ShapeDtypeStruct((b, OG_SLABS, n * dil, LANES), F32),
        grid=(b, nb),
        in_specs=[pl.BlockSpec((1, dil, QBLK, D_QKV),
                               lambda i, u: (i, 0, jnp.maximum(u * (rows // QBLK) - 1, 0), 0)),
                  pl.BlockSpec((1, dil, rows, D_QKV), lambda i, u: (i, 0, u, 0))],
        out_specs=pl.BlockSpec((1, OG_SLABS, DIL_STEP, LANES), lambda i, u: (i, 0, u, 0)),
        compiler_params=pltpu.CompilerParams(
            dimension_semantics=("arbitrary", "arbitrary"), vmem_limit_bytes=VMEM_LIMIT),
        name=f"dilated_attn_g{g}",
    )(qkv, qkv)


def _lower_bound(l0, l1):
    m = jnp.maximum(l0, l1)
    e0, e1 = jnp.exp(l0 - m), jnp.exp(l1 - m)
    return e0 / (e0 + e1)


def _head_rms(ob, gain):
    parts = []
    for h in range(HG_HEADS):
        sl = slice(h * HG_DIM, (h + 1) * HG_DIM)
        parts.append(_rms(ob[:, sl], gain[:, sl]))
    return jnp.concatenate(parts, axis=1)


def _merge_out(x, ua, ub, um, zg, wa_ref, wb_ref, wm_ref, wo_ref, nf):
    projs = [jnp.dot(u.astype(BF16), w_ref[...], preferred_element_type=F32)
             for u, w_ref in ((ua, wa_ref), (ub, wb_ref), (um, wm_ref))]
    gates = _sigmoid(zg.astype(BF16))
    merged = None
    for k, proj in enumerate(projs):
        term = gates[:, k * D_MODEL:(k + 1) * D_MODEL] * proj.astype(BF16)
        merged = term if merged is None else merged + term
    y = x + jnp.dot(merged, wo_ref[...], preferred_element_type=F32)
    return _rms(y, nf)


TAIL_TILE = 256


def _cumsum_rows(g, tri):
    g1 = g.astype(BF16)
    g2 = (g - g1.astype(F32)).astype(BF16)
    return jnp.dot(tri, g1, preferred_element_type=F32) + jnp.dot(tri, g2, preferred_element_type=F32)


def _hgrn_tile(fb, qb, v, lb, st_ref):
    t = fb.shape[0]
    c, sub = HG_CHUNK, HG_SUB
    n_sub = c // sub
    f = lb + (1.0 - lb) * _sigmoid(fb)
    kk = 1.0 - f
    row = lax.broadcasted_iota(jnp.int32, (t, t), 0)
    col = lax.broadcasted_iota(jnp.int32, (t, t), 1)
    same_chunk = (row // c) == (col // c)
    gcs = _cumsum_rows(jnp.log2(f), (same_chunk & (col <= row)).astype(BF16))
    q = _silu(qb).astype(F32)

    loc = lax.broadcasted_iota(jnp.int32, (c, 1), 0)
    qt_c, kt_ci, qe_c, kd_c, dec_c = [], [[] for _ in range(n_sub)], [], [], []
    for ci in range(t // c):
        rows = slice(ci * c, (ci + 1) * c)
        g_c, q_c, k_c = gcs[rows], q[rows], kk[rows]
        refs = [jnp.zeros((1, D_B), F32)] + [g_c[i * sub - 1:i * sub] for i in range(1, n_sub)]
        own = refs[n_sub - 1]
        for i in range(n_sub - 2, -1, -1):
            own = jnp.where(loc < (i + 1) * sub, refs[i], own)
        qt_c.append((q_c * jnp.exp2(g_c - own)).astype(BF16))
        k_own = k_c * jnp.exp2(own - g_c)
        for i in range(n_sub):
            pieces = []
            for jb in range(n_sub):
                piece = k_own[jb * sub:(jb + 1) * sub]
                if jb < i:
                    piece = piece * jnp.exp2(refs[i] - refs[jb])
                elif jb > i:
                    piece = jnp.zeros_like(piece)
                pieces.append(piece)
            kt_ci[i].append(jnp.concatenate(pieces, axis=0).astype(BF16))
        g_last = g_c[c - 1:c]
        qe_c.append((q_c * jnp.exp2(g_c)).astype(BF16))
        kd_c.append((k_c * jnp.exp2(g_last - g_c)).astype(BF16))
        dec_c.append(jnp.exp2(g_last))
    qt = jnp.concatenate(qt_c, axis=0)
    kts = [jnp.concatenate(parts, axis=0) for parts in kt_ci]
    sub_of_row = (lax.broadcasted_iota(jnp.int32, (t, 1), 0) % c) // sub
    causal = same_chunk & (col <= row)

    heads = [slice(h * HG_DIM, (h + 1) * HG_DIM) for h in range(HG_HEADS)]
    chunks = [slice(ci * c, (ci + 1) * c) for ci in range(t // c)]
    zero = jnp.zeros((t, HG_DIM), BF16)
    atts, incs = [], []
    for sl in heads:
        q_big = jnp.concatenate([jnp.where(sub_of_row == i, qt[:, sl], zero) for i in range(n_sub)], axis=1)
        k_big = jnp.concatenate([kts[i][:, sl] for i in range(n_sub)], axis=1)
        atts.append(lax.dot_general(q_big, k_big, NT_DIMS, preferred_element_type=F32))
        incs.append([lax.dot_general(v[rows, sl], kd_c[ci][:, sl], TN_DIMS, preferred_element_type=F32)
                     for ci, rows in enumerate(chunks)])
    states = []
    for h, sl in enumerate(heads):
        atts[h] = jnp.where(causal, atts[h], 0.0).astype(BF16)
        sts = [st_ref[h]]
        for ci in range(len(chunks)):
            sts.append(sts[-1] * dec_c[ci][:, sl] + incs[h][ci])
        st_ref[h] = sts[-1]
        states.append(sts)
    outs = []
    for h, sl in enumerate(heads):
        o_intra = jnp.dot(atts[h], v[:, sl], preferred_element_type=F32)
        inter = [lax.dot_general(qe_c[ci][:, sl], states[h][ci].astype(BF16), NT_DIMS, preferred_element_type=F32)
                 for ci in range(len(chunks))]
        outs.append(o_intra + jnp.concatenate(inter, axis=0))
    return jnp.concatenate(outs, axis=1)


def _prompt_tail_kernel(x_ref, gq_ref, hb_ref, zg_ref, fb_ref, o1_ref, o2_ref, o3_ref, mkv_ref, lbl_ref,
                        nh_ref, nf_ref, wa_ref, wb_ref, wm_ref, wo_ref,
                        zr_ref, c1_ref, c2_ref, c3_ref, cm_ref, sst_ref, slbl_ref, nh4_ref, ua_ref, nsta_ref,
                        y_ref, hst_ref, u_ref, nst_ref, st_ref):
    j = pl.program_id(1)
    t = TAIL_TILE

    @pl.when(j == 0)
    def _():
        st_ref[...] = jnp.zeros_like(st_ref)

    u_ref[0] = ua_ref[...]
    nst_ref[0] = nsta_ref[...]
    _sample_rows(zr_ref, c1_ref, c2_ref, c3_ref, cm_ref, sst_ref, slbl_ref, nh4_ref, u_ref.at[1], nst_ref.at[1])

    masks = _head_masks(D_M, A_HEAD_DIM, A_HEADS)
    qm = gq_ref[0, :, D_A:2 * D_A] * jnp.asarray(ATT_SCALE, BF16)
    qs = _stack_heads(qm, masks)
    mk = mkv_ref[0, :, 0:D_M].astype(BF16)
    mv = mkv_ref[0, :, D_M:2 * D_M].astype(BF16)
    s = lax.dot_general(qs, mk, NT_DIMS, preferred_element_type=F32)

    lb = _lower_bound(lbl_ref[0:1], lbl_ref[1:2])
    ob = _hgrn_tile(fb_ref[0], hb_ref[0, :, 0:D_B], hb_ref[0, :, D_B:2 * D_B], lb, st_ref)
    ub = _head_rms(ob, nh_ref[...]).astype(BF16) * _silu(hb_ref[0, :, 2 * D_B:3 * D_B])

    os_ = [jnp.concatenate([r[0, 0], r[0, 1]], axis=1) for r in (o1_ref, o2_ref, o3_ref)]
    ls_ = [jnp.concatenate([r[0, 2], r[0, 3]], axis=1) for r in (o1_ref, o2_ref, o3_ref)]
    mx = jnp.maximum(jnp.maximum(ls_[0], ls_[1]), ls_[2])
    es_ = [jnp.exp(l - mx) for l in ls_]
    oa = (es_[0] * os_[0] + es_[1] * os_[1] + es_[2] * os_[2]) / (es_[0] + es_[1] + es_[2])
    ua = oa.astype(BF16) * _silu(gq_ref[0, :, 0:D_A])

    p = jnp.exp(s - jnp.max(s, axis=-1, keepdims=True))
    l = jnp.sum(p, axis=-1, keepdims=True)
    om = _unstack_heads(jnp.dot(p.astype(BF16), mv, preferred_element_type=F32) / l, masks, t)
    um = om.astype(BF16) * _silu(gq_ref[0, :, 2 * D_A:3 * D_A])

    y_ref[0] = _merge_out(x_ref[0], ua, ub, um, zg_ref[0], wa_ref, wb_ref, wm_ref, wo_ref, nf_ref[...])

    @pl.when(j == pl.num_programs(1) - 1)
    def _():
        for h in range(HG_HEADS):
            hst_ref[0, h] = st_ref[h].T


def _prompt_tail_call(x, gq, hb, zg, fb, o1, o2, o3, mkv, lbl, nh, nf, wa, wb, wm, wo,
                      sample_ops, u_first, nst_first):
    b, s, _ = x.shape
    t = TAIL_TILE
    nt = s // t
    db = sample_ops[0].shape[0]
    half = u_first.shape[0]
    assert half + b * nt * S_TILE == db and 2 * half == db, "sample rows must split evenly over both grids"

    def tok(width):
        return pl.BlockSpec((1, t, width), lambda i, j: (i, j, 0))

    def const(shape):
        return pl.BlockSpec(shape, lambda i, j: (0,) * len(shape), pipeline_mode=pl.Buffered(1))

    def first(a):
        return pl.BlockSpec((S_TILE,) + a.shape[1:], lambda i, j: (i * nt + j, 0, 0))

    def halves(shape):
        return pl.BlockSpec((2, S_TILE) + shape, lambda i, j: (0, i * nt + j, 0, 0))

    y, hst, u, nst = pl.pallas_call(
        _prompt_tail_kernel,
        out_shape=(jax.ShapeDtypeStruct((b, s, D_MODEL), F32),
                   jax.ShapeDtypeStruct((b, HG_HEADS, HG_DIM, HG_DIM), F32),
                   jax.ShapeDtypeStruct((2, half, U_ROWS, LANES), F32),
                   jax.ShapeDtypeStruct((2, half, HG_HEADS * HG_DIM, HG_DIM), F32)),
        grid=(b, nt),
        in_specs=[tok(D_MODEL), tok(D_GQ), tok(D_HB), tok(D_ZG), tok(D_B),
                  *[pl.BlockSpec((1, OG_SLABS, t, LANES), lambda i, j: (i, 0, j, 0))] * N_GROUPS,
                  pl.BlockSpec((1, MEM_LEN, 2 * D_M), lambda i, j: (i, 0, 0)),
                  const((2, D_B)), const((1, D_B)), const((1, D_MODEL)),
                  const((D_A, D_MODEL)), const((D_B, D_MODEL)), const((D_M, D_MODEL)),
                  const((D_MODEL, D_MODEL)),
                  *_sample_specs(sample_ops, nt, half), first(u_first), first(nst_first)],
        out_specs=(tok(D_MODEL),
                   pl.BlockSpec((1, HG_HEADS, HG_DIM, HG_DIM), lambda i, j: (i, 0, 0, 0)),
                   halves((U_ROWS, LANES)), halves((HG_HEADS * HG_DIM, HG_DIM))),
        scratch_shapes=[pltpu.VMEM((HG_HEADS, HG_DIM, HG_DIM), F32)],
        compiler_params=pltpu.CompilerParams(
            dimension_semantics=("arbitrary", "arbitrary"), vmem_limit_bytes=VMEM_LIMIT),
        name="prompt_tail",
    )(x, gq, hb, zg, fb, o1, o2, o3, mkv, lbl, nh, nf, wa, wb, wm, wo, *sample_ops, u_first, nst_first)
    return y, hst, u.reshape(db, U_ROWS, LANES), nst.reshape(db, HG_HEADS * HG_DIM, HG_DIM)


S_COLS = 1024


def _sample_inproj_kernel(x_ref, gain_ref, w_ref, *refs):
    n_small = (len(refs) - 4) // 2
    small_in, (zr_ref, zg_ref, kvt_ref, wbf_ref), small_out = refs[:n_small], refs[n_small:n_small + 4], refs[n_small + 4:]
    for src, dst in zip(small_in, small_out):
        dst[...] = src[...].astype(BF16)
    c = pl.program_id(0)
    h = _rms(x_ref[:, 0, :], gain_ref[...]).astype(BF16)
    w = w_ref[...].astype(BF16)
    wbf_ref[...] = w
    z = jnp.dot(h, w, preferred_element_type=F32)
    for k in range(S_COLS // LANES):
        zr_ref[:, k, :] = z[:, k * LANES:(k + 1) * LANES]
    zg_ref[...] = z
    for step in range(OFF_GA // S_COLS + 1):
        pieces = [(g, kv, (OFF_KA + (kv * N_GROUPS + g) * D_A) % S_COLS)
                  for g in range(N_GROUPS) for kv in range(2)
                  if (OFF_KA + (kv * N_GROUPS + g) * D_A) // S_COLS == step]

        @pl.when(c == step)
        def _(pieces=pieces):
            zt = z.T
            for g, kv, row0 in pieces:
                kvt_ref[g, kv * D_A:(kv + 1) * D_A, :] = zt[row0:row0 + D_A]


def _sample_inproj_call(x, gain, w, small_weights):
    db = x.shape[0]
    steps = D_IN // S_COLS
    first_zg = OFF_ZG // S_COLS
    small_specs = [pl.BlockSpec((a.shape[0] // steps, a.shape[1]), lambda c: (c, 0)) for a in small_weights]
    return pl.pallas_call(
        _sample_inproj_kernel,
        out_shape=(jax.ShapeDtypeStruct((db, Z_ROWS, LANES), F32),
                   jax.ShapeDtypeStruct((db, D_ZG), F32),
                   jax.ShapeDtypeStruct((N_GROUPS, 2 * D_A, db), F32),
                   jax.ShapeDtypeStruct((D_MODEL, D_IN), BF16),
                   *[jax.ShapeDtypeStruct(a.shape, BF16) for a in small_weights]),
        grid=(steps,),
        in_specs=[pl.BlockSpec(x.shape, lambda c: (0, 0, 0)),
                  pl.BlockSpec((1, D_MODEL), lambda c: (0, 0)),
                  pl.BlockSpec((D_MODEL, S_COLS), lambda c: (0, c)),
                  *small_specs],
        out_specs=(pl.BlockSpec((db, S_COLS // LANES, LANES), lambda c: (0, c, 0)),
                   pl.BlockSpec((db, S_COLS), lambda c: (0, jnp.maximum(c - first_zg, 0))),
                   pl.BlockSpec((N_GROUPS, 2 * D_A, db), lambda c: (0, 0, 0)),
                   pl.BlockSpec((D_MODEL, S_COLS), lambda c: (0, c)),
                   *small_specs),
        compiler_params=pltpu.CompilerParams(dimension_semantics=("arbitrary",)),
        name="sample_inproj",
    )(x, gain, w, *small_weights)


S_TILE = 1
Z_ROWS = D_IN // LANES
U_ROWS = 8


def _sample_rows(zr_ref, c1_ref, c2_ref, c3_ref, cm_ref, st_ref, slbl_ref, nh_ref, u_ref, nst_ref):
    bt = S_TILE
    assert bt * Z_ROWS + HG_HEADS <= LANES
    lb_row = _lower_bound(slbl_ref[0:1], slbl_ref[1:2])
    lb_rows = [lb_row[:, h * HG_DIM:(h + 1) * HG_DIM] for h in range(HG_HEADS)]
    pad = jnp.zeros((LANES - bt * Z_ROWS - HG_HEADS, LANES), F32)
    zt = jnp.concatenate([zr_ref[b] for b in range(bt)] + lb_rows + [pad], axis=0).T

    def col(b, off, n=HG_DIM):
        j, l = divmod(off, LANES)
        return zt[l:l + n, Z_ROWS * b + j:Z_ROWS * b + j + 1]

    nh = nh_ref[...]

    def col2(b, off):
        return jnp.concatenate([col(b, off), col(b, off + LANES)], axis=0)

    def per_head(x):
        return jnp.concatenate([jnp.sum(x[h * A_HEAD_DIM:(h + 1) * A_HEAD_DIM], axis=0, keepdims=True)
                                for h in range(A_HEADS)], axis=0)

    def spread(x):
        return jnp.concatenate([jnp.broadcast_to(x[h:h + 1], (A_HEAD_DIM, 1)) for h in range(A_HEADS)], axis=0)

    def attend(b, srcs):
        parts, new_scores, qs = [], [], []
        for c_ref, dil, q_off, k_off, _ in srcs:
            q = col2(b, q_off) * ATT_SCALE
            qs.append(q)
            s = per_head(c_ref[b, 0:D_A, :] * q)
            if dil > 1:
                lane = lax.broadcasted_iota(jnp.int32, s.shape, 1)
                s = jnp.where(lane % dil == 0, s, -jnp.inf)
            parts.append(s)
            if k_off is not None:
                new_scores.append(per_head(col2(b, k_off) * q))
        if new_scores:
            lane = lax.broadcasted_iota(jnp.int32, (A_HEADS, LANES), 1)
            slab = jnp.full((A_HEADS, LANES), -jnp.inf, F32)
            for k, sn in enumerate(new_scores):
                slab = jnp.where(lane == k, sn, slab)
            parts.append(slab)
        s_all = jnp.concatenate(parts, axis=1)
        p_all = jnp.exp(s_all - jnp.max(s_all, axis=1, keepdims=True))
        den = jnp.sum(p_all, axis=1, keepdims=True)
        acc = [jnp.zeros((A_HEAD_DIM, LANES), F32) for _ in range(A_HEADS)]
        lo = 0
        for c_ref, _, _, _, _ in srcs:
            length = c_ref.shape[2]
            for k in range(length // LANES):
                ls = slice(k * LANES, (k + 1) * LANES)
                for h in range(A_HEADS):
                    vt = c_ref[b, D_A + h * A_HEAD_DIM:D_A + (h + 1) * A_HEAD_DIM, ls]
                    acc[h] = acc[h] + vt * p_all[h:h + 1, lo + k * LANES:lo + (k + 1) * LANES]
            lo += length
        num = jnp.sum(jnp.concatenate(acc, axis=0), axis=1, keepdims=True)
        k = 0
        for _, _, _, k_off, v_off in srcs:
            if k_off is not None:
                num = num + spread(p_all[:, lo + k:lo + k + 1]) * col2(b, v_off)
                k += 1
        return num / spread(den)

    window = [(c_ref, dil, OFF_QA + g * D_A, OFF_KA + g * D_A, OFF_VA + g * D_A)
              for g, (c_ref, (_, dil)) in enumerate(zip((c1_ref, c2_ref, c3_ref), WIN_GROUPS))][::-1]
    memory = [(cm_ref, 1, OFF_QM, None, None)]

    for b in range(bt):
        um = attend(b, memory) * _silu(col2(b, OFF_GM))
        ua = attend(b, window) * _silu(col2(b, OFF_GA))
        branch_cols = [ua[0:LANES], ua[LANES:2 * LANES], um[0:LANES], um[LANES:2 * LANES]]

        lane = lax.broadcasted_iota(jnp.int32, (LANES, LANES), 1)
        ut = jnp.zeros((LANES, LANES), F32)
        for k, cvec in enumerate(branch_cols):
            ut = ut + jnp.where(lane == k, cvec, 0.0)
        u_ref[b, 0:4, :] = ut.T[0:4, :]

        orows = []
        for h in range(HG_HEADS):
            sl = slice(h * HG_DIM, (h + 1) * HG_DIM)
            lb_col = zt[:, bt * Z_ROWS + h:bt * Z_ROWS + h + 1]
            fcol = lb_col + (1.0 - lb_col) * _sigmoid(col(b, OFF_FB + h * HG_DIM))
            qcol = _silu(col(b, OFF_QB + h * HG_DIM))
            vrow = zr_ref[b, OFF_IB // LANES + h:OFF_IB // LANES + h + 1, :]
            sn = fcol * st_ref[b, sl, :] + (1.0 - fcol) * vrow
            nst_ref[b, sl, :] = sn
            orows.append(jnp.sum(sn * qcol, axis=0, keepdims=True))
        gate = zr_ref[b, OFF_GB // LANES:OFF_GB // LANES + HG_HEADS, :]
        u_ref[b, 4:4 + HG_HEADS, :] = _rms(jnp.concatenate(orows, axis=0), nh) * _silu(gate)


def _stored_view(cache):
    db, length = cache.shape[0], cache.shape[1]
    return jnp.transpose(cache, (0, 2, 3, 4, 1)).reshape(db, 2 * D_A, length)


def _sample_out_kernel(x_ref, u_ref, zg_ref, nf_ref, wa_ref, wb_ref, wm_ref, wo_ref, y_ref):
    def rows(lo, hi):
        return jnp.concatenate([u_ref[:, k, :] for k in range(lo, hi)], axis=1)

    ua, um, ub = rows(0, 2), rows(2, 4), rows(4, U_ROWS)
    y_ref[:, 0, :] = _merge_out(x_ref[:, 0, :], ua, ub, um, zg_ref[...],
                                wa_ref, wb_ref, wm_ref, wo_ref, nf_ref[...])


def _sample_out_call(x, u, zg, nf, wa, wb, wm, wo):
    def full(a):
        return pl.BlockSpec(a.shape, lambda i: (0,) * a.ndim)

    args = (x, u, zg, nf, wa, wb, wm, wo)
    return pl.pallas_call(
        _sample_out_kernel,
        out_shape=jax.ShapeDtypeStruct(x.shape, F32),
        grid=(1,),
        in_specs=[full(a) for a in args],
        out_specs=full(x),
        compiler_params=pltpu.CompilerParams(vmem_limit_bytes=VMEM_LIMIT),
        name="sample_out",
    )(*args)


def _cache_from_stored(kvt):
    b, _, length = kvt.shape
    return jnp.transpose(kvt.reshape(b, 2, A_HEADS, A_HEAD_DIM, length), (0, 4, 1, 2, 3))[None]


def kernel(x_prompt, x_sample, mem_prompt, cache_w1_kv, cache_w2_kv, cache_w3_kv, cache_mem_kv, state_hgrn,
           norm_in, w_in, lb_logits, norm_hgrn, norm_mem, w_mem_kv, w_branch_a, w_branch_b, w_branch_m,
           w_out, norm_final):
    b, s, _ = x_prompt.shape
    db = x_sample.shape[0]
    gain_in = norm_in[0][None]
    nh = norm_hgrn[0][None]
    nf = norm_final[None]
    lbl = lb_logits.astype(F32)

    zr, zg_s, kvt_s, w_in_bf, wa, wb, wm, wo, w_mem_bf = _sample_inproj_call(
        x_sample, gain_in, w_in[0], [w[0] for w in (w_branch_a, w_branch_b, w_branch_m, w_out, w_mem_kv)])
    mkv, mkv_t = _mem_kv_call(mem_prompt, norm_mem[0][None], w_mem_bf)
    sample_ops = (zr,
                  *[_stored_view(c[0]) for c in (cache_w1_kv, cache_w2_kv, cache_w3_kv, cache_mem_kv)],
                  state_hgrn[0].reshape(db, HG_HEADS * HG_DIM, HG_DIM), lbl, nh.reshape(HG_HEADS, HG_DIM))
    qkv0, qkv1, qkv2, gq, hb, zg, fb, kvt0, kvt1, kvt2, u_first, nst_first = _inproj_call(
        x_prompt, gain_in, w_in_bf, sample_ops)
    og = [_dilated_call(qkv, dil, g) for g, (qkv, (_, dil)) in enumerate(zip((qkv0, qkv1, qkv2), WIN_GROUPS))]
    y_prompt, hg_prompt, u, nst = _prompt_tail_call(
        x_prompt, gq, hb, zg, fb, og[0], og[1], og[2], mkv, lbl, nh, nf, wa, wb, wm, wo,
        sample_ops, u_first, nst_first)
    pw = [_cache_from_stored(kvt) for kvt in (kvt0, kvt1, kvt2)]
    new_mem = _cache_from_stored(mkv_t)

    y_sample = _sample_out_call(x_sample, u, zg_s, nf, wa, wb, wm, wo)
    sw = [jnp.transpose(kvt_s[g].reshape(2, A_HEADS, A_HEAD_DIM, db), (3, 0, 1, 2))[None, :, None]
          for g in range(N_GROUPS)]
    new_hg_sample = nst.reshape(1, db, HG_HEADS, HG_DIM, HG_DIM)

    return (y_prompt, y_sample, pw[0], pw[1], pw[2], new_mem, hg_prompt[None],
            sw[0], sw[1], sw[2], new_hg_sample)
```

```python
import jax
import jax.numpy as jnp
from jax import lax
from jax.experimental import pallas as pl
from jax.experimental.pallas import tpu as pltpu

F32 = jnp.float32
BF16 = jnp.bfloat16

D_MODEL = 1024
WIN_GROUPS = ((128, 1), (512, 4), (2048, 16))
N_GROUPS = 3
A_HEADS = 4
A_HEAD_DIM = 64
D_A = 256
QBLK = 128
HG_HEADS = 4
HG_DIM = 128
D_B = 512
HG_CHUNK = 64
HG_SUB = 16
MEM_LEN = 256
D_M = 256
EPS = 1e-6
D_IN = 8192
LANES = 128
OFF_QA, OFF_KA, OFF_VA, OFF_GA = 0, 768, 1536, 2304
OFF_QB, OFF_FB, OFF_IB, OFF_GB = 2560, 3072, 3584, 4096
OFF_QM, OFF_GM, OFF_ZG = 4608, 4864, 5120
ATT_SCALE = A_HEAD_DIM ** -0.5

VMEM_LIMIT = 56 * 1024 * 1024

NT_DIMS = (((1,), (1,)), ((), ()))
TN_DIMS = (((0,), (0,)), ((), ()))


def _sigmoid(x):
    return 0.5 * jnp.tanh(0.5 * x) + 0.5


def _silu(x):
    h = 0.5 * x
    return h * jnp.tanh(h) + h


def _rms(x, gain):
    return x * lax.rsqrt(jnp.mean(x * x, axis=-1, keepdims=True) + EPS) * gain


def _head_masks(width, head_dim, n_heads):
    lane = lax.broadcasted_iota(jnp.int32, (1, width), 1)
    return [(lane >= h * head_dim) & (lane < (h + 1) * head_dim) for h in range(n_heads)]


def _stack_heads(q, masks):
    zero = jnp.zeros_like(q)
    return jnp.concatenate([jnp.where(m, q, zero) for m in masks], axis=0)


def _unstack_heads(o_all, masks, t):
    n = len(masks)
    out = o_all[(n - 1) * t:n * t]
    for h in range(n - 2, -1, -1):
        out = jnp.where(masks[h], o_all[h * t:(h + 1) * t], out)
    return out


def _expand_cols(cols, masks):
    out = jnp.broadcast_to(cols[-1], (cols[-1].shape[0], masks[0].shape[1]))
    for h in range(len(masks) - 2, -1, -1):
        out = jnp.where(masks[h], cols[h], out)
    return out


def _mem_kv_kernel(mem_ref, gain_ref, w_ref, out_ref, out_t_ref):
    hm = _rms(mem_ref[0], gain_ref[...]).astype(BF16)
    kv = jnp.dot(hm, w_ref[...], preferred_element_type=F32)
    out_ref[0] = kv
    out_t_ref[0] = kv.T


def _mem_kv_call(mem, gain, w_bf):
    b = mem.shape[0]
    return pl.pallas_call(
        _mem_kv_kernel,
        out_shape=(jax.ShapeDtypeStruct((b, MEM_LEN, 2 * D_M), F32),
                   jax.ShapeDtypeStruct((b, 2 * D_M, MEM_LEN), F32)),
        grid=(b,),
        in_specs=[pl.BlockSpec((1, MEM_LEN, D_MODEL), lambda i: (i, 0, 0)),
                  pl.BlockSpec((1, D_MODEL), lambda i: (0, 0)),
                  pl.BlockSpec((D_MODEL, 2 * D_M), lambda i: (0, 0))],
        out_specs=(pl.BlockSpec((1, MEM_LEN, 2 * D_M), lambda i: (i, 0, 0)),
                   pl.BlockSpec((1, 2 * D_M, MEM_LEN), lambda i: (i, 0, 0))),
        name="mem_kv",
    )(mem, gain, w_bf)


IN_TILE = 256
IN_CHUNK = 256
D_QKV = 3 * D_A
D_GQ = 3 * D_A
D_HB = 3 * D_B
D_ZG = 3 * D_MODEL


def _inproj_kernel(x_ref, gain_ref, w_ref, zr_ref, c1_ref, c2_ref, c3_ref, cm_ref, sst_ref, slbl_ref, nh4_ref,
                   qkv0_ref, qkv1_ref, qkv2_ref, gq_ref, hb_ref, zg_ref, fb_ref, kvt0_ref, kvt1_ref, kvt2_ref,
                   u_ref, nst_ref, scr_ref):
    qkv_refs = (qkv0_ref, qkv1_ref, qkv2_ref)
    kvt_refs = (kvt0_ref, kvt1_ref, kvt2_ref)
    t = IN_TILE
    h = _rms(x_ref[0], gain_ref[...]).astype(BF16)
    for c in range(D_IN // IN_CHUNK):
        lo = c * IN_CHUNK
        zc = jnp.dot(h, w_ref[:, lo:lo + IN_CHUNK], preferred_element_type=F32)
        if lo < OFF_GA:
            sec, g = divmod(c, N_GROUPS)
            dil = WIN_GROUPS[g][1]
            dst = slice(sec * D_A, (sec + 1) * D_A)
            if dil == 1:
                qkv_refs[g][0, 0, :, dst] = zc.astype(BF16)
            else:
                for k in range(IN_CHUNK // LANES):
                    scr_ref[k] = zc[:, k * LANES:(k + 1) * LANES]
                for r in range(dil):
                    qkv_refs[g][0, r, :, dst] = jnp.concatenate(
                        [scr_ref[k, pl.ds(r, t // dil, stride=dil), :] for k in range(IN_CHUNK // LANES)],
                        axis=1).astype(BF16)
            if sec > 0:
                width = kvt_refs[g].shape[2]
                kvt_refs[g][0, (sec - 1) * D_A:sec * D_A, :] = zc.T[:, t - width:]
        elif lo < OFF_QB:
            gq_ref[0, :, 0:D_A] = zc.astype(BF16)
        elif lo < OFF_FB:
            hb_ref[0, :, lo - OFF_QB:lo - OFF_QB + IN_CHUNK] = zc.astype(BF16)
        elif lo < OFF_IB:
            fb_ref[0, :, lo - OFF_FB:lo - OFF_FB + IN_CHUNK] = zc
        elif lo < OFF_QM:
            hb_ref[0, :, lo - OFF_IB + D_B:lo - OFF_IB + D_B + IN_CHUNK] = zc.astype(BF16)
        elif lo < OFF_ZG:
            gq_ref[0, :, lo - OFF_QM + D_A:lo - OFF_QM + D_A + IN_CHUNK] = zc.astype(BF16)
        else:
            zg_ref[0, :, lo - OFF_ZG:lo - OFF_ZG + IN_CHUNK] = zc.astype(BF16)
    _sample_rows(zr_ref, c1_ref, c2_ref, c3_ref, cm_ref, sst_ref, slbl_ref, nh4_ref, u_ref, nst_ref)


def _sample_specs(ops, nt, first_row):
    first_blk = first_row // S_TILE

    def rows(a):
        return pl.BlockSpec((S_TILE,) + a.shape[1:], lambda i, j: (first_blk + i * nt + j, 0, 0))

    def const(a):
        return pl.BlockSpec(a.shape, lambda i, j: (0, 0), pipeline_mode=pl.Buffered(1))

    return [rows(a) for a in ops[:6]] + [const(a) for a in ops[6:]]


def _inproj_call(x, gain, w_bf, sample_ops):
    b, s, _ = x.shape
    t = IN_TILE
    nt = s // t
    n_rows = b * nt * S_TILE

    def tok(width):
        return pl.BlockSpec((1, t, width), lambda i, j: (i, j, 0))

    def srow(shape):
        return pl.BlockSpec((S_TILE,) + shape, lambda i, j: (i * nt + j, 0, 0))

    qkv_shapes, qkv_specs = [], []
    for _, dil in WIN_GROUPS:
        qkv_shapes.append(jax.ShapeDtypeStruct((b, dil, s // dil, D_QKV), BF16))
        qkv_specs.append(pl.BlockSpec((1, dil, t // dil, D_QKV), lambda i, j: (i, 0, j, 0)))
    kvt_shapes, kvt_specs = [], []
    for window, _ in WIN_GROUPS:
        length = min(window, s)
        width = min(t, length)
        first = nt - length // width
        kvt_shapes.append(jax.ShapeDtypeStruct((b, 2 * D_A, length), F32))
        kvt_specs.append(pl.BlockSpec((1, 2 * D_A, width),
                                      lambda i, j, first=first: (i, 0, jnp.maximum(j - first, 0))))
    return pl.pallas_call(
        _inproj_kernel,
        out_shape=(*qkv_shapes,
                   jax.ShapeDtypeStruct((b, s, D_GQ), BF16),
                   jax.ShapeDtypeStruct((b, s, D_HB), BF16),
                   jax.ShapeDtypeStruct((b, s, D_ZG), BF16),
                   jax.ShapeDtypeStruct((b, s, D_B), F32),
                   *kvt_shapes,
                   jax.ShapeDtypeStruct((n_rows, U_ROWS, LANES), F32),
                   jax.ShapeDtypeStruct((n_rows, HG_HEADS * HG_DIM, HG_DIM), F32)),
        grid=(b, nt),
        in_specs=[tok(D_MODEL),
                  pl.BlockSpec((1, D_MODEL), lambda i, j: (0, 0)),
                  pl.BlockSpec((D_MODEL, D_IN), lambda i, j: (0, 0), pipeline_mode=pl.Buffered(1)),
                  *_sample_specs(sample_ops, nt, 0)],
        out_specs=(*qkv_specs, tok(D_GQ), tok(D_HB), tok(D_ZG), tok(D_B), *kvt_specs,
                   srow((U_ROWS, LANES)), srow((HG_HEADS * HG_DIM, HG_DIM))),
        scratch_shapes=[pltpu.VMEM((IN_CHUNK // LANES, t, LANES), F32)],
        compiler_params=pltpu.CompilerParams(
            dimension_semantics=("arbitrary", "arbitrary"), vmem_limit_bytes=VMEM_LIMIT),
        name="inproj",
    )(x, gain, w_bf, *sample_ops)


DIL_STEP = 2048
OG_SLABS = 2 * D_A // LANES


def _make_dilated_kernel(dil):
    nblk = DIL_STEP // (dil * QBLK)

    def body(prev_ref, cur_ref, out_ref):
        first_step = pl.program_id(1) == 0
        masks = _head_masks(D_A, A_HEAD_DIM, A_HEADS)
        qi = lax.broadcasted_iota(jnp.int32, (A_HEADS * QBLK, 2 * QBLK), 0) % QBLK
        kj = lax.broadcasted_iota(jnp.int32, (A_HEADS * QBLK, 2 * QBLK), 1)
        band = (kj >= qi) & (kj <= qi + QBLK)
        neg_band = jnp.where(band, 0.0, -jnp.inf)
        neg_first = jnp.where(first_step, jnp.where(band & (kj >= QBLK), 0.0, -jnp.inf), neg_band)

        def keys_values(r, jb, c):
            rows = slice(jb * QBLK, (jb + 1) * QBLK)
            if jb == 0:
                return jnp.concatenate([prev_ref[0, r, :, c * D_A:(c + 1) * D_A],
                                        cur_ref[0, r, rows, c * D_A:(c + 1) * D_A]], axis=0)
            return cur_ref[0, r, (jb - 1) * QBLK:(jb + 1) * QBLK, c * D_A:(c + 1) * D_A]

        for r, jb in [(r, jb) for r in range(dil) for jb in range(nblk)]:
            q = cur_ref[0, r, jb * QBLK:(jb + 1) * QBLK, 0:D_A]
            qs = _stack_heads(q * jnp.asarray(ATT_SCALE, BF16), masks)
            s = lax.dot_general(qs, keys_values(r, jb, 1), NT_DIMS, preferred_element_type=F32)
            s = s + (neg_first if jb == 0 else neg_band)
            mx = jnp.max(s, axis=-1, keepdims=True)
            p = jnp.exp(s - mx)
            l = jnp.sum(p, axis=-1, keepdims=True)
            o_all = jnp.dot(p.astype(BF16), keys_values(r, jb, 2), preferred_element_type=F32)
            l_e = _expand_cols([l[h * QBLK:(h + 1) * QBLK] for h in range(A_HEADS)], masks)
            mx_e = _expand_cols([mx[h * QBLK:(h + 1) * QBLK] for h in range(A_HEADS)], masks)
            res = (_unstack_heads(o_all, masks, QBLK) / l_e, mx_e + jnp.log(l_e))
            for k in range(OG_SLABS):
                src = res[k // 2][:, (k % 2) * LANES:(k % 2 + 1) * LANES]
                if dil == 1:
                    out_ref[0, k, jb * QBLK:(jb + 1) * QBLK, :] = src
                else:
                    out_ref[0, k, pl.ds(jb * QBLK * dil + r, QBLK, stride=dil), :] = src
    return body


def _dilated_call(qkv, dil, g):
    b, _, n, _ = qkv.shape
    rows = DIL_STEP // dil
    nb = n // rows
    return pl.pallas_call(
        _make_dilated_kernel(dil),
        out_shape=jax.ShapeDtypeStruct((b, OG_SLABS, n * dil, LANES), F32),
        grid=(b, nb),
        in_specs=[pl.BlockSpec((1, dil, QBLK, D_QKV),
                               lambda i, u: (i, 0, jnp.maximum(u * (rows // QBLK) - 1, 0), 0)),
                  pl.BlockSpec((1, dil, rows, D_QKV), lambda i, u: (i, 0, u, 0))],
        out_specs=pl.BlockSpec((1, OG_SLABS, DIL_STEP, LANES), lambda i, u: (i, 0, u, 0)),
        compiler_params=pltpu.CompilerParams(
            dimension_semantics=("arbitrary", "arbitrary"), vmem_limit_bytes=VMEM_LIMIT),
        name=f"dilated_attn_g{g}",
    )(qkv, qkv)


def _lower_bound(l0, l1):
    m = jnp.maximum(l0, l1)
    e0, e1 = jnp.exp(l0 - m), jnp.exp(l1 - m)
    return e0 / (e0 + e1)


def _head_rms(ob, gain):
    parts = []
    for h in range(HG_HEADS):
        sl = slice(h * HG_DIM, (h + 1) * HG_DIM)
        parts.append(_rms(ob[:, sl], gain[:, sl]))
    return jnp.concatenate(parts, axis=1)


def _merge_out(x, ua, ub, um, zg, wa_ref, wb_ref, wm_ref, wo_ref, nf):
    projs = [jnp.dot(u.astype(BF16), w_ref[...], preferred_element_type=F32)
             for u, w_ref in ((ua, wa_ref), (ub, wb_ref), (um, wm_ref))]
    gates = _sigmoid(zg.astype(BF16))
    merged = None
    for k, proj in enumerate(projs):
        term = gates[:, k * D_MODEL:(k + 1) * D_MODEL] * proj.astype(BF16)
        merged = term if merged is None else merged + term
    y = x + jnp.dot(merged, wo_ref[...], preferred_element_type=F32)
    return _rms(y, nf)


TAIL_TILE = 256


def _cumsum_rows(g, tri):
    g1 = g.astype(BF16)
    g2 = (g - g1.astype(F32)).astype(BF16)
    return jnp.dot(tri, g1, preferred_element_type=F32) + jnp.dot(tri, g2, preferred_element_type=F32)


def _hgrn_tile(fb, qb, v, lb, st_ref):
    t = fb.shape[0]
    c, sub = HG_CHUNK, HG_SUB
    n_sub = c // sub
    f = lb + (1.0 - lb) * _sigmoid(fb)
    kk = 1.0 - f
    row = lax.broadcasted_iota(jnp.int32, (t, t), 0)
    col = lax.broadcasted_iota(jnp.int32, (t, t), 1)
    same_chunk = (row // c) == (col // c)
    gcs = _cumsum_rows(jnp.log2(f), (same_chunk & (col <= row)).astype(BF16))
    q = _silu(qb).astype(F32)

    loc = lax.broadcasted_iota(jnp.int32, (c, 1), 0)
    qt_c, kt_ci, qe_c, kd_c, dec_c = [], [[] for _ in range(n_sub)], [], [], []
    for ci in range(t // c):
        rows = slice(ci * c, (ci + 1) * c)
        g_c, q_c, k_c = gcs[rows], q[rows], kk[rows]
        refs = [jnp.zeros((1, D_B), F32)] + [g_c[i * sub - 1:i * sub] for i in range(1, n_sub)]
        own = refs[n_sub - 1]
        for i in range(n_sub - 2, -1, -1):
            own = jnp.where(loc < (i + 1) * sub, refs[i], own)
        qt_c.append((q_c * jnp.exp2(g_c - own)).astype(BF16))
        k_own = k_c * jnp.exp2(own - g_c)
        for i in range(n_sub):
            pieces = []
            for jb in range(n_sub):
                piece = k_own[jb * sub:(jb + 1) * sub]
                if jb < i:
                    piece = piece * jnp.exp2(refs[i] - refs[jb])
                elif jb > i:
                    piece = jnp.zeros_like(piece)
                pieces.append(piece)
            kt_ci[i].append(jnp.concatenate(pieces, axis=0).astype(BF16))
        g_last = g_c[c - 1:c]
        qe_c.append((q_c * jnp.exp2(g_c)).astype(BF16))
        kd_c.append((k_c * jnp.exp2(g_last - g_c)).astype(BF16))
        dec_c.append(jnp.exp2(g_last))
    qt = jnp.concatenate(qt_c, axis=0)
    kts = [jnp.concatenate(parts, axis=0) for parts in kt_ci]
    sub_of_row = (lax.broadcasted_iota(jnp.int32, (t, 1), 0) % c) // sub
    causal = same_chunk & (col <= row)

    heads = [slice(h * HG_DIM, (h + 1) * HG_DIM) for h in range(HG_HEADS)]
    chunks = [slice(ci * c, (ci + 1) * c) for ci in range(t // c)]
    zero = jnp.zeros((t, HG_DIM), BF16)
    atts, incs = [], []
    for sl in heads:
        q_big = jnp.concatenate([jnp.where(sub_of_row == i, qt[:, sl], zero) for i in range(n_sub)], axis=1)
        k_big = jnp.concatenate([kts[i][:, sl] for i in range(n_sub)], axis=1)
        atts.append(lax.dot_general(q_big, k_big, NT_DIMS, preferred_element_type=F32))
        incs.append([lax.dot_general(v[rows, sl], kd_c[ci][:, sl], TN_DIMS, preferred_element_type=F32)
                     for ci, rows in enumerate(chunks)])
    states = []
    for h, sl in enumerate(heads):
        atts[h] = jnp.where(causal, atts[h], 0.0).astype(BF16)
        sts = [st_ref[h]]
        for ci in range(len(chunks)):
            sts.append(sts[-1] * dec_c[ci][:, sl] + incs[h][ci])
        st_ref[h] = sts[-1]
        states.append(sts)
    outs = []
    for h, sl in enumerate(heads):
        o_intra = jnp.dot(atts[h], v[:, sl], preferred_element_type=F32)
        inter = [lax.dot_general(qe_c[ci][:, sl], states[h][ci].astype(BF16), NT_DIMS, preferred_element_type=F32)
                 for ci in range(len(chunks))]
        outs.append(o_intra + jnp.concatenate(inter, axis=0))
    return jnp.concatenate(outs, axis=1)


def _prompt_tail_kernel(x_ref, gq_ref, hb_ref, zg_ref, fb_ref, o1_ref, o2_ref, o3_ref, mkv_ref, lbl_ref,
                        nh_ref, nf_ref, wa_ref, wb_ref, wm_ref, wo_ref,
                        zr_ref, c1_ref, c2_ref, c3_ref, cm_ref, sst_ref, slbl_ref, nh4_ref, ua_ref, nsta_ref,
                        y_ref, hst_ref, u_ref, nst_ref, st_ref):
    j = pl.program_id(1)
    t = TAIL_TILE

    @pl.when(j == 0)
    def _():
        st_ref[...] = jnp.zeros_like(st_ref)

    u_ref[0] = ua_ref[...]
    nst_ref[0] = nsta_ref[...]
    _sample_rows(zr_ref, c1_ref, c2_ref, c3_ref, cm_ref, sst_ref, slbl_ref, nh4_ref, u_ref.at[1], nst_ref.at[1])

    masks = _head_masks(D_M, A_HEAD_DIM, A_HEADS)
    qm = gq_ref[0, :, D_A:2 * D_A] * jnp.asarray(ATT_SCALE, BF16)
    qs = _stack_heads(qm, masks)
    mk = mkv_ref[0, :, 0:D_M].astype(BF16)
    mv = mkv_ref[0, :, D_M:2 * D_M].astype(BF16)
    s = lax.dot_general(qs, mk, NT_DIMS, preferred_element_type=F32)

    lb = _lower_bound(lbl_ref[0:1], lbl_ref[1:2])
    ob = _hgrn_tile(fb_ref[0], hb_ref[0, :, 0:D_B], hb_ref[0, :, D_B:2 * D_B], lb, st_ref)
    ub = _head_rms(ob, nh_ref[...]).astype(BF16) * _silu(hb_ref[0, :, 2 * D_B:3 * D_B])

    os_ = [jnp.concatenate([r[0, 0], r[0, 1]], axis=1) for r in (o1_ref, o2_ref, o3_ref)]
    ls_ = [jnp.concatenate([r[0, 2], r[0, 3]], axis=1) for r in (o1_ref, o2_ref, o3_ref)]
    mx = jnp.maximum(jnp.maximum(ls_[0], ls_[1]), ls_[2])
    es_ = [jnp.exp(l - mx) for l in ls_]
    oa = (es_[0] * os_[0] + es_[1] * os_[1] + es_[2] * os_[2]) / (es_[0] + es_[1] + es_[2])
    ua = oa.astype(BF16) * _silu(gq_ref[0, :, 0:D_A])

    p = jnp.exp(s - jnp.max(s, axis=-1, keepdims=True))
    l = jnp.sum(p, axis=-1, keepdims=True)
    om = _unstack_heads(jnp.dot(p.astype(BF16), mv, preferred_element_type=F32) / l, masks, t)
    um = om.astype(BF16) * _silu(gq_ref[0, :, 2 * D_A:3 * D_A])

    y_ref[0] = _merge_out(x_ref[0], ua, ub, um, zg_ref[0], wa_ref, wb_ref, wm_ref, wo_ref, nf_ref[...])

    @pl.when(j == pl.num_programs(1) - 1)
    def _():
        for h in range(HG_HEADS):
            hst_ref[0, h] = st_ref[h].T


def _prompt_tail_call(x, gq, hb, zg, fb, o1, o2, o3, mkv, lbl, nh, nf, wa, wb, wm, wo,
                      sample_ops, u_first, nst_first):
    b, s, _ = x.shape
    t = TAIL_TILE
    nt = s // t
    db = sample_ops[0].shape[0]
    half = u_first.shape[0]
    assert half + b * nt * S_TILE == db and 2 * half == db, "sample rows must split evenly over both grids"

    def tok(width):
        return pl.BlockSpec((1, t, width), lambda i, j: (i, j, 0))

    def const(shape):
        return pl.BlockSpec(shape, lambda i, j: (0,) * len(shape), pipeline_mode=pl.Buffered(1))

    def first(a):
        return pl.BlockSpec((S_TILE,) + a.shape[1:], lambda i, j: (i * nt + j, 0, 0))

    def halves(shape):
        return pl.BlockSpec((2, S_TILE) + shape, lambda i, j: (0, i * nt + j, 0, 0))

    y, hst, u, nst = pl.pallas_call(
        _prompt_tail_kernel,
        out_shape=(jax.ShapeDtypeStruct((b, s, D_MODEL), F32),
                   jax.ShapeDtypeStruct((b, HG_HEADS, HG_DIM, HG_DIM), F32),
                   jax.ShapeDtypeStruct((2, half, U_ROWS, LANES), F32),
                   jax.ShapeDtypeStruct((2, half, HG_HEADS * HG_DIM, HG_DIM), F32)),
        grid=(b, nt),
        in_specs=[tok(D_MODEL), tok(D_GQ), tok(D_HB), tok(D_ZG), tok(D_B),
                  *[pl.BlockSpec((1, OG_SLABS, t, LANES), lambda i, j: (i, 0, j, 0))] * N_GROUPS,
                  pl.BlockSpec((1, MEM_LEN, 2 * D_M), lambda i, j: (i, 0, 0)),
                  const((2, D_B)), const((1, D_B)), const((1, D_MODEL)),
                  const((D_A, D_MODEL)), const((D_B, D_MODEL)), const((D_M, D_MODEL)),
                  const((D_MODEL, D_MODEL)),
                  *_sample_specs(sample_ops, nt, half), first(u_first), first(nst_first)],
        out_specs=(tok(D_MODEL),
                   pl.BlockSpec((1, HG_HEADS, HG_DIM, HG_DIM), lambda i, j: (i, 0, 0, 0)),
                   halves((U_ROWS, LANES)), halves((HG_HEADS * HG_DIM, HG_DIM))),
        scratch_shapes=[pltpu.VMEM((HG_HEADS, HG_DIM, HG_DIM), F32)],
        compiler_params=pltpu.CompilerParams(
            dimension_semantics=("arbitrary", "arbitrary"), vmem_limit_bytes=VMEM_LIMIT),
        name="prompt_tail",
    )(x, gq, hb, zg, fb, o1, o2, o3, mkv, lbl, nh, nf, wa, wb, wm, wo, *sample_ops, u_first, nst_first)
    return y, hst, u.reshape(db, U_ROWS, LANES), nst.reshape(db, HG_HEADS * HG_DIM, HG_DIM)


S_COLS = 1024


def _sample_inproj_kernel(x_ref, gain_ref, w_ref, *refs):
    n_small = (len(refs) - 4) // 2
    small_in, (zr_ref, zg_ref, kvt_ref, wbf_ref), small_out = refs[:n_small], refs[n_small:n_small + 4], refs[n_small + 4:]
    for src, dst in zip(small_in, small_out):
        dst[...] = src[...].astype(BF16)
    c = pl.program_id(0)
    h = _rms(x_ref[:, 0, :], gain_ref[...]).astype(BF16)
    w = w_ref[...].astype(BF16)
    wbf_ref[...] = w
    z = jnp.dot(h, w, preferred_element_type=F32)
    for k in range(S_COLS // LANES):
        zr_ref[:, k, :] = z[:, k * LANES:(k + 1) * LANES]
    zg_ref[...] = z
    for step in range(OFF_GA // S_COLS + 1):
        pieces = [(g, kv, (OFF_KA + (kv * N_GROUPS + g) * D_A) % S_COLS)
                  for g in range(N_GROUPS) for kv in range(2)
                  if (OFF_KA + (kv * N_GROUPS + g) * D_A) // S_COLS == step]

        @pl.when(c == step)
        def _(pieces=pieces):
            zt = z.T
            for g, kv, row0 in pieces:
                kvt_ref[g, kv * D_A:(kv + 1) * D_A, :] = zt[row0:row0 + D_A]


def _sample_inproj_call(x, gain, w, small_weights):
    db = x.shape[0]
    steps = D_IN // S_COLS
    first_zg = OFF_ZG // S_COLS
    small_specs = [pl.BlockSpec((a.shape[0] // steps, a.shape[1]), lambda c: (c, 0)) for a in small_weights]
    return pl.pallas_call(
        _sample_inproj_kernel,
        out_shape=(jax.ShapeDtypeStruct((db, Z_ROWS, LANES), F32),
                   jax.ShapeDtypeStruct((db, D_ZG), F32),
                   jax.ShapeDtypeStruct((N_GROUPS, 2 * D_A, db), F32),
                   jax.ShapeDtypeStruct((D_MODEL, D_IN), BF16),
                   *[jax.ShapeDtypeStruct(a.shape, BF16) for a in small_weights]),
        grid=(steps,),
        in_specs=[pl.BlockSpec(x.shape, lambda c: (0, 0, 0)),
                  pl.BlockSpec((1, D_MODEL), lambda c: (0, 0)),
                  pl.BlockSpec((D_MODEL, S_COLS), lambda c: (0, c)),
                  *small_specs],
        out_specs=(pl.BlockSpec((db, S_COLS // LANES, LANES), lambda c: (0, c, 0)),
                   pl.BlockSpec((db, S_COLS), lambda c: (0, jnp.maximum(c - first_zg, 0))),
                   pl.BlockSpec((N_GROUPS, 2 * D_A, db), lambda c: (0, 0, 0)),
                   pl.BlockSpec((D_MODEL, S_COLS), lambda c: (0, c)),
                   *small_specs),
        compiler_params=pltpu.CompilerParams(dimension_semantics=("arbitrary",)),
        name="sample_inproj",
    )(x, gain, w, *small_weights)


S_TILE = 1
Z_ROWS = D_IN // LANES
U_ROWS = 8


def _sample_rows(zr_ref, c1_ref, c2_ref, c3_ref, cm_ref, st_ref, slbl_ref, nh_ref, u_ref, nst_ref):
    bt = S_TILE
    assert bt * Z_ROWS + HG_HEADS <= LANES
    lb_row = _lower_bound(slbl_ref[0:1], slbl_ref[1:2])
    lb_rows = [lb_row[:, h * HG_DIM:(h + 1) * HG_DIM] for h in range(HG_HEADS)]
    pad = jnp.zeros((LANES - bt * Z_ROWS - HG_HEADS, LANES), F32)
    zt = jnp.concatenate([zr_ref[b] for b in range(bt)] + lb_rows + [pad], axis=0).T

    def col(b, off, n=HG_DIM):
        j, l = divmod(off, LANES)
        return zt[l:l + n, Z_ROWS * b + j:Z_ROWS * b + j + 1]

    nh = nh_ref[...]

    def col2(b, off):
        return jnp.concatenate([col(b, off), col(b, off + LANES)], axis=0)

    def per_head(x):
        return jnp.concatenate([jnp.sum(x[h * A_HEAD_DIM:(h + 1) * A_HEAD_DIM], axis=0, keepdims=True)
                                for h in range(A_HEADS)], axis=0)

    def spread(x):
        return jnp.concatenate([jnp.broadcast_to(x[h:h + 1], (A_HEAD_DIM, 1)) for h in range(A_HEADS)], axis=0)

    def attend(b, srcs):
        parts, new_scores, qs = [], [], []
        for c_ref, dil, q_off, k_off, _ in srcs:
            q = col2(b, q_off) * ATT_SCALE
            qs.append(q)
            s = per_head(c_ref[b, 0:D_A, :] * q)
            if dil > 1:
                lane = lax.broadcasted_iota(jnp.int32, s.shape, 1)
                s = jnp.where(lane % dil == 0, s, -jnp.inf)
            parts.append(s)
            if k_off is not None:
                new_scores.append(per_head(col2(b, k_off) * q))
        if new_scores:
            lane = lax.broadcasted_iota(jnp.int32, (A_HEADS, LANES), 1)
            slab = jnp.full((A_HEADS, LANES), -jnp.inf, F32)
            for k, sn in enumerate(new_scores):
                slab = jnp.where(lane == k, sn, slab)
            parts.append(slab)
        s_all = jnp.concatenate(parts, axis=1)
        p_all = jnp.exp(s_all - jnp.max(s_all, axis=1, keepdims=True))
        den = jnp.sum(p_all, axis=1, keepdims=True)
        acc = [jnp.zeros((A_HEAD_DIM, LANES), F32) for _ in range(A_HEADS)]
        lo = 0
        for c_ref, _, _, _, _ in srcs:
            length = c_ref.shape[2]
            for k in range(length // LANES):
                ls = slice(k * LANES, (k + 1) * LANES)
                for h in range(A_HEADS):
                    vt = c_ref[b, D_A + h * A_HEAD_DIM:D_A + (h + 1) * A_HEAD_DIM, ls]
                    acc[h] = acc[h] + vt * p_all[h:h + 1, lo + k * LANES:lo + (k + 1) * LANES]
            lo += length
        num = jnp.sum(jnp.concatenate(acc, axis=0), axis=1, keepdims=True)
        k = 0
        for _, _, _, k_off, v_off in srcs:
            if k_off is not None:
                num = num + spread(p_all[:, lo + k:lo + k + 1]) * col2(b, v_off)
                k += 1
        return num / spread(den)

    window = [(c_ref, dil, OFF_QA + g * D_A, OFF_KA + g * D_A, OFF_VA + g * D_A)
              for g, (c_ref, (_, dil)) in enumerate(zip((c1_ref, c2_ref, c3_ref), WIN_GROUPS))]
    memory = [(cm_ref, 1, OFF_QM, None, None)]

    for b in range(bt):
        um = attend(b, memory) * _silu(col2(b, OFF_GM))
        ua = attend(b, window) * _silu(col2(b, OFF_GA))
        branch_cols = [ua[0:LANES], ua[LANES:2 * LANES], um[0:LANES], um[LANES:2 * LANES]]

        lane = lax.broadcasted_iota(jnp.int32, (LANES, LANES), 1)
        ut = jnp.zeros((LANES, LANES), F32)
        for k, cvec in enumerate(branch_cols):
            ut = ut + jnp.where(lane == k, cvec, 0.0)
        u_ref[b, 0:4, :] = ut.T[0:4, :]

        orows = []
        for h in range(HG_HEADS):
            sl = slice(h * HG_DIM, (h + 1) * HG_DIM)
            lb_col = zt[:, bt * Z_ROWS + h:bt * Z_ROWS + h + 1]
            fcol = lb_col + (1.0 - lb_col) * _sigmoid(col(b, OFF_FB + h * HG_DIM))
            qcol = _silu(col(b, OFF_QB + h * HG_DIM))
            vrow = zr_ref[b, OFF_IB // LANES + h:OFF_IB // LANES + h + 1, :]
            sn = fcol * st_ref[b, sl, :] + (1.0 - fcol) * vrow
            nst_ref[b, sl, :] = sn
            orows.append(jnp.sum(sn * qcol, axis=0, keepdims=True))
        gate = zr_ref[b, OFF_GB // LANES:OFF_GB // LANES + HG_HEADS, :]
        u_ref[b, 4:4 + HG_HEADS, :] = _rms(jnp.concatenate(orows, axis=0), nh) * _silu(gate)


def _stored_view(cache):
    db, length = cache.shape[0], cache.shape[1]
    return jnp.transpose(cache, (0, 2, 3, 4, 1)).reshape(db, 2 * D_A, length)


def _sample_out_kernel(x_ref, u_ref, zg_ref, nf_ref, wa_ref, wb_ref, wm_ref, wo_ref, y_ref):
    def rows(lo, hi):
        return jnp.concatenate([u_ref[:, k, :] for k in range(lo, hi)], axis=1)

    ua, um, ub = rows(0, 2), rows(2, 4), rows(4, U_ROWS)
    y_ref[:, 0, :] = _merge_out(x_ref[:, 0, :], ua, ub, um, zg_ref[...],
                                wa_ref, wb_ref, wm_ref, wo_ref, nf_ref[...])


def _sample_out_call(x, u, zg, nf, wa, wb, wm, wo):
    def full(a):
        return pl.BlockSpec(a.shape, lambda i: (0,) * a.ndim)

    args = (x, u, zg, nf, wa, wb, wm, wo)
    return pl.pallas_call(
        _sample_out_kernel,
        out_shape=jax.ShapeDtypeStruct(x.shape, F32),
        grid=(1,),
        in_specs=[full(a) for a in args],
        out_specs=full(x),
        compiler_params=pltpu.CompilerParams(vmem_limit_bytes=VMEM_LIMIT),
        name="sample_out",
    )(*args)


def _cache_from_stored(kvt):
    b, _, length = kvt.shape
    return jnp.transpose(kvt.reshape(b, 2, A_HEADS, A_HEAD_DIM, length), (0, 4, 1, 2, 3))[None]


def kernel(x_prompt, x_sample, mem_prompt, cache_w1_kv, cache_w2_kv, cache_w3_kv, cache_mem_kv, state_hgrn,
           norm_in, w_in, lb_logits, norm_hgrn, norm_mem, w_mem_kv, w_branch_a, w_branch_b, w_branch_m,
           w_out, norm_final):
    b, s, _ = x_prompt.shape
    db = x_sample.shape[0]
    gain_in = norm_in[0][None]
    nh = norm_hgrn[0][None]
    nf = norm_final[None]
    lbl = lb_logits.astype(F32)

    zr, zg_s, kvt_s, w_in_bf, wa, wb, wm, wo, w_mem_bf = _sample_inproj_call(
        x_sample, gain_in, w_in[0], [w[0] for w in (w_branch_a, w_branch_b, w_branch_m, w_out, w_mem_kv)])
    mkv, mkv_t = _mem_kv_call(mem_prompt, norm_mem[0][None], w_mem_bf)
    sample_ops = (zr,
                  *[_stored_view(c[0]) for c in (cache_w1_kv, cache_w2_kv, cache_w3_kv, cache_mem_kv)],
                  state_hgrn[0].reshape(db, HG_HEADS * HG_DIM, HG_DIM), lbl, nh.reshape(HG_HEADS, HG_DIM))
    qkv0, qkv1, qkv2, gq, hb, zg, fb, kvt0, kvt1, kvt2, u_first, nst_first = _inproj_call(
        x_prompt, gain_in, w_in_bf, sample_ops)
    og = [_dilated_call(qkv, dil, g) for g, (qkv, (_, dil)) in enumerate(zip((qkv0, qkv1, qkv2), WIN_GROUPS))]
    y_prompt, hg_prompt, u, nst = _prompt_tail_call(
        x_prompt, gq, hb, zg, fb, og[0], og[1], og[2], mkv, lbl, nh, nf, wa, wb, wm, wo,
        sample_ops, u_first, nst_first)
    pw = [_cache_from_stored(kvt) for kvt in (kvt0, kvt1, kvt2)]
    new_mem = _cache_from_stored(mkv_t)

    y_sample = _sample_out_call(x_sample, u, zg_s, nf, wa, wb, wm, wo)
    sw = [jnp.transpose(kvt_s[g].reshape(2, A_HEADS, A_HEAD_DIM, db), (3, 0, 1, 2))[None, :, None]
          for g in range(N_GROUPS)]
    new_hg_sample = nst.reshape(1, db, HG_HEADS, HG_DIM, HG_DIM)

    return (y_prompt, y_sample, pw[0], pw[1], pw[2], new_mem, hg_prompt[None],
            sw[0], sw[1], sw[2], new_hg_sample)
```

```python
import jax
import jax.numpy as jnp
from jax import lax
from jax.experimental import pallas as pl
from jax.experimental.pallas import tpu as pltpu

F32 = jnp.float32
BF16 = jnp.bfloat16

D_MODEL = 1024
WIN_GROUPS = ((128, 1), (512, 4), (2048, 16))
N_GROUPS = 3
A_HEADS = 4
A_HEAD_DIM = 64
D_A = 256
QBLK = 128
HG_HEADS = 4
HG_DIM = 128
D_B = 512
HG_CHUNK = 64
HG_SUB = 16
MEM_LEN = 256
D_M = 256
EPS = 1e-6
D_IN = 8192
LANES = 128
OFF_QA, OFF_KA, OFF_VA, OFF_GA = 0, 768, 1536, 2304
OFF_QB, OFF_FB, OFF_IB, OFF_GB = 2560, 3072, 3584, 4096
OFF_QM, OFF_GM, OFF_ZG = 4608, 4864, 5120
ATT_SCALE = A_HEAD_DIM ** -0.5

VMEM_LIMIT = 56 * 1024 * 1024

NT_DIMS = (((1,), (1,)), ((), ()))
TN_DIMS = (((0,), (0,)), ((), ()))


def _sigmoid(x):
    return 0.5 * jnp.tanh(0.5 * x) + 0.5


def _silu(x):
    h = 0.5 * x
    return h * jnp.tanh(h) + h


def _rms(x, gain):
    return x * lax.rsqrt(jnp.mean(x * x, axis=-1, keepdims=True) + EPS) * gain


def _head_masks(width, head_dim, n_heads):
    lane = lax.broadcasted_iota(jnp.int32, (1, width), 1)
    return [(lane >= h * head_dim) & (lane < (h + 1) * head_dim) for h in range(n_heads)]


def _stack_heads(q, masks):
    zero = jnp.zeros_like(q)
    return jnp.concatenate([jnp.where(m, q, zero) for m in masks], axis=0)


def _unstack_heads(o_all, masks, t):
    n = len(masks)
    out = o_all[(n - 1) * t:n * t]
    for h in range(n - 2, -1, -1):
        out = jnp.where(masks[h], o_all[h * t:(h + 1) * t], out)
    return out


def _expand_cols(cols, masks):
    out = jnp.broadcast_to(cols[-1], (cols[-1].shape[0], masks[0].shape[1]))
    for h in range(len(masks) - 2, -1, -1):
        out = jnp.where(masks[h], cols[h], out)
    return out


def _mem_kv_kernel(mem_ref, gain_ref, w_ref, out_ref, out_t_ref):
    hm = _rms(mem_ref[0], gain_ref[...]).astype(BF16)
    kv = jnp.dot(hm, w_ref[...], preferred_element_type=F32)
    out_ref[0] = kv
    out_t_ref[0] = kv.T


def _mem_kv_call(mem, gain, w_bf):
    b = mem.shape[0]
    return pl.pallas_call(
        _mem_kv_kernel,
        out_shape=(jax.ShapeDtypeStruct((b, MEM_LEN, 2 * D_M), F32),
                   jax.ShapeDtypeStruct((b, 2 * D_M, MEM_LEN), F32)),
        grid=(b,),
        in_specs=[pl.BlockSpec((1, MEM_LEN, D_MODEL), lambda i: (i, 0, 0)),
                  pl.BlockSpec((1, D_MODEL), lambda i: (0, 0)),
                  pl.BlockSpec((D_MODEL, 2 * D_M), lambda i: (0, 0))],
        out_specs=(pl.BlockSpec((1, MEM_LEN, 2 * D_M), lambda i: (i, 0, 0)),
                   pl.BlockSpec((1, 2 * D_M, MEM_LEN), lambda i: (i, 0, 0))),
        name="mem_kv",
    )(mem, gain, w_bf)


IN_TILE = 256
IN_CHUNK = 256
D_QKV = 3 * D_A
D_GQ = 3 * D_A
D_HB = 3 * D_B
D_ZG = 3 * D_MODEL


def _inproj_kernel(x_ref, gain_ref, w_ref, zr_ref, c1_ref, c2_ref, c3_ref, cm_ref, sst_ref, slbl_ref, nh4_ref,
                   qkv0_ref, qkv1_ref, qkv2_ref, gq_ref, hb_ref, zg_ref, fb_ref, kvt0_ref, kvt1_ref, kvt2_ref,
                   u_ref, nst_ref, scr_ref):
    qkv_refs = (qkv0_ref, qkv1_ref, qkv2_ref)
    kvt_refs = (kvt0_ref, kvt1_ref, kvt2_ref)
    t = IN_TILE
    h = _rms(x_ref[0], gain_ref[...]).astype(BF16)
    for c in range(D_IN // IN_CHUNK):
        lo = c * IN_CHUNK
        zc = jnp.dot(h, w_ref[:, lo:lo + IN_CHUNK], preferred_element_type=F32)
        if lo < OFF_GA:
            sec, g = divmod(c, N_GROUPS)
            dil = WIN_GROUPS[g][1]
            dst = slice(sec * D_A, (sec + 1) * D_A)
            if dil == 1:
                qkv_refs[g][0, 0, :, dst] = zc.astype(BF16)
            else:
                for k in range(IN_CHUNK // LANES):
                    scr_ref[k] = zc[:, k * LANES:(k + 1) * LANES]
                for r in range(dil):
                    qkv_refs[g][0, r, :, dst] = jnp.concatenate(
                        [scr_ref[k, pl.ds(r, t // dil, stride=dil), :] for k in range(IN_CHUNK // LANES)],
                        axis=1).astype(BF16)
            if sec > 0:
                width = kvt_refs[g].shape[2]
                kvt_refs[g][0, (sec - 1) * D_A:sec * D_A, :] = zc.T[:, t - width:]
        elif lo < OFF_QB:
            gq_ref[0, :, 0:D_A] = zc.astype(BF16)
        elif lo < OFF_FB:
            hb_ref[0, :, lo - OFF_QB:lo - OFF_QB + IN_CHUNK] = zc.astype(BF16)
        elif lo < OFF_IB:
            fb_ref[0, :, lo - OFF_FB:lo - OFF_FB + IN_CHUNK] = zc
        elif lo < OFF_QM:
            hb_ref[0, :, lo - OFF_IB + D_B:lo - OFF_IB + D_B + IN_CHUNK] = zc.astype(BF16)
        elif lo < OFF_ZG:
            gq_ref[0, :, lo - OFF_QM + D_A:lo - OFF_QM + D_A + IN_CHUNK] = zc.astype(BF16)
        else:
            zg_ref[0, :, lo - OFF_ZG:lo - OFF_ZG + IN_CHUNK] = zc.astype(BF16)
    _sample_rows(zr_ref, c1_ref, c2_ref, c3_ref, cm_ref, sst_ref, slbl_ref, nh4_ref, u_ref, nst_ref,
                 memory_first=True)


def _sample_specs(ops, nt, first_row):
    first_blk = first_row // S_TILE

    def rows(a):
        return pl.BlockSpec((S_TILE,) + a.shape[1:], lambda i, j: (first_blk + i * nt + j, 0, 0))

    def const(a):
        return pl.BlockSpec(a.shape, lambda i, j: (0, 0), pipeline_mode=pl.Buffered(1))

    return [rows(a) for a in ops[:6]] + [const(a) for a in ops[6:]]


def _inproj_call(x, gain, w_bf, sample_ops):
    b, s, _ = x.shape
    t = IN_TILE
    nt = s // t
    n_rows = b * nt * S_TILE

    def tok(width):
        return pl.BlockSpec((1, t, width), lambda i, j: (i, j, 0))

    def srow(shape):
        return pl.BlockSpec((S_TILE,) + shape, lambda i, j: (i * nt + j, 0, 0))

    qkv_shapes, qkv_specs = [], []
    for _, dil in WIN_GROUPS:
        qkv_shapes.append(jax.ShapeDtypeStruct((b, dil, s // dil, D_QKV), BF16))
        qkv_specs.append(pl.BlockSpec((1, dil, t // dil, D_QKV), lambda i, j: (i, 0, j, 0)))
    kvt_shapes, kvt_specs = [], []
    for window, _ in WIN_GROUPS:
        length = min(window, s)
        width = min(t, length)
        first = nt - length // width
        kvt_shapes.append(jax.ShapeDtypeStruct((b, 2 * D_A, length), F32))
        kvt_specs.append(pl.BlockSpec((1, 2 * D_A, width),
                                      lambda i, j, first=first: (i, 0, jnp.maximum(j - first, 0))))
    return pl.pallas_call(
        _inproj_kernel,
        out_shape=(*qkv_shapes,
                   jax.ShapeDtypeStruct((b, s, D_GQ), BF16),
                   jax.ShapeDtypeStruct((b, s, D_HB), BF16),
                   jax.ShapeDtypeStruct((b, s, D_ZG), BF16),
                   jax.ShapeDtypeStruct((b, s, D_B), F32),
                   *kvt_shapes,
                   jax.ShapeDtypeStruct((n_rows, U_ROWS, LANES), F32),
                   jax.ShapeDtypeStruct((n_rows, HG_HEADS * HG_DIM, HG_DIM), F32)),
        grid=(b, nt),
        in_specs=[tok(D_MODEL),
                  pl.BlockSpec((1, D_MODEL), lambda i, j: (0, 0)),
                  pl.BlockSpec((D_MODEL, D_IN), lambda i, j: (0, 0), pipeline_mode=pl.Buffered(1)),
                  *_sample_specs(sample_ops, nt, 0)],
        out_specs=(*qkv_specs, tok(D_GQ), tok(D_HB), tok(D_ZG), tok(D_B), *kvt_specs,
                   srow((U_ROWS, LANES)), srow((HG_HEADS * HG_DIM, HG_DIM))),
        scratch_shapes=[pltpu.VMEM((IN_CHUNK // LANES, t, LANES), F32)],
        compiler_params=pltpu.CompilerParams(
            dimension_semantics=("arbitrary", "arbitrary"), vmem_limit_bytes=VMEM_LIMIT),
        name="inproj",
    )(x, gain, w_bf, *sample_ops)


DIL_STEP = 2048
OG_SLABS = 2 * D_A // LANES


def _make_dilated_kernel(dil):
    nblk = DIL_STEP // (dil * QBLK)

    def body(prev_ref, cur_ref, out_ref):
        first_step = pl.program_id(1) == 0
        masks = _head_masks(D_A, A_HEAD_DIM, A_HEADS)
        qi = lax.broadcasted_iota(jnp.int32, (A_HEADS * QBLK, 2 * QBLK), 0) % QBLK
        kj = lax.broadcasted_iota(jnp.int32, (A_HEADS * QBLK, 2 * QBLK), 1)
        band = (kj >= qi) & (kj <= qi + QBLK)
        neg_band = jnp.where(band, 0.0, -jnp.inf)
        neg_first = jnp.where(first_step, jnp.where(band & (kj >= QBLK), 0.0, -jnp.inf), neg_band)

        def keys_values(r, jb, c):
            rows = slice(jb * QBLK, (jb + 1) * QBLK)
            if jb == 0:
                return jnp.concatenate([prev_ref[0, r, :, c * D_A:(c + 1) * D_A],
                                        cur_ref[0, r, rows, c * D_A:(c + 1) * D_A]], axis=0)
            return cur_ref[0, r, (jb - 1) * QBLK:(jb + 1) * QBLK, c * D_A:(c + 1) * D_A]

        for r, jb in [(r, jb) for r in range(dil) for jb in range(nblk)]:
            q = cur_ref[0, r, jb * QBLK:(jb + 1) * QBLK, 0:D_A]
            qs = _stack_heads(q * jnp.asarray(ATT_SCALE, BF16), masks)
            s = lax.dot_general(qs, keys_values(r, jb, 1), NT_DIMS, preferred_element_type=F32)
            s = s + (neg_first if jb == 0 else neg_band)
            mx = jnp.max(s, axis=-1, keepdims=True)
            p = jnp.exp(s - mx)
            l = jnp.sum(p, axis=-1, keepdims=True)
            o_all = jnp.dot(p.astype(BF16), keys_values(r, jb, 2), preferred_element_type=F32)
            l_e = _expand_cols([l[h * QBLK:(h + 1) * QBLK] for h in range(A_HEADS)], masks)
            mx_e = _expand_cols([mx[h * QBLK:(h + 1) * QBLK] for h in range(A_HEADS)], masks)
            res = (_unstack_heads(o_all, masks, QBLK) / l_e, mx_e + jnp.log(l_e))
            for k in range(OG_SLABS):
                src = res[k // 2][:, (k % 2) * LANES:(k % 2 + 1) * LANES]
                if dil == 1:
                    out_ref[0, k, jb * QBLK:(jb + 1) * QBLK, :] = src
                else:
                    out_ref[0, k, pl.ds(jb * QBLK * dil + r, QBLK, stride=dil), :] = src
    return body


def _dilated_call(qkv, dil, g):
    b, _, n, _ = qkv.shape
    rows = DIL_STEP // dil
    nb = n // rows
    return pl.pallas_call(
        _make_dilated_kernel(dil),
        out_shape=jax.ShapeDtypeStruct((b, OG_SLABS, n * dil, LANES), F32),
        grid=(b, nb),
        in_specs=[pl.BlockSpec((1, dil, QBLK, D_QKV),
                               lambda i, u: (i, 0, jnp.maximum(u * (rows // QBLK) - 1, 0), 0)),
                  pl.BlockSpec((1, dil, rows, D_QKV), lambda i, u: (i, 0, u, 0))],
        out_specs=pl.BlockSpec((1, OG_SLABS, DIL_STEP, LANES), lambda i, u: (i, 0, u, 0)),
        compiler_params=pltpu.CompilerParams(
            dimension_semantics=("arbitrary", "arbitrary"), vmem_limit_bytes=VMEM_LIMIT),
        name=f"dilated_attn_g{g}",
    )(qkv, qkv)


def _lower_bound(l0, l1):
    m = jnp.maximum(l0, l1)
    e0, e1 = jnp.exp(l0 - m), jnp.exp(l1 - m)
    return e0 / (e0 + e1)


def _head_rms(ob, gain):
    parts = []
    for h in range(HG_HEADS):
        sl = slice(h * HG_DIM, (h + 1) * HG_DIM)
        parts.append(_rms(ob[:, sl], gain[:, sl]))
    return jnp.concatenate(parts, axis=1)


def _merge_out(x, ua, ub, um, zg, wa_ref, wb_ref, wm_ref, wo_ref, nf):
    projs = [jnp.dot(u.astype(BF16), w_ref[...], preferred_element_type=F32)
             for u, w_ref in ((ua, wa_ref), (ub, wb_ref), (um, wm_ref))]
    gates = _sigmoid(zg.astype(BF16))
    merged = None
    for k, proj in enumerate(projs):
        term = gates[:, k * D_MODEL:(k + 1) * D_MODEL] * proj.astype(BF16)
        merged = term if merged is None else merged + term
    y = x + jnp.dot(merged, wo_ref[...], preferred_element_type=F32)
    return _rms(y, nf)


TAIL_TILE = 256


def _cumsum_rows(g, tri):
    g1 = g.astype(BF16)
    g2 = (g - g1.astype(F32)).astype(BF16)
    return jnp.dot(tri, g1, preferred_element_type=F32) + jnp.dot(tri, g2, preferred_element_type=F32)


def _hgrn_tile(fb, qb, v, lb, st_ref):
    t = fb.shape[0]
    c, sub = HG_CHUNK, HG_SUB
    n_sub = c // sub
    f = lb + (1.0 - lb) * _sigmoid(fb)
    kk = 1.0 - f
    row = lax.broadcasted_iota(jnp.int32, (t, t), 0)
    col = lax.broadcasted_iota(jnp.int32, (t, t), 1)
    same_chunk = (row // c) == (col // c)
    gcs = _cumsum_rows(jnp.log2(f), (same_chunk & (col <= row)).astype(BF16))
    q = _silu(qb).astype(F32)

    loc = lax.broadcasted_iota(jnp.int32, (c, 1), 0)
    qt_c, kt_ci, qe_c, kd_c, dec_c = [], [[] for _ in range(n_sub)], [], [], []
    for ci in range(t // c):
        rows = slice(ci * c, (ci + 1) * c)
        g_c, q_c, k_c = gcs[rows], q[rows], kk[rows]
        refs = [jnp.zeros((1, D_B), F32)] + [g_c[i * sub - 1:i * sub] for i in range(1, n_sub)]
        own = refs[n_sub - 1]
        for i in range(n_sub - 2, -1, -1):
            own = jnp.where(loc < (i + 1) * sub, refs[i], own)
        qt_c.append((q_c * jnp.exp2(g_c - own)).astype(BF16))
        k_own = k_c * jnp.exp2(own - g_c)
        for i in range(n_sub):
            pieces = []
            for jb in range(n_sub):
                piece = k_own[jb * sub:(jb + 1) * sub]
                if jb < i:
                    piece = piece * jnp.exp2(refs[i] - refs[jb])
                elif jb > i:
                    piece = jnp.zeros_like(piece)
                pieces.append(piece)
            kt_ci[i].append(jnp.concatenate(pieces, axis=0).astype(BF16))
        g_last = g_c[c - 1:c]
        qe_c.append((q_c * jnp.exp2(g_c)).astype(BF16))
        kd_c.append((k_c * jnp.exp2(g_last - g_c)).astype(BF16))
        dec_c.append(jnp.exp2(g_last))
    qt = jnp.concatenate(qt_c, axis=0)
    kts = [jnp.concatenate(parts, axis=0) for parts in kt_ci]
    sub_of_row = (lax.broadcasted_iota(jnp.int32, (t, 1), 0) % c) // sub
    causal = same_chunk & (col <= row)

    heads = [slice(h * HG_DIM, (h + 1) * HG_DIM) for h in range(HG_HEADS)]
    chunks = [slice(ci * c, (ci + 1) * c) for ci in range(t // c)]
    zero = jnp.zeros((t, HG_DIM), BF16)
    atts, incs = [], []
    for sl in heads:
        q_big = jnp.concatenate([jnp.where(sub_of_row == i, qt[:, sl], zero) for i in range(n_sub)], axis=1)
        k_big = jnp.concatenate([kts[i][:, sl] for i in range(n_sub)], axis=1)
        atts.append(lax.dot_general(q_big, k_big, NT_DIMS, preferred_element_type=F32))
        incs.append([lax.dot_general(v[rows, sl], kd_c[ci][:, sl], TN_DIMS, preferred_element_type=F32)
                     for ci, rows in enumerate(chunks)])
    states = []
    for h, sl in enumerate(heads):
        atts[h] = jnp.where(causal, atts[h], 0.0).astype(BF16)
        sts = [st_ref[h]]
        for ci in range(len(chunks)):
            sts.append(sts[-1] * dec_c[ci][:, sl] + incs[h][ci])
        st_ref[h] = sts[-1]
        states.append(sts)
    outs = []
    for h, sl in enumerate(heads):
        o_intra = jnp.dot(atts[h], v[:, sl], preferred_element_type=F32)
        inter = [lax.dot_general(qe_c[ci][:, sl], states[h][ci].astype(BF16), NT_DIMS, preferred_element_type=F32)
                 for ci in range(len(chunks))]
        outs.append(o_intra + jnp.concatenate(inter, axis=0))
    return jnp.concatenate(outs, axis=1)


def _prompt_tail_kernel(x_ref, gq_ref, hb_ref, zg_ref, fb_ref, o1_ref, o2_ref, o3_ref, mkv_ref, lbl_ref,
                        nh_ref, nf_ref, wa_ref, wb_ref, wm_ref, wo_ref,
                        zr_ref, c1_ref, c2_ref, c3_ref, cm_ref, sst_ref, slbl_ref, nh4_ref, ua_ref, nsta_ref,
                        y_ref, hst_ref, u_ref, nst_ref, st_ref):
    j = pl.program_id(1)
    t = TAIL_TILE

    @pl.when(j == 0)
    def _():
        st_ref[...] = jnp.zeros_like(st_ref)

    u_ref[0] = ua_ref[...]
    nst_ref[0] = nsta_ref[...]
    _sample_rows(zr_ref, c1_ref, c2_ref, c3_ref, cm_ref, sst_ref, slbl_ref, nh4_ref, u_ref.at[1], nst_ref.at[1],
                 memory_first=False)

    masks = _head_masks(D_M, A_HEAD_DIM, A_HEADS)
    qm = gq_ref[0, :, D_A:2 * D_A] * jnp.asarray(ATT_SCALE, BF16)
    qs = _stack_heads(qm, masks)
    mk = mkv_ref[0, :, 0:D_M].astype(BF16)
    mv = mkv_ref[0, :, D_M:2 * D_M].astype(BF16)
    s = lax.dot_general(qs, mk, NT_DIMS, preferred_element_type=F32)

    lb = _lower_bound(lbl_ref[0:1], lbl_ref[1:2])
    ob = _hgrn_tile(fb_ref[0], hb_ref[0, :, 0:D_B], hb_ref[0, :, D_B:2 * D_B], lb, st_ref)
    ub = _head_rms(ob, nh_ref[...]).astype(BF16) * _silu(hb_ref[0, :, 2 * D_B:3 * D_B])

    os_ = [jnp.concatenate([r[0, 0], r[0, 1]], axis=1) for r in (o1_ref, o2_ref, o3_ref)]
    ls_ = [jnp.concatenate([r[0, 2], r[0, 3]], axis=1) for r in (o1_ref, o2_ref, o3_ref)]
    mx = jnp.maximum(jnp.maximum(ls_[0], ls_[1]), ls_[2])
    es_ = [jnp.exp(l - mx) for l in ls_]
    oa = (es_[0] * os_[0] + es_[1] * os_[1] + es_[2] * os_[2]) / (es_[0] + es_[1] + es_[2])
    ua = oa.astype(BF16) * _silu(gq_ref[0, :, 0:D_A])

    p = jnp.exp(s - jnp.max(s, axis=-1, keepdims=True))
    l = jnp.sum(p, axis=-1, keepdims=True)
    om = _unstack_heads(jnp.dot(p.astype(BF16), mv, preferred_element_type=F32) / l, masks, t)
    um = om.astype(BF16) * _silu(gq_ref[0, :, 2 * D_A:3 * D_A])

    y_ref[0] = _merge_out(x_ref[0], ua, ub, um, zg_ref[0], wa_ref, wb_ref, wm_ref, wo_ref, nf_ref[...])

    @pl.when(j == pl.num_programs(1) - 1)
    def _():
        for h in range(HG_HEADS):
            hst_ref[0, h] = st_ref[h].T


def _prompt_tail_call(x, gq, hb, zg, fb, o1, o2, o3, mkv, lbl, nh, nf, wa, wb, wm, wo,
                      sample_ops, u_first, nst_first):
    b, s, _ = x.shape
    t = TAIL_TILE
    nt = s // t
    db = sample_ops[0].shape[0]
    half = u_first.shape[0]
    assert half + b * nt * S_TILE == db and 2 * half == db, "sample rows must split evenly over both grids"

    def tok(width):
        return pl.BlockSpec((1, t, width), lambda i, j: (i, j, 0))

    def const(shape):
        return pl.BlockSpec(shape, lambda i, j: (0,) * len(shape), pipeline_mode=pl.Buffered(1))

    def first(a):
        return pl.BlockSpec((S_TILE,) + a.shape[1:], lambda i, j: (i * nt + j, 0, 0))

    def halves(shape):
        return pl.BlockSpec((2, S_TILE) + shape, lambda i, j: (0, i * nt + j, 0, 0))

    y, hst, u, nst = pl.pallas_call(
        _prompt_tail_kernel,
        out_shape=(jax.ShapeDtypeStruct((b, s, D_MODEL), F32),
                   jax.ShapeDtypeStruct((b, HG_HEADS, HG_DIM, HG_DIM), F32),
                   jax.ShapeDtypeStruct((2, half, U_ROWS, LANES), F32),
                   jax.ShapeDtypeStruct((2, half, HG_HEADS * HG_DIM, HG_DIM), F32)),
        grid=(b, nt),
        in_specs=[tok(D_MODEL), tok(D_GQ), tok(D_HB), tok(D_ZG), tok(D_B),
                  *[pl.BlockSpec((1, OG_SLABS, t, LANES), lambda i, j: (i, 0, j, 0))] * N_GROUPS,
                  pl.BlockSpec((1, MEM_LEN, 2 * D_M), lambda i, j: (i, 0, 0)),
                  const((2, D_B)), const((1, D_B)), const((1, D_MODEL)),
                  const((D_A, D_MODEL)), const((D_B, D_MODEL)), const((D_M, D_MODEL)),
                  const((D_MODEL, D_MODEL)),
                  *_sample_specs(sample_ops, nt, half), first(u_first), first(nst_first)],
        out_specs=(tok(D_MODEL),
                   pl.BlockSpec((1, HG_HEADS, HG_DIM, HG_DIM), lambda i, j: (i, 0, 0, 0)),
                   halves((U_ROWS, LANES)), halves((HG_HEADS * HG_DIM, HG_DIM))),
        scratch_shapes=[pltpu.VMEM((HG_HEADS, HG_DIM, HG_DIM), F32)],
        compiler_params=pltpu.CompilerParams(
            dimension_semantics=("arbitrary", "arbitrary"), vmem_limit_bytes=VMEM_LIMIT),
        name="prompt_tail",
    )(x, gq, hb, zg, fb, o1, o2, o3, mkv, lbl, nh, nf, wa, wb, wm, wo, *sample_ops, u_first, nst_first)
    return y, hst, u.reshape(db, U_ROWS, LANES), nst.reshape(db, HG_HEADS * HG_DIM, HG_DIM)


S_COLS = 1024


def _sample_inproj_kernel(x_ref, gain_ref, w_ref, *refs):
    n_small = (len(refs) - 4) // 2
    small_in, (zr_ref, zg_ref, kvt_ref, wbf_ref), small_out = refs[:n_small], refs[n_small:n_small + 4], refs[n_small + 4:]
    for src, dst in zip(small_in, small_out):
        dst[...] = src[...].astype(BF16)
    c = pl.program_id(0)
    h = _rms(x_ref[:, 0, :], gain_ref[...]).astype(BF16)
    w = w_ref[...].astype(BF16)
    wbf_ref[...] = w
    z = jnp.dot(h, w, preferred_element_type=F32)
    for k in range(S_COLS // LANES):
        zr_ref[:, k, :] = z[:, k * LANES:(k + 1) * LANES]
    zg_ref[...] = z
    for step in range(OFF_GA // S_COLS + 1):
        pieces = [(g, kv, (OFF_KA + (kv * N_GROUPS + g) * D_A) % S_COLS)
                  for g in range(N_GROUPS) for kv in range(2)
                  if (OFF_KA + (kv * N_GROUPS + g) * D_A) // S_COLS == step]

        @pl.when(c == step)
        def _(pieces=pieces):
            zt = z.T
            for g, kv, row0 in pieces:
                kvt_ref[g, kv * D_A:(kv + 1) * D_A, :] = zt[row0:row0 + D_A]


def _sample_inproj_call(x, gain, w, small_weights):
    db = x.shape[0]
    steps = D_IN // S_COLS
    first_zg = OFF_ZG // S_COLS
    small_specs = [pl.BlockSpec((a.shape[0] // steps, a.shape[1]), lambda c: (c, 0)) for a in small_weights]
    return pl.pallas_call(
        _sample_inproj_kernel,
        out_shape=(jax.ShapeDtypeStruct((db, Z_ROWS, LANES), F32),
                   jax.ShapeDtypeStruct((db, D_ZG), F32),
                   jax.ShapeDtypeStruct((N_GROUPS, 2 * D_A, db), F32),
                   jax.ShapeDtypeStruct((D_MODEL, D_IN), BF16),
                   *[jax.ShapeDtypeStruct(a.shape, BF16) for a in small_weights]),
        grid=(steps,),
        in_specs=[pl.BlockSpec(x.shape, lambda c: (0, 0, 0)),
                  pl.BlockSpec((1, D_MODEL), lambda c: (0, 0)),
                  pl.BlockSpec((D_MODEL, S_COLS), lambda c: (0, c)),
                  *small_specs],
        out_specs=(pl.BlockSpec((db, S_COLS // LANES, LANES), lambda c: (0, c, 0)),
                   pl.BlockSpec((db, S_COLS), lambda c: (0, jnp.maximum(c - first_zg, 0))),
                   pl.BlockSpec((N_GROUPS, 2 * D_A, db), lambda c: (0, 0, 0)),
                   pl.BlockSpec((D_MODEL, S_COLS), lambda c: (0, c)),
                   *small_specs),
        compiler_params=pltpu.CompilerParams(dimension_semantics=("arbitrary",)),
        name="sample_inproj",
    )(x, gain, w, *small_weights)


S_TILE = 1
Z_ROWS = D_IN // LANES
U_ROWS = 8


def _sample_rows(zr_ref, c1_ref, c2_ref, c3_ref, cm_ref, st_ref, slbl_ref, nh_ref, u_ref, nst_ref, *, memory_first):
    bt = S_TILE
    assert bt * Z_ROWS + HG_HEADS <= LANES
    lb_row = _lower_bound(slbl_ref[0:1], slbl_ref[1:2])
    lb_rows = [lb_row[:, h * HG_DIM:(h + 1) * HG_DIM] for h in range(HG_HEADS)]
    pad = jnp.zeros((LANES - bt * Z_ROWS - HG_HEADS, LANES), F32)
    zt = jnp.concatenate([zr_ref[b] for b in range(bt)] + lb_rows + [pad], axis=0).T

    def col(b, off, n=HG_DIM):
        j, l = divmod(off, LANES)
        return zt[l:l + n, Z_ROWS * b + j:Z_ROWS * b + j + 1]

    nh = nh_ref[...]

    def col2(b, off):
        return jnp.concatenate([col(b, off), col(b, off + LANES)], axis=0)

    def per_head(x):
        return jnp.concatenate([jnp.sum(x[h * A_HEAD_DIM:(h + 1) * A_HEAD_DIM], axis=0, keepdims=True)
                                for h in range(A_HEADS)], axis=0)

    def spread(x):
        return jnp.concatenate([jnp.broadcast_to(x[h:h + 1], (A_HEAD_DIM, 1)) for h in range(A_HEADS)], axis=0)

    def attend(b, srcs):
        parts, new_scores, qs = [], [], []
        for c_ref, dil, q_off, k_off, _ in srcs:
            q = col2(b, q_off) * ATT_SCALE
            qs.append(q)
            s = per_head(c_ref[b, 0:D_A, :] * q)
            if dil > 1:
                lane = lax.broadcasted_iota(jnp.int32, s.shape, 1)
                s = jnp.where(lane % dil == 0, s, -jnp.inf)
            parts.append(s)
            if k_off is not None:
                new_scores.append(per_head(col2(b, k_off) * q))
        if new_scores:
            lane = lax.broadcasted_iota(jnp.int32, (A_HEADS, LANES), 1)
            slab = jnp.full((A_HEADS, LANES), -jnp.inf, F32)
            for k, sn in enumerate(new_scores):
                slab = jnp.where(lane == k, sn, slab)
            parts.append(slab)
        s_all = jnp.concatenate(parts, axis=1)
        p_all = jnp.exp(s_all - jnp.max(s_all, axis=1, keepdims=True))
        den = jnp.sum(p_all, axis=1, keepdims=True)
        acc = [jnp.zeros((A_HEAD_DIM, LANES), F32) for _ in range(A_HEADS)]
        lo = 0
        for c_ref, _, _, _, _ in srcs:
            length = c_ref.shape[2]
            for k in range(length // LANES):
                ls = slice(k * LANES, (k + 1) * LANES)
                for h in range(A_HEADS):
                    vt = c_ref[b, D_A + h * A_HEAD_DIM:D_A + (h + 1) * A_HEAD_DIM, ls]
                    acc[h] = acc[h] + vt * p_all[h:h + 1, lo + k * LANES:lo + (k + 1) * LANES]
            lo += length
        num = jnp.sum(jnp.concatenate(acc, axis=0), axis=1, keepdims=True)
        k = 0
        for _, _, _, k_off, v_off in srcs:
            if k_off is not None:
                num = num + spread(p_all[:, lo + k:lo + k + 1]) * col2(b, v_off)
                k += 1
        return num / spread(den)

    window = [(c_ref, dil, OFF_QA + g * D_A, OFF_KA + g * D_A, OFF_VA + g * D_A)
              for g, (c_ref, (_, dil)) in enumerate(zip((c1_ref, c2_ref, c3_ref), WIN_GROUPS))]
    memory = [(cm_ref, 1, OFF_QM, None, None)]

    for b in range(bt):
        if memory_first:
            um = attend(b, memory) * _silu(col2(b, OFF_GM))
            ua = attend(b, window) * _silu(col2(b, OFF_GA))
        else:
            ua = attend(b, window) * _silu(col2(b, OFF_GA))
            um = attend(b, memory) * _silu(col2(b, OFF_GM))
        branch_cols = [ua[0:LANES], ua[LANES:2 * LANES], um[0:LANES], um[LANES:2 * LANES]]

        lane = lax.broadcasted_iota(jnp.int32, (LANES, LANES), 1)
        ut = jnp.zeros((LANES, LANES), F32)
        for k, cvec in enumerate(branch_cols):
            ut = ut + jnp.where(lane == k, cvec, 0.0)
        u_ref[b, 0:4, :] = ut.T[0:4, :]

        orows = []
        for h in range(HG_HEADS):
            sl = slice(h * HG_DIM, (h + 1) * HG_DIM)
            lb_col = zt[:, bt * Z_ROWS + h:bt * Z_ROWS + h + 1]
            fcol = lb_col + (1.0 - lb_col) * _sigmoid(col(b, OFF_FB + h * HG_DIM))
            qcol = _silu(col(b, OFF_QB + h * HG_DIM))
            vrow = zr_ref[b, OFF_IB // LANES + h:OFF_IB // LANES + h + 1, :]
            sn = fcol * st_ref[b, sl, :] + (1.0 - fcol) * vrow
            nst_ref[b, sl, :] = sn
            orows.append(jnp.sum(sn * qcol, axis=0, keepdims=True))
        gate = zr_ref[b, OFF_GB // LANES:OFF_GB // LANES + HG_HEADS, :]
        u_ref[b, 4:4 + HG_HEADS, :] = _rms(jnp.concatenate(orows, axis=0), nh) * _silu(gate)


def _stored_view(cache):
    db, length = cache.shape[0], cache.shape[1]
    return jnp.transpose(cache, (0, 2, 3, 4, 1)).reshape(db, 2 * D_A, length)


def _sample_out_kernel(x_ref, u_ref, zg_ref, nf_ref, wa_ref, wb_ref, wm_ref, wo_ref, y_ref):
    def rows(lo, hi):
        return jnp.concatenate([u_ref[:, k, :] for k in range(lo, hi)], axis=1)

    ua, um, ub = rows(0, 2), rows(2, 4), rows(4, U_ROWS)
    y_ref[:, 0, :] = _merge_out(x_ref[:, 0, :], ua, ub, um, zg_ref[...],
                                wa_ref, wb_ref, wm_ref, wo_ref, nf_ref[...])


def _sample_out_call(x, u, zg, nf, wa, wb, wm, wo):
    def full(a):
        return pl.BlockSpec(a.shape, lambda i: (0,) * a.ndim)

    args = (x, u, zg, nf, wa, wb, wm, wo)
    return pl.pallas_call(
        _sample_out_kernel,
        out_shape=jax.ShapeDtypeStruct(x.shape, F32),
        grid=(1,),
        in_specs=[full(a) for a in args],
        out_specs=full(x),
        compiler_params=pltpu.CompilerParams(vmem_limit_bytes=VMEM_LIMIT),
        name="sample_out",
    )(*args)


def _cache_from_stored(kvt):
    b, _, length = kvt.shape
    return jnp.transpose(kvt.reshape(b, 2, A_HEADS, A_HEAD_DIM, length), (0, 4, 1, 2, 3))[None]


def kernel(x_prompt, x_sample, mem_prompt, cache_w1_kv, cache_w2_kv, cache_w3_kv, cache_mem_kv, state_hgrn,
           norm_in, w_in, lb_logits, norm_hgrn, norm_mem, w_mem_kv, w_branch_a, w_branch_b, w_branch_m,
           w_out, norm_final):
    b, s, _ = x_prompt.shape
    db = x_sample.shape[0]
    gain_in = norm_in[0][None]
    nh = norm_hgrn[0][None]
    nf = norm_final[None]
    lbl = lb_logits.astype(F32)

    zr, zg_s, kvt_s, w_in_bf, wa, wb, wm, wo, w_mem_bf = _sample_inproj_call(
        x_sample, gain_in, w_in[0], [w[0] for w in (w_branch_a, w_branch_b, w_branch_m, w_out, w_mem_kv)])
    mkv, mkv_t = _mem_kv_call(mem_prompt, norm_mem[0][None], w_mem_bf)
    sample_ops = (zr,
                  *[_stored_view(c[0]) for c in (cache_w1_kv, cache_w2_kv, cache_w3_kv, cache_mem_kv)],
                  state_hgrn[0].reshape(db, HG_HEADS * HG_DIM, HG_DIM), lbl, nh.reshape(HG_HEADS, HG_DIM))
    qkv0, qkv1, qkv2, gq, hb, zg, fb, kvt0, kvt1, kvt2, u_first, nst_first = _inproj_call(
        x_prompt, gain_in, w_in_bf, sample_ops)
    og = [_dilated_call(qkv, dil, g) for g, (qkv, (_, dil)) in enumerate(zip((qkv0, qkv1, qkv2), WIN_GROUPS))]
    y_prompt, hg_prompt, u, nst = _prompt_tail_call(
        x_prompt, gq, hb, zg, fb, og[0], og[1], og[2], mkv, lbl, nh, nf, wa, wb, wm, wo,
        sample_ops, u_first, nst_first)
    pw = [_cache_from_stored(kvt) for kvt in (kvt0, kvt1, kvt2)]
    new_mem = _cache_from_stored(mkv_t)

    y_sample = _sample_out_call(x_sample, u, zg_s, nf, wa, wb, wm, wo)
    sw = [jnp.transpose(kvt_s[g].reshape(2, A_HEADS, A_HEAD_DIM, db), (3, 0, 1, 2))[None, :, None]
          for g in range(N_GROUPS)]
    new_hg_sample = nst.reshape(1, db, HG_HEADS, HG_DIM, HG_DIM)

    return (y_prompt, y_sample, pw[0], pw[1], pw[2], new_mem, hg_prompt[None],
            sw[0], sw[1], sw[2], new_hg_sample)
```

```python
import jax
import jax.numpy as jnp
from jax import lax
from jax.experimental import pallas as pl
from jax.experimental.pallas import tpu as pltpu

F32 = jnp.float32
BF16 = jnp.bfloat16

D_MODEL = 1024
WIN_GROUPS = ((128, 1), (512, 4), (2048, 16))
N_GROUPS = 3
A_HEADS = 4
A_HEAD_DIM = 64
D_A = 256
QBLK = 128
HG_HEADS = 4
HG_DIM = 128
D_B = 512
HG_CHUNK = 64
HG_SUB = 16
MEM_LEN = 256
D_M = 256
EPS = 1e-6
D_IN = 8192
LANES = 128
OFF_QA, OFF_KA, OFF_VA, OFF_GA = 0, 768, 1536, 2304
OFF_QB, OFF_FB, OFF_IB, OFF_GB = 2560, 3072, 3584, 4096
OFF_QM, OFF_GM, OFF_ZG = 4608, 4864, 5120
ATT_SCALE = A_HEAD_DIM ** -0.5

VMEM_LIMIT = 56 * 1024 * 1024

NT_DIMS = (((1,), (1,)), ((), ()))
TN_DIMS = (((0,), (0,)), ((), ()))


def _sigmoid(x):
    return 0.5 * jnp.tanh(0.5 * x) + 0.5


def _silu(x):
    h = 0.5 * x
    return h * jnp.tanh(h) + h


def _rms(x, gain):
    return x * lax.rsqrt(jnp.mean(x * x, axis=-1, keepdims=True) + EPS) * gain


def _head_masks(width, head_dim, n_heads):
    lane = lax.broadcasted_iota(jnp.int32, (1, width), 1)
    return [(lane >= h * head_dim) & (lane < (h + 1) * head_dim) for h in range(n_heads)]


def _stack_heads(q, masks):
    zero = jnp.zeros_like(q)
    return jnp.concatenate([jnp.where(m, q, zero) for m in masks], axis=0)


def _unstack_heads(o_all, masks, t):
    n = len(masks)
    out = o_all[(n - 1) * t:n * t]
    for h in range(n - 2, -1, -1):
        out = jnp.where(masks[h], o_all[h * t:(h + 1) * t], out)
    return out


def _expand_cols(cols, masks):
    out = jnp.broadcast_to(cols[-1], (cols[-1].shape[0], masks[0].shape[1]))
    for h in range(len(masks) - 2, -1, -1):
        out = jnp.where(masks[h], cols[h], out)
    return out


def _mem_kv_kernel(mem_ref, gain_ref, w_ref, out_ref, out_t_ref):
    hm = _rms(mem_ref[0], gain_ref[...]).astype(BF16)
    kv = jnp.dot(hm, w_ref[...], preferred_element_type=F32)
    out_ref[0] = kv
    out_t_ref[0] = kv.T


def _mem_kv_call(mem, gain, w_bf):
    b = mem.shape[0]
    return pl.pallas_call(
        _mem_kv_kernel,
        out_shape=(jax.ShapeDtypeStruct((b, MEM_LEN, 2 * D_M), F32),
                   jax.ShapeDtypeStruct((b, 2 * D_M, MEM_LEN), F32)),
        grid=(b,),
        in_specs=[pl.BlockSpec((1, MEM_LEN, D_MODEL), lambda i: (i, 0, 0)),
                  pl.BlockSpec((1, D_MODEL), lambda i: (0, 0)),
                  pl.BlockSpec((D_MODEL, 2 * D_M), lambda i: (0, 0))],
        out_specs=(pl.BlockSpec((1, MEM_LEN, 2 * D_M), lambda i: (i, 0, 0)),
                   pl.BlockSpec((1, 2 * D_M, MEM_LEN), lambda i: (i, 0, 0))),
        name="mem_kv",
    )(mem, gain, w_bf)


IN_TILE = 256
IN_CHUNK = 256
D_QKV = 3 * D_A
D_GQ = 3 * D_A
D_HB = 3 * D_B
D_ZG = 3 * D_MODEL


def _inproj_kernel(x_ref, gain_ref, w_ref, zr_ref, c1_ref, c2_ref, c3_ref, cm_ref, sst_ref, slbl_ref, nh4_ref,
                   qkv0_ref, qkv1_ref, qkv2_ref, gq_ref, hb_ref, zg_ref, fb_ref, kvt0_ref, kvt1_ref, kvt2_ref,
                   u_ref, nst_ref, scr_ref):
    qkv_refs = (qkv0_ref, qkv1_ref, qkv2_ref)
    kvt_refs = (kvt0_ref, kvt1_ref, kvt2_ref)
    t = IN_TILE
    h = _rms(x_ref[0], gain_ref[...]).astype(BF16)
    for c in range(D_IN // IN_CHUNK):
        lo = c * IN_CHUNK
        zc = jnp.dot(h, w_ref[:, lo:lo + IN_CHUNK], preferred_element_type=F32)
        if lo < OFF_GA:
            sec, g = divmod(c, N_GROUPS)
            dil = WIN_GROUPS[g][1]
            dst = slice(sec * D_A, (sec + 1) * D_A)
            if dil == 1:
                qkv_refs[g][0, 0, :, dst] = zc.astype(BF16)
            else:
                for k in range(IN_CHUNK // LANES):
                    scr_ref[k] = zc[:, k * LANES:(k + 1) * LANES]
                for r in range(dil):
                    qkv_refs[g][0, r, :, dst] = jnp.concatenate(
                        [scr_ref[k, pl.ds(r, t // dil, stride=dil), :] for k in range(IN_CHUNK // LANES)],
                        axis=1).astype(BF16)
            if sec > 0:
                width = kvt_refs[g].shape[2]
                kvt_refs[g][0, (sec - 1) * D_A:sec * D_A, :] = zc.T[:, t - width:]
        elif lo < OFF_QB:
            gq_ref[0, :, 0:D_A] = zc.astype(BF16)
        elif lo < OFF_FB:
            hb_ref[0, :, lo - OFF_QB:lo - OFF_QB + IN_CHUNK] = zc.astype(BF16)
        elif lo < OFF_IB:
            fb_ref[0, :, lo - OFF_FB:lo - OFF_FB + IN_CHUNK] = zc
        elif lo < OFF_QM:
            hb_ref[0, :, lo - OFF_IB + D_B:lo - OFF_IB + D_B + IN_CHUNK] = zc.astype(BF16)
        elif lo < OFF_ZG:
            gq_ref[0, :, lo - OFF_QM + D_A:lo - OFF_QM + D_A + IN_CHUNK] = zc.astype(BF16)
        else:
            zg_ref[0, :, lo - OFF_ZG:lo - OFF_ZG + IN_CHUNK] = zc.astype(BF16)
    _sample_rows(zr_ref, c1_ref, c2_ref, c3_ref, cm_ref, sst_ref, slbl_ref, nh4_ref, u_ref, nst_ref,
                 memory_first=True)


def _sample_specs(ops, nt, first_row):
    first_blk = first_row // S_TILE

    def rows(a):
        return pl.BlockSpec((S_TILE,) + a.shape[1:], lambda i, j: (first_blk + i * nt + j, 0, 0))

    def const(a):
        return pl.BlockSpec(a.shape, lambda i, j: (0, 0), pipeline_mode=pl.Buffered(1))

    return [rows(a) for a in ops[:6]] + [const(a) for a in ops[6:]]


def _inproj_call(x, gain, w_bf, sample_ops):
    b, s, _ = x.shape
    t = IN_TILE
    nt = s // t
    n_rows = b * nt * S_TILE

    def tok(width):
        return pl.BlockSpec((1, t, width), lambda i, j: (i, j, 0))

    def srow(shape):
        return pl.BlockSpec((S_TILE,) + shape, lambda i, j: (i * nt + j, 0, 0))

    qkv_shapes, qkv_specs = [], []
    for _, dil in WIN_GROUPS:
        qkv_shapes.append(jax.ShapeDtypeStruct((b, dil, s // dil, D_QKV), BF16))
        qkv_specs.append(pl.BlockSpec((1, dil, t // dil, D_QKV), lambda i, j: (i, 0, j, 0)))
    kvt_shapes, kvt_specs = [], []
    for window, _ in WIN_GROUPS:
        length = min(window, s)
        width = min(t, length)
        first = nt - length // width
        kvt_shapes.append(jax.ShapeDtypeStruct((b, 2 * D_A, length), F32))
        kvt_specs.append(pl.BlockSpec((1, 2 * D_A, width),
                                      lambda i, j, first=first: (i, 0, jnp.maximum(j - first, 0))))
    return pl.pallas_call(
        _inproj_kernel,
        out_shape=(*qkv_shapes,
                   jax.ShapeDtypeStruct((b, s, D_GQ), BF16),
                   jax.ShapeDtypeStruct((b, s, D_HB), BF16),
                   jax.ShapeDtypeStruct((b, s, D_ZG), BF16),
                   jax.ShapeDtypeStruct((b, s, D_B), F32),
                   *kvt_shapes,
                   jax.ShapeDtypeStruct((n_rows, U_ROWS, LANES), F32),
                   jax.ShapeDtypeStruct((n_rows, HG_HEADS * HG_DIM, HG_DIM), F32)),
        grid=(b, nt),
        in_specs=[tok(D_MODEL),
                  pl.BlockSpec((1, D_MODEL), lambda i, j: (0, 0)),
                  pl.BlockSpec((D_MODEL, D_IN), lambda i, j: (0, 0), pipeline_mode=pl.Buffered(1)),
                  *_sample_specs(sample_ops, nt, 0)],
        out_specs=(*qkv_specs, tok(D_GQ), tok(D_HB), tok(D_ZG), tok(D_B), *kvt_specs,
                   srow((U_ROWS, LANES)), srow((HG_HEADS * HG_DIM, HG_DIM))),
        scratch_shapes=[pltpu.VMEM((IN_CHUNK // LANES, t, LANES), F32)],
        compiler_params=pltpu.CompilerParams(
            dimension_semantics=("arbitrary", "arbitrary"), vmem_limit_bytes=VMEM_LIMIT),
        name="inproj",
    )(x, gain, w_bf, *sample_ops)


DIL_STEP = 4096
OG_SLABS = 2 * D_A // LANES


def _make_dilated_kernel(dil):
    nblk = DIL_STEP // (dil * QBLK)

    def body(prev_ref, cur_ref, out_ref):
        first_step = pl.program_id(1) == 0
        masks = _head_masks(D_A, A_HEAD_DIM, A_HEADS)
        qi = lax.broadcasted_iota(jnp.int32, (A_HEADS * QBLK, 2 * QBLK), 0) % QBLK
        kj = lax.broadcasted_iota(jnp.int32, (A_HEADS * QBLK, 2 * QBLK), 1)
        band = (kj >= qi) & (kj <= qi + QBLK)
        neg_band = jnp.where(band, 0.0, -jnp.inf)
        neg_first = jnp.where(first_step, jnp.where(band & (kj >= QBLK), 0.0, -jnp.inf), neg_band)

        def keys_values(r, jb, c):
            rows = slice(jb * QBLK, (jb + 1) * QBLK)
            if jb == 0:
                return jnp.concatenate([prev_ref[0, r, :, c * D_A:(c + 1) * D_A],
                                        cur_ref[0, r, rows, c * D_A:(c + 1) * D_A]], axis=0)
            return cur_ref[0, r, (jb - 1) * QBLK:(jb + 1) * QBLK, c * D_A:(c + 1) * D_A]

        for r, jb in [(r, jb) for r in range(dil) for jb in range(nblk)]:
            q = cur_ref[0, r, jb * QBLK:(jb + 1) * QBLK, 0:D_A]
            qs = _stack_heads(q * jnp.asarray(ATT_SCALE, BF16), masks)
            s = lax.dot_general(qs, keys_values(r, jb, 1), NT_DIMS, preferred_element_type=F32)
            s = s + (neg_first if jb == 0 else neg_band)
            mx = jnp.max(s, axis=-1, keepdims=True)
            p = jnp.exp(s - mx)
            l = jnp.sum(p, axis=-1, keepdims=True)
            o_all = jnp.dot(p.astype(BF16), keys_values(r, jb, 2), preferred_element_type=F32)
            l_e = _expand_cols([l[h * QBLK:(h + 1) * QBLK] for h in range(A_HEADS)], masks)
            mx_e = _expand_cols([mx[h * QBLK:(h + 1) * QBLK] for h in range(A_HEADS)], masks)
            res = (_unstack_heads(o_all, masks, QBLK) / l_e, mx_e + jnp.log(l_e))
            for k in range(OG_SLABS):
                src = res[k // 2][:, (k % 2) * LANES:(k % 2 + 1) * LANES]
                if dil == 1:
                    out_ref[0, k, jb * QBLK:(jb + 1) * QBLK, :] = src
                else:
                    out_ref[0, k, pl.ds(jb * QBLK * dil + r, QBLK, stride=dil), :] = src
    return body


def _dilated_call(qkv, dil, g):
    b, _, n, _ = qkv.shape
    rows = DIL_STEP // dil
    nb = n // rows
    return pl.pallas_call(
        _make_dilated_kernel(dil),
        out_shape=jax.ShapeDtypeStruct((b, OG_SLABS, n * dil, LANES), F32),
        grid=(b, nb),
        in_specs=[pl.BlockSpec((1, dil, QBLK, D_QKV),
                               lambda i, u: (i, 0, jnp.maximum(u * (rows // QBLK) - 1, 0), 0)),
                  pl.BlockSpec((1, dil, rows, D_QKV), lambda i, u: (i, 0, u, 0))],
        out_specs=pl.BlockSpec((1, OG_SLABS, DIL_STEP, LANES), lambda i, u: (i, 0, u, 0)),
        compiler_params=pltpu.CompilerParams(
            dimension_semantics=("arbitrary", "arbitrary"), vmem_limit_bytes=VMEM_LIMIT),
        name=f"dilated_attn_g{g}",
    )(qkv, qkv)


def _lower_bound(l0, l1):
    m = jnp.maximum(l0, l1)
    e0, e1 = jnp.exp(l0 - m), jnp.exp(l1 - m)
    return e0 / (e0 + e1)


def _head_rms(ob, gain):
    parts = []
    for h in range(HG_HEADS):
        sl = slice(h * HG_DIM, (h + 1) * HG_DIM)
        parts.append(_rms(ob[:, sl], gain[:, sl]))
    return jnp.concatenate(parts, axis=1)


def _merge_out(x, ua, ub, um, zg, wa_ref, wb_ref, wm_ref, wo_ref, nf):
    projs = [jnp.dot(u.astype(BF16), w_ref[...], preferred_element_type=F32)
             for u, w_ref in ((ua, wa_ref), (ub, wb_ref), (um, wm_ref))]
    gates = _sigmoid(zg.astype(BF16))
    merged = None
    for k, proj in enumerate(projs):
        term = gates[:, k * D_MODEL:(k + 1) * D_MODEL] * proj.astype(BF16)
        merged = term if merged is None else merged + term
    y = x + jnp.dot(merged, wo_ref[...], preferred_element_type=F32)
    return _rms(y, nf)


TAIL_TILE = 256


def _cumsum_rows(g, tri):
    g1 = g.astype(BF16)
    g2 = (g - g1.astype(F32)).astype(BF16)
    return jnp.dot(tri, g1, preferred_element_type=F32) + jnp.dot(tri, g2, preferred_element_type=F32)


def _hgrn_tile(fb, qb, v, lb, st_ref):
    t = fb.shape[0]
    c, sub = HG_CHUNK, HG_SUB
    n_sub = c // sub
    f = lb + (1.0 - lb) * _sigmoid(fb)
    kk = 1.0 - f
    row = lax.broadcasted_iota(jnp.int32, (t, t), 0)
    col = lax.broadcasted_iota(jnp.int32, (t, t), 1)
    same_chunk = (row // c) == (col // c)
    gcs = _cumsum_rows(jnp.log2(f), (same_chunk & (col <= row)).astype(BF16))
    q = _silu(qb).astype(F32)

    loc = lax.broadcasted_iota(jnp.int32, (c, 1), 0)
    qt_c, kt_ci, qe_c, kd_c, dec_c = [], [[] for _ in range(n_sub)], [], [], []
    for ci in range(t // c):
        rows = slice(ci * c, (ci + 1) * c)
        g_c, q_c, k_c = gcs[rows], q[rows], kk[rows]
        refs = [jnp.zeros((1, D_B), F32)] + [g_c[i * sub - 1:i * sub] for i in range(1, n_sub)]
        own = refs[n_sub - 1]
        for i in range(n_sub - 2, -1, -1):
            own = jnp.where(loc < (i + 1) * sub, refs[i], own)
        qt_c.append((q_c * jnp.exp2(g_c - own)).astype(BF16))
        k_own = k_c * jnp.exp2(own - g_c)
        for i in range(n_sub):
            pieces = []
            for jb in range(n_sub):
                piece = k_own[jb * sub:(jb + 1) * sub]
                if jb < i:
                    piece = piece * jnp.exp2(refs[i] - refs[jb])
                elif jb > i:
                    piece = jnp.zeros_like(piece)
                pieces.append(piece)
            kt_ci[i].append(jnp.concatenate(pieces, axis=0).astype(BF16))
        g_last = g_c[c - 1:c]
        qe_c.append((q_c * jnp.exp2(g_c)).astype(BF16))
        kd_c.append((k_c * jnp.exp2(g_last - g_c)).astype(BF16))
        dec_c.append(jnp.exp2(g_last))
    qt = jnp.concatenate(qt_c, axis=0)
    kts = [jnp.concatenate(parts, axis=0) for parts in kt_ci]
    sub_of_row = (lax.broadcasted_iota(jnp.int32, (t, 1), 0) % c) // sub
    causal = same_chunk & (col <= row)

    heads = [slice(h * HG_DIM, (h + 1) * HG_DIM) for h in range(HG_HEADS)]
    chunks = [slice(ci * c, (ci + 1) * c) for ci in range(t // c)]
    zero = jnp.zeros((t, HG_DIM), BF16)
    atts, incs = [], []
    for sl in heads:
        q_big = jnp.concatenate([jnp.where(sub_of_row == i, qt[:, sl], zero) for i in range(n_sub)], axis=1)
        k_big = jnp.concatenate([kts[i][:, sl] for i in range(n_sub)], axis=1)
        atts.append(lax.dot_general(q_big, k_big, NT_DIMS, preferred_element_type=F32))
        incs.append([lax.dot_general(v[rows, sl], kd_c[ci][:, sl], TN_DIMS, preferred_element_type=F32)
                     for ci, rows in enumerate(chunks)])
    states = []
    for h, sl in enumerate(heads):
        atts[h] = jnp.where(causal, atts[h], 0.0).astype(BF16)
        sts = [st_ref[h]]
        for ci in range(len(chunks)):
            sts.append(sts[-1] * dec_c[ci][:, sl] + incs[h][ci])
        st_ref[h] = sts[-1]
        states.append(sts)
    outs = []
    for h, sl in enumerate(heads):
        o_intra = jnp.dot(atts[h], v[:, sl], preferred_element_type=F32)
        inter = [lax.dot_general(qe_c[ci][:, sl], states[h][ci].astype(BF16), NT_DIMS, preferred_element_type=F32)
                 for ci in range(len(chunks))]
        outs.append(o_intra + jnp.concatenate(inter, axis=0))
    return jnp.concatenate(outs, axis=1)


def _prompt_tail_kernel(x_ref, gq_ref, hb_ref, zg_ref, fb_ref, o1_ref, o2_ref, o3_ref, mkv_ref, lbl_ref,
                        nh_ref, nf_ref, wa_ref, wb_ref, wm_ref, wo_ref,
                        zr_ref, c1_ref, c2_ref, c3_ref, cm_ref, sst_ref, slbl_ref, nh4_ref, ua_ref, nsta_ref,
                        y_ref, hst_ref, u_ref, nst_ref, st_ref):
    j = pl.program_id(1)
    t = TAIL_TILE

    @pl.when(j == 0)
    def _():
        st_ref[...] = jnp.zeros_like(st_ref)

    u_ref[0] = ua_ref[...]
    nst_ref[0] = nsta_ref[...]
    _sample_rows(zr_ref, c1_ref, c2_ref, c3_ref, cm_ref, sst_ref, slbl_ref, nh4_ref, u_ref.at[1], nst_ref.at[1],
                 memory_first=False)

    masks = _head_masks(D_M, A_HEAD_DIM, A_HEADS)
    qm = gq_ref[0, :, D_A:2 * D_A] * jnp.asarray(ATT_SCALE, BF16)
    qs = _stack_heads(qm, masks)
    mk = mkv_ref[0, :, 0:D_M].astype(BF16)
    mv = mkv_ref[0, :, D_M:2 * D_M].astype(BF16)
    s = lax.dot_general(qs, mk, NT_DIMS, preferred_element_type=F32)

    lb = _lower_bound(lbl_ref[0:1], lbl_ref[1:2])
    ob = _hgrn_tile(fb_ref[0], hb_ref[0, :, 0:D_B], hb_ref[0, :, D_B:2 * D_B], lb, st_ref)
    ub = _head_rms(ob, nh_ref[...]).astype(BF16) * _silu(hb_ref[0, :, 2 * D_B:3 * D_B])

    os_ = [jnp.concatenate([r[0, 0], r[0, 1]], axis=1) for r in (o1_ref, o2_ref, o3_ref)]
    ls_ = [jnp.concatenate([r[0, 2], r[0, 3]], axis=1) for r in (o1_ref, o2_ref, o3_ref)]
    mx = jnp.maximum(jnp.maximum(ls_[0], ls_[1]), ls_[2])
    es_ = [jnp.exp(l - mx) for l in ls_]
    oa = (es_[0] * os_[0] + es_[1] * os_[1] + es_[2] * os_[2]) / (es_[0] + es_[1] + es_[2])
    ua = oa.astype(BF16) * _silu(gq_ref[0, :, 0:D_A])

    p = jnp.exp(s - jnp.max(s, axis=-1, keepdims=True))
    l = jnp.sum(p, axis=-1, keepdims=True)
    om = _unstack_heads(jnp.dot(p.astype(BF16), mv, preferred_element_type=F32) / l, masks, t)
    um = om.astype(BF16) * _silu(gq_ref[0, :, 2 * D_A:3 * D_A])

    y_ref[0] = _merge_out(x_ref[0], ua, ub, um, zg_ref[0], wa_ref, wb_ref, wm_ref, wo_ref, nf_ref[...])

    @pl.when(j == pl.num_programs(1) - 1)
    def _():
        for h in range(HG_HEADS):
            hst_ref[0, h] = st_ref[h].T


def _prompt_tail_call(x, gq, hb, zg, fb, o1, o2, o3, mkv, lbl, nh, nf, wa, wb, wm, wo,
                      sample_ops, u_first, nst_first):
    b, s, _ = x.shape
    t = TAIL_TILE
    nt = s // t
    db = sample_ops[0].shape[0]
    half = u_first.shape[0]
    assert half + b * nt * S_TILE == db and 2 * half == db, "sample rows must split evenly over both grids"

    def tok(width):
        return pl.BlockSpec((1, t, width), lambda i, j: (i, j, 0))

    def const(shape):
        return pl.BlockSpec(shape, lambda i, j: (0,) * len(shape), pipeline_mode=pl.Buffered(1))

    def first(a):
        return pl.BlockSpec((S_TILE,) + a.shape[1:], lambda i, j: (i * nt + j, 0, 0))

    def halves(shape):
        return pl.BlockSpec((2, S_TILE) + shape, lambda i, j: (0, i * nt + j, 0, 0))

    y, hst, u, nst = pl.pallas_call(
        _prompt_tail_kernel,
        out_shape=(jax.ShapeDtypeStruct((b, s, D_MODEL), F32),
                   jax.ShapeDtypeStruct((b, HG_HEADS, HG_DIM, HG_DIM), F32),
                   jax.ShapeDtypeStruct((2, half, U_ROWS, LANES), F32),
                   jax.ShapeDtypeStruct((2, half, HG_HEADS * HG_DIM, HG_DIM), F32)),
        grid=(b, nt),
        in_specs=[tok(D_MODEL), tok(D_GQ), tok(D_HB), tok(D_ZG), tok(D_B),
                  *[pl.BlockSpec((1, OG_SLABS, t, LANES), lambda i, j: (i, 0, j, 0))] * N_GROUPS,
                  pl.BlockSpec((1, MEM_LEN, 2 * D_M), lambda i, j: (i, 0, 0)),
                  const((2, D_B)), const((1, D_B)), const((1, D_MODEL)),
                  const((D_A, D_MODEL)), const((D_B, D_MODEL)), const((D_M, D_MODEL)),
                  const((D_MODEL, D_MODEL)),
                  *_sample_specs(sample_ops, nt, half), first(u_first), first(nst_first)],
        out_specs=(tok(D_MODEL),
                   pl.BlockSpec((1, HG_HEADS, HG_DIM, HG_DIM), lambda i, j: (i, 0, 0, 0)),
                   halves((U_ROWS, LANES)), halves((HG_HEADS * HG_DIM, HG_DIM))),
        scratch_shapes=[pltpu.VMEM((HG_HEADS, HG_DIM, HG_DIM), F32)],
        compiler_params=pltpu.CompilerParams(
            dimension_semantics=("arbitrary", "arbitrary"), vmem_limit_bytes=VMEM_LIMIT),
        name="prompt_tail",
    )(x, gq, hb, zg, fb, o1, o2, o3, mkv, lbl, nh, nf, wa, wb, wm, wo, *sample_ops, u_first, nst_first)
    return y, hst, u.reshape(db, U_ROWS, LANES), nst.reshape(db, HG_HEADS * HG_DIM, HG_DIM)


S_COLS = 1024


def _sample_inproj_kernel(x_ref, gain_ref, w_ref, *refs):
    n_small = (len(refs) - 4) // 2
    small_in, (zr_ref, zg_ref, kvt_ref, wbf_ref), small_out = refs[:n_small], refs[n_small:n_small + 4], refs[n_small + 4:]
    for src, dst in zip(small_in, small_out):
        dst[...] = src[...].astype(BF16)
    c = pl.program_id(0)
    h = _rms(x_ref[:, 0, :], gain_ref[...]).astype(BF16)
    w = w_ref[...].astype(BF16)
    wbf_ref[...] = w
    z = jnp.dot(h, w, preferred_element_type=F32)
    for k in range(S_COLS // LANES):
        zr_ref[:, k, :] = z[:, k * LANES:(k + 1) * LANES]
    zg_ref[...] = z
    for step in range(OFF_GA // S_COLS + 1):
        pieces = [(g, kv, (OFF_KA + (kv * N_GROUPS + g) * D_A) % S_COLS)
                  for g in range(N_GROUPS) for kv in range(2)
                  if (OFF_KA + (kv * N_GROUPS + g) * D_A) // S_COLS == step]

        @pl.when(c == step)
        def _(pieces=pieces):
            zt = z.T
            for g, kv, row0 in pieces:
                kvt_ref[g, kv * D_A:(kv + 1) * D_A, :] = zt[row0:row0 + D_A]


def _sample_inproj_call(x, gain, w, small_weights):
    db = x.shape[0]
    steps = D_IN // S_COLS
    first_zg = OFF_ZG // S_COLS
    small_specs = [pl.BlockSpec((a.shape[0] // steps, a.shape[1]), lambda c: (c, 0)) for a in small_weights]
    return pl.pallas_call(
        _sample_inproj_kernel,
        out_shape=(jax.ShapeDtypeStruct((db, Z_ROWS, LANES), F32),
                   jax.ShapeDtypeStruct((db, D_ZG), F32),
                   jax.ShapeDtypeStruct((N_GROUPS, 2 * D_A, db), F32),
                   jax.ShapeDtypeStruct((D_MODEL, D_IN), BF16),
                   *[jax.ShapeDtypeStruct(a.shape, BF16) for a in small_weights]),
        grid=(steps,),
        in_specs=[pl.BlockSpec(x.shape, lambda c: (0, 0, 0)),
                  pl.BlockSpec((1, D_MODEL), lambda c: (0, 0)),
                  pl.BlockSpec((D_MODEL, S_COLS), lambda c: (0, c)),
                  *small_specs],
        out_specs=(pl.BlockSpec((db, S_COLS // LANES, LANES), lambda c: (0, c, 0)),
                   pl.BlockSpec((db, S_COLS), lambda c: (0, jnp.maximum(c - first_zg, 0))),
                   pl.BlockSpec((N_GROUPS, 2 * D_A, db), lambda c: (0, 0, 0)),
                   pl.BlockSpec((D_MODEL, S_COLS), lambda c: (0, c)),
                   *small_specs),
        compiler_params=pltpu.CompilerParams(dimension_semantics=("arbitrary",)),
        name="sample_inproj",
    )(x, gain, w, *small_weights)


S_TILE = 1
Z_ROWS = D_IN // LANES
U_ROWS = 8


def _sample_rows(zr_ref, c1_ref, c2_ref, c3_ref, cm_ref, st_ref, slbl_ref, nh_ref, u_ref, nst_ref, *, memory_first):
    bt = S_TILE
    assert bt * Z_ROWS + HG_HEADS <= LANES
    lb_row = _lower_bound(slbl_ref[0:1], slbl_ref[1:2])
    lb_rows = [lb_row[:, h * HG_DIM:(h + 1) * HG_DIM] for h in range(HG_HEADS)]
    pad = jnp.zeros((LANES - bt * Z_ROWS - HG_HEADS, LANES), F32)
    zt = jnp.concatenate([zr_ref[b] for b in range(bt)] + lb_rows + [pad], axis=0).T

    def col(b, off, n=HG_DIM):
        j, l = divmod(off, LANES)
        return zt[l:l + n, Z_ROWS * b + j:Z_ROWS * b + j + 1]

    nh = nh_ref[...]

    def col2(b, off):
        return jnp.concatenate([col(b, off), col(b, off + LANES)], axis=0)

    def per_head(x):
        return jnp.concatenate([jnp.sum(x[h * A_HEAD_DIM:(h + 1) * A_HEAD_DIM], axis=0, keepdims=True)
                                for h in range(A_HEADS)], axis=0)

    def spread(x):
        return jnp.concatenate([jnp.broadcast_to(x[h:h + 1], (A_HEAD_DIM, 1)) for h in range(A_HEADS)], axis=0)

    def attend(b, srcs):
        parts, new_scores, qs = [], [], []
        for c_ref, dil, q_off, k_off, _ in srcs:
            q = col2(b, q_off) * ATT_SCALE
            qs.append(q)
            s = per_head(c_ref[b, 0:D_A, :] * q)
            if dil > 1:
                lane = lax.broadcasted_iota(jnp.int32, s.shape, 1)
                s = jnp.where(lane % dil == 0, s, -jnp.inf)
            parts.append(s)
            if k_off is not None:
                new_scores.append(per_head(col2(b, k_off) * q))
        if new_scores:
            lane = lax.broadcasted_iota(jnp.int32, (A_HEADS, LANES), 1)
            slab = jnp.full((A_HEADS, LANES), -jnp.inf, F32)
            for k, sn in enumerate(new_scores):
                slab = jnp.where(lane == k, sn, slab)
            parts.append(slab)
        s_all = jnp.concatenate(parts, axis=1)
        p_all = jnp.exp(s_all - jnp.max(s_all, axis=1, keepdims=True))
        den = jnp.sum(p_all, axis=1, keepdims=True)
        acc = [jnp.zeros((A_HEAD_DIM, LANES), F32) for _ in range(A_HEADS)]
        lo = 0
        for c_ref, _, _, _, _ in srcs:
            length = c_ref.shape[2]
            for k in range(length // LANES):
                ls = slice(k * LANES, (k + 1) * LANES)
                for h in range(A_HEADS):
                    vt = c_ref[b, D_A + h * A_HEAD_DIM:D_A + (h + 1) * A_HEAD_DIM, ls]
                    acc[h] = acc[h] + vt * p_all[h:h + 1, lo + k * LANES:lo + (k + 1) * LANES]
            lo += length
        num = jnp.sum(jnp.concatenate(acc, axis=0), axis=1, keepdims=True)
        k = 0
        for _, _, _, k_off, v_off in srcs:
            if k_off is not None:
                num = num + spread(p_all[:, lo + k:lo + k + 1]) * col2(b, v_off)
                k += 1
        return num / spread(den)

    window = [(c_ref, dil, OFF_QA + g * D_A, OFF_KA + g * D_A, OFF_VA + g * D_A)
              for g, (c_ref, (_, dil)) in enumerate(zip((c1_ref, c2_ref, c3_ref), WIN_GROUPS))]
    memory = [(cm_ref, 1, OFF_QM, None, None)]

    for b in range(bt):
        if memory_first:
            um = attend(b, memory) * _silu(col2(b, OFF_GM))
            ua = attend(b, window) * _silu(col2(b, OFF_GA))
        else:
            ua = attend(b, window) * _silu(col2(b, OFF_GA))
            um = attend(b, memory) * _silu(col2(b, OFF_GM))
        branch_cols = [ua[0:LANES], ua[LANES:2 * LANES], um[0:LANES], um[LANES:2 * LANES]]

        lane = lax.broadcasted_iota(jnp.int32, (LANES, LANES), 1)
        ut = jnp.zeros((LANES, LANES), F32)
        for k, cvec in enumerate(branch_cols):
            ut = ut + jnp.where(lane == k, cvec, 0.0)
        u_ref[b, 0:4, :] = ut.T[0:4, :]

        orows = []
        for h in range(HG_HEADS):
            sl = slice(h * HG_DIM, (h + 1) * HG_DIM)
            lb_col = zt[:, bt * Z_ROWS + h:bt * Z_ROWS + h + 1]
            fcol = lb_col + (1.0 - lb_col) * _sigmoid(col(b, OFF_FB + h * HG_DIM))
            qcol = _silu(col(b, OFF_QB + h * HG_DIM))
            vrow = zr_ref[b, OFF_IB // LANES + h:OFF_IB // LANES + h + 1, :]
            sn = fcol * st_ref[b, sl, :] + (1.0 - fcol) * vrow
            nst_ref[b, sl, :] = sn
            orows.append(jnp.sum(sn * qcol, axis=0, keepdims=True))
        gate = zr_ref[b, OFF_GB // LANES:OFF_GB // LANES + HG_HEADS, :]
        u_ref[b, 4:4 + HG_HEADS, :] = _rms(jnp.concatenate(orows, axis=0), nh) * _silu(gate)


def _stored_view(cache):
    db, length = cache.shape[0], cache.shape[1]
    return jnp.transpose(cache, (0, 2, 3, 4, 1)).reshape(db, 2 * D_A, length)


def _sample_out_kernel(x_ref, u_ref, zg_ref, nf_ref, wa_ref, wb_ref, wm_ref, wo_ref, y_ref):
    def rows(lo, hi):
        return jnp.concatenate([u_ref[:, k, :] for k in range(lo, hi)], axis=1)

    ua, um, ub = rows(0, 2), rows(2, 4), rows(4, U_ROWS)
    y_ref[:, 0, :] = _merge_out(x_ref[:, 0, :], ua, ub, um, zg_ref[...],
                                wa_ref, wb_ref, wm_ref, wo_ref, nf_ref[...])


def _sample_out_call(x, u, zg, nf, wa, wb, wm, wo):
    def full(a):
        return pl.BlockSpec(a.shape, lambda i: (0,) * a.ndim)

    args = (x, u, zg, nf, wa, wb, wm, wo)
    return pl.pallas_call(
        _sample_out_kernel,
        out_shape=jax.ShapeDtypeStruct(x.shape, F32),
        grid=(1,),
        in_specs=[full(a) for a in args],
        out_specs=full(x),
        compiler_params=pltpu.CompilerParams(vmem_limit_bytes=VMEM_LIMIT),
        name="sample_out",
    )(*args)


def _cache_from_stored(kvt):
    b, _, length = kvt.shape
    return jnp.transpose(kvt.reshape(b, 2, A_HEADS, A_HEAD_DIM, length), (0, 4, 1, 2, 3))[None]


def kernel(x_prompt, x_sample, mem_prompt, cache_w1_kv, cache_w2_kv, cache_w3_kv, cache_mem_kv, state_hgrn,
           norm_in, w_in, lb_logits, norm_hgrn, norm_mem, w_mem_kv, w_branch_a, w_branch_b, w_branch_m,
           w_out, norm_final):
    b, s, _ = x_prompt.shape
    db = x_sample.shape[0]
    gain_in = norm_in[0][None]
    nh = norm_hgrn[0][None]
    nf = norm_final[None]
    lbl = lb_logits.astype(F32)

    zr, zg_s, kvt_s, w_in_bf, wa, wb, wm, wo, w_mem_bf = _sample_inproj_call(
        x_sample, gain_in, w_in[0], [w[0] for w in (w_branch_a, w_branch_b, w_branch_m, w_out, w_mem_kv)])
    mkv, mkv_t = _mem_kv_call(mem_prompt, norm_mem[0][None], w_mem_bf)
    sample_ops = (zr,
                  *[_stored_view(c[0]) for c in (cache_w1_kv, cache_w2_kv, cache_w3_kv, cache_mem_kv)],
                  state_hgrn[0].reshape(db, HG_HEADS * HG_DIM, HG_DIM), lbl, nh.reshape(HG_HEADS, HG_DIM))
    qkv0, qkv1, qkv2, gq, hb, zg, fb, kvt0, kvt1, kvt2, u_first, nst_first = _inproj_call(
        x_prompt, gain_in, w_in_bf, sample_ops)
    og = [_dilated_call(qkv, dil, g) for g, (qkv, (_, dil)) in enumerate(zip((qkv0, qkv1, qkv2), WIN_GROUPS))]
    y_prompt, hg_prompt, u, nst = _prompt_tail_call(
        x_prompt, gq, hb, zg, fb, og[0], og[1], og[2], mkv, lbl, nh, nf, wa, wb, wm, wo,
        sample_ops, u_first, nst_first)
    pw = [_cache_from_stored(kvt) for kvt in (kvt0, kvt1, kvt2)]
    new_mem = _cache_from_stored(mkv_t)

    y_sample = _sample_out_call(x_sample, u, zg_s, nf, wa, wb, wm, wo)
    sw = [jnp.transpose(kvt_s[g].reshape(2, A_HEADS, A_HEAD_DIM, db), (3, 0, 1, 2))[None, :, None]
          for g in range(N_GROUPS)]
    new_hg_sample = nst.reshape(1, db, HG_HEADS, HG_DIM, HG_DIM)

    return (y_prompt, y_sample, pw[0], pw[1], pw[2], new_mem, hg_prompt[None],
            sw[0], sw[1], sw[2], new_hg_sample)
```

```python
import jax
import jax.numpy as jnp
from jax import lax
from jax.experimental import pallas as pl
from jax.experimental.pallas import tpu as pltpu

F32 = jnp.float32
BF16 = jnp.bfloat16

D_MODEL = 1024
WIN_GROUPS = ((128, 1), (512, 4), (2048, 16))
N_GROUPS = 3
A_HEADS = 4
A_HEAD_DIM = 64
D_A = 256
QBLK = 128
HG_HEADS = 4
HG_DIM = 128
D_B = 512
HG_CHUNK = 64
HG_SUB = 16
MEM_LEN = 256
D_M = 256
EPS = 1e-6
D_IN = 8192
LANES = 128
OFF_QA, OFF_KA, OFF_VA, OFF_GA = 0, 768, 1536, 2304
OFF_QB, OFF_FB, OFF_IB, OFF_GB = 2560, 3072, 3584, 4096
OFF_QM, OFF_GM, OFF_ZG = 4608, 4864, 5120
ATT_SCALE = A_HEAD_DIM ** -0.5

VMEM_LIMIT = 56 * 1024 * 1024

NT_DIMS = (((1,), (1,)), ((), ()))
TN_DIMS = (((0,), (0,)), ((), ()))


def _sigmoid(x):
    return 0.5 * jnp.tanh(0.5 * x) + 0.5


def _silu(x):
    h = 0.5 * x
    return h * jnp.tanh(h) + h


def _rms(x, gain):
    return x * lax.rsqrt(jnp.mean(x * x, axis=-1, keepdims=True) + EPS) * gain


def _head_masks(width, head_dim, n_heads):
    lane = lax.broadcasted_iota(jnp.int32, (1, width), 1)
    return [(lane >= h * head_dim) & (lane < (h + 1) * head_dim) for h in range(n_heads)]


def _stack_heads(q, masks):
    zero = jnp.zeros_like(q)
    return jnp.concatenate([jnp.where(m, q, zero) for m in masks], axis=0)


def _unstack_heads(o_all, masks, t):
    n = len(masks)
    out = o_all[(n - 1) * t:n * t]
    for h in range(n - 2, -1, -1):
        out = jnp.where(masks[h], o_all[h * t:(h + 1) * t], out)
    return out


def _expand_cols(cols, masks):
    out = jnp.broadcast_to(cols[-1], (cols[-1].shape[0], masks[0].shape[1]))
    for h in range(len(masks) - 2, -1, -1):
        out = jnp.where(masks[h], cols[h], out)
    return out


def _mem_kv_kernel(mem_ref, gain_ref, w_ref, out_ref, out_t_ref):
    hm = _rms(mem_ref[0], gain_ref[...]).astype(BF16)
    kv = jnp.dot(hm, w_ref[...], preferred_element_type=F32)
    out_ref[0] = kv
    out_t_ref[0] = kv.T


def _mem_kv_call(mem, gain, w_bf):
    b = mem.shape[0]
    return pl.pallas_call(
        _mem_kv_kernel,
        out_shape=(jax.ShapeDtypeStruct((b, MEM_LEN, 2 * D_M), F32),
                   jax.ShapeDtypeStruct((b, 2 * D_M, MEM_LEN), F32)),
        grid=(b,),
        in_specs=[pl.BlockSpec((1, MEM_LEN, D_MODEL), lambda i: (i, 0, 0)),
                  pl.BlockSpec((1, D_MODEL), lambda i: (0, 0)),
                  pl.BlockSpec((D_MODEL, 2 * D_M), lambda i: (0, 0))],
        out_specs=(pl.BlockSpec((1, MEM_LEN, 2 * D_M), lambda i: (i, 0, 0)),
                   pl.BlockSpec((1, 2 * D_M, MEM_LEN), lambda i: (i, 0, 0))),
        name="mem_kv",
    )(mem, gain, w_bf)


IN_TILE = 256
IN_CHUNK = 256
D_QKV = 3 * D_A
D_GQ = 3 * D_A
D_HB = 3 * D_B
D_ZG = 3 * D_MODEL


def _inproj_kernel(x_ref, gain_ref, w_ref, zr_ref, c1_ref, c2_ref, c3_ref, cm_ref, sst_ref, slbl_ref, nh4_ref,
                   qkv0_ref, qkv1_ref, qkv2_ref, gq_ref, hb_ref, zg_ref, fb_ref, kvt0_ref, kvt1_ref, kvt2_ref,
                   u_ref, nst_ref, scr_ref, c3_buf, c3_sem):
    step = pl.program_id(0) * pl.num_programs(1) + pl.program_id(1)
    n_steps = pl.num_programs(0) * pl.num_programs(1)

    def fetch(row, slot):
        return pltpu.make_async_copy(c3_ref.at[pl.ds(row, 1)], c3_buf.at[pl.ds(slot, 1)], c3_sem.at[slot])

    @pl.when(step == 0)
    def _():
        for r in range(SAMPLE_CACHE_BUFFERS - 1):
            fetch(r, r).start()

    ahead = step + SAMPLE_CACHE_BUFFERS - 1

    @pl.when(ahead < n_steps)
    def _():
        fetch(ahead, ahead % SAMPLE_CACHE_BUFFERS).start()

    slot = step % SAMPLE_CACHE_BUFFERS
    fetch(step, slot).wait()
    c3_row = c3_buf.at[pl.ds(slot, 1)]
    qkv_refs = (qkv0_ref, qkv1_ref, qkv2_ref)
    kvt_refs = (kvt0_ref, kvt1_ref, kvt2_ref)
    t = IN_TILE
    h = _rms(x_ref[0], gain_ref[...]).astype(BF16)
    for c in range(D_IN // IN_CHUNK):
        lo = c * IN_CHUNK
        zc = jnp.dot(h, w_ref[:, lo:lo + IN_CHUNK], preferred_element_type=F32)
        if lo < OFF_GA:
            sec, g = divmod(c, N_GROUPS)
            dil = WIN_GROUPS[g][1]
            dst = slice(sec * D_A, (sec + 1) * D_A)
            if dil == 1:
                qkv_refs[g][0, 0, :, dst] = zc.astype(BF16)
            else:
                for k in range(IN_CHUNK // LANES):
                    scr_ref[k] = zc[:, k * LANES:(k + 1) * LANES]
                for r in range(dil):
                    qkv_refs[g][0, r, :, dst] = jnp.concatenate(
                        [scr_ref[k, pl.ds(r, t // dil, stride=dil), :] for k in range(IN_CHUNK // LANES)],
                        axis=1).astype(BF16)
            if sec > 0:
                width = kvt_refs[g].shape[2]
                kvt_refs[g][0, (sec - 1) * D_A:sec * D_A, :] = zc.T[:, t - width:]
        elif lo < OFF_QB:
            gq_ref[0, :, 0:D_A] = zc.astype(BF16)
        elif lo < OFF_FB:
            hb_ref[0, :, lo - OFF_QB:lo - OFF_QB + IN_CHUNK] = zc.astype(BF16)
        elif lo < OFF_IB:
            fb_ref[0, :, lo - OFF_FB:lo - OFF_FB + IN_CHUNK] = zc
        elif lo < OFF_QM:
            hb_ref[0, :, lo - OFF_IB + D_B:lo - OFF_IB + D_B + IN_CHUNK] = zc.astype(BF16)
        elif lo < OFF_ZG:
            gq_ref[0, :, lo - OFF_QM + D_A:lo - OFF_QM + D_A + IN_CHUNK] = zc.astype(BF16)
        else:
            zg_ref[0, :, lo - OFF_ZG:lo - OFF_ZG + IN_CHUNK] = zc.astype(BF16)
    _sample_rows(zr_ref, c1_ref, c2_ref, c3_row, cm_ref, sst_ref, slbl_ref, nh4_ref, u_ref, nst_ref,
                 memory_first=True)


SAMPLE_CACHE_BUFFERS = 3


def _sample_specs(ops, nt, first_row):
    first_blk = first_row // S_TILE

    def rows(a):
        return pl.BlockSpec((S_TILE,) + a.shape[1:], lambda i, j: (first_blk + i * nt + j, 0, 0))

    def const(a):
        return pl.BlockSpec(a.shape, lambda i, j: (0, 0), pipeline_mode=pl.Buffered(1))

    return [rows(a) for a in ops[:6]] + [const(a) for a in ops[6:]]


def _inproj_call(x, gain, w_bf, sample_ops):
    b, s, _ = x.shape
    t = IN_TILE
    nt = s // t
    n_rows = b * nt * S_TILE

    def tok(width):
        return pl.BlockSpec((1, t, width), lambda i, j: (i, j, 0))

    def srow(shape):
        return pl.BlockSpec((S_TILE,) + shape, lambda i, j: (i * nt + j, 0, 0))

    qkv_shapes, qkv_specs = [], []
    for _, dil in WIN_GROUPS:
        qkv_shapes.append(jax.ShapeDtypeStruct((b, dil, s // dil, D_QKV), BF16))
        qkv_specs.append(pl.BlockSpec((1, dil, t // dil, D_QKV), lambda i, j: (i, 0, j, 0)))
    kvt_shapes, kvt_specs = [], []
    for window, _ in WIN_GROUPS:
        length = min(window, s)
        width = min(t, length)
        first = nt - length // width
        kvt_shapes.append(jax.ShapeDtypeStruct((b, 2 * D_A, length), F32))
        kvt_specs.append(pl.BlockSpec((1, 2 * D_A, width),
                                      lambda i, j, first=first: (i, 0, jnp.maximum(j - first, 0))))
    assert S_TILE == 1 and n_rows >= SAMPLE_CACHE_BUFFERS
    sample_specs = _sample_specs(sample_ops, nt, 0)
    sample_specs[3] = pl.BlockSpec(memory_space=pl.ANY)
    return pl.pallas_call(
        _inproj_kernel,
        out_shape=(*qkv_shapes,
                   jax.ShapeDtypeStruct((b, s, D_GQ), BF16),
                   jax.ShapeDtypeStruct((b, s, D_HB), BF16),
                   jax.ShapeDtypeStruct((b, s, D_ZG), BF16),
                   jax.ShapeDtypeStruct((b, s, D_B), F32),
                   *kvt_shapes,
                   jax.ShapeDtypeStruct((n_rows, U_ROWS, LANES), F32),
                   jax.ShapeDtypeStruct((n_rows, HG_HEADS * HG_DIM, HG_DIM), F32)),
        grid=(b, nt),
        in_specs=[tok(D_MODEL),
                  pl.BlockSpec((1, D_MODEL), lambda i, j: (0, 0)),
                  pl.BlockSpec((D_MODEL, D_IN), lambda i, j: (0, 0), pipeline_mode=pl.Buffered(1)),
                  *sample_specs],
        out_specs=(*qkv_specs, tok(D_GQ), tok(D_HB), tok(D_ZG), tok(D_B), *kvt_specs,
                   srow((U_ROWS, LANES)), srow((HG_HEADS * HG_DIM, HG_DIM))),
        scratch_shapes=[pltpu.VMEM((IN_CHUNK // LANES, t, LANES), F32),
                        pltpu.VMEM((SAMPLE_CACHE_BUFFERS,) + sample_ops[3].shape[1:], F32),
                        pltpu.SemaphoreType.DMA((SAMPLE_CACHE_BUFFERS,))],
        compiler_params=pltpu.CompilerParams(
            dimension_semantics=("arbitrary", "arbitrary"), vmem_limit_bytes=VMEM_LIMIT),
        name="inproj",
    )(x, gain, w_bf, *sample_ops)


DIL_STEP = 2048
OG_SLABS = 2 * D_A // LANES


def _make_dilated_kernel(dil):
    nblk = DIL_STEP // (dil * QBLK)

    def body(prev_ref, cur_ref, out_ref):
        first_step = pl.program_id(1) == 0
        masks = _head_masks(D_A, A_HEAD_DIM, A_HEADS)
        qi = lax.broadcasted_iota(jnp.int32, (A_HEADS * QBLK, 2 * QBLK), 0) % QBLK
        kj = lax.broadcasted_iota(jnp.int32, (A_HEADS * QBLK, 2 * QBLK), 1)
        band = (kj >= qi) & (kj <= qi + QBLK)
        neg_band = jnp.where(band, 0.0, -jnp.inf)
        neg_first = jnp.where(first_step, jnp.where(band & (kj >= QBLK), 0.0, -jnp.inf), neg_band)

        def keys_values(r, jb, c):
            rows = slice(jb * QBLK, (jb + 1) * QBLK)
            if jb == 0:
                return jnp.concatenate([prev_ref[0, r, :, c * D_A:(c + 1) * D_A],
                                        cur_ref[0, r, rows, c * D_A:(c + 1) * D_A]], axis=0)
            return cur_ref[0, r, (jb - 1) * QBLK:(jb + 1) * QBLK, c * D_A:(c + 1) * D_A]

        for r, jb in [(r, jb) for r in range(dil) for jb in range(nblk)]:
            q = cur_ref[0, r, jb * QBLK:(jb + 1) * QBLK, 0:D_A]
            qs = _stack_heads(q * jnp.asarray(ATT_SCALE, BF16), masks)
            s = lax.dot_general(qs, keys_values(r, jb, 1), NT_DIMS, preferred_element_type=F32)
            s = s + (neg_first if jb == 0 else neg_band)
            mx = jnp.max(s, axis=-1, keepdims=True)
            p = jnp.exp(s - mx)
            l = jnp.sum(p, axis=-1, keepdims=True)
            o_all = jnp.dot(p.astype(BF16), keys_values(r, jb, 2), preferred_element_type=F32)
            l_e = _expand_cols([l[h * QBLK:(h + 1) * QBLK] for h in range(A_HEADS)], masks)
            mx_e = _expand_cols([mx[h * QBLK:(h + 1) * QBLK] for h in range(A_HEADS)], masks)
            res = (_unstack_heads(o_all, masks, QBLK) / l_e, mx_e + jnp.log(l_e))
            for k in range(OG_SLABS):
                src = res[k // 2][:, (k % 2) * LANES:(k % 2 + 1) * LANES]
                if dil == 1:
                    out_ref[0, k, jb * QBLK:(jb + 1) * QBLK, :] = src
                else:
                    out_ref[0, k, pl.ds(jb * QBLK * dil + r, QBLK, stride=dil), :] = src
    return body


def _dilated_call(qkv, dil, g):
    b, _, n, _ = qkv.shape
    rows = DIL_STEP // dil
    nb = n // rows
    return pl.pallas_call(
        _make_dilated_kernel(dil),
        out_shape=jax.ShapeDtypeStruct((b, OG_SLABS, n * dil, LANES), F32),
        grid=(b, nb),
        in_specs=[pl.BlockSpec((1, dil, QBLK, D_QKV),
                               lambda i, u: (i, 0, jnp.maximum(u * (rows // QBLK) - 1, 0), 0)),
                  pl.BlockSpec((1, dil, rows, D_QKV), lambda i, u: (i, 0, u, 0))],
        out_specs=pl.BlockSpec((1, OG_SLABS, DIL_STEP, LANES), lambda i, u: (i, 0, u, 0)),
        compiler_params=pltpu.CompilerParams(
            dimension_semantics=("arbitrary", "arbitrary"), vmem_limit_bytes=VMEM_LIMIT),
        name=f"dilated_attn_g{g}",
    )(qkv, qkv)


def _lower_bound(l0, l1):
    m = jnp.maximum(l0, l1)
    e0, e1 = jnp.exp(l0 - m), jnp.exp(l1 - m)
    return e0 / (e0 + e1)


def _head_rms(ob, gain):
    parts = []
    for h in range(HG_HEADS):
        sl = slice(h * HG_DIM, (h + 1) * HG_DIM)
        parts.append(_rms(ob[:, sl], gain[:, sl]))
    return jnp.concatenate(parts, axis=1)


def _merge_out(x, ua, ub, um, zg, wa_ref, wb_ref, wm_ref, wo_ref, nf):
    projs = [jnp.dot(u.astype(BF16), w_ref[...], preferred_element_type=F32)
             for u, w_ref in ((ua, wa_ref), (ub, wb_ref), (um, wm_ref))]
    gates = _sigmoid(zg.astype(BF16))
    merged = None
    for k, proj in enumerate(projs):
        term = gates[:, k * D_MODEL:(k + 1) * D_MODEL] * proj.astype(BF16)
        merged = term if merged is None else merged + term
    y = x + jnp.dot(merged, wo_ref[...], preferred_element_type=F32)
    return _rms(y, nf)


TAIL_TILE = 256


def _cumsum_rows(g, tri):
    g1 = g.astype(BF16)
    g2 = (g - g1.astype(F32)).astype(BF16)
    return jnp.dot(tri, g1, preferred_element_type=F32) + jnp.dot(tri, g2, preferred_element_type=F32)


def _hgrn_tile(fb, qb, v, lb, st_ref):
    t = fb.shape[0]
    c, sub = HG_CHUNK, HG_SUB
    n_sub = c // sub
    f = lb + (1.0 - lb) * _sigmoid(fb)
    kk = 1.0 - f
    row = lax.broadcasted_iota(jnp.int32, (t, t), 0)
    col = lax.broadcasted_iota(jnp.int32, (t, t), 1)
    same_chunk = (row // c) == (col // c)
    gcs = _cumsum_rows(jnp.log2(f), (same_chunk & (col <= row)).astype(BF16))
    q = _silu(qb).astype(F32)

    loc = lax.broadcasted_iota(jnp.int32, (c, 1), 0)
    qt_c, kt_ci, qe_c, kd_c, dec_c = [], [[] for _ in range(n_sub)], [], [], []
    for ci in range(t // c):
        rows = slice(ci * c, (ci + 1) * c)
        g_c, q_c, k_c = gcs[rows], q[rows], kk[rows]
        refs = [jnp.zeros((1, D_B), F32)] + [g_c[i * sub - 1:i * sub] for i in range(1, n_sub)]
        own = refs[n_sub - 1]
        for i in range(n_sub - 2, -1, -1):
            own = jnp.where(loc < (i + 1) * sub, refs[i], own)
        qt_c.append((q_c * jnp.exp2(g_c - own)).astype(BF16))
        k_own = k_c * jnp.exp2(own - g_c)
        for i in range(n_sub):
            pieces = []
            for jb in range(n_sub):
                piece = k_own[jb * sub:(jb + 1) * sub]
                if jb < i:
                    piece = piece * jnp.exp2(refs[i] - refs[jb])
                elif jb > i:
                    piece = jnp.zeros_like(piece)
                pieces.append(piece)
            kt_ci[i].append(jnp.concatenate(pieces, axis=0).astype(BF16))
        g_last = g_c[c - 1:c]
        qe_c.append((q_c * jnp.exp2(g_c)).astype(BF16))
        kd_c.append((k_c * jnp.exp2(g_last - g_c)).astype(BF16))
        dec_c.append(jnp.exp2(g_last))
    qt = jnp.concatenate(qt_c, axis=0)
    kts = [jnp.concatenate(parts, axis=0) for parts in kt_ci]
    sub_of_row = (lax.broadcasted_iota(jnp.int32, (t, 1), 0) % c) // sub
    causal = same_chunk & (col <= row)

    heads = [slice(h * HG_DIM, (h + 1) * HG_DIM) for h in range(HG_HEADS)]
    chunks = [slice(ci * c, (ci + 1) * c) for ci in range(t // c)]
    zero = jnp.zeros((t, HG_DIM), BF16)
    atts, incs = [], []
    for sl in heads:
        q_big = jnp.concatenate([jnp.where(sub_of_row == i, qt[:, sl], zero) for i in range(n_sub)], axis=1)
        k_big = jnp.concatenate([kts[i][:, sl] for i in range(n_sub)], axis=1)
        atts.append(lax.dot_general(q_big, k_big, NT_DIMS, preferred_element_type=F32))
        incs.append([lax.dot_general(v[rows, sl], kd_c[ci][:, sl], TN_DIMS, preferred_element_type=F32)
                     for ci, rows in enumerate(chunks)])
    states = []
    for h, sl in enumerate(heads):
        atts[h] = jnp.where(causal, atts[h], 0.0).astype(BF16)
        sts = [st_ref[h]]
        for ci in range(len(chunks)):
            sts.append(sts[-1] * dec_c[ci][:, sl] + incs[h][ci])
        st_ref[h] = sts[-1]
        states.append(sts)
    outs = []
    for h, sl in enumerate(heads):
        o_intra = jnp.dot(atts[h], v[:, sl], preferred_element_type=F32)
        inter = [lax.dot_general(qe_c[ci][:, sl], states[h][ci].astype(BF16), NT_DIMS, preferred_element_type=F32)
                 for ci in range(len(chunks))]
        outs.append(o_intra + jnp.concatenate(inter, axis=0))
    return jnp.concatenate(outs, axis=1)


def _prompt_tail_kernel(x_ref, gq_ref, hb_ref, zg_ref, fb_ref, o1_ref, o2_ref, o3_ref, mkv_ref, lbl_ref,
                        nh_ref, nf_ref, wa_ref, wb_ref, wm_ref, wo_ref,
                        zr_ref, c1_ref, c2_ref, c3_ref, cm_ref, sst_ref, slbl_ref, nh4_ref, ua_ref, nsta_ref,
                        y_ref, hst_ref, u_ref, nst_ref, st_ref):
    j = pl.program_id(1)
    t = TAIL_TILE

    @pl.when(j == 0)
    def _():
        st_ref[...] = jnp.zeros_like(st_ref)

    u_ref[0] = ua_ref[...]
    nst_ref[0] = nsta_ref[...]
    _sample_rows(zr_ref, c1_ref, c2_ref, c3_ref, cm_ref, sst_ref, slbl_ref, nh4_ref, u_ref.at[1], nst_ref.at[1],
                 memory_first=False)

    masks = _head_masks(D_M, A_HEAD_DIM, A_HEADS)
    qm = gq_ref[0, :, D_A:2 * D_A] * jnp.asarray(ATT_SCALE, BF16)
    qs = _stack_heads(qm, masks)
    mk = mkv_ref[0, :, 0:D_M].astype(BF16)
    mv = mkv_ref[0, :, D_M:2 * D_M].astype(BF16)
    s = lax.dot_general(qs, mk, NT_DIMS, preferred_element_type=F32)

    lb = _lower_bound(lbl_ref[0:1], lbl_ref[1:2])
    ob = _hgrn_tile(fb_ref[0], hb_ref[0, :, 0:D_B], hb_ref[0, :, D_B:2 * D_B], lb, st_ref)
    ub = _head_rms(ob, nh_ref[...]).astype(BF16) * _silu(hb_ref[0, :, 2 * D_B:3 * D_B])

    os_ = [jnp.concatenate([r[0, 0], r[0, 1]], axis=1) for r in (o1_ref, o2_ref, o3_ref)]
    ls_ = [jnp.concatenate([r[0, 2], r[0, 3]], axis=1) for r in (o1_ref, o2_ref, o3_ref)]
    mx = jnp.maximum(jnp.maximum(ls_[0], ls_[1]), ls_[2])
    es_ = [jnp.exp(l - mx) for l in ls_]
    oa = (es_[0] * os_[0] + es_[1] * os_[1] + es_[2] * os_[2]) / (es_[0] + es_[1] + es_[2])
    ua = oa.astype(BF16) * _silu(gq_ref[0, :, 0:D_A])

    p = jnp.exp(s - jnp.max(s, axis=-1, keepdims=True))
    l = jnp.sum(p, axis=-1, keepdims=True)
    om = _unstack_heads(jnp.dot(p.astype(BF16), mv, preferred_element_type=F32) / l, masks, t)
    um = om.astype(BF16) * _silu(gq_ref[0, :, 2 * D_A:3 * D_A])

    y_ref[0] = _merge_out(x_ref[0], ua, ub, um, zg_ref[0], wa_ref, wb_ref, wm_ref, wo_ref, nf_ref[...])

    @pl.when(j == pl.num_programs(1) - 1)
    def _():
        for h in range(HG_HEADS):
            hst_ref[0, h] = st_ref[h].T


def _prompt_tail_call(x, gq, hb, zg, fb, o1, o2, o3, mkv, lbl, nh, nf, wa, wb, wm, wo,
                      sample_ops, u_first, nst_first):
    b, s, _ = x.shape
    t = TAIL_TILE
    nt = s // t
    db = sample_ops[0].shape[0]
    half = u_first.shape[0]
    assert half + b * nt * S_TILE == db and 2 * half == db, "sample rows must split evenly over both grids"

    def tok(width):
        return pl.BlockSpec((1, t, width), lambda i, j: (i, j, 0))

    def const(shape):
        return pl.BlockSpec(shape, lambda i, j: (0,) * len(shape), pipeline_mode=pl.Buffered(1))

    def first(a):
        return pl.BlockSpec((S_TILE,) + a.shape[1:], lambda i, j: (i * nt + j, 0, 0))

    def halves(shape):
        return pl.BlockSpec((2, S_TILE) + shape, lambda i, j: (0, i * nt + j, 0, 0))

    y, hst, u, nst = pl.pallas_call(
        _prompt_tail_kernel,
        out_shape=(jax.ShapeDtypeStruct((b, s, D_MODEL), F32),
                   jax.ShapeDtypeStruct((b, HG_HEADS, HG_DIM, HG_DIM), F32),
                   jax.ShapeDtypeStruct((2, half, U_ROWS, LANES), F32),
                   jax.ShapeDtypeStruct((2, half, HG_HEADS * HG_DIM, HG_DIM), F32)),
        grid=(b, nt),
        in_specs=[tok(D_MODEL), tok(D_GQ), tok(D_HB), tok(D_ZG), tok(D_B),
                  *[pl.BlockSpec((1, OG_SLABS, t, LANES), lambda i, j: (i, 0, j, 0))] * N_GROUPS,
                  pl.BlockSpec((1, MEM_LEN, 2 * D_M), lambda i, j: (i, 0, 0)),
                  const((2, D_B)), const((1, D_B)), const((1, D_MODEL)),
                  const((D_A, D_MODEL)), const((D_B, D_MODEL)), const((D_M, D_MODEL)),
                  const((D_MODEL, D_MODEL)),
                  *_sample_specs(sample_ops, nt, half), first(u_first), first(nst_first)],
        out_specs=(tok(D_MODEL),
                   pl.BlockSpec((1, HG_HEADS, HG_DIM, HG_DIM), lambda i, j: (i, 0, 0, 0)),
                   halves((U_ROWS, LANES)), halves((HG_HEADS * HG_DIM, HG_DIM))),
        scratch_shapes=[pltpu.VMEM((HG_HEADS, HG_DIM, HG_DIM), F32)],
        compiler_params=pltpu.CompilerParams(
            dimension_semantics=("arbitrary", "arbitrary"), vmem_limit_bytes=VMEM_LIMIT),
        name="prompt_tail",
    )(x, gq, hb, zg, fb, o1, o2, o3, mkv, lbl, nh, nf, wa, wb, wm, wo, *sample_ops, u_first, nst_first)
    return y, hst, u.reshape(db, U_ROWS, LANES), nst.reshape(db, HG_HEADS * HG_DIM, HG_DIM)


S_COLS = 1024


def _sample_inproj_kernel(x_ref, gain_ref, w_ref, *refs):
    n_small = (len(refs) - 4) // 2
    small_in, (zr_ref, zg_ref, kvt_ref, wbf_ref), small_out = refs[:n_small], refs[n_small:n_small + 4], refs[n_small + 4:]
    for src, dst in zip(small_in, small_out):
        dst[...] = src[...].astype(BF16)
    c = pl.program_id(0)
    h = _rms(x_ref[:, 0, :], gain_ref[...]).astype(BF16)
    w = w_ref[...].astype(BF16)
    wbf_ref[...] = w
    z = jnp.dot(h, w, preferred_element_type=F32)
    for k in range(S_COLS // LANES):
        zr_ref[:, k, :] = z[:, k * LANES:(k + 1) * LANES]
    zg_ref[...] = z
    for step in range(OFF_GA // S_COLS + 1):
        pieces = [(g, kv, (OFF_KA + (kv * N_GROUPS + g) * D_A) % S_COLS)
                  for g in range(N_GROUPS) for kv in range(2)
                  if (OFF_KA + (kv * N_GROUPS + g) * D_A) // S_COLS == step]

        @pl.when(c == step)
        def _(pieces=pieces):
            zt = z.T
            for g, kv, row0 in pieces:
                kvt_ref[g, kv * D_A:(kv + 1) * D_A, :] = zt[row0:row0 + D_A]


def _sample_inproj_call(x, gain, w, small_weights):
    db = x.shape[0]
    steps = D_IN // S_COLS
    first_zg = OFF_ZG // S_COLS
    small_specs = [pl.BlockSpec((a.shape[0] // steps, a.shape[1]), lambda c: (c, 0)) for a in small_weights]
    return pl.pallas_call(
        _sample_inproj_kernel,
        out_shape=(jax.ShapeDtypeStruct((db, Z_ROWS, LANES), F32),
                   jax.ShapeDtypeStruct((db, D_ZG), F32),
                   jax.ShapeDtypeStruct((N_GROUPS, 2 * D_A, db), F32),
                   jax.ShapeDtypeStruct((D_MODEL, D_IN), BF16),
                   *[jax.ShapeDtypeStruct(a.shape, BF16) for a in small_weights]),
        grid=(steps,),
        in_specs=[pl.BlockSpec(x.shape, lambda c: (0, 0, 0)),
                  pl.BlockSpec((1, D_MODEL), lambda c: (0, 0)),
                  pl.BlockSpec((D_MODEL, S_COLS), lambda c: (0, c)),
                  *small_specs],
        out_specs=(pl.BlockSpec((db, S_COLS // LANES, LANES), lambda c: (0, c, 0)),
                   pl.BlockSpec((db, S_COLS), lambda c: (0, jnp.maximum(c - first_zg, 0))),
                   pl.BlockSpec((N_GROUPS, 2 * D_A, db), lambda c: (0, 0, 0)),
                   pl.BlockSpec((D_MODEL, S_COLS), lambda c: (0, c)),
                   *small_specs),
        compiler_params=pltpu.CompilerParams(dimension_semantics=("arbitrary",)),
        name="sample_inproj",
    )(x, gain, w, *small_weights)


S_TILE = 1
Z_ROWS = D_IN // LANES
U_ROWS = 8


def _sample_rows(zr_ref, c1_ref, c2_ref, c3_ref, cm_ref, st_ref, slbl_ref, nh_ref, u_ref, nst_ref, *, memory_first):
    bt = S_TILE
    assert bt * Z_ROWS + HG_HEADS <= LANES
    lb_row = _lower_bound(slbl_ref[0:1], slbl_ref[1:2])
    lb_rows = [lb_row[:, h * HG_DIM:(h + 1) * HG_DIM] for h in range(HG_HEADS)]
    pad = jnp.zeros((LANES - bt * Z_ROWS - HG_HEADS, LANES), F32)
    zt = jnp.concatenate([zr_ref[b] for b in range(bt)] + lb_rows + [pad], axis=0).T

    def col(b, off, n=HG_DIM):
        j, l = divmod(off, LANES)
        return zt[l:l + n, Z_ROWS * b + j:Z_ROWS * b + j + 1]

    nh = nh_ref[...]

    def col2(b, off):
        return jnp.concatenate([col(b, off), col(b, off + LANES)], axis=0)

    def per_head(x):
        return jnp.concatenate([jnp.sum(x[h * A_HEAD_DIM:(h + 1) * A_HEAD_DIM], axis=0, keepdims=True)
                                for h in range(A_HEADS)], axis=0)

    def spread(x):
        return jnp.concatenate([jnp.broadcast_to(x[h:h + 1], (A_HEAD_DIM, 1)) for h in range(A_HEADS)], axis=0)

    def attend(b, srcs):
        parts, new_scores, qs = [], [], []
        for c_ref, dil, q_off, k_off, _ in srcs:
            q = col2(b, q_off) * ATT_SCALE
            qs.append(q)
            s = per_head(c_ref[b, 0:D_A, :] * q)
            if dil > 1:
                lane = lax.broadcasted_iota(jnp.int32, s.shape, 1)
                s = jnp.where(lane % dil == 0, s, -jnp.inf)
            parts.append(s)
            if k_off is not None:
                new_scores.append(per_head(col2(b, k_off) * q))
        if new_scores:
            lane = lax.broadcasted_iota(jnp.int32, (A_HEADS, LANES), 1)
            slab = jnp.full((A_HEADS, LANES), -jnp.inf, F32)
            for k, sn in enumerate(new_scores):
                slab = jnp.where(lane == k, sn, slab)
            parts.append(slab)
        s_all = jnp.concatenate(parts, axis=1)
        p_all = jnp.exp(s_all - jnp.max(s_all, axis=1, keepdims=True))
        den = jnp.sum(p_all, axis=1, keepdims=True)
        acc = [jnp.zeros((A_HEAD_DIM, LANES), F32) for _ in range(A_HEADS)]
        lo = 0
        for c_ref, _, _, _, _ in srcs:
            length = c_ref.shape[2]
            for k in range(length // LANES):
                ls = slice(k * LANES, (k + 1) * LANES)
                for h in range(A_HEADS):
                    vt = c_ref[b, D_A + h * A_HEAD_DIM:D_A + (h + 1) * A_HEAD_DIM, ls]
                    acc[h] = acc[h] + vt * p_all[h:h + 1, lo + k * LANES:lo + (k + 1) * LANES]
            lo += length
        num = jnp.sum(jnp.concatenate(acc, axis=0), axis=1, keepdims=True)
        k = 0
        for _, _, _, k_off, v_off in srcs:
            if k_off is not None:
                num = num + spread(p_all[:, lo + k:lo + k + 1]) * col2(b, v_off)
                k += 1
        return num / spread(den)

    window = [(c_ref, dil, OFF_QA + g * D_A, OFF_KA + g * D_A, OFF_VA + g * D_A)
              for g, (c_ref, (_, dil)) in enumerate(zip((c1_ref, c2_ref, c3_ref), WIN_GROUPS))]
    memory = [(cm_ref, 1, OFF_QM, None, None)]

    for b in range(bt):
        if memory_first:
            um = attend(b, memory) * _silu(col2(b, OFF_GM))
            ua = attend(b, window) * _silu(col2(b, OFF_GA))
        else:
            ua = attend(b, window) * _silu(col2(b, OFF_GA))
            um = attend(b, memory) * _silu(col2(b, OFF_GM))
        branch_cols = [ua[0:LANES], ua[LANES:2 * LANES], um[0:LANES], um[LANES:2 * LANES]]

        lane = lax.broadcasted_iota(jnp.int32, (LANES, LANES), 1)
        ut = jnp.zeros((LANES, LANES), F32)
        for k, cvec in enumerate(branch_cols):
            ut = ut + jnp.where(lane == k, cvec, 0.0)
        u_ref[b, 0:4, :] = ut.T[0:4, :]

        orows = []
        for h in range(HG_HEADS):
            sl = slice(h * HG_DIM, (h + 1) * HG_DIM)
            lb_col = zt[:, bt * Z_ROWS + h:bt * Z_ROWS + h + 1]
            fcol = lb_col + (1.0 - lb_col) * _sigmoid(col(b, OFF_FB + h * HG_DIM))
            qcol = _silu(col(b, OFF_QB + h * HG_DIM))
            vrow = zr_ref[b, OFF_IB // LANES + h:OFF_IB // LANES + h + 1, :]
            sn = fcol * st_ref[b, sl, :] + (1.0 - fcol) * vrow
            nst_ref[b, sl, :] = sn
            orows.append(jnp.sum(sn * qcol, axis=0, keepdims=True))
        gate = zr_ref[b, OFF_GB // LANES:OFF_GB // LANES + HG_HEADS, :]
        u_ref[b, 4:4 + HG_HEADS, :] = _rms(jnp.concatenate(orows, axis=0), nh) * _silu(gate)


def _stored_view(cache):
    db, length = cache.shape[0], cache.shape[1]
    return jnp.transpose(cache, (0, 2, 3, 4, 1)).reshape(db, 2 * D_A, length)


def _sample_out_kernel(x_ref, u_ref, zg_ref, nf_ref, wa_ref, wb_ref, wm_ref, wo_ref, y_ref):
    def rows(lo, hi):
        return jnp.concatenate([u_ref[:, k, :] for k in range(lo, hi)], axis=1)

    ua, um, ub = rows(0, 2), rows(2, 4), rows(4, U_ROWS)
    y_ref[:, 0, :] = _merge_out(x_ref[:, 0, :], ua, ub, um, zg_ref[...],
                                wa_ref, wb_ref, wm_ref, wo_ref, nf_ref[...])


def _sample_out_call(x, u, zg, nf, wa, wb, wm, wo):
    def full(a):
        return pl.BlockSpec(a.shape, lambda i: (0,) * a.ndim)

    args = (x, u, zg, nf, wa, wb, wm, wo)
    return pl.pallas_call(
        _sample_out_kernel,
        out_shape=jax.ShapeDtypeStruct(x.shape, F32),
        grid=(1,),
        in_specs=[full(a) for a in args],
        out_specs=full(x),
        compiler_params=pltpu.CompilerParams(vmem_limit_bytes=VMEM_LIMIT),
        name="sample_out",
    )(*args)


def _cache_from_stored(kvt):
    b, _, length = kvt.shape
    return jnp.transpose(kvt.reshape(b, 2, A_HEADS, A_HEAD_DIM, length), (0, 4, 1, 2, 3))[None]


def kernel(x_prompt, x_sample, mem_prompt, cache_w1_kv, cache_w2_kv, cache_w3_kv, cache_mem_kv, state_hgrn,
           norm_in, w_in, lb_logits, norm_hgrn, norm_mem, w_mem_kv, w_branch_a, w_branch_b, w_branch_m,
           w_out, norm_final):
    b, s, _ = x_prompt.shape
    db = x_sample.shape[0]
    gain_in = norm_in[0][None]
    nh = norm_hgrn[0][None]
    nf = norm_final[None]
    lbl = lb_logits.astype(F32)

    zr, zg_s, kvt_s, w_in_bf, wa, wb, wm, wo, w_mem_bf = _sample_inproj_call(
        x_sample, gain_in, w_in[0], [w[0] for w in (w_branch_a, w_branch_b, w_branch_m, w_out, w_mem_kv)])
    mkv, mkv_t = _mem_kv_call(mem_prompt, norm_mem[0][None], w_mem_bf)
    sample_ops = (zr,
                  *[_stored_view(c[0]) for c in (cache_w1_kv, cache_w2_kv, cache_w3_kv, cache_mem_kv)],
                  state_hgrn[0].reshape(db, HG_HEADS * HG_DIM, HG_DIM), lbl, nh.reshape(HG_HEADS, HG_DIM))
    qkv0, qkv1, qkv2, gq, hb, zg, fb, kvt0, kvt1, kvt2, u_first, nst_first = _inproj_call(
        x_prompt, gain_in, w_in_bf, sample_ops)
    og = [_dilated_call(qkv, dil, g) for g, (qkv, (_, dil)) in enumerate(zip((qkv0, qkv1, qkv2), WIN_GROUPS))]
    y_prompt, hg_prompt, u, nst = _prompt_tail_call(
        x_prompt, gq, hb, zg, fb, og[0], og[1], og[2], mkv, lbl, nh, nf, wa, wb, wm, wo,
        sample_ops, u_first, nst_first)
    pw = [_cache_from_stored(kvt) for kvt in (kvt0, kvt1, kvt2)]
    new_mem = _cache_from_stored(mkv_t)

    y_sample = _sample_out_call(x_sample, u, zg_s, nf, wa, wb, wm, wo)
    sw = [jnp.transpose(kvt_s[g].reshape(2, A_HEADS, A_HEAD_DIM, db), (3, 0, 1, 2))[None, :, None]
          for g in range(N_GROUPS)]
    new_hg_sample = nst.reshape(1, db, HG_HEADS, HG_DIM, HG_DIM)

    return (y_prompt, y_sample, pw[0], pw[1], pw[2], new_mem, hg_prompt[None],
            sw[0], sw[1], sw[2], new_hg_sample)
```

```python
import jax
import jax.numpy as jnp
from jax import lax
from jax.experimental import pallas as pl
from jax.experimental.pallas import tpu as pltpu

F32 = jnp.float32
BF16 = jnp.bfloat16

D_MODEL = 1024
WIN_GROUPS = ((128, 1), (512, 4), (2048, 16))
N_GROUPS = 3
A_HEADS = 4
A_HEAD_DIM = 64
D_A = 256
QBLK = 128
HG_HEADS = 4
HG_DIM = 128
D_B = 512
HG_CHUNK = 64
HG_SUB = 16
MEM_LEN = 256
D_M = 256
EPS = 1e-6
D_IN = 8192
LANES = 128
OFF_QA, OFF_KA, OFF_VA, OFF_GA = 0, 768, 1536, 2304
OFF_QB, OFF_FB, OFF_IB, OFF_GB = 2560, 3072, 3584, 4096
OFF_QM, OFF_GM, OFF_ZG = 4608, 4864, 5120
ATT_SCALE = A_HEAD_DIM ** -0.5

VMEM_LIMIT = 56 * 1024 * 1024

NT_DIMS = (((1,), (1,)), ((), ()))
TN_DIMS = (((0,), (0,)), ((), ()))


def _sigmoid(x):
    return 0.5 * jnp.tanh(0.5 * x) + 0.5


def _silu(x):
    h = 0.5 * x
    return h * jnp.tanh(h) + h


def _rms(x, gain):
    return x * lax.rsqrt(jnp.mean(x * x, axis=-1, keepdims=True) + EPS) * gain


def _head_masks(width, head_dim, n_heads):
    lane = lax.broadcasted_iota(jnp.int32, (1, width), 1)
    return [(lane >= h * head_dim) & (lane < (h + 1) * head_dim) for h in range(n_heads)]


def _stack_heads(q, masks):
    zero = jnp.zeros_like(q)
    return jnp.concatenate([jnp.where(m, q, zero) for m in masks], axis=0)


def _unstack_heads(o_all, masks, t):
    n = len(masks)
    out = o_all[(n - 1) * t:n * t]
    for h in range(n - 2, -1, -1):
        out = jnp.where(masks[h], o_all[h * t:(h + 1) * t], out)
    return out


def _expand_cols(cols, masks):
    out = jnp.broadcast_to(cols[-1], (cols[-1].shape[0], masks[0].shape[1]))
    for h in range(len(masks) - 2, -1, -1):
        out = jnp.where(masks[h], cols[h], out)
    return out


def _mem_kv_kernel(mem_ref, gain_ref, w_ref, out_ref, out_t_ref):
    hm = _rms(mem_ref[0], gain_ref[...]).astype(BF16)
    kv = jnp.dot(hm, w_ref[...], preferred_element_type=F32)
    out_ref[0] = kv
    out_t_ref[0] = kv.T


def _mem_kv_call(mem, gain, w_bf):
    b = mem.shape[0]
    return pl.pallas_call(
        _mem_kv_kernel,
        out_shape=(jax.ShapeDtypeStruct((b, MEM_LEN, 2 * D_M), F32),
                   jax.ShapeDtypeStruct((b, 2 * D_M, MEM_LEN), F32)),
        grid=(b,),
        in_specs=[pl.BlockSpec((1, MEM_LEN, D_MODEL), lambda i: (i, 0, 0)),
                  pl.BlockSpec((1, D_MODEL), lambda i: (0, 0)),
                  pl.BlockSpec((D_MODEL, 2 * D_M), lambda i: (0, 0))],
        out_specs=(pl.BlockSpec((1, MEM_LEN, 2 * D_M), lambda i: (i, 0, 0)),
                   pl.BlockSpec((1, 2 * D_M, MEM_LEN), lambda i: (i, 0, 0))),
        name="mem_kv",
    )(mem, gain, w_bf)


IN_TILE = 256
IN_CHUNK = 256
D_QKV = 3 * D_A
D_GQ = 3 * D_A
D_HB = 3 * D_B
D_ZG = 3 * D_MODEL


def _inproj_kernel(x_ref, gain_ref, w_ref, zr_ref, c1_ref, c2_ref, c3_ref, cm_ref, sst_ref, slbl_ref, nh4_ref,
                   qkv0_ref, qkv1_ref, qkv2_ref, gq_ref, hb_ref, zg_ref, fb_ref, kvt0_ref, kvt1_ref, kvt2_ref,
                   u_ref, nst_ref, scr_ref):
    qkv_refs = (qkv0_ref, qkv1_ref, qkv2_ref)
    kvt_refs = (kvt0_ref, kvt1_ref, kvt2_ref)
    t = IN_TILE
    h = _rms(x_ref[0], gain_ref[...]).astype(BF16)
    for c in range(D_IN // IN_CHUNK):
        lo = c * IN_CHUNK
        zc = jnp.dot(h, w_ref[:, lo:lo + IN_CHUNK], preferred_element_type=F32)
        if lo < OFF_GA:
            sec, g = divmod(c, N_GROUPS)
            dil = WIN_GROUPS[g][1]
            dst = slice(sec * D_A, (sec + 1) * D_A)
            if dil == 1:
                qkv_refs[g][0, 0, :, dst] = zc.astype(BF16)
            else:
                slot = scr_ref.at[sec * (N_GROUPS - 1) + g - 1]
                for k in range(IN_CHUNK // LANES):
                    slot[k] = zc[:, k * LANES:(k + 1) * LANES]
                for r in range(dil):
                    qkv_refs[g][0, r, :, dst] = jnp.concatenate(
                        [slot[k, pl.ds(r, t // dil, stride=dil), :] for k in range(IN_CHUNK // LANES)],
                        axis=1).astype(BF16)
            if sec > 0:
                width = kvt_refs[g].shape[2]
                kvt_refs[g][0, (sec - 1) * D_A:sec * D_A, :] = zc.T[:, t - width:]
        elif lo < OFF_QB:
            gq_ref[0, :, 0:D_A] = zc.astype(BF16)
        elif lo < OFF_FB:
            hb_ref[0, :, lo - OFF_QB:lo - OFF_QB + IN_CHUNK] = zc.astype(BF16)
        elif lo < OFF_IB:
            fb_ref[0, :, lo - OFF_FB:lo - OFF_FB + IN_CHUNK] = zc
        elif lo < OFF_QM:
            hb_ref[0, :, lo - OFF_IB + D_B:lo - OFF_IB + D_B + IN_CHUNK] = zc.astype(BF16)
        elif lo < OFF_ZG:
            gq_ref[0, :, lo - OFF_QM + D_A:lo - OFF_QM + D_A + IN_CHUNK] = zc.astype(BF16)
        else:
            zg_ref[0, :, lo - OFF_ZG:lo - OFF_ZG + IN_CHUNK] = zc.astype(BF16)
    _sample_rows(zr_ref, c1_ref, c2_ref, c3_ref, cm_ref, sst_ref, slbl_ref, nh4_ref, u_ref, nst_ref,
                 memory_first=True)


def _sample_specs(ops, nt, first_row):
    first_blk = first_row // S_TILE

    def rows(a):
        return pl.BlockSpec((S_TILE,) + a.shape[1:], lambda i, j: (first_blk + i * nt + j, 0, 0))

    def const(a):
        return pl.BlockSpec(a.shape, lambda i, j: (0, 0), pipeline_mode=pl.Buffered(1))

    return [rows(a) for a in ops[:6]] + [const(a) for a in ops[6:]]


def _inproj_call(x, gain, w_bf, sample_ops):
    b, s, _ = x.shape
    t = IN_TILE
    nt = s // t
    n_rows = b * nt * S_TILE

    def tok(width):
        return pl.BlockSpec((1, t, width), lambda i, j: (i, j, 0))

    def srow(shape):
        return pl.BlockSpec((S_TILE,) + shape, lambda i, j: (i * nt + j, 0, 0))

    qkv_shapes, qkv_specs = [], []
    for _, dil in WIN_GROUPS:
        qkv_shapes.append(jax.ShapeDtypeStruct((b, dil, s // dil, D_QKV), BF16))
        qkv_specs.append(pl.BlockSpec((1, dil, t // dil, D_QKV), lambda i, j: (i, 0, j, 0)))
    kvt_shapes, kvt_specs = [], []
    for window, _ in WIN_GROUPS:
        length = min(window, s)
        width = min(t, length)
        first = nt - length // width
        kvt_shapes.append(jax.ShapeDtypeStruct((b, 2 * D_A, length), F32))
        kvt_specs.append(pl.BlockSpec((1, 2 * D_A, width),
                                      lambda i, j, first=first: (i, 0, jnp.maximum(j - first, 0))))
    return pl.pallas_call(
        _inproj_kernel,
        out_shape=(*qkv_shapes,
                   jax.ShapeDtypeStruct((b, s, D_GQ), BF16),
                   jax.ShapeDtypeStruct((b, s, D_HB), BF16),
                   jax.ShapeDtypeStruct((b, s, D_ZG), BF16),
                   jax.ShapeDtypeStruct((b, s, D_B), F32),
                   *kvt_shapes,
                   jax.ShapeDtypeStruct((n_rows, U_ROWS, LANES), F32),
                   jax.ShapeDtypeStruct((n_rows, HG_HEADS * HG_DIM, HG_DIM), F32)),
        grid=(b, nt),
        in_specs=[tok(D_MODEL),
                  pl.BlockSpec((1, D_MODEL), lambda i, j: (0, 0)),
                  pl.BlockSpec((D_MODEL, D_IN), lambda i, j: (0, 0), pipeline_mode=pl.Buffered(1)),
                  *_sample_specs(sample_ops, nt, 0)],
        out_specs=(*qkv_specs, tok(D_GQ), tok(D_HB), tok(D_ZG), tok(D_B), *kvt_specs,
                   srow((U_ROWS, LANES)), srow((HG_HEADS * HG_DIM, HG_DIM))),
        scratch_shapes=[pltpu.VMEM((3 * (N_GROUPS - 1), IN_CHUNK // LANES, t, LANES), F32)],
        compiler_params=pltpu.CompilerParams(
            dimension_semantics=("arbitrary", "arbitrary"), vmem_limit_bytes=VMEM_LIMIT),
        name="inproj",
    )(x, gain, w_bf, *sample_ops)


DIL_STEP = 2048
OG_SLABS = 2 * D_A // LANES


def _make_dilated_kernel(dil):
    nblk = DIL_STEP // (dil * QBLK)

    def body(prev_ref, cur_ref, out_ref):
        first_step = pl.program_id(1) == 0
        masks = _head_masks(D_A, A_HEAD_DIM, A_HEADS)
        qi = lax.broadcasted_iota(jnp.int32, (A_HEADS * QBLK, 2 * QBLK), 0) % QBLK
        kj = lax.broadcasted_iota(jnp.int32, (A_HEADS * QBLK, 2 * QBLK), 1)
        band = (kj >= qi) & (kj <= qi + QBLK)
        neg_band = jnp.where(band, 0.0, -jnp.inf)
        neg_first = jnp.where(first_step, jnp.where(band & (kj >= QBLK), 0.0, -jnp.inf), neg_band)

        def keys_values(r, jb, c):
            rows = slice(jb * QBLK, (jb + 1) * QBLK)
            if jb == 0:
                return jnp.concatenate([prev_ref[0, r, :, c * D_A:(c + 1) * D_A],
                                        cur_ref[0, r, rows, c * D_A:(c + 1) * D_A]], axis=0)
            return cur_ref[0, r, (jb - 1) * QBLK:(jb + 1) * QBLK, c * D_A:(c + 1) * D_A]

        for r, jb in [(r, jb) for r in range(dil) for jb in range(nblk)]:
            q = cur_ref[0, r, jb * QBLK:(jb + 1) * QBLK, 0:D_A]
            qs = _stack_heads(q * jnp.asarray(ATT_SCALE, BF16), masks)
            s = lax.dot_general(qs, keys_values(r, jb, 1), NT_DIMS, preferred_element_type=F32)
            s = s + (neg_first if jb == 0 else neg_band)
            mx = jnp.max(s, axis=-1, keepdims=True)
            p = jnp.exp(s - mx)
            l = jnp.sum(p, axis=-1, keepdims=True)
            o_all = jnp.dot(p.astype(BF16), keys_values(r, jb, 2), preferred_element_type=F32)
            l_e = _expand_cols([l[h * QBLK:(h + 1) * QBLK] for h in range(A_HEADS)], masks)
            mx_e = _expand_cols([mx[h * QBLK:(h + 1) * QBLK] for h in range(A_HEADS)], masks)
            res = (_unstack_heads(o_all, masks, QBLK) / l_e, mx_e + jnp.log(l_e))
            for k in range(OG_SLABS):
                src = res[k // 2][:, (k % 2) * LANES:(k % 2 + 1) * LANES]
                if dil == 1:
                    out_ref[0, k, jb * QBLK:(jb + 1) * QBLK, :] = src
                else:
                    out_ref[0, k, pl.ds(jb * QBLK * dil + r, QBLK, stride=dil), :] = src
    return body


def _dilated_call(qkv, dil, g):
    b, _, n, _ = qkv.shape
    rows = DIL_STEP // dil
    nb = n // rows
    return pl.pallas_call(
        _make_dilated_kernel(dil),
        out_shape=jax.ShapeDtypeStruct((b, OG_SLABS, n * dil, LANES), F32),
        grid=(b, nb),
        in_specs=[pl.BlockSpec((1, dil, QBLK, D_QKV),
                               lambda i, u: (i, 0, jnp.maximum(u * (rows // QBLK) - 1, 0), 0)),
                  pl.BlockSpec((1, dil, rows, D_QKV), lambda i, u: (i, 0, u, 0))],
        out_specs=pl.BlockSpec((1, OG_SLABS, DIL_STEP, LANES), lambda i, u: (i, 0, u, 0)),
        compiler_params=pltpu.CompilerParams(
            dimension_semantics=("arbitrary", "arbitrary"), vmem_limit_bytes=VMEM_LIMIT),
        name=f"dilated_attn_g{g}",
    )(qkv, qkv)


def _lower_bound(l0, l1):
    m = jnp.maximum(l0, l1)
    e0, e1 = jnp.exp(l0 - m), jnp.exp(l1 - m)
    return e0 / (e0 + e1)


def _head_rms(ob, gain):
    parts = []
    for h in range(HG_HEADS):
        sl = slice(h * HG_DIM, (h + 1) * HG_DIM)
        parts.append(_rms(ob[:, sl], gain[:, sl]))
    return jnp.concatenate(parts, axis=1)


def _merge_out(x, ua, ub, um, zg, wa_ref, wb_ref, wm_ref, wo_ref, nf):
    projs = [jnp.dot(u.astype(BF16), w_ref[...], preferred_element_type=F32)
             for u, w_ref in ((ua, wa_ref), (ub, wb_ref), (um, wm_ref))]
    gates = _sigmoid(zg.astype(BF16))
    merged = None
    for k, proj in enumerate(projs):
        term = gates[:, k * D_MODEL:(k + 1) * D_MODEL] * proj.astype(BF16)
        merged = term if merged is None else merged + term
    y = x + jnp.dot(merged, wo_ref[...], preferred_element_type=F32)
    return _rms(y, nf)


TAIL_TILE = 256


def _cumsum_rows(g, tri):
    g1 = g.astype(BF16)
    g2 = (g - g1.astype(F32)).astype(BF16)
    return jnp.dot(tri, g1, preferred_element_type=F32) + jnp.dot(tri, g2, preferred_element_type=F32)


def _hgrn_tile(fb, qb, v, lb, st_ref):
    t = fb.shape[0]
    c, sub = HG_CHUNK, HG_SUB
    n_sub = c // sub
    f = lb + (1.0 - lb) * _sigmoid(fb)
    kk = 1.0 - f
    row = lax.broadcasted_iota(jnp.int32, (t, t), 0)
    col = lax.broadcasted_iota(jnp.int32, (t, t), 1)
    same_chunk = (row // c) == (col // c)
    gcs = _cumsum_rows(jnp.log2(f), (same_chunk & (col <= row)).astype(BF16))
    q = _silu(qb).astype(F32)

    loc = lax.broadcasted_iota(jnp.int32, (c, 1), 0)
    qt_c, kt_ci, qe_c, kd_c, dec_c = [], [[] for _ in range(n_sub)], [], [], []
    for ci in range(t // c):
        rows = slice(ci * c, (ci + 1) * c)
        g_c, q_c, k_c = gcs[rows], q[rows], kk[rows]
        refs = [jnp.zeros((1, D_B), F32)] + [g_c[i * sub - 1:i * sub] for i in range(1, n_sub)]
        own = refs[n_sub - 1]
        for i in range(n_sub - 2, -1, -1):
            own = jnp.where(loc < (i + 1) * sub, refs[i], own)
        qt_c.append((q_c * jnp.exp2(g_c - own)).astype(BF16))
        k_own = k_c * jnp.exp2(own - g_c)
        for i in range(n_sub):
            pieces = []
            for jb in range(n_sub):
                piece = k_own[jb * sub:(jb + 1) * sub]
                if jb < i:
                    piece = piece * jnp.exp2(refs[i] - refs[jb])
                elif jb > i:
                    piece = jnp.zeros_like(piece)
                pieces.append(piece)
            kt_ci[i].append(jnp.concatenate(pieces, axis=0).astype(BF16))
        g_last = g_c[c - 1:c]
        qe_c.append((q_c * jnp.exp2(g_c)).astype(BF16))
        kd_c.append((k_c * jnp.exp2(g_last - g_c)).astype(BF16))
        dec_c.append(jnp.exp2(g_last))
    qt = jnp.concatenate(qt_c, axis=0)
    kts = [jnp.concatenate(parts, axis=0) for parts in kt_ci]
    sub_of_row = (lax.broadcasted_iota(jnp.int32, (t, 1), 0) % c) // sub
    causal = same_chunk & (col <= row)

    heads = [slice(h * HG_DIM, (h + 1) * HG_DIM) for h in range(HG_HEADS)]
    chunks = [slice(ci * c, (ci + 1) * c) for ci in range(t // c)]
    zero = jnp.zeros((t, HG_DIM), BF16)
    atts, incs = [], []
    for sl in heads:
        q_big = jnp.concatenate([jnp.where(sub_of_row == i, qt[:, sl], zero) for i in range(n_sub)], axis=1)
        k_big = jnp.concatenate([kts[i][:, sl] for i in range(n_sub)], axis=1)
        atts.append(lax.dot_general(q_big, k_big, NT_DIMS, preferred_element_type=F32))
        incs.append([lax.dot_general(v[rows, sl], kd_c[ci][:, sl], TN_DIMS, preferred_element_type=F32)
                     for ci, rows in enumerate(chunks)])
    states = []
    for h, sl in enumerate(heads):
        atts[h] = jnp.where(causal, atts[h], 0.0).astype(BF16)
        sts = [st_ref[h]]
        for ci in range(len(chunks)):
            sts.append(sts[-1] * dec_c[ci][:, sl] + incs[h][ci])
        st_ref[h] = sts[-1]
        states.append(sts)
    outs = []
    for h, sl in enumerate(heads):
        o_intra = jnp.dot(atts[h], v[:, sl], preferred_element_type=F32)
        inter = [lax.dot_general(qe_c[ci][:, sl], states[h][ci].astype(BF16), NT_DIMS, preferred_element_type=F32)
                 for ci in range(len(chunks))]
        outs.append(o_intra + jnp.concatenate(inter, axis=0))
    return jnp.concatenate(outs, axis=1)


def _prompt_tail_kernel(x_ref, gq_ref, hb_ref, zg_ref, fb_ref, o1_ref, o2_ref, o3_ref, mkv_ref, lbl_ref,
                        nh_ref, nf_ref, wa_ref, wb_ref, wm_ref, wo_ref,
                        zr_ref, c1_ref, c2_ref, c3_ref, cm_ref, sst_ref, slbl_ref, nh4_ref, ua_ref, nsta_ref,
                        y_ref, hst_ref, u_ref, nst_ref, st_ref):
    j = pl.program_id(1)
    t = TAIL_TILE

    @pl.when(j == 0)
    def _():
        st_ref[...] = jnp.zeros_like(st_ref)

    u_ref[0] = ua_ref[...]
    nst_ref[0] = nsta_ref[...]
    _sample_rows(zr_ref, c1_ref, c2_ref, c3_ref, cm_ref, sst_ref, slbl_ref, nh4_ref, u_ref.at[1], nst_ref.at[1],
                 memory_first=False)

    masks = _head_masks(D_M, A_HEAD_DIM, A_HEADS)
    qm = gq_ref[0, :, D_A:2 * D_A] * jnp.asarray(ATT_SCALE, BF16)
    qs = _stack_heads(qm, masks)
    mk = mkv_ref[0, :, 0:D_M].astype(BF16)
    mv = mkv_ref[0, :, D_M:2 * D_M].astype(BF16)
    s = lax.dot_general(qs, mk, NT_DIMS, preferred_element_type=F32)

    lb = _lower_bound(lbl_ref[0:1], lbl_ref[1:2])
    ob = _hgrn_tile(fb_ref[0], hb_ref[0, :, 0:D_B], hb_ref[0, :, D_B:2 * D_B], lb, st_ref)
    ub = _head_rms(ob, nh_ref[...]).astype(BF16) * _silu(hb_ref[0, :, 2 * D_B:3 * D_B])

    os_ = [jnp.concatenate([r[0, 0], r[0, 1]], axis=1) for r in (o1_ref, o2_ref, o3_ref)]
    ls_ = [jnp.concatenate([r[0, 2], r[0, 3]], axis=1) for r in (o1_ref, o2_ref, o3_ref)]
    mx = jnp.maximum(jnp.maximum(ls_[0], ls_[1]), ls_[2])
    es_ = [jnp.exp(l - mx) for l in ls_]
    oa = (es_[0] * os_[0] + es_[1] * os_[1] + es_[2] * os_[2]) / (es_[0] + es_[1] + es_[2])
    ua = oa.astype(BF16) * _silu(gq_ref[0, :, 0:D_A])

    p = jnp.exp(s - jnp.max(s, axis=-1, keepdims=True))
    l = jnp.sum(p, axis=-1, keepdims=True)
    om = _unstack_heads(jnp.dot(p.astype(BF16), mv, preferred_element_type=F32) / l, masks, t)
    um = om.astype(BF16) * _silu(gq_ref[0, :, 2 * D_A:3 * D_A])

    y_ref[0] = _merge_out(x_ref[0], ua, ub, um, zg_ref[0], wa_ref, wb_ref, wm_ref, wo_ref, nf_ref[...])

    @pl.when(j == pl.num_programs(1) - 1)
    def _():
        for h in range(HG_HEADS):
            hst_ref[0, h] = st_ref[h].T


def _prompt_tail_call(x, gq, hb, zg, fb, o1, o2, o3, mkv, lbl, nh, nf, wa, wb, wm, wo,
                      sample_ops, u_first, nst_first):
    b, s, _ = x.shape
    t = TAIL_TILE
    nt = s // t
    db = sample_ops[0].shape[0]
    half = u_first.shape[0]
    assert half + b * nt * S_TILE == db and 2 * half == db, "sample rows must split evenly over both grids"

    def tok(width):
        return pl.BlockSpec((1, t, width), lambda i, j: (i, j, 0))

    def const(shape):
        return pl.BlockSpec(shape, lambda i, j: (0,) * len(shape), pipeline_mode=pl.Buffered(1))

    def first(a):
        return pl.BlockSpec((S_TILE,) + a.shape[1:], lambda i, j: (i * nt + j, 0, 0))

    def halves(shape):
        return pl.BlockSpec((2, S_TILE) + shape, lambda i, j: (0, i * nt + j, 0, 0))

    y, hst, u, nst = pl.pallas_call(
        _prompt_tail_kernel,
        out_shape=(jax.ShapeDtypeStruct((b, s, D_MODEL), F32),
                   jax.ShapeDtypeStruct((b, HG_HEADS, HG_DIM, HG_DIM), F32),
                   jax.ShapeDtypeStruct((2, half, U_ROWS, LANES), F32),
                   jax.ShapeDtypeStruct((2, half, HG_HEADS * HG_DIM, HG_DIM), F32)),
        grid=(b, nt),
        in_specs=[tok(D_MODEL), tok(D_GQ), tok(D_HB), tok(D_ZG), tok(D_B),
                  *[pl.BlockSpec((1, OG_SLABS, t, LANES), lambda i, j: (i, 0, j, 0))] * N_GROUPS,
                  pl.BlockSpec((1, MEM_LEN, 2 * D_M), lambda i, j: (i, 0, 0)),
                  const((2, D_B)), const((1, D_B)), const((1, D_MODEL)),
                  const((D_A, D_MODEL)), const((D_B, D_MODEL)), const((D_M, D_MODEL)),
                  const((D_MODEL, D_MODEL)),
                  *_sample_specs(sample_ops, nt, half), first(u_first), first(nst_first)],
        out_specs=(tok(D_MODEL),
                   pl.BlockSpec((1, HG_HEADS, HG_DIM, HG_DIM), lambda i, j: (i, 0, 0, 0)),
                   halves((U_ROWS, LANES)), halves((HG_HEADS * HG_DIM, HG_DIM))),
        scratch_shapes=[pltpu.VMEM((HG_HEADS, HG_DIM, HG_DIM), F32)],
        compiler_params=pltpu.CompilerParams(
            dimension_semantics=("arbitrary", "arbitrary"), vmem_limit_bytes=VMEM_LIMIT),
        name="prompt_tail",
    )(x, gq, hb, zg, fb, o1, o2, o3, mkv, lbl, nh, nf, wa, wb, wm, wo, *sample_ops, u_first, nst_first)
    return y, hst, u.reshape(db, U_ROWS, LANES), nst.reshape(db, HG_HEADS * HG_DIM, HG_DIM)


S_COLS = 1024


def _sample_inproj_kernel(x_ref, gain_ref, w_ref, *refs):
    n_small = (len(refs) - 4) // 2
    small_in, (zr_ref, zg_ref, kvt_ref, wbf_ref), small_out = refs[:n_small], refs[n_small:n_small + 4], refs[n_small + 4:]
    for src, dst in zip(small_in, small_out):
        dst[...] = src[...].astype(BF16)
    c = pl.program_id(0)
    h = _rms(x_ref[:, 0, :], gain_ref[...]).astype(BF16)
    w = w_ref[...].astype(BF16)
    wbf_ref[...] = w
    z = jnp.dot(h, w, preferred_element_type=F32)
    for k in range(S_COLS // LANES):
        zr_ref[:, k, :] = z[:, k * LANES:(k + 1) * LANES]
    zg_ref[...] = z
    for step in range(OFF_GA // S_COLS + 1):
        pieces = [(g, kv, (OFF_KA + (kv * N_GROUPS + g) * D_A) % S_COLS)
                  for g in range(N_GROUPS) for kv in range(2)
                  if (OFF_KA + (kv * N_GROUPS + g) * D_A) // S_COLS == step]

        @pl.when(c == step)
        def _(pieces=pieces):
            zt = z.T
            for g, kv, row0 in pieces:
                kvt_ref[g, kv * D_A:(kv + 1) * D_A, :] = zt[row0:row0 + D_A]


def _sample_inproj_call(x, gain, w, small_weights):
    db = x.shape[0]
    steps = D_IN // S_COLS
    first_zg = OFF_ZG // S_COLS
    small_specs = [pl.BlockSpec((a.shape[0] // steps, a.shape[1]), lambda c: (c, 0)) for a in small_weights]
    return pl.pallas_call(
        _sample_inproj_kernel,
        out_shape=(jax.ShapeDtypeStruct((db, Z_ROWS, LANES), F32),
                   jax.ShapeDtypeStruct((db, D_ZG), F32),
                   jax.ShapeDtypeStruct((N_GROUPS, 2 * D_A, db), F32),
                   jax.ShapeDtypeStruct((D_MODEL, D_IN), BF16),
                   *[jax.ShapeDtypeStruct(a.shape, BF16) for a in small_weights]),
        grid=(steps,),
        in_specs=[pl.BlockSpec(x.shape, lambda c: (0, 0, 0)),
                  pl.BlockSpec((1, D_MODEL), lambda c: (0, 0)),
                  pl.BlockSpec((D_MODEL, S_COLS), lambda c: (0, c)),
                  *small_specs],
        out_specs=(pl.BlockSpec((db, S_COLS // LANES, LANES), lambda c: (0, c, 0)),
                   pl.BlockSpec((db, S_COLS), lambda c: (0, jnp.maximum(c - first_zg, 0))),
                   pl.BlockSpec((N_GROUPS, 2 * D_A, db), lambda c: (0, 0, 0)),
                   pl.BlockSpec((D_MODEL, S_COLS), lambda c: (0, c)),
                   *small_specs),
        compiler_params=pltpu.CompilerParams(dimension_semantics=("arbitrary",)),
        name="sample_inproj",
    )(x, gain, w, *small_weights)


S_TILE = 1
Z_ROWS = D_IN // LANES
U_ROWS = 8


def _sample_rows(zr_ref, c1_ref, c2_ref, c3_ref, cm_ref, st_ref, slbl_ref, nh_ref, u_ref, nst_ref, *, memory_first):
    bt = S_TILE
    assert bt * Z_ROWS + HG_HEADS <= LANES
    lb_row = _lower_bound(slbl_ref[0:1], slbl_ref[1:2])
    lb_rows = [lb_row[:, h * HG_DIM:(h + 1) * HG_DIM] for h in range(HG_HEADS)]
    pad = jnp.zeros((LANES - bt * Z_ROWS - HG_HEADS, LANES), F32)
    zt = jnp.concatenate([zr_ref[b] for b in range(bt)] + lb_rows + [pad], axis=0).T

    def col(b, off, n=HG_DIM):
        j, l = divmod(off, LANES)
        return zt[l:l + n, Z_ROWS * b + j:Z_ROWS * b + j + 1]

    nh = nh_ref[...]

    def col2(b, off):
        return jnp.concatenate([col(b, off), col(b, off + LANES)], axis=0)

    def per_head(x):
        return jnp.concatenate([jnp.sum(x[h * A_HEAD_DIM:(h + 1) * A_HEAD_DIM], axis=0, keepdims=True)
                                for h in range(A_HEADS)], axis=0)

    def spread(x):
        return jnp.concatenate([jnp.broadcast_to(x[h:h + 1], (A_HEAD_DIM, 1)) for h in range(A_HEADS)], axis=0)

    def attend(b, srcs):
        parts, new_scores, qs = [], [], []
        for c_ref, dil, q_off, k_off, _ in srcs:
            q = col2(b, q_off) * ATT_SCALE
            qs.append(q)
            s = per_head(c_ref[b, 0:D_A, :] * q)
            if dil > 1:
                lane = lax.broadcasted_iota(jnp.int32, s.shape, 1)
                s = jnp.where(lane % dil == 0, s, -jnp.inf)
            parts.append(s)
            if k_off is not None:
                new_scores.append(per_head(col2(b, k_off) * q))
        if new_scores:
            lane = lax.broadcasted_iota(jnp.int32, (A_HEADS, LANES), 1)
            slab = jnp.full((A_HEADS, LANES), -jnp.inf, F32)
            for k, sn in enumerate(new_scores):
                slab = jnp.where(lane == k, sn, slab)
            parts.append(slab)
        s_all = jnp.concatenate(parts, axis=1)
        p_all = jnp.exp(s_all - jnp.max(s_all, axis=1, keepdims=True))
        den = jnp.sum(p_all, axis=1, keepdims=True)
        acc = [jnp.zeros((A_HEAD_DIM, LANES), F32) for _ in range(A_HEADS)]
        lo = 0
        for c_ref, _, _, _, _ in srcs:
            length = c_ref.shape[2]
            for k in range(length // LANES):
                ls = slice(k * LANES, (k + 1) * LANES)
                for h in range(A_HEADS):
                    vt = c_ref[b, D_A + h * A_HEAD_DIM:D_A + (h + 1) * A_HEAD_DIM, ls]
                    acc[h] = acc[h] + vt * p_all[h:h + 1, lo + k * LANES:lo + (k + 1) * LANES]
            lo += length
        num = jnp.sum(jnp.concatenate(acc, axis=0), axis=1, keepdims=True)
        k = 0
        for _, _, _, k_off, v_off in srcs:
            if k_off is not None:
                num = num + spread(p_all[:, lo + k:lo + k + 1]) * col2(b, v_off)
                k += 1
        return num / spread(den)

    window = [(c_ref, dil, OFF_QA + g * D_A, OFF_KA + g * D_A, OFF_VA + g * D_A)
              for g, (c_ref, (_, dil)) in enumerate(zip((c1_ref, c2_ref, c3_ref), WIN_GROUPS))]
    memory = [(cm_ref, 1, OFF_QM, None, None)]

    for b in range(bt):
        if memory_first:
            um = attend(b, memory) * _silu(col2(b, OFF_GM))
            ua = attend(b, window) * _silu(col2(b, OFF_GA))
        else:
            ua = attend(b, window) * _silu(col2(b, OFF_GA))
            um = attend(b, memory) * _silu(col2(b, OFF_GM))
        branch_cols = [ua[0:LANES], ua[LANES:2 * LANES], um[0:LANES], um[LANES:2 * LANES]]

        lane = lax.broadcasted_iota(jnp.int32, (LANES, LANES), 1)
        ut = jnp.zeros((LANES, LANES), F32)
        for k, cvec in enumerate(branch_cols):
            ut = ut + jnp.where(lane == k, cvec, 0.0)
        u_ref[b, 0:4, :] = ut.T[0:4, :]

        orows = []
        for h in range(HG_HEADS):
            sl = slice(h * HG_DIM, (h + 1) * HG_DIM)
            lb_col = zt[:, bt * Z_ROWS + h:bt * Z_ROWS + h + 1]
            fcol = lb_col + (1.0 - lb_col) * _sigmoid(col(b, OFF_FB + h * HG_DIM))
            qcol = _silu(col(b, OFF_QB + h * HG_DIM))
            vrow = zr_ref[b, OFF_IB // LANES + h:OFF_IB // LANES + h + 1, :]
            sn = fcol * st_ref[b, sl, :] + (1.0 - fcol) * vrow
            nst_ref[b, sl, :] = sn
            orows.append(jnp.sum(sn * qcol, axis=0, keepdims=True))
        gate = zr_ref[b, OFF_GB // LANES:OFF_GB // LANES + HG_HEADS, :]
        u_ref[b, 4:4 + HG_HEADS, :] = _rms(jnp.concatenate(orows, axis=0), nh) * _silu(gate)


def _stored_view(cache):
    db, length = cache.shape[0], cache.shape[1]
    return jnp.transpose(cache, (0, 2, 3, 4, 1)).reshape(db, 2 * D_A, length)


def _sample_out_kernel(x_ref, u_ref, zg_ref, nf_ref, wa_ref, wb_ref, wm_ref, wo_ref, y_ref):
    def rows(lo, hi):
        return jnp.concatenate([u_ref[:, k, :] for k in range(lo, hi)], axis=1)

    ua, um, ub = rows(0, 2), rows(2, 4), rows(4, U_ROWS)
    y_ref[:, 0, :] = _merge_out(x_ref[:, 0, :], ua, ub, um, zg_ref[...],
                                wa_ref, wb_ref, wm_ref, wo_ref, nf_ref[...])


def _sample_out_call(x, u, zg, nf, wa, wb, wm, wo):
    def full(a):
        return pl.BlockSpec(a.shape, lambda i: (0,) * a.ndim)

    args = (x, u, zg, nf, wa, wb, wm, wo)
    return pl.pallas_call(
        _sample_out_kernel,
        out_shape=jax.ShapeDtypeStruct(x.shape, F32),
        grid=(1,),
        in_specs=[full(a) for a in args],
        out_specs=full(x),
        compiler_params=pltpu.CompilerParams(vmem_limit_bytes=VMEM_LIMIT),
        name="sample_out",
    )(*args)


def _cache_from_stored(kvt):
    b, _, length = kvt.shape
    return jnp.transpose(kvt.reshape(b, 2, A_HEADS, A_HEAD_DIM, length), (0, 4, 1, 2, 3))[None]


def kernel(x_prompt, x_sample, mem_prompt, cache_w1_kv, cache_w2_kv, cache_w3_kv, cache_mem_kv, state_hgrn,
           norm_in, w_in, lb_logits, norm_hgrn, norm_mem, w_mem_kv, w_branch_a, w_branch_b, w_branch_m,
           w_out, norm_final):
    b, s, _ = x_prompt.shape
    db = x_sample.shape[0]
    gain_in = norm_in[0][None]
    nh = norm_hgrn[0][None]
    nf = norm_final[None]
    lbl = lb_logits.astype(F32)

    zr, zg_s, kvt_s, w_in_bf, wa, wb, wm, wo, w_mem_bf = _sample_inproj_call(
        x_sample, gain_in, w_in[0], [w[0] for w in (w_branch_a, w_branch_b, w_branch_m, w_out, w_mem_kv)])
    mkv, mkv_t = _mem_kv_call(mem_prompt, norm_mem[0][None], w_mem_bf)
    sample_ops = (zr,
                  *[_stored_view(c[0]) for c in (cache_w1_kv, cache_w2_kv, cache_w3_kv, cache_mem_kv)],
                  state_hgrn[0].reshape(db, HG_HEADS * HG_DIM, HG_DIM), lbl, nh.reshape(HG_HEADS, HG_DIM))
    qkv0, qkv1, qkv2, gq, hb, zg, fb, kvt0, kvt1, kvt2, u_first, nst_first = _inproj_call(
        x_prompt, gain_in, w_in_bf, sample_ops)
    og = [_dilated_call(qkv, dil, g) for g, (qkv, (_, dil)) in enumerate(zip((qkv0, qkv1, qkv2), WIN_GROUPS))]
    y_prompt, hg_prompt, u, nst = _prompt_tail_call(
        x_prompt, gq, hb, zg, fb, og[0], og[1], og[2], mkv, lbl, nh, nf, wa, wb, wm, wo,
        sample_ops, u_first, nst_first)
    pw = [_cache_from_stored(kvt) for kvt in (kvt0, kvt1, kvt2)]
    new_mem = _cache_from_stored(mkv_t)

    y_sample = _sample_out_call(x_sample, u, zg_s, nf, wa, wb, wm, wo)
    sw = [jnp.transpose(kvt_s[g].reshape(2, A_HEADS, A_HEAD_DIM, db), (3, 0, 1, 2))[None, :, None]
          for g in range(N_GROUPS)]
    new_hg_sample = nst.reshape(1, db, HG_HEADS, HG_DIM, HG_DIM)

    return (y_prompt, y_sample, pw[0], pw[1], pw[2], new_mem, hg_prompt[None],
            sw[0], sw[1], sw[2], new_hg_sample)
```
